```python
import math
import jax, jax.numpy as jnp
from jax import lax
import numpy as np

D_MODEL = 1024
BATCH = 16
SEQ = 4096
DEPTH = 4

CHUNK = 64
N_PREV_CHUNKS = 8
BAND = CHUNK * (N_PREV_CHUNKS + 1)
MAX_REL = 128
HEAD_DIM = 64
MIX_WIDTH = D_MODEL
D_SB = MIX_WIDTH // 2
D_CA = MIX_WIDTH - D_SB
H_SB = D_SB // HEAD_DIM
H_CA = D_CA // HEAD_DIM
SB_QBLOCK = 128
D_FF = int(round(8 * D_MODEL / 3 / 256)) * 256
FFN_RESIDUAL = 0.5
ALPHA = (2 * DEPTH) ** 0.25
BETA_INIT = (8 * DEPTH) ** -0.25
LN_EPS = 1e-5
RMS_EPS = 1e-6

kernel_name = "hybrid_stickbreak_chunkattn_macaron_deepnorm"


def layer_norm(x, g, b):
    xf = x.astype(jnp.float32)
    mu = jnp.mean(xf, axis=-1, keepdims=True)
    xc = xf - mu
    var = jnp.mean(xc * xc, axis=-1, keepdims=True)
    y = xc * lax.rsqrt(var + LN_EPS) * g.astype(jnp.float32) + b.astype(jnp.float32)
    return y.astype(x.dtype)


def swiglu(x, w_gate, w_up, w_down):
    return (jax.nn.silu(x @ w_gate) * (x @ w_up)) @ w_down


def head_rmsnorm(o, g):
    of = o.astype(jnp.float32)
    of = of * lax.rsqrt(jnp.mean(of * of, axis=-1, keepdims=True) + RMS_EPS)
    b, s, h, d = o.shape
    return (of.reshape(b, s, h * d) * g.astype(jnp.float32)).astype(o.dtype)


def stick_breaking_attention(q, k, v):
    seq = q.shape[2]
    scale = 1.0 / math.sqrt(q.shape[-1])
    outs = []
    for i in range(seq // SB_QBLOCK):
        q0 = i * SB_QBLOCK
        q1 = q0 + SB_QBLOCK
        qi = q[:, :, q0:q1]
        kp = k[:, :, :q1]
        vp = v[:, :, :q1]
        z = jnp.einsum('bhqd,bhkd->bhqk', qi, kp).astype(jnp.float32) * scale
        t_pos = q0 + jnp.arange(SB_QBLOCK)[:, None]
        s_pos = jnp.arange(q1)[None, :]
        causal = s_pos < t_pos
        log_keep = jnp.where(causal, jax.nn.log_sigmoid(-z), 0.0)
        log_surv = lax.cumsum(log_keep, axis=3, reverse=True) - log_keep
        w = jnp.where(causal, jnp.exp(jax.nn.log_sigmoid(z) + log_surv), 0.0)
        outs.append(jnp.einsum('bhqk,bhkd->bhqd', w.astype(v.dtype), vp))
    return jnp.concatenate(outs, axis=2)


def chunked_rel_attention(q, k, v, rel_bias):
    b, seq, h, d = q.shape
    n_chunks = seq // CHUNK
    pad = N_PREV_CHUNKS * CHUNK
    scale = 1.0 / math.sqrt(d)
    k_pad = jnp.pad(k, ((0, 0), (pad, 0), (0, 0), (0, 0)))
    v_pad = jnp.pad(v, ((0, 0), (pad, 0), (0, 0), (0, 0)))
    qi = jnp.arange(CHUNK)[:, None]
    kj = jnp.arange(BAND)[None, :]
    rel = qi + pad - kj
    bias = rel_bias[:, jnp.clip(rel, -MAX_REL, MAX_REL) + MAX_REL].astype(jnp.float32)
    neg = jnp.finfo(jnp.float32).min

    def one_chunk(c):
        start = c * CHUNK
        qc = lax.dynamic_slice_in_dim(q, start, CHUNK, axis=1)
        kc = lax.dynamic_slice_in_dim(k_pad, start, BAND, axis=1)
        vc = lax.dynamic_slice_in_dim(v_pad, start, BAND, axis=1)
        sc = jnp.einsum('bqhd,bkhd->bhqk', qc, kc).astype(jnp.float32) * scale + bias
        valid = (start - pad + jnp.arange(BAND)) >= 0
        sc = jnp.where(valid[None, None, None, :], sc, neg)
        p = jax.nn.softmax(sc, axis=-1).astype(v.dtype)
        return jnp.einsum('bhqk,bkhd->bqhd', p, vc)

    out = lax.map(one_chunk, jnp.arange(n_chunks))
    return jnp.moveaxis(out, 0, 1).reshape(b, seq, h, d)


def _fwd_setup_inputs(seed: int = 0) -> dict:
    key = jax.random.key(seed)
    ks = jax.random.split(key, 32)
    f32 = jnp.float32
    L, D, F = DEPTH, D_MODEL, D_FF
    std_d = D ** -0.5
    std_f = F ** -0.5

    def nrm(k, shape, std):
        return jax.random.normal(k, shape, f32) * std

    def gain(k, shape):
        return 1.0 + 0.02 * jax.random.normal(k, shape, f32)

    x = jax.random.normal(ks[0], (BATCH, SEQ, D), f32)
    w_in = jnp.concatenate([
        nrm(ks[1], (L, D, D_SB), std_d),
        nrm(ks[2], (L, D, D_SB), std_d),
        nrm(ks[3], (L, D, D_SB), std_d * BETA_INIT),
        nrm(ks[4], (L, D, D_CA), std_d),
        nrm(ks[5], (L, D, D_CA), std_d),
        nrm(ks[6], (L, D, D_CA), std_d * BETA_INIT),
    ], axis=-1)
    return {
        "x": x,
        "ffn1_w_gate": nrm(ks[7], (L, D, F), std_d),
        "ffn1_w_up": nrm(ks[8], (L, D, F), std_d),
        "ffn1_w_down": nrm(ks[9], (L, F, D), std_f * BETA_INIT),
        "ln1_g": gain(ks[10], (L, D)),
        "ln1_b": 0.02 * jax.random.normal(ks[11], (L, D), f32),
        "w_in": w_in,
        "rel_bias": 0.1 * jax.random.normal(ks[12], (L, H_CA, 2 * MAX_REL + 1), f32),
        "sb_out_g": gain(ks[13], (L, D_SB)),
        "ca_out_g": gain(ks[14], (L, D_CA)),
        "w_out": nrm(ks[15], (L, MIX_WIDTH, D), MIX_WIDTH ** -0.5 * BETA_INIT),
        "ln2_g": gain(ks[16], (L, D)),
        "ln2_b": 0.02 * jax.random.normal(ks[17], (L, D), f32),
        "ffn2_w_gate": nrm(ks[18], (L, D, F), std_d),
        "ffn2_w_up": nrm(ks[19], (L, D, F), std_d),
        "ffn2_w_down": nrm(ks[20], (L, F, D), std_f * BETA_INIT),
        "ln3_g": gain(ks[21], (L, D)),
        "ln3_b": 0.02 * jax.random.normal(ks[22], (L, D), f32),
    }


def _fwd_reference(x, ffn1_w_gate, ffn1_w_up, ffn1_w_down, ln1_g, ln1_b, w_in, rel_bias,
              sb_out_g, ca_out_g, w_out, ln2_g, ln2_b, ffn2_w_gate, ffn2_w_up,
              ffn2_w_down, ln3_g, ln3_b):
    b, seq, _ = x.shape
    cuts = [D_SB, 2 * D_SB, 3 * D_SB, 3 * D_SB + D_CA, 3 * D_SB + 2 * D_CA]
    for l in range(DEPTH):
        x = layer_norm(ALPHA * x + FFN_RESIDUAL * swiglu(x, ffn1_w_gate[l], ffn1_w_up[l], ffn1_w_down[l]),
                       ln1_g[l], ln1_b[l])
        h = x @ w_in[l]
        qa, ka, va, qb, kb, vb = jnp.split(h, cuts, axis=-1)
        to_bhsd = lambda t, nh: t.reshape(b, seq, nh, HEAD_DIM).transpose(0, 2, 1, 3)
        oa = stick_breaking_attention(to_bhsd(qa, H_SB), to_bhsd(ka, H_SB), to_bhsd(va, H_SB))
        oa = oa.transpose(0, 2, 1, 3)
        to_bshd = lambda t, nh: t.reshape(b, seq, nh, HEAD_DIM)
        ob = chunked_rel_attention(to_bshd(qb, H_CA), to_bshd(kb, H_CA), to_bshd(vb, H_CA), rel_bias[l])
        mixed = jnp.concatenate([head_rmsnorm(oa, sb_out_g[l]), head_rmsnorm(ob, ca_out_g[l])], axis=-1)
        x = layer_norm(ALPHA * x + mixed @ w_out[l], ln2_g[l], ln2_b[l])
        x = layer_norm(ALPHA * x + FFN_RESIDUAL * swiglu(x, ffn2_w_gate[l], ffn2_w_up[l], ffn2_w_down[l]),
                       ln3_g[l], ln3_b[l])
    return x


import jax as _jax
import jax.numpy as _jnp

TWIN_FORMAT = 'train_step'
FWD_PARAMS = ['x', 'ffn1_w_gate', 'ffn1_w_up', 'ffn1_w_down', 'ln1_g', 'ln1_b', 'w_in', 'rel_bias', 'sb_out_g', 'ca_out_g', 'w_out', 'ln2_g', 'ln2_b', 'ffn2_w_gate', 'ffn2_w_up', 'ffn2_w_down', 'ln3_g', 'ln3_b']
TWIN_WEIGHTS = ['ffn1_w_gate', 'ffn1_w_up', 'ffn1_w_down', 'ln1_g', 'ln1_b', 'w_in', 'rel_bias', 'sb_out_g', 'ca_out_g', 'w_out', 'ln2_g', 'ln2_b', 'ffn2_w_gate', 'ffn2_w_up', 'ffn2_w_down', 'ln3_g', 'ln3_b']
TWIN_DIFF_INPUT = 'x'
TWIN_INPUTS = ['x', 'ffn1_w_gate', 'ffn1_w_up', 'ffn1_w_down', 'ln1_g', 'ln1_b', 'w_in', 'rel_bias', 'sb_out_g', 'ca_out_g', 'w_out', 'ln2_g', 'ln2_b', 'ffn2_w_gate', 'ffn2_w_up', 'ffn2_w_down', 'ln3_g', 'ln3_b', 'loss_target', 'm_ffn1_w_gate', 'm_ffn1_w_up', 'm_ffn1_w_down', 'm_ln1_g', 'm_ln1_b', 'm_w_in', 'm_rel_bias', 'm_sb_out_g', 'm_ca_out_g', 'm_w_out', 'm_ln2_g', 'm_ln2_b', 'm_ffn2_w_gate', 'm_ffn2_w_up', 'm_ffn2_w_down', 'm_ln3_g', 'm_ln3_b', 'v_ffn1_w_gate', 'v_ffn1_w_up', 'v_ffn1_w_down', 'v_ln1_g', 'v_ln1_b', 'v_w_in', 'v_rel_bias', 'v_sb_out_g', 'v_ca_out_g', 'v_w_out', 'v_ln2_g', 'v_ln2_b', 'v_ffn2_w_gate', 'v_ffn2_w_up', 'v_ffn2_w_down', 'v_ln3_g', 'v_ln3_b']
TWIN_OUTPUTS = ['loss', 'grad_x', 'grad_ffn1_w_gate', 'grad_ffn1_w_up', 'grad_ffn1_w_down', 'grad_ln1_g', 'grad_ln1_b', 'grad_w_in', 'grad_rel_bias', 'grad_sb_out_g', 'grad_ca_out_g', 'grad_w_out', 'grad_ln2_g', 'grad_ln2_b', 'grad_ffn2_w_gate', 'grad_ffn2_w_up', 'grad_ffn2_w_down', 'grad_ln3_g', 'grad_ln3_b', 'delta_ffn1_w_gate', 'delta_ffn1_w_up', 'delta_ffn1_w_down', 'delta_ln1_g', 'delta_ln1_b', 'delta_w_in', 'delta_rel_bias', 'delta_sb_out_g', 'delta_ca_out_g', 'delta_w_out', 'delta_ln2_g', 'delta_ln2_b', 'delta_ffn2_w_gate', 'delta_ffn2_w_up', 'delta_ffn2_w_down', 'delta_ln3_g', 'delta_ln3_b', 'new_m_ffn1_w_gate', 'new_m_ffn1_w_up', 'new_m_ffn1_w_down', 'new_m_ln1_g', 'new_m_ln1_b', 'new_m_w_in', 'new_m_rel_bias', 'new_m_sb_out_g', 'new_m_ca_out_g', 'new_m_w_out', 'new_m_ln2_g', 'new_m_ln2_b', 'new_m_ffn2_w_gate', 'new_m_ffn2_w_up', 'new_m_ffn2_w_down', 'new_m_ln3_g', 'new_m_ln3_b', 'new_v_ffn1_w_gate', 'new_v_ffn1_w_up', 'new_v_ffn1_w_down', 'new_v_ln1_g', 'new_v_ln1_b', 'new_v_w_in', 'new_v_rel_bias', 'new_v_sb_out_g', 'new_v_ca_out_g', 'new_v_w_out', 'new_v_ln2_g', 'new_v_ln2_b', 'new_v_ffn2_w_gate', 'new_v_ffn2_w_up', 'new_v_ffn2_w_down', 'new_v_ln3_g', 'new_v_ln3_b']
TWIN_LEAF_KINDS = {'loss': 'loss', 'grad_x': 'grad_x', 'grad_ffn1_w_gate': 'grad_w', 'grad_ffn1_w_up': 'grad_w', 'grad_ffn1_w_down': 'grad_w', 'grad_ln1_g': 'grad_w', 'grad_ln1_b': 'grad_w', 'grad_w_in': 'grad_w', 'grad_rel_bias': 'grad_w', 'grad_sb_out_g': 'grad_w', 'grad_ca_out_g': 'grad_w', 'grad_w_out': 'grad_w', 'grad_ln2_g': 'grad_w', 'grad_ln2_b': 'grad_w', 'grad_ffn2_w_gate': 'grad_w', 'grad_ffn2_w_up': 'grad_w', 'grad_ffn2_w_down': 'grad_w', 'grad_ln3_g': 'grad_w', 'grad_ln3_b': 'grad_w', 'delta_ffn1_w_gate': 'delta_w', 'delta_ffn1_w_up': 'delta_w', 'delta_ffn1_w_down': 'delta_w', 'delta_ln1_g': 'delta_w', 'delta_ln1_b': 'delta_w', 'delta_w_in': 'delta_w', 'delta_rel_bias': 'delta_w', 'delta_sb_out_g': 'delta_w', 'delta_ca_out_g': 'delta_w', 'delta_w_out': 'delta_w', 'delta_ln2_g': 'delta_w', 'delta_ln2_b': 'delta_w', 'delta_ffn2_w_gate': 'delta_w', 'delta_ffn2_w_up': 'delta_w', 'delta_ffn2_w_down': 'delta_w', 'delta_ln3_g': 'delta_w', 'delta_ln3_b': 'delta_w', 'new_m_ffn1_w_gate': 'new_m', 'new_m_ffn1_w_up': 'new_m', 'new_m_ffn1_w_down': 'new_m', 'new_m_ln1_g': 'new_m', 'new_m_ln1_b': 'new_m', 'new_m_w_in': 'new_m', 'new_m_rel_bias': 'new_m', 'new_m_sb_out_g': 'new_m', 'new_m_ca_out_g': 'new_m', 'new_m_w_out': 'new_m', 'new_m_ln2_g': 'new_m', 'new_m_ln2_b': 'new_m', 'new_m_ffn2_w_gate': 'new_m', 'new_m_ffn2_w_up': 'new_m', 'new_m_ffn2_w_down': 'new_m', 'new_m_ln3_g': 'new_m', 'new_m_ln3_b': 'new_m', 'new_v_ffn1_w_gate': 'new_v', 'new_v_ffn1_w_up': 'new_v', 'new_v_ffn1_w_down': 'new_v', 'new_v_ln1_g': 'new_v', 'new_v_ln1_b': 'new_v', 'new_v_w_in': 'new_v', 'new_v_rel_bias': 'new_v', 'new_v_sb_out_g': 'new_v', 'new_v_ca_out_g': 'new_v', 'new_v_w_out': 'new_v', 'new_v_ln2_g': 'new_v', 'new_v_ln2_b': 'new_v', 'new_v_ffn2_w_gate': 'new_v', 'new_v_ffn2_w_up': 'new_v', 'new_v_ffn2_w_down': 'new_v', 'new_v_ln3_g': 'new_v', 'new_v_ln3_b': 'new_v'}


def _forward(args):
    return _fwd_reference(*[args[k] for k in FWD_PARAMS])


def _output_shape():
    out = _jax.eval_shape(lambda: _forward(_fwd_setup_inputs(0)))
    return out.shape, out.dtype

N_MICROBATCH = 1
ADAM_LR = 0.001
ADAM_B1 = 0.9
ADAM_B2 = 0.999
ADAM_EPS = 1e-08
ADAM_WD = 0.01
ADAM_STEP = 10
PER_EXAMPLE_BATCH_AXIS = {'x': 0, 'loss_target': 0}
SHARED_INPUTS = []
_WEIGHT_DTYPES = {'ffn1_w_gate': _jnp.float32, 'ffn1_w_up': _jnp.float32, 'ffn1_w_down': _jnp.float32, 'ln1_g': _jnp.float32, 'ln1_b': _jnp.float32, 'w_in': _jnp.float32, 'rel_bias': _jnp.float32, 'sb_out_g': _jnp.float32, 'ca_out_g': _jnp.float32, 'w_out': _jnp.float32, 'ln2_g': _jnp.float32, 'ln2_b': _jnp.float32, 'ffn2_w_gate': _jnp.float32, 'ffn2_w_up': _jnp.float32, 'ffn2_w_down': _jnp.float32, 'ln3_g': _jnp.float32, 'ln3_b': _jnp.float32}
MOMENT_SCALE = {'ffn1_w_gate': 1.205877e-02, 'ffn1_w_up': 1.171913e-02, 'ffn1_w_down': 4.612412e-02, 'ln1_g': 1.612675e+00, 'ln1_b': 3.750519e+00, 'w_in': 1.538287e-01, 'rel_bias': 2.106452e-02, 'sb_out_g': 6.426903e-02, 'ca_out_g': 1.427747e-01, 'w_out': 2.531781e-01, 'ln2_g': 1.754704e+00, 'ln2_b': 1.467023e+00, 'ffn2_w_gate': 1.176509e-02, 'ffn2_w_up': 1.141703e-02, 'ffn2_w_down': 4.502072e-02, 'ln3_g': 3.220863e+01, 'ln3_b': 5.933349e+00}


def _to_microbatches(a, axis):
    t = _jnp.moveaxis(a, axis, 0)
    t = t.reshape((N_MICROBATCH, t.shape[0] // N_MICROBATCH) + t.shape[1:])
    return _jnp.moveaxis(t, 1, axis + 1)


def setup_inputs(seed: int = 0) -> dict:
    inp = _fwd_setup_inputs(seed)
    key = _jax.random.fold_in(_jax.random.key(seed), 7919)
    shape, _ = _output_shape()
    out = dict(inp)
    out["loss_target"] = _jax.random.normal(_jax.random.fold_in(key, 0), shape, _jnp.float32)
    for i, name in enumerate(TWIN_WEIGHTS):
        w = inp[name].astype(_jnp.float32)
        if MOMENT_SCALE is None:
            s = _jnp.sqrt(_jnp.mean(_jnp.square(w)) + 1e-30)
        else:
            s = MOMENT_SCALE[name]
        km, kv = _jax.random.split(_jax.random.fold_in(key, i + 1))
        out[name] = w
        out["m_" + name] = s * _jax.random.normal(km, w.shape, _jnp.float32)
        out["v_" + name] = (s * s) * _jax.random.uniform(kv, w.shape, _jnp.float32, 0.5, 1.5)
    if N_MICROBATCH > 1:
        for name, axis in PER_EXAMPLE_BATCH_AXIS.items():
            out[name] = _to_microbatches(out[name], axis)
    return {'x': out['x'], 'ffn1_w_gate': out['ffn1_w_gate'], 'ffn1_w_up': out['ffn1_w_up'], 'ffn1_w_down': out['ffn1_w_down'], 'ln1_g': out['ln1_g'], 'ln1_b': out['ln1_b'], 'w_in': out['w_in'], 'rel_bias': out['rel_bias'], 'sb_out_g': out['sb_out_g'], 'ca_out_g': out['ca_out_g'], 'w_out': out['w_out'], 'ln2_g': out['ln2_g'], 'ln2_b': out['ln2_b'], 'ffn2_w_gate': out['ffn2_w_gate'], 'ffn2_w_up': out['ffn2_w_up'], 'ffn2_w_down': out['ffn2_w_down'], 'ln3_g': out['ln3_g'], 'ln3_b': out['ln3_b'], 'loss_target': out['loss_target'], 'm_ffn1_w_gate': out['m_ffn1_w_gate'], 'm_ffn1_w_up': out['m_ffn1_w_up'], 'm_ffn1_w_down': out['m_ffn1_w_down'], 'm_ln1_g': out['m_ln1_g'], 'm_ln1_b': out['m_ln1_b'], 'm_w_in': out['m_w_in'], 'm_rel_bias': out['m_rel_bias'], 'm_sb_out_g': out['m_sb_out_g'], 'm_ca_out_g': out['m_ca_out_g'], 'm_w_out': out['m_w_out'], 'm_ln2_g': out['m_ln2_g'], 'm_ln2_b': out['m_ln2_b'], 'm_ffn2_w_gate': out['m_ffn2_w_gate'], 'm_ffn2_w_up': out['m_ffn2_w_up'], 'm_ffn2_w_down': out['m_ffn2_w_down'], 'm_ln3_g': out['m_ln3_g'], 'm_ln3_b': out['m_ln3_b'], 'v_ffn1_w_gate': out['v_ffn1_w_gate'], 'v_ffn1_w_up': out['v_ffn1_w_up'], 'v_ffn1_w_down': out['v_ffn1_w_down'], 'v_ln1_g': out['v_ln1_g'], 'v_ln1_b': out['v_ln1_b'], 'v_w_in': out['v_w_in'], 'v_rel_bias': out['v_rel_bias'], 'v_sb_out_g': out['v_sb_out_g'], 'v_ca_out_g': out['v_ca_out_g'], 'v_w_out': out['v_w_out'], 'v_ln2_g': out['v_ln2_g'], 'v_ln2_b': out['v_ln2_b'], 'v_ffn2_w_gate': out['v_ffn2_w_gate'], 'v_ffn2_w_up': out['v_ffn2_w_up'], 'v_ffn2_w_down': out['v_ffn2_w_down'], 'v_ln3_g': out['v_ln3_g'], 'v_ln3_b': out['v_ln3_b']}


def _loss(weights, diff, rest, loss_target):
    with _jax.named_scope("forward"):
        args = {**rest, TWIN_DIFF_INPUT: diff, **{k: w.astype(_WEIGHT_DTYPES[k]) for k, w in weights.items()}}
        y = _forward(args)
    with _jax.named_scope("loss_head"):
        err = _jnp.square(y.astype(_jnp.float32) - loss_target)
        return 0.5 * _jnp.sum(_jnp.mean(err, axis=-1)) if err.ndim else 0.5 * err


def _adamw(w, g, m, v):
    m = ADAM_B1 * m + (1.0 - ADAM_B1) * g
    v = ADAM_B2 * v + (1.0 - ADAM_B2) * _jnp.square(g)
    m_hat = m / (1.0 - ADAM_B1 ** ADAM_STEP)
    v_hat = v / (1.0 - ADAM_B2 ** ADAM_STEP)
    delta = -ADAM_LR * (m_hat / (_jnp.sqrt(v_hat) + ADAM_EPS) + ADAM_WD * w)
    return delta, m, v


def reference(x, ffn1_w_gate, ffn1_w_up, ffn1_w_down, ln1_g, ln1_b, w_in, rel_bias, sb_out_g, ca_out_g, w_out, ln2_g, ln2_b, ffn2_w_gate, ffn2_w_up, ffn2_w_down, ln3_g, ln3_b, loss_target, m_ffn1_w_gate, m_ffn1_w_up, m_ffn1_w_down, m_ln1_g, m_ln1_b, m_w_in, m_rel_bias, m_sb_out_g, m_ca_out_g, m_w_out, m_ln2_g, m_ln2_b, m_ffn2_w_gate, m_ffn2_w_up, m_ffn2_w_down, m_ln3_g, m_ln3_b, v_ffn1_w_gate, v_ffn1_w_up, v_ffn1_w_down, v_ln1_g, v_ln1_b, v_w_in, v_rel_bias, v_sb_out_g, v_ca_out_g, v_w_out, v_ln2_g, v_ln2_b, v_ffn2_w_gate, v_ffn2_w_up, v_ffn2_w_down, v_ln3_g, v_ln3_b):
    given = dict(x=x, ffn1_w_gate=ffn1_w_gate, ffn1_w_up=ffn1_w_up, ffn1_w_down=ffn1_w_down, ln1_g=ln1_g, ln1_b=ln1_b, w_in=w_in, rel_bias=rel_bias, sb_out_g=sb_out_g, ca_out_g=ca_out_g, w_out=w_out, ln2_g=ln2_g, ln2_b=ln2_b, ffn2_w_gate=ffn2_w_gate, ffn2_w_up=ffn2_w_up, ffn2_w_down=ffn2_w_down, ln3_g=ln3_g, ln3_b=ln3_b, loss_target=loss_target, m_ffn1_w_gate=m_ffn1_w_gate, m_ffn1_w_up=m_ffn1_w_up, m_ffn1_w_down=m_ffn1_w_down, m_ln1_g=m_ln1_g, m_ln1_b=m_ln1_b, m_w_in=m_w_in, m_rel_bias=m_rel_bias, m_sb_out_g=m_sb_out_g, m_ca_out_g=m_ca_out_g, m_w_out=m_w_out, m_ln2_g=m_ln2_g, m_ln2_b=m_ln2_b, m_ffn2_w_gate=m_ffn2_w_gate, m_ffn2_w_up=m_ffn2_w_up, m_ffn2_w_down=m_ffn2_w_down, m_ln3_g=m_ln3_g, m_ln3_b=m_ln3_b, v_ffn1_w_gate=v_ffn1_w_gate, v_ffn1_w_up=v_ffn1_w_up, v_ffn1_w_down=v_ffn1_w_down, v_ln1_g=v_ln1_g, v_ln1_b=v_ln1_b, v_w_in=v_w_in, v_rel_bias=v_rel_bias, v_sb_out_g=v_sb_out_g, v_ca_out_g=v_ca_out_g, v_w_out=v_w_out, v_ln2_g=v_ln2_g, v_ln2_b=v_ln2_b, v_ffn2_w_gate=v_ffn2_w_gate, v_ffn2_w_up=v_ffn2_w_up, v_ffn2_w_down=v_ffn2_w_down, v_ln3_g=v_ln3_g, v_ln3_b=v_ln3_b)
    weights = {n: given[n] for n in TWIN_WEIGHTS}
    shared = {n: given[n] for n in SHARED_INPUTS}
    per_example = {n: given[n] for n in ['x']}
    grad_fn = _jax.value_and_grad(_loss, argnums=(0, 1))

    def one_microbatch(ex, loss_target):
        ex = dict(ex)
        diff = ex.pop(TWIN_DIFF_INPUT)
        return grad_fn(weights, diff, {**shared, **ex}, loss_target)

    if N_MICROBATCH == 1:
        loss, (grad_w, grad_x) = one_microbatch(per_example, given["loss_target"])
    else:
        def body(carry, xs):
            loss_sum, grad_sum = carry
            l_k, (gw_k, gx_k) = one_microbatch(xs[0], xs[1])
            with _jax.named_scope("update"):
                return (loss_sum + l_k, _jax.tree.map(_jnp.add, grad_sum, gw_k)), gx_k

        init = (_jnp.zeros((), _jnp.float32), _jax.tree.map(_jnp.zeros_like, weights))
        (loss, grad_w), grad_x = _jax.lax.scan(body, init, (per_example, given["loss_target"]))
    with _jax.named_scope("update"):
        delta_w, new_m, new_v = {}, {}, {}
        for n in TWIN_WEIGHTS:
            delta_w[n], new_m[n], new_v[n] = _adamw(weights[n], grad_w[n], given["m_" + n], given["v_" + n])
    return (loss, grad_x, *[grad_w[n] for n in TWIN_WEIGHTS], *[delta_w[n] for n in TWIN_WEIGHTS],
            *[new_m[n] for n in TWIN_WEIGHTS], *[new_v[n] for n in TWIN_WEIGHTS])
```

```python
import functools
import math

import jax
import jax.numpy as jnp
from jax import lax
from jax.experimental import pallas as pl
from jax.experimental.pallas import tpu as pltpu

F32 = jnp.float32
BF = jnp.bfloat16

CHUNK = 64
N_PREV_CHUNKS = 8
BAND = CHUNK * (N_PREV_CHUNKS + 1)
MAX_REL = 128
HEAD_DIM = 64
FFN_RESIDUAL = 0.5
LN_EPS = 1e-5
RMS_EPS = 1e-6
ADAM_LR = 0.001
ADAM_B1 = 0.9
ADAM_B2 = 0.999
ADAM_EPS = 1e-08
ADAM_WD = 0.01
ADAM_STEP = 10

LANES = 128
ATT_TILE = 4 * CHUNK
NEG_BIAS = -1e30
VMEM_LIMIT = 56 * 1024 * 1024
ROW_TILE = 256
MESH_IDS = pl.DeviceIdType.MESH

_NT = (((1,), (1,)), ((), ()))
_TN = (((0,), (0,)), ((), ()))


def _dot(a, b):
    return jnp.dot(a, b, preferred_element_type=F32)


def _dot_nt(a, b):
    return lax.dot_general(a, b, _NT, preferred_element_type=F32)


def _dot_tn(a, b):
    return lax.dot_general(a, b, _TN, preferred_element_type=F32)


def _tile(n, pref):
    t = min(n, pref)
    while n % t:
        t //= 2
    return t


def _cparams(sem):
    return pltpu.CompilerParams(dimension_semantics=sem, vmem_limit_bytes=VMEM_LIMIT)


def _sds(shape, dtype):
    return jax.ShapeDtypeStruct(tuple(shape), dtype)


def _rows8(v):
    tm, c = v.shape
    return v.reshape(tm // 8, 8, c).sum(axis=0)


def _rowcall(name, body, n_rows, tm, row_ins, res_ins, row_outs, acc_outs=()):
    def rspec(shape):
        if len(shape) == 2:
            return pl.BlockSpec((tm, shape[1]), lambda m: (m, 0))
        return pl.BlockSpec((shape[0], tm, shape[2]), lambda m: (0, m, 0))

    def cspec(shape):
        zeros = (0,) * len(shape)
        return pl.BlockSpec(tuple(shape), lambda m: zeros)

    def split(items, mk):
        arrs, specs = [], []
        for it in items:
            if isinstance(it, tuple):
                arrs.append(it[0])
                specs.append(it[1])
            else:
                arrs.append(it)
                specs.append(mk(it.shape))
        return arrs, specs

    ra, rs = split(row_ins, rspec)
    ca, cs = split(res_ins, cspec)
    out_specs = [rspec(s.shape) for s in row_outs] + [cspec(s.shape) for s in acc_outs]
    sem = ("arbitrary",) if acc_outs else ("parallel",)
    return pl.pallas_call(
        body, name=name, grid=(n_rows // tm,), in_specs=rs + cs, out_specs=out_specs,
        out_shape=list(row_outs) + list(acc_outs), compiler_params=_cparams(sem),
    )(*ra, *ca)


def _ffn_up(x, wg, wu):
    t, _ = x.shape
    s_n, _, fs = wg.shape
    tm = _tile(t, ROW_TILE)

    def body(x_ref, wg_ref, wu_ref, g_ref, u_ref, h_ref):
        xb = x_ref[...].astype(BF)
        for s in range(s_n):
            g = _dot(xb, wg_ref[s])
            u = _dot(xb, wu_ref[s])
            g_ref[s] = g.astype(BF)
            u_ref[s] = u.astype(BF)
            h_ref[s] = (g * jax.nn.sigmoid(g) * u).astype(BF)

    out = _sds((s_n, t, fs), BF)
    return _rowcall("ffn_up", body, t, tm, [x], [wg, wu], [out, out, out])


def _proj_ln(name, a_list, b_list, xres, gain, bias, scale, alpha):
    t, d = xres.shape
    tm = _tile(t, ROW_TILE)
    na = len(a_list)
    a_nd = [a.ndim for a in a_list]
    a_s = [a.shape[0] for a in a_list]

    def body(*refs):
        a_refs = refs[:na]
        x_ref = refs[na]
        b_refs = refs[na + 1:2 * na + 1]
        g_ref, bb_ref = refs[2 * na + 1:2 * na + 3]
        y_ref, xh_ref, rs_ref = refs[2 * na + 3:]
        acc = None
        for i in range(na):
            if a_nd[i] == 3:
                parts = [_dot(a_refs[i][s].astype(BF), b_refs[i][s]) for s in range(a_s[i])]
            else:
                parts = [_dot(a_refs[i][...].astype(BF), b_refs[i][...])]
            for p in parts:
                acc = p if acc is None else acc + p
        if scale != 1.0:
            acc = scale * acc
        r = alpha * x_ref[...] + acc
        mu = jnp.mean(r, axis=-1, keepdims=True)
        xc = r - mu
        var = jnp.mean(xc * xc, axis=-1, keepdims=True)
        rstd = lax.rsqrt(var + LN_EPS)
        xh = xc * rstd
        y_ref[...] = xh * g_ref[...] + bb_ref[...]
        xh_ref[...] = xh
        rs_ref[...] = rstd

    outs = [_sds((t, d), F32), _sds((t, d), F32), _sds((t, 1), F32)]
    return _rowcall(name, body, t, tm, list(a_list) + [xres], list(b_list) + [gain, bias], outs)


def _proj_nn(name, x, w, out_dtype):
    t, _ = x.shape
    n = w.shape[1]
    tm = _tile(t, ROW_TILE)

    def body(x_ref, w_ref, o_ref):
        o_ref[...] = _dot(x_ref[...].astype(BF), w_ref[...]).astype(out_dtype)

    return _rowcall(name, body, t, tm, [x], [w], [_sds((t, n), out_dtype)])[0]


def _proj_nt(name, a_list, b_list, n_out, resid, alpha, out_dtype):
    first = a_list[0][0] if isinstance(a_list[0], tuple) else a_list[0]
    t = first.shape[-2]
    tm = _tile(t, ROW_TILE)
    na = len(a_list)
    a_nd = [(a[0] if isinstance(a, tuple) else a).ndim for a in a_list]
    a_s = [(a[0] if isinstance(a, tuple) else a).shape[0] for a in a_list]
    has_res = resid is not None

    def body(*refs):
        a_refs = refs[:na]
        pos = na
        if has_res:
            r_ref = refs[pos]
            pos += 1
        b_refs = refs[pos:pos + na]
        o_ref = refs[pos + na]
        acc = None
        for i in range(na):
            if a_nd[i] == 3:
                parts = [_dot_nt(a_refs[i][s].astype(BF), b_refs[i][s]) for s in range(a_s[i])]
            else:
                parts = [_dot_nt(a_refs[i][...].astype(BF), b_refs[i][...])]
            for p in parts:
                acc = p if acc is None else acc + p
        if has_res:
            acc = acc + alpha * r_ref[...]
        o_ref[...] = acc.astype(out_dtype)

    rows = list(a_list) + ([resid] if has_res else [])
    return _rowcall(name, body, t, tm, rows, list(b_list), [_sds((t, n_out), out_dtype)])[0]


def _ln_bwd(dy, xh, rstd, gain, scale):
    t, d = dy.shape
    tm = _tile(t, ROW_TILE)

    def body(dy_ref, xh_ref, rs_ref, g_ref, dr_ref, drs_ref, dg_ref, db_ref):
        m = pl.program_id(0)
        dyv = dy_ref[...]
        xhv = xh_ref[...]
        dxh = dyv * g_ref[...]
        m1 = jnp.mean(dxh, axis=-1, keepdims=True)
        m2 = jnp.mean(dxh * xhv, axis=-1, keepdims=True)
        dr = rs_ref[...] * (dxh - m1 - xhv * m2)
        dr_ref[...] = dr
        drs_ref[...] = (scale * dr).astype(BF)
        pg = _rows8(dyv * xhv)
        pb = _rows8(dyv)

        @pl.when(m == 0)
        def _():
            dg_ref[...] = pg
            db_ref[...] = pb

        @pl.when(m > 0)
        def _():
            dg_ref[...] += pg
            db_ref[...] += pb

    outs = [_sds((t, d), F32), _sds((t, d), BF)]
    accs = [_sds((8, d), F32), _sds((8, d), F32)]
    return _rowcall("ln_bwd", body, t, tm, [dy, xh, rstd], [gain], outs, accs)


def _swiglu_bwd(drs, wd, g, u):
    s_n, t, fs = g.shape
    tm = _tile(t, ROW_TILE)

    def body(d_ref, g_ref, u_ref, wd_ref, dg_ref, du_ref):
        dv = d_ref[...]
        for s in range(s_n):
            dh = _dot_nt(dv, wd_ref[s])
            gv = g_ref[s].astype(F32)
            uv = u_ref[s].astype(F32)
            sg = jax.nn.sigmoid(gv)
            dg_ref[s] = (dh * uv * (sg * (1.0 + gv * (1.0 - sg)))).astype(BF)
            du_ref[s] = (dh * (gv * sg)).astype(BF)

    out = _sds((s_n, t, fs), BF)
    return _rowcall("swiglu_bwd", body, t, tm, [drs, g, u], [wd], [out, out])


def _rms_bwd(dmix, half, o, gain):
    t, dh = o.shape
    tm = _tile(t, ROW_TILE)

    def body(dm_ref, o_ref, g_ref, do_ref, dg_ref):
        m = pl.program_id(0)
        lo = lax.broadcasted_iota(jnp.int32, (1, LANES), 1) < HEAD_DIM
        for gi in range(dh // LANES):
            sl = slice(gi * LANES, (gi + 1) * LANES)
            ov = o_ref[:, sl]
            dm = dm_ref[:, sl]
            ss = ov * ov
            s0 = jnp.sum(jnp.where(lo, ss, 0.0), axis=1, keepdims=True)
            s1 = jnp.sum(jnp.where(lo, 0.0, ss), axis=1, keepdims=True)
            r = jnp.where(lo, lax.rsqrt(s0 / HEAD_DIM + RMS_EPS), lax.rsqrt(s1 / HEAD_DIM + RMS_EPS))
            n = ov * r
            dn = dm * g_ref[:, sl]
            tt = dn * n
            t0 = jnp.sum(jnp.where(lo, tt, 0.0), axis=1, keepdims=True) / HEAD_DIM
            t1 = jnp.sum(jnp.where(lo, 0.0, tt), axis=1, keepdims=True) / HEAD_DIM
            do_ref[:, sl] = (r * (dn - n * jnp.where(lo, t0, t1))).astype(BF)
            part = _rows8(dm * n)

            @pl.when(m == 0)
            def _():
                dg_ref[:, sl] = part

            @pl.when(m > 0)
            def _():
                dg_ref[:, sl] += part

    dm_spec = pl.BlockSpec((tm, dh), lambda m: (m, half))
    return _rowcall("rms_bwd", body, t, tm, [(dmix, dm_spec), o], [gain], [_sds((t, dh), BF)], [_sds((8, dh), F32)])


def _loss_grad(y, target):
    t, d = y.shape
    tm = _tile(t, ROW_TILE)

    def body(y_ref, t_ref, dy_ref, acc_ref):
        m = pl.program_id(0)
        diff = y_ref[...] - t_ref[...]
        dy_ref[...] = diff / d
        part = _rows8(diff * diff)

        @pl.when(m == 0)
        def _():
            acc_ref[...] = part

        @pl.when(m > 0)
        def _():
            acc_ref[...] += part

    return _rowcall("loss_grad", body, t, tm, [y, target], [], [_sds((t, d), F32)], [_sds((8, d), F32)])


def _add_n(name, xs, out_dtype):
    shape = xs[0].shape
    c = shape[-1]
    rows = math.prod(shape[:-1])
    tm = _tile(rows, 512)
    n = len(xs)

    def body(*refs):
        acc = refs[0][...].astype(F32)
        for r in refs[1:n]:
            acc = acc + r[...].astype(F32)
        refs[n][...] = acc.astype(out_dtype)

    out = _rowcall(name, body, rows, tm, [x.reshape(rows, c) for x in xs], [], [_sds((rows, c), out_dtype)])[0]
    return out.reshape(shape)


def _adamw(w, g, m, v):
    shape = w.shape
    c = shape[-1]
    rows = math.prod(shape[:-1])
    tm = _tile(rows, 512)

    def body(w_ref, g_ref, m_ref, v_ref, d_ref, nm_ref, nv_ref):
        gv = g_ref[...]
        nm = ADAM_B1 * m_ref[...] + (1.0 - ADAM_B1) * gv
        nv = ADAM_B2 * v_ref[...] + (1.0 - ADAM_B2) * (gv * gv)
        m_hat = nm / (1.0 - ADAM_B1 ** ADAM_STEP)
        v_hat = nv / (1.0 - ADAM_B2 ** ADAM_STEP)
        d_ref[...] = -ADAM_LR * (m_hat / (jnp.sqrt(v_hat) + ADAM_EPS) + ADAM_WD * w_ref[...])
        nm_ref[...] = nm
        nv_ref[...] = nv

    flat = [a.reshape(rows, c) for a in (w, g, m, v)]
    out = _sds((rows, c), F32)
    res = _rowcall("adamw", body, rows, tm, flat, [], [out, out, out])
    return tuple(r.reshape(shape) for r in res)


def _wgrad(name, a_list, b_list, out_sub, sub_shape, a_get, b_get):
    t = a_list[0].shape[-2]
    tk = _tile(t, 512)
    na, nb = len(a_list), len(b_list)
    nk = t // tk

    def tspec(shape):
        if len(shape) == 2:
            return pl.BlockSpec((tk, shape[1]), lambda k: (k, 0))
        return pl.BlockSpec((shape[0], tk, shape[2]), lambda k: (0, k, 0))

    def body(*refs):
        a_refs = refs[:na]
        b_refs = refs[na:na + nb]
        o_ref = refs[na + nb]
        acc_ref = refs[na + nb + 1]
        k = pl.program_id(0)

        @pl.when(k == 0)
        def _():
            acc_ref[...] = jnp.zeros_like(acc_ref)

        for s in range(out_sub):
            acc_ref[s] += _dot_tn(a_get(a_refs, s).astype(BF), b_get(b_refs, s).astype(BF))

        @pl.when(k == nk - 1)
        def _():
            o_ref[...] = acc_ref[...].astype(BF)

    full = (out_sub,) + tuple(sub_shape)
    return pl.pallas_call(
        body, name=name, grid=(nk,), in_specs=[tspec(a.shape) for a in a_list] + [tspec(b.shape) for b in b_list],
        out_specs=pl.BlockSpec(full, lambda k: (0, 0, 0)), out_shape=_sds(full, BF),
        scratch_shapes=[pltpu.VMEM(full, F32)], compiler_params=_cparams(("arbitrary",)),
    )(*a_list, *b_list)


def _head_masks(x):
    lo = lax.broadcasted_iota(jnp.int32, (1, LANES), 1) < HEAD_DIM
    zero = jnp.zeros_like(x)
    return lo, (jnp.where(lo, x, zero), jnp.where(lo, zero, x))


def _log_keep(z):
    return -(jnp.maximum(z, 0.0) + jnp.log(1.0 + jnp.exp(-jnp.abs(z))))


def _split_dot(x, tri):
    hi = x.astype(BF)
    lo = (x - hi.astype(F32)).astype(BF)
    return _dot(hi, tri) + _dot(lo, tri)


def _sb_specs(bl, seq, p_n, col0):
    tq = ATT_TILE
    nq = seq // tq
    q_spec = pl.BlockSpec((tq, LANES), lambda b, p, i: (b * nq + i, col0 + p))
    k_spec = pl.BlockSpec((seq, LANES), lambda b, p, i: (b, col0 + p_n + p))
    v_spec = pl.BlockSpec((seq, LANES), lambda b, p, i: (b, col0 + 2 * p_n + p))
    tile_spec = pl.BlockSpec((tq, LANES), lambda b, p, i: (b * nq + i, p))
    seq_spec = pl.BlockSpec((seq, LANES), lambda b, p, i: (b, p))
    g_spec = pl.BlockSpec((1, LANES), lambda b, p, i: (0, p))
    return nq, q_spec, k_spec, v_spec, tile_spec, seq_spec, g_spec


def _rms_out(acc, lo, gain):
    ss = acc * acc
    s0 = jnp.sum(jnp.where(lo, ss, 0.0), axis=1, keepdims=True)
    s1 = jnp.sum(jnp.where(lo, 0.0, ss), axis=1, keepdims=True)
    r = jnp.where(lo, lax.rsqrt(s0 / HEAD_DIM + RMS_EPS), lax.rsqrt(s1 / HEAD_DIM + RMS_EPS))
    return acc * r * gain


def _sb_fwd(qkv, bl, seq, dg, gain):
    t = bl * seq
    p_n = dg // LANES
    tq = ATT_TILE
    scale = 1.0 / math.sqrt(HEAD_DIM)
    nq, q_spec, k_spec, v_spec, tile_spec, _, g_spec = _sb_specs(bl, seq, p_n, 0)
    assert nq <= LANES

    def body(q_ref, k_ref, v_ref, g_ref, o_ref, m_ref, c_ref):
        i = pl.program_id(2)
        row = lax.broadcasted_iota(jnp.int32, (tq, tq), 0)
        col = lax.broadcasted_iota(jnp.int32, (tq, tq), 1)
        gt = row > col
        tri = jnp.where(gt, 1.0, 0.0).astype(BF)
        lo, qh = _head_masks(q_ref[...])
        lane = lax.broadcasted_iota(jnp.int32, (1, LANES), 1)

        def step(j, carry, masked):
            acc, c0, c1, s0, s1 = carry
            off = pl.multiple_of(j * tq, tq)
            kj = k_ref[pl.ds(off, tq), :]
            vj = v_ref[pl.ds(off, tq), :]
            pvs, cars, seen = [], [], []
            for h, car, sv in ((0, c0, s0), (1, c1, s1)):
                z = _dot_nt(qh[h], kj) * scale
                lk = _log_keep(z)
                if masked:
                    lk = jnp.where(gt, lk, 0.0)
                surv = _split_dot(lk, tri) + car
                w = jnp.exp(z + lk + surv)
                if masked:
                    w = jnp.where(gt, w, 0.0)
                pvs.append(_dot(w.astype(BF), vj))
                seen.append(jnp.where(lane == j, car, sv))
                cars.append(car + jnp.sum(lk, axis=1, keepdims=True))
            return acc + jnp.where(lo, pvs[0], pvs[1]), cars[0], cars[1], seen[0], seen[1]

        zc = jnp.zeros((tq, 1), F32)
        zt = jnp.zeros((tq, LANES), F32)
        carry = step(i, (zt, zc, zc, zt, zt), True)
        carry = lax.fori_loop(0, i, lambda n, c: step(i - 1 - n, c, False), carry)
        acc = carry[0]
        o_ref[...] = acc
        m_ref[...] = _rms_out(acc, lo, g_ref[...]).astype(BF)
        c_ref[:, :LANES] = carry[3]
        c_ref[:, LANES:] = carry[4]

    car_spec = pl.BlockSpec((tq, 2 * LANES), lambda b, p, i: (b * nq + i, p))
    return pl.pallas_call(
        body, name="sb_fwd", grid=(bl, p_n, nq), in_specs=[q_spec, k_spec, v_spec, g_spec],
        out_specs=[tile_spec, tile_spec, car_spec],
        out_shape=[_sds((t, dg), F32), _sds((t, dg), BF), _sds((t, 2 * dg), F32)],
        compiler_params=_cparams(("parallel", "parallel", "parallel")),
    )(qkv, qkv, qkv, gain)


def _sb_bwd(qkv, cars, do, bl, seq, dg):
    t = bl * seq
    p_n = dg // LANES
    tq = ATT_TILE
    scale = 1.0 / math.sqrt(HEAD_DIM)
    nq, q_spec, k_spec, v_spec, tile_spec, seq_spec, _ = _sb_specs(bl, seq, p_n, 0)
    car_spec = pl.BlockSpec((tq, 2 * LANES), lambda b, p, i: (b * nq + i, p))

    def body(q_ref, k_ref, v_ref, c_ref, do_ref, dq_ref, dk_ref, dv_ref, dk_acc, dv_acc):
        i = pl.program_id(2)

        @pl.when(i == 0)
        def _():
            dk_acc[...] = jnp.zeros_like(dk_acc)
            dv_acc[...] = jnp.zeros_like(dv_acc)

        row = lax.broadcasted_iota(jnp.int32, (tq, tq), 0)
        col = lax.broadcasted_iota(jnp.int32, (tq, tq), 1)
        gt = row > col
        tri = jnp.where(gt, 1.0, 0.0).astype(BF)
        tri_pre = jnp.where(row < col, 1.0, 0.0).astype(BF)
        lane = lax.broadcasted_iota(jnp.int32, (1, LANES), 1)
        q = q_ref[...]
        dob = do_ref[...]
        lo, qh = _head_masks(q)
        _, doh = _head_masks(dob)
        seen = (c_ref[:, :LANES], c_ref[:, LANES:])

        def step(j, carry, masked):
            dq, e0, e1 = carry
            off = pl.multiple_of(j * tq, tq)
            kj = k_ref[pl.ds(off, tq), :]
            vj = v_ref[pl.ds(off, tq), :]
            dqs, dks, dvs, ecars = [], [], [], []
            for h, ecar in ((0, e0), (1, e1)):
                z = _dot_nt(qh[h], kj) * scale
                lk = _log_keep(z)
                if masked:
                    lk = jnp.where(gt, lk, 0.0)
                car = jnp.sum(jnp.where(lane == j, seen[h], 0.0), axis=1, keepdims=True)
                surv = _split_dot(lk, tri) + car
                w = jnp.exp(z + lk + surv)
                if masked:
                    w = jnp.where(gt, w, 0.0)
                e = _dot_nt(doh[h], vj) * w
                sig = jax.nn.sigmoid(z)
                before = _split_dot(e, tri_pre) + ecar
                dz = (e * (1.0 - sig) - sig * before) * scale
                if masked:
                    dz = jnp.where(gt, dz, 0.0)
                dzb = dz.astype(BF)
                dqs.append(_dot(dzb, kj))
                dks.append(_dot_tn(dzb, q))
                dvs.append(_dot_tn(w.astype(BF), dob))
                ecars.append(ecar + jnp.sum(e, axis=1, keepdims=True))
            dk_acc[pl.ds(off, tq), :] += jnp.where(lo, dks[0], dks[1])
            dv_acc[pl.ds(off, tq), :] += jnp.where(lo, dvs[0], dvs[1])
            return dq + jnp.where(lo, dqs[0], dqs[1]), ecars[0], ecars[1]

        zc = jnp.zeros((tq, 1), F32)
        carry = lax.fori_loop(0, i, lambda j, c: step(j, c, False), (jnp.zeros((tq, LANES), F32), zc, zc))
        carry = step(i, carry, True)
        dq_ref[...] = carry[0].astype(BF)

        @pl.when(i == nq - 1)
        def _():
            dk_ref[...] = dk_acc[...].astype(BF)
            dv_ref[...] = dv_acc[...].astype(BF)

    out = _sds((t, dg), BF)
    return pl.pallas_call(
        body, name="sb_bwd", grid=(bl, p_n, nq),
        in_specs=[q_spec, k_spec, v_spec, car_spec, tile_spec],
        out_specs=[tile_spec, seq_spec, seq_spec], out_shape=[out, out, out],
        scratch_shapes=[pltpu.VMEM((seq, LANES), F32), pltpu.VMEM((seq, LANES), F32)],
        compiler_params=_cparams(("parallel", "parallel", "arbitrary")),
    )(qkv, qkv, qkv, cars, do)


def _bias_expand(rb):
    h = rb.shape[0]
    pad = CHUNK * N_PREV_CHUNKS
    n_f = BAND + CHUNK - 1
    f = jnp.concatenate([rb[:, MAX_REL - (CHUNK - 1):2 * MAX_REL],
                         jnp.broadcast_to(rb[:, 2 * MAX_REL:], (h, pad + CHUNK - MAX_REL))], axis=1)
    fr = f[:, ::-1]
    skew = jnp.pad(jnp.broadcast_to(fr[:, None, :], (h, CHUNK, n_f)), ((0, 0), (0, 0), (0, 1)))
    skew = skew.reshape(h, CHUNK * (n_f + 1))[:, :CHUNK * n_f].reshape(h, CHUNK, n_f)
    table = skew[:, :, CHUNK - 1:CHUNK - 1 + BAND]
    n_c = ATT_TILE // CHUNK
    width = 3 * ATT_TILE
    rows = [jnp.pad(table, ((0, 0), (0, 0), (c * CHUNK, width - BAND - c * CHUNK)), constant_values=NEG_BIAS)
            for c in range(n_c)]
    return jnp.concatenate(rows, axis=1)


def _ca_scores(qh_h, k_ref, bias_ref, h, i, scale):
    tq = ATT_TILE
    ss, offs = [], []
    for kb in range(3):
        jb = i - 2 + kb
        off = pl.multiple_of(jnp.maximum(jb, 0) * tq, tq)
        s = _dot_nt(qh_h, k_ref[pl.ds(off, tq), :]) * scale + bias_ref[h, :, kb * tq:(kb + 1) * tq]
        ss.append(jnp.where(jb >= 0, s, NEG_BIAS))
        offs.append(off)
    m = jnp.maximum(jnp.maximum(jnp.max(ss[0], axis=1, keepdims=True), jnp.max(ss[1], axis=1, keepdims=True)),
                    jnp.max(ss[2], axis=1, keepdims=True))
    ps = [jnp.exp(s - m) for s in ss]
    den = ps[0].sum(axis=1, keepdims=True) + ps[1].sum(axis=1, keepdims=True) + ps[2].sum(axis=1, keepdims=True)
    inv = 1.0 / den
    return [p * inv for p in ps], offs


def _ca_fwd(qkv, bias4, bl, seq, dg, gain):
    t = bl * seq
    p_n = dg // LANES
    tq = ATT_TILE
    scale = 1.0 / math.sqrt(HEAD_DIM)
    nq, q_spec, k_spec, v_spec, tile_spec, _, g_spec = _sb_specs(bl, seq, p_n, 3 * p_n)
    b_spec = pl.BlockSpec((2, tq, 3 * tq), lambda b, p, i: (p, 0, 0))

    def body(q_ref, k_ref, v_ref, bias_ref, g_ref, o_ref, m_ref):
        i = pl.program_id(2)
        lo, qh = _head_masks(q_ref[...])
        outs = []
        for h in range(2):
            ps, offs = _ca_scores(qh[h], k_ref, bias_ref, h, i, scale)
            acc = None
            for kb in range(3):
                pv = _dot(ps[kb].astype(BF), v_ref[pl.ds(offs[kb], tq), :])
                acc = pv if acc is None else acc + pv
            outs.append(acc)
        acc = jnp.where(lo, outs[0], outs[1])
        o_ref[...] = acc
        m_ref[...] = _rms_out(acc, lo, g_ref[...]).astype(BF)

    return pl.pallas_call(
        body, name="ca_fwd", grid=(bl, p_n, nq), in_specs=[q_spec, k_spec, v_spec, b_spec, g_spec],
        out_specs=[tile_spec, tile_spec], out_shape=[_sds((t, dg), F32), _sds((t, dg), BF)],
        compiler_params=_cparams(("parallel", "parallel", "parallel")),
    )(qkv, qkv, qkv, bias4, gain)


def _ca_bwd(qkv, bias4, o, do, bl, seq, dg):
    t = bl * seq
    p_n = dg // LANES
    tq = ATT_TILE
    scale = 1.0 / math.sqrt(HEAD_DIM)
    nq = seq // tq
    c0 = 3 * p_n
    q_spec = pl.BlockSpec((tq, LANES), lambda p, b, i: (b * nq + i, c0 + p))
    k_spec = pl.BlockSpec((seq, LANES), lambda p, b, i: (b, c0 + p_n + p))
    v_spec = pl.BlockSpec((seq, LANES), lambda p, b, i: (b, c0 + 2 * p_n + p))
    tile_spec = pl.BlockSpec((tq, LANES), lambda p, b, i: (b * nq + i, p))
    seq_spec = pl.BlockSpec((seq, LANES), lambda p, b, i: (b, p))
    b_spec = pl.BlockSpec((2, tq, 3 * tq), lambda p, b, i: (p, 0, 0))

    def body(q_ref, k_ref, v_ref, bias_ref, o_ref, do_ref, dq_ref, dk_ref, dv_ref, db_ref, dk_acc, dv_acc):
        b = pl.program_id(1)
        i = pl.program_id(2)

        @pl.when(i == 0)
        def _():
            dk_acc[...] = jnp.zeros_like(dk_acc)
            dv_acc[...] = jnp.zeros_like(dv_acc)

        @pl.when((i == 0) & (b == 0))
        def _():
            db_ref[...] = jnp.zeros_like(db_ref)

        q = q_ref[...]
        dob = do_ref[...]
        lo, qh = _head_masks(q)
        _, doh = _head_masks(dob)
        prod = dob.astype(F32) * o_ref[...]
        delta = (jnp.sum(jnp.where(lo, prod, 0.0), axis=1, keepdims=True),
                 jnp.sum(jnp.where(lo, 0.0, prod), axis=1, keepdims=True))
        dqs = []
        for h in range(2):
            ps, offs = _ca_scores(qh[h], k_ref, bias_ref, h, i, scale)
            dq = None
            dks, dvs = [], []
            for kb in range(3):
                dp = _dot_nt(doh[h], v_ref[pl.ds(offs[kb], tq), :])
                ds = ps[kb] * (dp - delta[h])
                db_ref[h, :, kb * tq:(kb + 1) * tq] += ds
                dsb = (ds * scale).astype(BF)
                part = _dot(dsb, k_ref[pl.ds(offs[kb], tq), :])
                dq = part if dq is None else dq + part
                dks.append(_dot_tn(dsb, q))
                dvs.append(_dot_tn(ps[kb].astype(BF), dob))
            dqs.append(dq)
            hm = lo if h == 0 else jnp.logical_not(lo)
            for kb in range(3):
                dk_acc[pl.ds(offs[kb], tq), :] += jnp.where(hm, dks[kb], 0.0)
                dv_acc[pl.ds(offs[kb], tq), :] += jnp.where(hm, dvs[kb], 0.0)
        dq_ref[...] = jnp.where(lo, dqs[0], dqs[1]).astype(BF)

        @pl.when(i == nq - 1)
        def _():
            dk_ref[...] = dk_acc[...].astype(BF)
            dv_ref[...] = dv_acc[...].astype(BF)

    out = _sds((t, dg), BF)
    return pl.pallas_call(
        body, name="ca_bwd", grid=(p_n, bl, nq),
        in_specs=[q_spec, k_spec, v_spec, b_spec, tile_spec, tile_spec],
        out_specs=[tile_spec, seq_spec, seq_spec, b_spec], out_shape=[out, out, out, _sds(bias4.shape, F32)],
        scratch_shapes=[pltpu.VMEM((seq, LANES), F32), pltpu.VMEM((seq, LANES), F32)],
        compiler_params=_cparams(("parallel", "arbitrary", "arbitrary")),
    )(qkv, qkv, qkv, bias4, o, do)


_ANY = pl.BlockSpec(memory_space=pl.ANY)


def _mesh_pos():
    x, y, c = lax.axis_index("x"), lax.axis_index("y"), lax.axis_index("c")
    chips = [(1 - x, y), (x, 1 - y), (1 - x, 1 - y)]
    return x, y, c, chips


def _half(ref_rows, c):
    return pl.ds(c * (ref_rows // 2), ref_rows // 2)


def _gather_weights(shards):
    nt = len(shards)

    def body(*refs):
        ins, outs = refs[:nt], refs[nt:2 * nt]
        send_sems, recv_sems, local_sems = refs[2 * nt:]
        x, y, c, chips = _mesh_pos()
        s_me = 2 * x + y

        def rows(t, cc):
            return _half(shards[t].shape[1], cc)

        def copy(k, src, dst, to):
            return pltpu.make_async_remote_copy(src_ref=src, dst_ref=dst, send_sem=send_sems.at[k],
                                                recv_sem=recv_sems.at[k], device_id=to, device_id_type=MESH_IDS)

        mine = [pltpu.make_async_copy(ins[t], outs[t].at[:, s_me], local_sems.at[t]) for t in range(nt)]
        for cp in mine:
            cp.start()
        first = []
        for j, (px, py) in enumerate(chips):
            for t in range(nt):
                first.append(copy(j * nt + t, ins[t].at[:, rows(t, c)], outs[t].at[:, s_me, rows(t, c)], (px, py, c)))
        for cp in first:
            cp.start()
        passed = []
        for j, (px, py) in enumerate(chips):
            s_p = 2 * px + py
            for t in range(nt):
                blk = outs[t].at[:, s_p, rows(t, c)]
                copy(j * nt + t, blk, blk, (px, py, c)).wait_recv()
                fwd = copy(3 * nt + j * nt + t, blk, blk, (x, y, 1 - c))
                fwd.start()
                passed.append(fwd)
        for j, (px, py) in enumerate(chips):
            s_p = 2 * px + py
            for t in range(nt):
                blk = outs[t].at[:, s_p, rows(t, 1 - c)]
                copy(3 * nt + j * nt + t, blk, blk, (x, y, 1 - c)).wait_recv()
        for cp in first + passed:
            cp.wait_send()
        for cp in mine:
            cp.wait()

    out_shape = [_sds((s.shape[0], 4) + s.shape[1:], s.dtype) for s in shards]
    return pl.pallas_call(
        body, name="gather_weights", in_specs=[_ANY] * nt, out_specs=[_ANY] * nt, out_shape=out_shape,
        scratch_shapes=[pltpu.SemaphoreType.DMA((6 * nt,)), pltpu.SemaphoreType.DMA((6 * nt,)),
                        pltpu.SemaphoreType.DMA((nt,))],
    )(*shards)


def _swap_halves(grads):
    nt = len(grads)

    def body(*refs):
        ins, mine_o, theirs_o = refs[:nt], refs[nt:2 * nt], refs[2 * nt:3 * nt]
        send_sems, recv_sems, local_sems = refs[3 * nt:]
        x, y, c, _ = _mesh_pos()
        local = [pltpu.make_async_copy(ins[t].at[:, :, _half(grads[t].shape[2], c)], mine_o[t], local_sems.at[t])
                 for t in range(nt)]
        for cp in local:
            cp.start()
        sends = [pltpu.make_async_remote_copy(
            src_ref=ins[t].at[:, :, _half(grads[t].shape[2], 1 - c)], dst_ref=theirs_o[t],
            send_sem=send_sems.at[t], recv_sem=recv_sems.at[t], device_id=(x, y, 1 - c), device_id_type=MESH_IDS)
            for t in range(nt)]
        for cp in sends:
            cp.start()
        for cp in sends:
            cp.wait_recv()
        for cp in sends:
            cp.wait_send()
        for cp in local:
            cp.wait()

    half = [_sds(g.shape[:2] + (g.shape[2] // 2, g.shape[3]), g.dtype) for g in grads]
    res = pl.pallas_call(
        body, name="swap_halves", in_specs=[_ANY] * nt, out_specs=[_ANY] * (2 * nt), out_shape=half + half,
        scratch_shapes=[pltpu.SemaphoreType.DMA((nt,)), pltpu.SemaphoreType.DMA((nt,)),
                        pltpu.SemaphoreType.DMA((nt,))],
    )(*grads)
    return res[:nt], res[nt:]


def _scatter_chips(psums):
    nt = len(psums)

    def body(*refs):
        ins, own_o, got_o = refs[:nt], refs[nt:2 * nt], refs[2 * nt:3 * nt]
        send_sems, recv_sems, local_sems = refs[3 * nt:]
        x, y, c, chips = _mesh_pos()
        s_me = 2 * x + y
        local = [pltpu.make_async_copy(ins[t].at[:, s_me], own_o[t], local_sems.at[t]) for t in range(nt)]
        for cp in local:
            cp.start()
        sends = []
        for j, (px, py) in enumerate(chips):
            for t in range(nt):
                sends.append(pltpu.make_async_remote_copy(
                    src_ref=ins[t].at[:, 2 * px + py], dst_ref=got_o[t].at[j],
                    send_sem=send_sems.at[j * nt + t], recv_sem=recv_sems.at[j * nt + t],
                    device_id=(px, py, c), device_id_type=MESH_IDS))
        for cp in sends:
            cp.start()
        for cp in sends:
            cp.wait_recv()
        for cp in sends:
            cp.wait_send()
        for cp in local:
            cp.wait()

    own = [_sds((p.shape[0],) + p.shape[2:], p.dtype) for p in psums]
    got = [_sds((3, p.shape[0]) + p.shape[2:], p.dtype) for p in psums]
    res = pl.pallas_call(
        body, name="scatter_chips", in_specs=[_ANY] * nt, out_specs=[_ANY] * (2 * nt), out_shape=own + got,
        scratch_shapes=[pltpu.SemaphoreType.DMA((3 * nt,)), pltpu.SemaphoreType.DMA((3 * nt,)),
                        pltpu.SemaphoreType.DMA((nt,))],
    )(*psums)
    return res[:nt], res[nt:]


def _join_halves(halves):
    nt = len(halves)

    def body(*refs):
        ins, outs = refs[:nt], refs[nt:2 * nt]
        send_sems, recv_sems, local_sems = refs[2 * nt:]
        x, y, c, _ = _mesh_pos()
        local, sends = [], []
        for t in range(nt):
            dst = outs[t].at[:, _half(2 * halves[t].shape[1], c)]
            local.append(pltpu.make_async_copy(ins[t], dst, local_sems.at[t]))
            sends.append(pltpu.make_async_remote_copy(
                src_ref=ins[t], dst_ref=dst, send_sem=send_sems.at[t], recv_sem=recv_sems.at[t],
                device_id=(x, y, 1 - c), device_id_type=MESH_IDS))
        for cp in local + sends:
            cp.start()
        for t in range(nt):
            theirs = outs[t].at[:, _half(2 * halves[t].shape[1], 1 - c)]
            pltpu.make_async_remote_copy(
                src_ref=ins[t], dst_ref=theirs, send_sem=send_sems.at[t], recv_sem=recv_sems.at[t],
                device_id=(x, y, 1 - c), device_id_type=MESH_IDS).wait_recv()
        for cp in sends:
            cp.wait_send()
        for cp in local:
            cp.wait()

    out_shape = [_sds((h.shape[0], 2 * h.shape[1], h.shape[2]), h.dtype) for h in halves]
    return pl.pallas_call(
        body, name="join_halves", in_specs=[_ANY] * nt, out_specs=[_ANY] * nt, out_shape=out_shape,
        scratch_shapes=[pltpu.SemaphoreType.DMA((nt,)), pltpu.SemaphoreType.DMA((nt,)),
                        pltpu.SemaphoreType.DMA((nt,))],
    )(*halves)


def _allreduce_small(part):
    r, d = part.shape
    flips = [(fx, fy, fc) for fx in (0, 1) for fy in (0, 1) for fc in (0, 1) if fx + fy + fc]

    def body(p_ref, o_ref, buf, send_sems, recv_sems):
        x, y, c, _ = _mesh_pos()
        me = 4 * x + 2 * y + c
        buf[me] = p_ref[...]
        sends = []
        for k, (fx, fy, fc) in enumerate(flips):
            sends.append(pltpu.make_async_remote_copy(
                src_ref=p_ref, dst_ref=buf.at[me], send_sem=send_sems.at[k], recv_sem=recv_sems.at[k],
                device_id=(x ^ fx, y ^ fy, c ^ fc), device_id_type=MESH_IDS))
        for cp in sends:
            cp.start()
        for k, (fx, fy, fc) in enumerate(flips):
            peer = 4 * (x ^ fx) + 2 * (y ^ fy) + (c ^ fc)
            pltpu.make_async_remote_copy(
                src_ref=p_ref, dst_ref=buf.at[peer], send_sem=send_sems.at[k], recv_sem=recv_sems.at[k],
                device_id=(x ^ fx, y ^ fy, c ^ fc), device_id_type=MESH_IDS).wait_recv()
        acc = buf[0]
        for dev in range(1, 8):
            acc = acc + buf[dev]
        o_ref[...] = acc
        for cp in sends:
            cp.wait_send()

    vmem = pl.BlockSpec(memory_space=pltpu.VMEM)
    return pl.pallas_call(
        body, name="allreduce_small", in_specs=[vmem], out_specs=vmem, out_shape=_sds((r, d), F32),
        scratch_shapes=[pltpu.VMEM((8, r, d), F32), pltpu.SemaphoreType.DMA((7,)), pltpu.SemaphoreType.DMA((7,))],
    )(part)


BIG = ("ffn1_w_gate", "ffn1_w_up", "ffn1_w_down", "w_in", "w_out", "ffn2_w_gate", "ffn2_w_up", "ffn2_w_down")
SMALL = ("ln1_g", "ln1_b", "rel_bias", "sb_out_g", "ca_out_g", "ln2_g", "ln2_b", "ln3_g", "ln3_b")
ORDER = ("ffn1_w_gate", "ffn1_w_up", "ffn1_w_down", "ln1_g", "ln1_b", "w_in", "rel_bias", "sb_out_g", "ca_out_g",
         "w_out", "ln2_g", "ln2_b", "ffn2_w_gate", "ffn2_w_up", "ffn2_w_down", "ln3_g", "ln3_b")


def _pack_small(vals, depth, d):
    rows = []
    for l in range(depth):
        for n in ("ln1_g", "ln1_b", "ln2_g", "ln2_b", "ln3_g", "ln3_b"):
            rows.append(vals[n][l][None, :])
        rows.append(jnp.concatenate([vals["sb_out_g"][l], vals["ca_out_g"][l]])[None, :])
        rb = vals["rel_bias"][l]
        rows.append(jnp.pad(rb, ((0, 0), (0, d - rb.shape[1]))))
    packed = jnp.concatenate(rows, axis=0)
    pad = (-packed.shape[0]) % 8
    return jnp.pad(packed, ((0, pad), (0, 0)))


def _unpack_small(packed, depth, d, heads, n_rel):
    per = 7 + heads
    out = {n: [] for n in SMALL}
    for l in range(depth):
        base = l * per
        for i, n in enumerate(("ln1_g", "ln1_b", "ln2_g", "ln2_b", "ln3_g", "ln3_b")):
            out[n].append(packed[base + i])
        out["sb_out_g"].append(packed[base + 6, :d // 2])
        out["ca_out_g"].append(packed[base + 6, d // 2:])
        out["rel_bias"].append(packed[base + 7:base + 7 + heads, :n_rel])
    return {n: jnp.stack(v) for n, v in out.items()}


def kernel(x, ffn1_w_gate, ffn1_w_up, ffn1_w_down, ln1_g, ln1_b, w_in, rel_bias, sb_out_g, ca_out_g, w_out, ln2_g, ln2_b, ffn2_w_gate, ffn2_w_up, ffn2_w_down, ln3_g, ln3_b, loss_target, m_ffn1_w_gate, m_ffn1_w_up, m_ffn1_w_down, m_ln1_g, m_ln1_b, m_w_in, m_rel_bias, m_sb_out_g, m_ca_out_g, m_w_out, m_ln2_g, m_ln2_b, m_ffn2_w_gate, m_ffn2_w_up, m_ffn2_w_down, m_ln3_g, m_ln3_b, v_ffn1_w_gate, v_ffn1_w_up, v_ffn1_w_down, v_ln1_g, v_ln1_b, v_w_in, v_rel_bias, v_sb_out_g, v_ca_out_g, v_w_out, v_ln2_g, v_ln2_b, v_ffn2_w_gate, v_ffn2_w_up, v_ffn2_w_down, v_ln3_g, v_ln3_b):
    w = dict(ffn1_w_gate=ffn1_w_gate, ffn1_w_up=ffn1_w_up, ffn1_w_down=ffn1_w_down, ln1_g=ln1_g, ln1_b=ln1_b, w_in=w_in, rel_bias=rel_bias, sb_out_g=sb_out_g, ca_out_g=ca_out_g, w_out=w_out, ln2_g=ln2_g, ln2_b=ln2_b, ffn2_w_gate=ffn2_w_gate, ffn2_w_up=ffn2_w_up, ffn2_w_down=ffn2_w_down, ln3_g=ln3_g, ln3_b=ln3_b)
    mom = dict(ffn1_w_gate=m_ffn1_w_gate, ffn1_w_up=m_ffn1_w_up, ffn1_w_down=m_ffn1_w_down, ln1_g=m_ln1_g, ln1_b=m_ln1_b, w_in=m_w_in, rel_bias=m_rel_bias, sb_out_g=m_sb_out_g, ca_out_g=m_ca_out_g, w_out=m_w_out, ln2_g=m_ln2_g, ln2_b=m_ln2_b, ffn2_w_gate=m_ffn2_w_gate, ffn2_w_up=m_ffn2_w_up, ffn2_w_down=m_ffn2_w_down, ln3_g=m_ln3_g, ln3_b=m_ln3_b)
    var = dict(ffn1_w_gate=v_ffn1_w_gate, ffn1_w_up=v_ffn1_w_up, ffn1_w_down=v_ffn1_w_down, ln1_g=v_ln1_g, ln1_b=v_ln1_b, w_in=v_w_in, rel_bias=v_rel_bias, sb_out_g=v_sb_out_g, ca_out_g=v_ca_out_g, w_out=v_w_out, ln2_g=v_ln2_g, ln2_b=v_ln2_b, ffn2_w_gate=v_ffn2_w_gate, ffn2_w_up=v_ffn2_w_up, ffn2_w_down=v_ffn2_w_down, ln3_g=v_ln3_g, ln3_b=v_ln3_b)

    bl, seq, d = x.shape
    t = bl * seq
    depth = w_in.shape[0]
    dg = d // 2
    heads = dg // HEAD_DIM
    alpha = (2 * depth) ** 0.25
    n_rel = rel_bias.shape[-1]

    full = dict(zip(BIG, _gather_weights([w[n].astype(BF) for n in BIG])))

    xs = x.reshape(t, d)
    saved = []
    for l in range(depth):
        wg1, wu1, wd1 = full["ffn1_w_gate"][l], full["ffn1_w_up"][l], full["ffn1_w_down"][l]
        wg2, wu2, wd2 = full["ffn2_w_gate"][l], full["ffn2_w_up"][l], full["ffn2_w_down"][l]
        win = jnp.transpose(full["w_in"][l], (1, 0, 2)).reshape(d, 3 * d)
        wout = full["w_out"][l].reshape(2, dg, d)
        row = lambda a: a[l][None, :]
        g1, u1, h1 = _ffn_up(xs, wg1, wu1)
        x1, xh1, rs1 = _proj_ln("ffn_down_ln", [h1], [wd1], xs, row(ln1_g), row(ln1_b), FFN_RESIDUAL, alpha)
        qkv = _proj_nn("qkv_proj", x1, win, BF)
        bias4, bias_vjp = jax.vjp(_bias_expand, rel_bias[l])
        o_sb, m_sb, c_sb = _sb_fwd(qkv, bl, seq, dg, row(sb_out_g))
        o_ca, m_ca = _ca_fwd(qkv, bias4, bl, seq, dg, row(ca_out_g))
        x2, xh2, rs2 = _proj_ln("attn_out_ln", [m_sb, m_ca], [wout[0], wout[1]], x1, row(ln2_g), row(ln2_b), 1.0, alpha)
        g2, u2, h2 = _ffn_up(x2, wg2, wu2)
        x3, xh3, rs3 = _proj_ln("ffn_down_ln", [h2], [wd2], x2, row(ln3_g), row(ln3_b), FFN_RESIDUAL, alpha)
        saved.append(dict(xin=xs, g1=g1, u1=u1, h1=h1, x1=x1, xh1=xh1, rs1=rs1, qkv=qkv, bias4=bias4,
                          bias_vjp=bias_vjp, o_sb=o_sb, m_sb=m_sb, c_sb=c_sb, o_ca=o_ca, m_ca=m_ca, x2=x2, xh2=xh2, rs2=rs2,
                          g2=g2, u2=u2, h2=h2, xh3=xh3, rs3=rs3, wg1=wg1, wu1=wu1, wd1=wd1, wg2=wg2, wu2=wu2,
                          wd2=wd2, win=win, wout=wout))
        xs = x3

    dy, sq = _loss_grad(xs, loss_target.reshape(t, d))
    loss = lax.psum(0.5 * jnp.sum(sq) / d, ("x", "y", "c"))

    big_grads = {n: [None] * depth for n in BIG}
    small_grads = {n: [None] * depth for n in SMALL}
    s8 = lambda a: jnp.sum(a, axis=0)
    for l in reversed(range(depth)):
        sv = saved[l]
        row = lambda a: a[l][None, :]
        dr3, drs3, pg, pb = _ln_bwd(dy, sv["xh3"], sv["rs3"], row(ln3_g), FFN_RESIDUAL)
        small_grads["ln3_g"][l], small_grads["ln3_b"][l] = s8(pg), s8(pb)
        dgt, dut = _swiglu_bwd(drs3, sv["wd2"], sv["g2"], sv["u2"])
        dx2 = _proj_nt("ffn_dx", [dgt, dut], [sv["wg2"], sv["wu2"]], d, dr3, alpha, F32)
        big_grads["ffn2_w_down"][l] = _wgrad("wgrad_down", [sv["h2"]], [drs3], 4, sv["wd2"].shape[1:],
                                             lambda a, s: a[0][s], lambda b, s: b[0][...])
        big_grads["ffn2_w_gate"][l] = _wgrad("wgrad_up", [sv["x2"]], [dgt], 4, sv["wg2"].shape[1:],
                                             lambda a, s: a[0][...], lambda b, s: b[0][s])
        big_grads["ffn2_w_up"][l] = _wgrad("wgrad_up", [sv["x2"]], [dut], 4, sv["wu2"].shape[1:],
                                           lambda a, s: a[0][...], lambda b, s: b[0][s])
        dr2, drs2, pg, pb = _ln_bwd(dx2, sv["xh2"], sv["rs2"], row(ln2_g), 1.0)
        small_grads["ln2_g"][l], small_grads["ln2_b"][l] = s8(pg), s8(pb)
        dmix = _proj_nt("attn_dmix", [drs2], [sv["wout"].reshape(d, d)], d, None, alpha, F32)
        qd = dg // 2
        gw_out = _wgrad("wgrad_out", [sv["m_sb"], sv["m_ca"]], [drs2], 4, (qd, d),
                        lambda a, s: a[s // 2][:, (s % 2) * qd:(s % 2 + 1) * qd], lambda b, s: b[0][...])
        do_sb, pgs = _rms_bwd(dmix, 0, sv["o_sb"], row(sb_out_g))
        do_ca, pgc = _rms_bwd(dmix, 1, sv["o_ca"], row(ca_out_g))
        small_grads["sb_out_g"][l], small_grads["ca_out_g"][l] = s8(pgs), s8(pgc)
        dqa, dka, dva = _sb_bwd(sv["qkv"], sv["c_sb"], do_sb, bl, seq, dg)
        dqb, dkb, dvb, dbias4 = _ca_bwd(sv["qkv"], sv["bias4"], sv["o_ca"], do_ca, bl, seq, dg)
        small_grads["rel_bias"][l] = sv["bias_vjp"](dbias4)[0]
        dqkv = [dqa, dka, dva, dqb, dkb, dvb]
        tm = _tile(t, ROW_TILE)
        win_cols = [(sv["win"], pl.BlockSpec((d, dg), functools.partial(lambda m, gi: (0, gi), gi=gi)))
                    for gi in range(6)]
        dx1 = _proj_nt("attn_dx", dqkv, win_cols, d, dr2, alpha, F32)
        gw_in = _wgrad("wgrad_in", [sv["x1"]], dqkv, 6, (d, dg), lambda a, s: a[0][...], lambda b, s: b[s][...])
        big_grads["w_in"][l] = jnp.transpose(
            jnp.transpose(gw_in, (1, 0, 2)).reshape(d, 4, 3 * d // 4), (1, 0, 2))
        dr1, drs1, pg, pb = _ln_bwd(dx1, sv["xh1"], sv["rs1"], row(ln1_g), FFN_RESIDUAL)
        small_grads["ln1_g"][l], small_grads["ln1_b"][l] = s8(pg), s8(pb)
        dgt, dut = _swiglu_bwd(drs1, sv["wd1"], sv["g1"], sv["u1"])
        dy = _proj_nt("ffn_dx", [dgt, dut], [sv["wg1"], sv["wu1"]], d, dr1, alpha, F32)
        big_grads["ffn1_w_down"][l] = _wgrad("wgrad_down", [sv["h1"]], [drs1], 4, sv["wd1"].shape[1:],
                                             lambda a, s: a[0][s], lambda b, s: b[0][...])
        big_grads["ffn1_w_gate"][l] = _wgrad("wgrad_up", [sv["xin"]], [dgt], 4, sv["wg1"].shape[1:],
                                             lambda a, s: a[0][...], lambda b, s: b[0][s])
        big_grads["ffn1_w_up"][l] = _wgrad("wgrad_up", [sv["xin"]], [dut], 4, sv["wu1"].shape[1:],
                                           lambda a, s: a[0][...], lambda b, s: b[0][s])
        big_grads["w_out"][l] = gw_out
    grad_x = dy.reshape(bl, seq, d)

    partial = [jnp.stack(big_grads[n]) for n in BIG]
    mine, theirs = _swap_halves(partial)
    chip_sum = [_add_n("add_pair", [a, b], BF) for a, b in zip(mine, theirs)]
    own, got = _scatter_chips(chip_sum)
    halves = [_add_n("add_chips", [o, g[0], g[1], g[2]], F32) for o, g in zip(own, got)]
    grads = dict(zip(BIG, _join_halves(halves)))

    packed = _allreduce_small(_pack_small({n: jnp.stack(small_grads[n]) for n in SMALL}, depth, d))
    grads.update(_unpack_small(packed, depth, d, heads, n_rel))

    delta, new_m, new_v = {}, {}, {}
    for n in BIG:
        delta[n], new_m[n], new_v[n] = _adamw(w[n], grads[n], mom[n], var[n])
    pw, pm, pv = (_pack_small({n: src[n] for n in SMALL}, depth, d) for src in (w, mom, var))
    sd, sm, sv_ = _adamw(pw, packed, pm, pv)
    for dst, src in ((delta, sd), (new_m, sm), (new_v, sv_)):
        dst.update(_unpack_small(src, depth, d, heads, n_rel))

    return (loss, grad_x, *[grads[n] for n in ORDER], *[delta[n] for n in ORDER],
            *[new_m[n] for n in ORDER], *[new_v[n] for n in ORDER])
```

```python
import functools
import math

import jax
import jax.numpy as jnp
from jax import lax
from jax.experimental import pallas as pl
from jax.experimental.pallas import tpu as pltpu

F32 = jnp.float32
BF = jnp.bfloat16

CHUNK = 64
N_PREV_CHUNKS = 8
BAND = CHUNK * (N_PREV_CHUNKS + 1)
MAX_REL = 128
HEAD_DIM = 64
FFN_RESIDUAL = 0.5
LN_EPS = 1e-5
RMS_EPS = 1e-6
ADAM_LR = 0.001
ADAM_B1 = 0.9
ADAM_B2 = 0.999
ADAM_EPS = 1e-08
ADAM_WD = 0.01
ADAM_STEP = 10

LANES = 128
ATT_TILE = 4 * CHUNK
SB_TQ = 512
SB_TK = 256
NEG_BIAS = -1e30
VMEM_LIMIT = 56 * 1024 * 1024
ROW_TILE = 256
MESH_IDS = pl.DeviceIdType.MESH

_NT = (((1,), (1,)), ((), ()))
_TN = (((0,), (0,)), ((), ()))


def _dot(a, b):
    return jnp.dot(a, b, preferred_element_type=F32)


def _dot_nt(a, b):
    return lax.dot_general(a, b, _NT, preferred_element_type=F32)


def _dot_tn(a, b):
    return lax.dot_general(a, b, _TN, preferred_element_type=F32)


def _tile(n, pref):
    t = min(n, pref)
    while n % t:
        t //= 2
    return t


def _cparams(sem):
    return pltpu.CompilerParams(dimension_semantics=sem, vmem_limit_bytes=VMEM_LIMIT)


def _sds(shape, dtype):
    return jax.ShapeDtypeStruct(tuple(shape), dtype)


def _rows8(v):
    tm, c = v.shape
    return v.reshape(tm // 8, 8, c).sum(axis=0)


def _rowcall(name, body, n_rows, tm, row_ins, res_ins, row_outs, acc_outs=()):
    def rspec(shape):
        if len(shape) == 2:
            return pl.BlockSpec((tm, shape[1]), lambda m: (m, 0))
        return pl.BlockSpec((shape[0], tm, shape[2]), lambda m: (0, m, 0))

    def cspec(shape):
        zeros = (0,) * len(shape)
        return pl.BlockSpec(tuple(shape), lambda m: zeros)

    def split(items, mk):
        arrs, specs = [], []
        for it in items:
            if isinstance(it, tuple):
                arrs.append(it[0])
                specs.append(it[1])
            else:
                arrs.append(it)
                specs.append(mk(it.shape))
        return arrs, specs

    ra, rs = split(row_ins, rspec)
    ca, cs = split(res_ins, cspec)
    out_specs = [rspec(s.shape) for s in row_outs] + [cspec(s.shape) for s in acc_outs]
    sem = ("arbitrary",) if acc_outs else ("parallel",)
    return pl.pallas_call(
        body, name=name, grid=(n_rows // tm,), in_specs=rs + cs, out_specs=out_specs,
        out_shape=list(row_outs) + list(acc_outs), compiler_params=_cparams(sem),
    )(*ra, *ca)


def _ffn_up(x, wg, wu):
    t, _ = x.shape
    s_n, _, fs = wg.shape
    tm = _tile(t, ROW_TILE)

    def body(x_ref, wg_ref, wu_ref, g_ref, u_ref, h_ref):
        xb = x_ref[...].astype(BF)
        for s in range(s_n):
            g = _dot(xb, wg_ref[s])
            u = _dot(xb, wu_ref[s])
            g_ref[s] = g.astype(BF)
            u_ref[s] = u.astype(BF)
            h_ref[s] = (g * jax.nn.sigmoid(g) * u).astype(BF)

    out = _sds((s_n, t, fs), BF)
    return _rowcall("ffn_up", body, t, tm, [x], [wg, wu], [out, out, out])


def _proj_ln(name, a_list, b_list, xres, gain, bias, scale, alpha):
    t, d = xres.shape
    tm = _tile(t, ROW_TILE)
    na = len(a_list)
    a_nd = [a.ndim for a in a_list]
    a_s = [a.shape[0] for a in a_list]

    def body(*refs):
        a_refs = refs[:na]
        x_ref = refs[na]
        b_refs = refs[na + 1:2 * na + 1]
        g_ref, bb_ref = refs[2 * na + 1:2 * na + 3]
        y_ref, xh_ref, rs_ref = refs[2 * na + 3:]
        acc = None
        for i in range(na):
            if a_nd[i] == 3:
                parts = [_dot(a_refs[i][s].astype(BF), b_refs[i][s]) for s in range(a_s[i])]
            else:
                parts = [_dot(a_refs[i][...].astype(BF), b_refs[i][...])]
            for p in parts:
                acc = p if acc is None else acc + p
        if scale != 1.0:
            acc = scale * acc
        r = alpha * x_ref[...] + acc
        mu = jnp.mean(r, axis=-1, keepdims=True)
        xc = r - mu
        var = jnp.mean(xc * xc, axis=-1, keepdims=True)
        rstd = lax.rsqrt(var + LN_EPS)
        xh = xc * rstd
        y_ref[...] = xh * g_ref[...] + bb_ref[...]
        xh_ref[...] = xh
        rs_ref[...] = rstd

    outs = [_sds((t, d), F32), _sds((t, d), F32), _sds((t, 1), F32)]
    return _rowcall(name, body, t, tm, list(a_list) + [xres], list(b_list) + [gain, bias], outs)


def _proj_nn(name, x, w, out_dtype):
    t, _ = x.shape
    n = w.shape[1]
    tm = _tile(t, ROW_TILE)

    def body(x_ref, w_ref, o_ref):
        o_ref[...] = _dot(x_ref[...].astype(BF), w_ref[...]).astype(out_dtype)

    return _rowcall(name, body, t, tm, [x], [w], [_sds((t, n), out_dtype)])[0]


def _proj_nt(name, a_list, b_list, n_out, resid, alpha, out_dtype):
    first = a_list[0][0] if isinstance(a_list[0], tuple) else a_list[0]
    t = first.shape[-2]
    tm = _tile(t, ROW_TILE)
    na = len(a_list)
    a_nd = [(a[0] if isinstance(a, tuple) else a).ndim for a in a_list]
    a_s = [(a[0] if isinstance(a, tuple) else a).shape[0] for a in a_list]
    has_res = resid is not None

    def body(*refs):
        a_refs = refs[:na]
        pos = na
        if has_res:
            r_ref = refs[pos]
            pos += 1
        b_refs = refs[pos:pos + na]
        o_ref = refs[pos + na]
        acc = None
        for i in range(na):
            if a_nd[i] == 3:
                parts = [_dot_nt(a_refs[i][s].astype(BF), b_refs[i][s]) for s in range(a_s[i])]
            else:
                parts = [_dot_nt(a_refs[i][...].astype(BF), b_refs[i][...])]
            for p in parts:
                acc = p if acc is None else acc + p
        if has_res:
            acc = acc + alpha * r_ref[...]
        o_ref[...] = acc.astype(out_dtype)

    rows = list(a_list) + ([resid] if has_res else [])
    return _rowcall(name, body, t, tm, rows, list(b_list), [_sds((t, n_out), out_dtype)])[0]


def _ln_bwd(dy, xh, rstd, gain, scale):
    t, d = dy.shape
    tm = _tile(t, ROW_TILE)

    def body(dy_ref, xh_ref, rs_ref, g_ref, dr_ref, drs_ref, dg_ref, db_ref):
        m = pl.program_id(0)
        dyv = dy_ref[...]
        xhv = xh_ref[...]
        dxh = dyv * g_ref[...]
        m1 = jnp.mean(dxh, axis=-1, keepdims=True)
        m2 = jnp.mean(dxh * xhv, axis=-1, keepdims=True)
        dr = rs_ref[...] * (dxh - m1 - xhv * m2)
        dr_ref[...] = dr
        drs_ref[...] = (scale * dr).astype(BF)
        pg = _rows8(dyv * xhv)
        pb = _rows8(dyv)

        @pl.when(m == 0)
        def _():
            dg_ref[...] = pg
            db_ref[...] = pb

        @pl.when(m > 0)
        def _():
            dg_ref[...] += pg
            db_ref[...] += pb

    outs = [_sds((t, d), F32), _sds((t, d), BF)]
    accs = [_sds((8, d), F32), _sds((8, d), F32)]
    return _rowcall("ln_bwd", body, t, tm, [dy, xh, rstd], [gain], outs, accs)


def _swiglu_bwd(drs, wd, g, u):
    s_n, t, fs = g.shape
    tm = _tile(t, ROW_TILE)

    def body(d_ref, g_ref, u_ref, wd_ref, dg_ref, du_ref):
        dv = d_ref[...]
        for s in range(s_n):
            dh = _dot_nt(dv, wd_ref[s])
            gv = g_ref[s].astype(F32)
            uv = u_ref[s].astype(F32)
            sg = jax.nn.sigmoid(gv)
            dg_ref[s] = (dh * uv * (sg * (1.0 + gv * (1.0 - sg)))).astype(BF)
            du_ref[s] = (dh * (gv * sg)).astype(BF)

    out = _sds((s_n, t, fs), BF)
    return _rowcall("swiglu_bwd", body, t, tm, [drs, g, u], [wd], [out, out])


def _rms_bwd(dmix, half, o, gain):
    t, dh = o.shape
    tm = _tile(t, ROW_TILE)

    def body(dm_ref, o_ref, g_ref, do_ref, dg_ref):
        m = pl.program_id(0)
        lo = lax.broadcasted_iota(jnp.int32, (1, LANES), 1) < HEAD_DIM
        for gi in range(dh // LANES):
            sl = slice(gi * LANES, (gi + 1) * LANES)
            ov = o_ref[:, sl]
            dm = dm_ref[:, sl]
            ss = ov * ov
            s0 = jnp.sum(jnp.where(lo, ss, 0.0), axis=1, keepdims=True)
            s1 = jnp.sum(jnp.where(lo, 0.0, ss), axis=1, keepdims=True)
            r = jnp.where(lo, lax.rsqrt(s0 / HEAD_DIM + RMS_EPS), lax.rsqrt(s1 / HEAD_DIM + RMS_EPS))
            n = ov * r
            dn = dm * g_ref[:, sl]
            tt = dn * n
            t0 = jnp.sum(jnp.where(lo, tt, 0.0), axis=1, keepdims=True) / HEAD_DIM
            t1 = jnp.sum(jnp.where(lo, 0.0, tt), axis=1, keepdims=True) / HEAD_DIM
            do_ref[:, sl] = (r * (dn - n * jnp.where(lo, t0, t1))).astype(BF)
            part = _rows8(dm * n)

            @pl.when(m == 0)
            def _():
                dg_ref[:, sl] = part

            @pl.when(m > 0)
            def _():
                dg_ref[:, sl] += part

    dm_spec = pl.BlockSpec((tm, dh), lambda m: (m, half))
    return _rowcall("rms_bwd", body, t, tm, [(dmix, dm_spec), o], [gain], [_sds((t, dh), BF)], [_sds((8, dh), F32)])


def _loss_grad(y, target):
    t, d = y.shape
    tm = _tile(t, ROW_TILE)

    def body(y_ref, t_ref, dy_ref, acc_ref):
        m = pl.program_id(0)
        diff = y_ref[...] - t_ref[...]
        dy_ref[...] = diff / d
        part = _rows8(diff * diff)

        @pl.when(m == 0)
        def _():
            acc_ref[...] = part

        @pl.when(m > 0)
        def _():
            acc_ref[...] += part

    return _rowcall("loss_grad", body, t, tm, [y, target], [], [_sds((t, d), F32)], [_sds((8, d), F32)])


def _adamw(w, g, m, v):
    shape = w.shape
    c = shape[-1]
    rows = math.prod(shape[:-1])
    tm = _tile(rows, 512)

    def body(w_ref, g_ref, m_ref, v_ref, d_ref, nm_ref, nv_ref):
        gv = g_ref[...]
        nm = ADAM_B1 * m_ref[...] + (1.0 - ADAM_B1) * gv
        nv = ADAM_B2 * v_ref[...] + (1.0 - ADAM_B2) * (gv * gv)
        m_hat = nm / (1.0 - ADAM_B1 ** ADAM_STEP)
        v_hat = nv / (1.0 - ADAM_B2 ** ADAM_STEP)
        d_ref[...] = -ADAM_LR * (m_hat / (jnp.sqrt(v_hat) + ADAM_EPS) + ADAM_WD * w_ref[...])
        nm_ref[...] = nm
        nv_ref[...] = nv

    flat = [a.reshape(rows, c) for a in (w, g, m, v)]
    out = _sds((rows, c), F32)
    res = _rowcall("adamw", body, rows, tm, flat, [], [out, out, out])
    return tuple(r.reshape(shape) for r in res)


def _wgrad(name, a_list, b_list, out_sub, sub_shape, a_get, b_get):
    t = a_list[0].shape[-2]
    tk = _tile(t, 512)
    na, nb = len(a_list), len(b_list)
    nk = t // tk

    def tspec(shape):
        if len(shape) == 2:
            return pl.BlockSpec((tk, shape[1]), lambda k: (k, 0))
        return pl.BlockSpec((shape[0], tk, shape[2]), lambda k: (0, k, 0))

    def body(*refs):
        a_refs = refs[:na]
        b_refs = refs[na:na + nb]
        o_ref = refs[na + nb]
        acc_ref = refs[na + nb + 1]
        k = pl.program_id(0)

        @pl.when(k == 0)
        def _():
            acc_ref[...] = jnp.zeros_like(acc_ref)

        for s in range(out_sub):
            acc_ref[s] += _dot_tn(a_get(a_refs, s).astype(BF), b_get(b_refs, s).astype(BF))

        @pl.when(k == nk - 1)
        def _():
            o_ref[...] = acc_ref[...].astype(BF)

    full = (out_sub,) + tuple(sub_shape)
    return pl.pallas_call(
        body, name=name, grid=(nk,), in_specs=[tspec(a.shape) for a in a_list] + [tspec(b.shape) for b in b_list],
        out_specs=pl.BlockSpec(full, lambda k: (0, 0, 0)), out_shape=_sds(full, BF),
        scratch_shapes=[pltpu.VMEM(full, F32)], compiler_params=_cparams(("arbitrary",)),
    )(*a_list, *b_list)


def _head_masks(x):
    lo = lax.broadcasted_iota(jnp.int32, (1, LANES), 1) < HEAD_DIM
    zero = jnp.zeros_like(x)
    return lo, (jnp.where(lo, x, zero), jnp.where(lo, zero, x))


def _logistic_parts(z):
    nz = -z
    t = jnp.exp(jnp.minimum(z, nz))
    one_t = 1.0 + t
    return jnp.minimum(nz, 0.0) - jnp.log(one_t), t, one_t


def _split2(x):
    hi = x.astype(BF)
    lo = (x - hi.astype(F32)).astype(BF)
    return jnp.concatenate([hi, lo], axis=1)


def _tri2(cond):
    tri = jnp.where(cond, 1.0, 0.0).astype(BF)
    return jnp.concatenate([tri, tri], axis=0)


def _sb_specs(bl, seq, p_n, col0, tq):
    nq = seq // tq
    q_spec = pl.BlockSpec((tq, LANES), lambda b, p, i: (b * nq + i, col0 + p))
    k_spec = pl.BlockSpec((seq, LANES), lambda b, p, i: (b, col0 + p_n + p))
    v_spec = pl.BlockSpec((seq, LANES), lambda b, p, i: (b, col0 + 2 * p_n + p))
    tile_spec = pl.BlockSpec((tq, LANES), lambda b, p, i: (b * nq + i, p))
    seq_spec = pl.BlockSpec((seq, LANES), lambda b, p, i: (b, p))
    g_spec = pl.BlockSpec((1, LANES), lambda b, p, i: (0, p))
    return nq, q_spec, k_spec, v_spec, tile_spec, seq_spec, g_spec


def _rms_out(acc, lo, gain):
    ss = acc * acc
    s0 = jnp.sum(jnp.where(lo, ss, 0.0), axis=1, keepdims=True)
    s1 = jnp.sum(jnp.where(lo, 0.0, ss), axis=1, keepdims=True)
    r = jnp.where(lo, lax.rsqrt(s0 / HEAD_DIM + RMS_EPS), lax.rsqrt(s1 / HEAD_DIM + RMS_EPS))
    return acc * r * gain


def _sb_fwd(qkv, bl, seq, dg, gain):
    t = bl * seq
    p_n = dg // LANES
    tq, tk = min(SB_TQ, seq), SB_TK
    ratio = tq // tk
    scale = 1.0 / math.sqrt(HEAD_DIM)
    nq, q_spec, k_spec, v_spec, tile_spec, _, g_spec = _sb_specs(bl, seq, p_n, 0, tq)
    assert seq // tk <= LANES

    def body(q_ref, k_ref, v_ref, g_ref, o_ref, m_ref, c_ref):
        i = pl.program_id(2)
        row = lax.broadcasted_iota(jnp.int32, (tq, tk), 0)
        col = lax.broadcasted_iota(jnp.int32, (tq, tk), 1)
        krow = lax.broadcasted_iota(jnp.int32, (tk, tk), 0)
        kcol = lax.broadcasted_iota(jnp.int32, (tk, tk), 1)
        tri2 = _tri2(krow > kcol)
        lo, qh = _head_masks(q_ref[...] * jnp.asarray(scale, BF))
        lane = lax.broadcasted_iota(jnp.int32, (1, LANES), 1)

        def step(j, carry, diag):
            acc, c0, c1, s0, s1 = carry
            cars, seen = (c0, c1), (s0, s1)
            mask = None if diag is None else row > col + diag
            off = pl.multiple_of(j * tk, tk)
            kj = k_ref[pl.ds(off, tk), :]
            vj = v_ref[pl.ds(off, tk), :]
            zs = [_dot_nt(qh[h], kj) for h in range(2)]
            lks = [_logistic_parts(z)[0] for z in zs]
            if mask is not None:
                lks = [jnp.where(mask, lk, 0.0) for lk in lks]
            sins = [_dot(_split2(lk), tri2) for lk in lks]
            ws = []
            for h in range(2):
                w = jnp.exp((lks[h] + zs[h]) + (sins[h] + cars[h]))
                ws.append((w if mask is None else jnp.where(mask, w, 0.0)).astype(BF))
            pvs = [_dot(w, vj) for w in ws]
            new_seen = [jnp.where(lane == j, cars[h], seen[h]) for h in range(2)]
            new_cars = [cars[h] + jnp.sum(lks[h], axis=1, keepdims=True) for h in range(2)]
            return acc + jnp.where(lo, pvs[0], pvs[1]), new_cars[0], new_cars[1], new_seen[0], new_seen[1]

        zc = jnp.zeros((tq, 1), F32)
        zt = jnp.zeros((tq, LANES), F32)
        carry = (zt, zc, zc, zt, zt)
        for m in reversed(range(ratio)):
            carry = step(ratio * i + m, carry, m * tk)
        carry = lax.fori_loop(0, ratio * i, lambda n, c: step(ratio * i - 1 - n, c, None), carry)
        acc = carry[0]
        o_ref[...] = acc
        m_ref[...] = _rms_out(acc, lo, g_ref[...]).astype(BF)
        c_ref[:, :LANES] = carry[3]
        c_ref[:, LANES:] = carry[4]

    car_spec = pl.BlockSpec((tq, 2 * LANES), lambda b, p, i: (b * nq + i, p))
    return pl.pallas_call(
        body, name="sb_fwd", grid=(bl, p_n, nq), in_specs=[q_spec, k_spec, v_spec, g_spec],
        out_specs=[tile_spec, tile_spec, car_spec],
        out_shape=[_sds((t, dg), F32), _sds((t, dg), BF), _sds((t, 2 * dg), F32)],
        compiler_params=_cparams(("parallel", "parallel", "parallel")),
    )(qkv, qkv, qkv, gain)


def _sb_bwd(qkv, cars, do, bl, seq, dg):
    t = bl * seq
    p_n = dg // LANES
    tq, tk = min(SB_TQ, seq), SB_TK
    ratio = tq // tk
    scale = 1.0 / math.sqrt(HEAD_DIM)
    nq, q_spec, k_spec, v_spec, tile_spec, seq_spec, _ = _sb_specs(bl, seq, p_n, 0, tq)
    car_spec = pl.BlockSpec((tq, 2 * LANES), lambda b, p, i: (b * nq + i, p))

    def body(q_ref, k_ref, v_ref, c_ref, do_ref, dq_ref, dk_ref, dv_ref, dk_acc, dv_acc):
        i = pl.program_id(2)

        @pl.when(i == 0)
        def _():
            dk_acc[...] = jnp.zeros_like(dk_acc)
            dv_acc[...] = jnp.zeros_like(dv_acc)

        row = lax.broadcasted_iota(jnp.int32, (tq, tk), 0)
        col = lax.broadcasted_iota(jnp.int32, (tq, tk), 1)
        krow = lax.broadcasted_iota(jnp.int32, (tk, tk), 0)
        kcol = lax.broadcasted_iota(jnp.int32, (tk, tk), 1)
        tri2 = _tri2(krow > kcol)
        tri_pre2 = _tri2(krow < kcol)
        lane = lax.broadcasted_iota(jnp.int32, (1, LANES), 1)
        sc = jnp.asarray(scale, BF)
        qs = q_ref[...] * sc
        dob = do_ref[...]
        lo, qh = _head_masks(qs)
        _, doh = _head_masks(dob)
        seen = (c_ref[:, :LANES], c_ref[:, LANES:])

        def step(j, carry, diag):
            dq, e0, e1 = carry
            ecars = (e0, e1)
            mask = None if diag is None else row > col + diag
            off = pl.multiple_of(j * tk, tk)
            kj = k_ref[pl.ds(off, tk), :]
            vj = v_ref[pl.ds(off, tk), :]
            zs = [_dot_nt(qh[h], kj) for h in range(2)]
            dws = [_dot_nt(doh[h], vj) for h in range(2)]
            parts = [_logistic_parts(z) for z in zs]
            lks = [p[0] if mask is None else jnp.where(mask, p[0], 0.0) for p in parts]
            sins = [_dot(_split2(lk), tri2) for lk in lks]
            ws, es = [], []
            for h in range(2):
                car = jnp.sum(jnp.where(lane == j, seen[h], 0.0), axis=1, keepdims=True)
                w = jnp.exp((lks[h] + zs[h]) + (sins[h] + car))
                w = w if mask is None else jnp.where(mask, w, 0.0)
                ws.append(w.astype(BF))
                es.append(dws[h] * w)
            pres = [_dot(_split2(e), tri_pre2) for e in es]
            dzs = []
            for h in range(2):
                t_, e = parts[h][1], es[h]
                before = pres[h] + ecars[h]
                dz = pl.reciprocal(parts[h][2], approx=True) * jnp.where(zs[h] > 0.0, e * t_ - before, e - t_ * before)
                dzs.append((dz if mask is None else jnp.where(mask, dz, 0.0)).astype(BF))
            kjs = kj * sc
            dqs = [_dot(dz, kjs) for dz in dzs]
            dks = [_dot_tn(dz, qs) for dz in dzs]
            dvs = [_dot_tn(w, dob) for w in ws]
            dk_acc[pl.ds(off, tk), :] += jnp.where(lo, dks[0], dks[1])
            dv_acc[pl.ds(off, tk), :] += jnp.where(lo, dvs[0], dvs[1])
            new_e = [ecars[h] + jnp.sum(es[h], axis=1, keepdims=True) for h in range(2)]
            return dq + jnp.where(lo, dqs[0], dqs[1]), new_e[0], new_e[1]

        zc = jnp.zeros((tq, 1), F32)
        carry = lax.fori_loop(0, ratio * i, lambda j, c: step(j, c, None), (jnp.zeros((tq, LANES), F32), zc, zc))
        for m in range(ratio):
            carry = step(ratio * i + m, carry, m * tk)
        dq_ref[...] = carry[0].astype(BF)

        @pl.when(i == nq - 1)
        def _():
            dk_ref[...] = dk_acc[...].astype(BF)
            dv_ref[...] = dv_acc[...].astype(BF)

    out = _sds((t, dg), BF)
    return pl.pallas_call(
        body, name="sb_bwd", grid=(bl, p_n, nq),
        in_specs=[q_spec, k_spec, v_spec, car_spec, tile_spec],
        out_specs=[tile_spec, seq_spec, seq_spec], out_shape=[out, out, out],
        scratch_shapes=[pltpu.VMEM((seq, LANES), F32), pltpu.VMEM((seq, LANES), F32)],
        compiler_params=_cparams(("parallel", "parallel", "arbitrary")),
    )(qkv, qkv, qkv, cars, do)


def _bias_expand(rb):
    h = rb.shape[0]
    pad = CHUNK * N_PREV_CHUNKS
    n_f = BAND + CHUNK - 1
    f = jnp.concatenate([rb[:, MAX_REL - (CHUNK - 1):2 * MAX_REL],
                         jnp.broadcast_to(rb[:, 2 * MAX_REL:], (h, pad + CHUNK - MAX_REL))], axis=1)
    fr = f[:, ::-1]
    skew = jnp.pad(jnp.broadcast_to(fr[:, None, :], (h, CHUNK, n_f)), ((0, 0), (0, 0), (0, 1)))
    skew = skew.reshape(h, CHUNK * (n_f + 1))[:, :CHUNK * n_f].reshape(h, CHUNK, n_f)
    table = skew[:, :, CHUNK - 1:CHUNK - 1 + BAND]
    n_c = ATT_TILE // CHUNK
    width = 3 * ATT_TILE
    rows = [jnp.pad(table, ((0, 0), (0, 0), (c * CHUNK, width - BAND - c * CHUNK)), constant_values=NEG_BIAS)
            for c in range(n_c)]
    return jnp.concatenate(rows, axis=1)


def _ca_scores(qh_h, k_ref, bias_ref, h, i, scale):
    tq = ATT_TILE
    ss, offs = [], []
    for kb in range(3):
        jb = i - 2 + kb
        off = pl.multiple_of(jnp.maximum(jb, 0) * tq, tq)
        s = _dot_nt(qh_h, k_ref[pl.ds(off, tq), :]) * scale + bias_ref[h, :, kb * tq:(kb + 1) * tq]
        ss.append(jnp.where(jb >= 0, s, NEG_BIAS))
        offs.append(off)
    m = jnp.maximum(jnp.maximum(jnp.max(ss[0], axis=1, keepdims=True), jnp.max(ss[1], axis=1, keepdims=True)),
                    jnp.max(ss[2], axis=1, keepdims=True))
    ps = [jnp.exp(s - m) for s in ss]
    den = ps[0].sum(axis=1, keepdims=True) + ps[1].sum(axis=1, keepdims=True) + ps[2].sum(axis=1, keepdims=True)
    inv = 1.0 / den
    return [p * inv for p in ps], offs


def _ca_fwd(qkv, bias4, bl, seq, dg, gain):
    t = bl * seq
    p_n = dg // LANES
    tq = ATT_TILE
    scale = 1.0 / math.sqrt(HEAD_DIM)
    nq, q_spec, k_spec, v_spec, tile_spec, _, g_spec = _sb_specs(bl, seq, p_n, 3 * p_n, tq)
    b_spec = pl.BlockSpec((2, tq, 3 * tq), lambda b, p, i: (p, 0, 0))

    def body(q_ref, k_ref, v_ref, bias_ref, g_ref, o_ref, m_ref):
        i = pl.program_id(2)
        lo, qh = _head_masks(q_ref[...])
        outs = []
        for h in range(2):
            ps, offs = _ca_scores(qh[h], k_ref, bias_ref, h, i, scale)
            acc = None
            for kb in range(3):
                pv = _dot(ps[kb].astype(BF), v_ref[pl.ds(offs[kb], tq), :])
                acc = pv if acc is None else acc + pv
            outs.append(acc)
        acc = jnp.where(lo, outs[0], outs[1])
        o_ref[...] = acc
        m_ref[...] = _rms_out(acc, lo, g_ref[...]).astype(BF)

    return pl.pallas_call(
        body, name="ca_fwd", grid=(bl, p_n, nq), in_specs=[q_spec, k_spec, v_spec, b_spec, g_spec],
        out_specs=[tile_spec, tile_spec], out_shape=[_sds((t, dg), F32), _sds((t, dg), BF)],
        compiler_params=_cparams(("parallel", "parallel", "parallel")),
    )(qkv, qkv, qkv, bias4, gain)


def _ca_bwd(qkv, bias4, o, do, bl, seq, dg):
    t = bl * seq
    p_n = dg // LANES
    tq = ATT_TILE
    scale = 1.0 / math.sqrt(HEAD_DIM)
    nq = seq // tq
    c0 = 3 * p_n
    q_spec = pl.BlockSpec((tq, LANES), lambda p, b, i: (b * nq + i, c0 + p))
    k_spec = pl.BlockSpec((seq, LANES), lambda p, b, i: (b, c0 + p_n + p))
    v_spec = pl.BlockSpec((seq, LANES), lambda p, b, i: (b, c0 + 2 * p_n + p))
    tile_spec = pl.BlockSpec((tq, LANES), lambda p, b, i: (b * nq + i, p))
    seq_spec = pl.BlockSpec((seq, LANES), lambda p, b, i: (b, p))
    b_spec = pl.BlockSpec((2, tq, 3 * tq), lambda p, b, i: (p, 0, 0))

    def body(q_ref, k_ref, v_ref, bias_ref, o_ref, do_ref, dq_ref, dk_ref, dv_ref, db_ref, dk_acc, dv_acc):
        b = pl.program_id(1)
        i = pl.program_id(2)

        @pl.when(i == 0)
        def _():
            dk_acc[...] = jnp.zeros_like(dk_acc)
            dv_acc[...] = jnp.zeros_like(dv_acc)

        @pl.when((i == 0) & (b == 0))
        def _():
            db_ref[...] = jnp.zeros_like(db_ref)

        q = q_ref[...]
        dob = do_ref[...]
        lo, qh = _head_masks(q)
        _, doh = _head_masks(dob)
        prod = dob.astype(F32) * o_ref[...]
        delta = (jnp.sum(jnp.where(lo, prod, 0.0), axis=1, keepdims=True),
                 jnp.sum(jnp.where(lo, 0.0, prod), axis=1, keepdims=True))
        dqs = []
        for h in range(2):
            ps, offs = _ca_scores(qh[h], k_ref, bias_ref, h, i, scale)
            dq = None
            dks, dvs = [], []
            for kb in range(3):
                dp = _dot_nt(doh[h], v_ref[pl.ds(offs[kb], tq), :])
                ds = ps[kb] * (dp - delta[h])
                db_ref[h, :, kb * tq:(kb + 1) * tq] += ds
                dsb = (ds * scale).astype(BF)
                part = _dot(dsb, k_ref[pl.ds(offs[kb], tq), :])
                dq = part if dq is None else dq + part
                dks.append(_dot_tn(dsb, q))
                dvs.append(_dot_tn(ps[kb].astype(BF), dob))
            dqs.append(dq)
            hm = lo if h == 0 else jnp.logical_not(lo)
            for kb in range(3):
                dk_acc[pl.ds(offs[kb], tq), :] += jnp.where(hm, dks[kb], 0.0)
                dv_acc[pl.ds(offs[kb], tq), :] += jnp.where(hm, dvs[kb], 0.0)
        dq_ref[...] = jnp.where(lo, dqs[0], dqs[1]).astype(BF)

        @pl.when(i == nq - 1)
        def _():
            dk_ref[...] = dk_acc[...].astype(BF)
            dv_ref[...] = dv_acc[...].astype(BF)

    out = _sds((t, dg), BF)
    return pl.pallas_call(
        body, name="ca_bwd", grid=(p_n, bl, nq),
        in_specs=[q_spec, k_spec, v_spec, b_spec, tile_spec, tile_spec],
        out_specs=[tile_spec, seq_spec, seq_spec, b_spec], out_shape=[out, out, out, _sds(bias4.shape, F32)],
        scratch_shapes=[pltpu.VMEM((seq, LANES), F32), pltpu.VMEM((seq, LANES), F32)],
        compiler_params=_cparams(("parallel", "arbitrary", "arbitrary")),
    )(qkv, qkv, qkv, bias4, o, do)


_ANY = pl.BlockSpec(memory_space=pl.ANY)


def _mesh_pos():
    x, y, c = lax.axis_index("x"), lax.axis_index("y"), lax.axis_index("c")
    chips = [(1 - x, y), (x, 1 - y), (1 - x, 1 - y)]
    return x, y, c, chips


def _half(ref_rows, c):
    return pl.ds(c * (ref_rows // 2), ref_rows // 2)


def _cast_into_slot(w, slot):
    l_n, r, c_n = w.shape
    tr = _tile(r, 512)

    def body(s_ref, w_ref, o_ref):
        o_ref[...] = w_ref[...].astype(BF)

    return pl.pallas_call(
        body, name="cast_into_slot", out_shape=_sds((l_n, 4, r, c_n), BF),
        grid_spec=pltpu.PrefetchScalarGridSpec(
            num_scalar_prefetch=1, grid=(l_n, r // tr),
            in_specs=[pl.BlockSpec((None, tr, c_n), lambda l, m, s: (l, m, 0))],
            out_specs=pl.BlockSpec((None, None, tr, c_n), lambda l, m, s: (l, s[0], m, 0))),
        compiler_params=_cparams(("parallel", "parallel")),
    )(slot, w)


def _gather_weights(bufs):
    nt = len(bufs)

    def body(*refs):
        outs = refs[nt:2 * nt]
        send_sems, recv_sems = refs[2 * nt:]
        x, y, c, chips = _mesh_pos()
        s_me = 2 * x + y

        def rows(t, cc):
            return _half(bufs[t].shape[2], cc)

        def copy(k, src, dst, to):
            return pltpu.make_async_remote_copy(src_ref=src, dst_ref=dst, send_sem=send_sems.at[k],
                                                recv_sem=recv_sems.at[k], device_id=to, device_id_type=MESH_IDS)

        first = []
        for j, (px, py) in enumerate(chips):
            for t in range(nt):
                blk = outs[t].at[:, s_me, rows(t, c)]
                first.append(copy(j * nt + t, blk, blk, (px, py, c)))
        for cp in first:
            cp.start()
        passed = []
        for j, (px, py) in enumerate(chips):
            s_p = 2 * px + py
            for t in range(nt):
                blk = outs[t].at[:, s_p, rows(t, c)]
                copy(j * nt + t, blk, blk, (px, py, c)).wait_recv()
                fwd = copy(3 * nt + j * nt + t, blk, blk, (x, y, 1 - c))
                fwd.start()
                passed.append(fwd)
        for j, (px, py) in enumerate(chips):
            s_p = 2 * px + py
            for t in range(nt):
                blk = outs[t].at[:, s_p, rows(t, 1 - c)]
                copy(3 * nt + j * nt + t, blk, blk, (x, y, 1 - c)).wait_recv()
        for cp in first + passed:
            cp.wait_send()

    return pl.pallas_call(
        body, name="gather_weights", in_specs=[_ANY] * nt, out_specs=[_ANY] * nt,
        out_shape=[_sds(b.shape, b.dtype) for b in bufs], input_output_aliases={t: t for t in range(nt)},
        scratch_shapes=[pltpu.SemaphoreType.DMA((6 * nt,)), pltpu.SemaphoreType.DMA((6 * nt,))],
    )(*bufs)


def _swap_halves(grads):
    nt = len(grads)

    def body(*refs):
        ins, theirs_o = refs[:nt], refs[nt:2 * nt]
        send_sems, recv_sems = refs[2 * nt:]
        x, y, c, _ = _mesh_pos()
        sends = [pltpu.make_async_remote_copy(
            src_ref=ins[t].at[:, :, _half(grads[t].shape[2], 1 - c)], dst_ref=theirs_o[t],
            send_sem=send_sems.at[t], recv_sem=recv_sems.at[t], device_id=(x, y, 1 - c), device_id_type=MESH_IDS)
            for t in range(nt)]
        for cp in sends:
            cp.start()
        for cp in sends:
            cp.wait_recv()
        for cp in sends:
            cp.wait_send()

    half = [_sds(g.shape[:2] + (g.shape[2] // 2, g.shape[3]), g.dtype) for g in grads]
    return pl.pallas_call(
        body, name="swap_halves", in_specs=[_ANY] * nt, out_specs=[_ANY] * nt, out_shape=half,
        scratch_shapes=[pltpu.SemaphoreType.DMA((nt,)), pltpu.SemaphoreType.DMA((nt,))],
    )(*grads)


def _add_pair(mine, theirs):
    l_n, s_n, r2, c_n = theirs.shape

    def body(a_ref, b_ref, o_ref):
        rows = pl.ds(pl.multiple_of(lax.axis_index("c") * r2, 16), r2)
        o_ref[...] = (a_ref[rows, :].astype(F32) + b_ref[...].astype(F32)).astype(BF)

    blk = (None, None, r2, c_n)
    return pl.pallas_call(
        body, name="add_pair", out_shape=_sds(theirs.shape, BF), grid=(l_n, s_n),
        in_specs=[pl.BlockSpec((None, None, 2 * r2, c_n), lambda l, s: (l, s, 0, 0)),
                  pl.BlockSpec(blk, lambda l, s: (l, s, 0, 0))],
        out_specs=pl.BlockSpec(blk, lambda l, s: (l, s, 0, 0)),
        compiler_params=_cparams(("parallel", "parallel")),
    )(mine, theirs)


def _scatter_chips(psums):
    nt = len(psums)

    def body(*refs):
        ins, got_o = refs[:nt], refs[nt:2 * nt]
        send_sems, recv_sems = refs[2 * nt:]
        x, y, c, chips = _mesh_pos()
        sends = []
        for j, (px, py) in enumerate(chips):
            for t in range(nt):
                sends.append(pltpu.make_async_remote_copy(
                    src_ref=ins[t].at[:, 2 * px + py], dst_ref=got_o[t].at[j],
                    send_sem=send_sems.at[j * nt + t], recv_sem=recv_sems.at[j * nt + t],
                    device_id=(px, py, c), device_id_type=MESH_IDS))
        for cp in sends:
            cp.start()
        for cp in sends:
            cp.wait_recv()
        for cp in sends:
            cp.wait_send()

    got = [_sds((3, p.shape[0]) + p.shape[2:], p.dtype) for p in psums]
    return pl.pallas_call(
        body, name="scatter_chips", in_specs=[_ANY] * nt, out_specs=[_ANY] * nt, out_shape=got,
        scratch_shapes=[pltpu.SemaphoreType.DMA((3 * nt,)), pltpu.SemaphoreType.DMA((3 * nt,))],
    )(*psums)


def _add_chips(psum, got):
    l_n, _, r2, c_n = psum.shape

    def body(a_ref, g_ref, o_ref):
        c = lax.axis_index("c")
        acc = a_ref[2 * lax.axis_index("x") + lax.axis_index("y")].astype(F32)
        for j in range(3):
            acc = acc + g_ref[j].astype(F32)
        o_ref[pl.ds(pl.multiple_of(c * r2, 8), r2), :] = acc
        o_ref[pl.ds(pl.multiple_of((1 - c) * r2, 8), r2), :] = jnp.zeros_like(acc)

    return pl.pallas_call(
        body, name="add_chips", out_shape=_sds((l_n, 2 * r2, c_n), F32), grid=(l_n,),
        in_specs=[pl.BlockSpec((None, 4, r2, c_n), lambda l: (l, 0, 0, 0)),
                  pl.BlockSpec((3, None, r2, c_n), lambda l: (0, l, 0, 0))],
        out_specs=pl.BlockSpec((None, 2 * r2, c_n), lambda l: (l, 0, 0)),
        compiler_params=_cparams(("parallel",)),
    )(psum, got)


def _join_halves(fulls):
    nt = len(fulls)

    def body(*refs):
        outs = refs[nt:2 * nt]
        send_sems, recv_sems = refs[2 * nt:]
        x, y, c, _ = _mesh_pos()

        def copy(t, cc):
            blk = outs[t].at[:, _half(fulls[t].shape[1], cc)]
            return pltpu.make_async_remote_copy(
                src_ref=blk, dst_ref=blk, send_sem=send_sems.at[t], recv_sem=recv_sems.at[t],
                device_id=(x, y, 1 - c), device_id_type=MESH_IDS)

        sends = [copy(t, c) for t in range(nt)]
        for cp in sends:
            cp.start()
        for t in range(nt):
            copy(t, 1 - c).wait_recv()
        for cp in sends:
            cp.wait_send()

    return pl.pallas_call(
        body, name="join_halves", in_specs=[_ANY] * nt, out_specs=[_ANY] * nt,
        out_shape=[_sds(f.shape, f.dtype) for f in fulls], input_output_aliases={t: t for t in range(nt)},
        scratch_shapes=[pltpu.SemaphoreType.DMA((nt,)), pltpu.SemaphoreType.DMA((nt,))],
    )(*fulls)


def _allreduce_small(part):
    r, d = part.shape
    flips = [(fx, fy, fc) for fx in (0, 1) for fy in (0, 1) for fc in (0, 1) if fx + fy + fc]

    def body(p_ref, o_ref, buf, send_sems, recv_sems):
        x, y, c, _ = _mesh_pos()
        me = 4 * x + 2 * y + c
        buf[me] = p_ref[...]
        sends = []
        for k, (fx, fy, fc) in enumerate(flips):
            sends.append(pltpu.make_async_remote_copy(
                src_ref=p_ref, dst_ref=buf.at[me], send_sem=send_sems.at[k], recv_sem=recv_sems.at[k],
                device_id=(x ^ fx, y ^ fy, c ^ fc), device_id_type=MESH_IDS))
        for cp in sends:
            cp.start()
        for k, (fx, fy, fc) in enumerate(flips):
            peer = 4 * (x ^ fx) + 2 * (y ^ fy) + (c ^ fc)
            pltpu.make_async_remote_copy(
                src_ref=p_ref, dst_ref=buf.at[peer], send_sem=send_sems.at[k], recv_sem=recv_sems.at[k],
                device_id=(x ^ fx, y ^ fy, c ^ fc), device_id_type=MESH_IDS).wait_recv()
        acc = buf[0]
        for dev in range(1, 8):
            acc = acc + buf[dev]
        o_ref[...] = acc
        for cp in sends:
            cp.wait_send()

    vmem = pl.BlockSpec(memory_space=pltpu.VMEM)
    return pl.pallas_call(
        body, name="allreduce_small", in_specs=[vmem], out_specs=vmem, out_shape=_sds((r, d), F32),
        scratch_shapes=[pltpu.VMEM((8, r, d), F32), pltpu.SemaphoreType.DMA((7,)), pltpu.SemaphoreType.DMA((7,))],
    )(part)


BIG = ("ffn1_w_gate", "ffn1_w_up", "ffn1_w_down", "w_in", "w_out", "ffn2_w_gate", "ffn2_w_up", "ffn2_w_down")
SMALL = ("ln1_g", "ln1_b", "rel_bias", "sb_out_g", "ca_out_g", "ln2_g", "ln2_b", "ln3_g", "ln3_b")
ORDER = ("ffn1_w_gate", "ffn1_w_up", "ffn1_w_down", "ln1_g", "ln1_b", "w_in", "rel_bias", "sb_out_g", "ca_out_g",
         "w_out", "ln2_g", "ln2_b", "ffn2_w_gate", "ffn2_w_up", "ffn2_w_down", "ln3_g", "ln3_b")


def _pack_small(vals, depth, d):
    rows = []
    for l in range(depth):
        for n in ("ln1_g", "ln1_b", "ln2_g", "ln2_b", "ln3_g", "ln3_b"):
            rows.append(vals[n][l][None, :])
        rows.append(jnp.concatenate([vals["sb_out_g"][l], vals["ca_out_g"][l]])[None, :])
        rb = vals["rel_bias"][l]
        rows.append(jnp.pad(rb, ((0, 0), (0, d - rb.shape[1]))))
    packed = jnp.concatenate(rows, axis=0)
    pad = (-packed.shape[0]) % 8
    return jnp.pad(packed, ((0, pad), (0, 0)))


def _unpack_small(packed, depth, d, heads, n_rel):
    per = 7 + heads
    out = {n: [] for n in SMALL}
    for l in range(depth):
        base = l * per
        for i, n in enumerate(("ln1_g", "ln1_b", "ln2_g", "ln2_b", "ln3_g", "ln3_b")):
            out[n].append(packed[base + i])
        out["sb_out_g"].append(packed[base + 6, :d // 2])
        out["ca_out_g"].append(packed[base + 6, d // 2:])
        out["rel_bias"].append(packed[base + 7:base + 7 + heads, :n_rel])
    return {n: jnp.stack(v) for n, v in out.items()}


def kernel(x, ffn1_w_gate, ffn1_w_up, ffn1_w_down, ln1_g, ln1_b, w_in, rel_bias, sb_out_g, ca_out_g, w_out, ln2_g, ln2_b, ffn2_w_gate, ffn2_w_up, ffn2_w_down, ln3_g, ln3_b, loss_target, m_ffn1_w_gate, m_ffn1_w_up, m_ffn1_w_down, m_ln1_g, m_ln1_b, m_w_in, m_rel_bias, m_sb_out_g, m_ca_out_g, m_w_out, m_ln2_g, m_ln2_b, m_ffn2_w_gate, m_ffn2_w_up, m_ffn2_w_down, m_ln3_g, m_ln3_b, v_ffn1_w_gate, v_ffn1_w_up, v_ffn1_w_down, v_ln1_g, v_ln1_b, v_w_in, v_rel_bias, v_sb_out_g, v_ca_out_g, v_w_out, v_ln2_g, v_ln2_b, v_ffn2_w_gate, v_ffn2_w_up, v_ffn2_w_down, v_ln3_g, v_ln3_b):
    w = dict(ffn1_w_gate=ffn1_w_gate, ffn1_w_up=ffn1_w_up, ffn1_w_down=ffn1_w_down, ln1_g=ln1_g, ln1_b=ln1_b, w_in=w_in, rel_bias=rel_bias, sb_out_g=sb_out_g, ca_out_g=ca_out_g, w_out=w_out, ln2_g=ln2_g, ln2_b=ln2_b, ffn2_w_gate=ffn2_w_gate, ffn2_w_up=ffn2_w_up, ffn2_w_down=ffn2_w_down, ln3_g=ln3_g, ln3_b=ln3_b)
    mom = dict(ffn1_w_gate=m_ffn1_w_gate, ffn1_w_up=m_ffn1_w_up, ffn1_w_down=m_ffn1_w_down, ln1_g=m_ln1_g, ln1_b=m_ln1_b, w_in=m_w_in, rel_bias=m_rel_bias, sb_out_g=m_sb_out_g, ca_out_g=m_ca_out_g, w_out=m_w_out, ln2_g=m_ln2_g, ln2_b=m_ln2_b, ffn2_w_gate=m_ffn2_w_gate, ffn2_w_up=m_ffn2_w_up, ffn2_w_down=m_ffn2_w_down, ln3_g=m_ln3_g, ln3_b=m_ln3_b)
    var = dict(ffn1_w_gate=v_ffn1_w_gate, ffn1_w_up=v_ffn1_w_up, ffn1_w_down=v_ffn1_w_down, ln1_g=v_ln1_g, ln1_b=v_ln1_b, w_in=v_w_in, rel_bias=v_rel_bias, sb_out_g=v_sb_out_g, ca_out_g=v_ca_out_g, w_out=v_w_out, ln2_g=v_ln2_g, ln2_b=v_ln2_b, ffn2_w_gate=v_ffn2_w_gate, ffn2_w_up=v_ffn2_w_up, ffn2_w_down=v_ffn2_w_down, ln3_g=v_ln3_g, ln3_b=v_ln3_b)

    bl, seq, d = x.shape
    t = bl * seq
    depth = w_in.shape[0]
    dg = d // 2
    heads = dg // HEAD_DIM
    alpha = (2 * depth) ** 0.25
    n_rel = rel_bias.shape[-1]

    chip = (2 * lax.axis_index("x") + lax.axis_index("y")).astype(jnp.int32).reshape(1)
    full = dict(zip(BIG, _gather_weights([_cast_into_slot(w[n], chip) for n in BIG])))

    xs = x.reshape(t, d)
    saved = []
    for l in range(depth):
        wg1, wu1, wd1 = full["ffn1_w_gate"][l], full["ffn1_w_up"][l], full["ffn1_w_down"][l]
        wg2, wu2, wd2 = full["ffn2_w_gate"][l], full["ffn2_w_up"][l], full["ffn2_w_down"][l]
        win = jnp.transpose(full["w_in"][l], (1, 0, 2)).reshape(d, 3 * d)
        wout = full["w_out"][l].reshape(2, dg, d)
        row = lambda a: a[l][None, :]
        g1, u1, h1 = _ffn_up(xs, wg1, wu1)
        x1, xh1, rs1 = _proj_ln("ffn_down_ln", [h1], [wd1], xs, row(ln1_g), row(ln1_b), FFN_RESIDUAL, alpha)
        qkv = _proj_nn("qkv_proj", x1, win, BF)
        bias4, bias_vjp = jax.vjp(_bias_expand, rel_bias[l])
        o_sb, m_sb, c_sb = _sb_fwd(qkv, bl, seq, dg, row(sb_out_g))
        o_ca, m_ca = _ca_fwd(qkv, bias4, bl, seq, dg, row(ca_out_g))
        x2, xh2, rs2 = _proj_ln("attn_out_ln", [m_sb, m_ca], [wout[0], wout[1]], x1, row(ln2_g), row(ln2_b), 1.0, alpha)
        g2, u2, h2 = _ffn_up(x2, wg2, wu2)
        x3, xh3, rs3 = _proj_ln("ffn_down_ln", [h2], [wd2], x2, row(ln3_g), row(ln3_b), FFN_RESIDUAL, alpha)
        saved.append(dict(xin=xs, g1=g1, u1=u1, h1=h1, x1=x1, xh1=xh1, rs1=rs1, qkv=qkv, bias4=bias4,
                          bias_vjp=bias_vjp, o_sb=o_sb, m_sb=m_sb, c_sb=c_sb, o_ca=o_ca, m_ca=m_ca, x2=x2, xh2=xh2, rs2=rs2,
                          g2=g2, u2=u2, h2=h2, xh3=xh3, rs3=rs3, wg1=wg1, wu1=wu1, wd1=wd1, wg2=wg2, wu2=wu2,
                          wd2=wd2, win=win, wout=wout))
        xs = x3

    dy, sq = _loss_grad(xs, loss_target.reshape(t, d))
    loss = lax.psum(0.5 * jnp.sum(sq) / d, ("x", "y", "c"))

    big_grads = {n: [None] * depth for n in BIG}
    small_grads = {n: [None] * depth for n in SMALL}
    s8 = lambda a: jnp.sum(a, axis=0)
    for l in reversed(range(depth)):
        sv = saved[l]
        row = lambda a: a[l][None, :]
        dr3, drs3, pg, pb = _ln_bwd(dy, sv["xh3"], sv["rs3"], row(ln3_g), FFN_RESIDUAL)
        small_grads["ln3_g"][l], small_grads["ln3_b"][l] = s8(pg), s8(pb)
        dgt, dut = _swiglu_bwd(drs3, sv["wd2"], sv["g2"], sv["u2"])
        dx2 = _proj_nt("ffn_dx", [dgt, dut], [sv["wg2"], sv["wu2"]], d, dr3, alpha, F32)
        big_grads["ffn2_w_down"][l] = _wgrad("wgrad_down", [sv["h2"]], [drs3], 4, sv["wd2"].shape[1:],
                                             lambda a, s: a[0][s], lambda b, s: b[0][...])
        big_grads["ffn2_w_gate"][l] = _wgrad("wgrad_up", [sv["x2"]], [dgt], 4, sv["wg2"].shape[1:],
                                             lambda a, s: a[0][...], lambda b, s: b[0][s])
        big_grads["ffn2_w_up"][l] = _wgrad("wgrad_up", [sv["x2"]], [dut], 4, sv["wu2"].shape[1:],
                                           lambda a, s: a[0][...], lambda b, s: b[0][s])
        dr2, drs2, pg, pb = _ln_bwd(dx2, sv["xh2"], sv["rs2"], row(ln2_g), 1.0)
        small_grads["ln2_g"][l], small_grads["ln2_b"][l] = s8(pg), s8(pb)
        dmix = _proj_nt("attn_dmix", [drs2], [sv["wout"].reshape(d, d)], d, None, alpha, F32)
        qd = dg // 2
        gw_out = _wgrad("wgrad_out", [sv["m_sb"], sv["m_ca"]], [drs2], 4, (qd, d),
                        lambda a, s: a[s // 2][:, (s % 2) * qd:(s % 2 + 1) * qd], lambda b, s: b[0][...])
        do_sb, pgs = _rms_bwd(dmix, 0, sv["o_sb"], row(sb_out_g))
        do_ca, pgc = _rms_bwd(dmix, 1, sv["o_ca"], row(ca_out_g))
        small_grads["sb_out_g"][l], small_grads["ca_out_g"][l] = s8(pgs), s8(pgc)
        dqa, dka, dva = _sb_bwd(sv["qkv"], sv["c_sb"], do_sb, bl, seq, dg)
        dqb, dkb, dvb, dbias4 = _ca_bwd(sv["qkv"], sv["bias4"], sv["o_ca"], do_ca, bl, seq, dg)
        small_grads["rel_bias"][l] = sv["bias_vjp"](dbias4)[0]
        dqkv = [dqa, dka, dva, dqb, dkb, dvb]
        win_cols = [(sv["win"], pl.BlockSpec((d, dg), functools.partial(lambda m, gi: (0, gi), gi=gi)))
                    for gi in range(6)]
        dx1 = _proj_nt("attn_dx", dqkv, win_cols, d, dr2, alpha, F32)
        gw_in = _wgrad("wgrad_in", [sv["x1"]], dqkv, 6, (d, dg), lambda a, s: a[0][...], lambda b, s: b[s][...])
        big_grads["w_in"][l] = jnp.transpose(
            jnp.transpose(gw_in, (1, 0, 2)).reshape(d, 4, 3 * d // 4), (1, 0, 2))
        dr1, drs1, pg, pb = _ln_bwd(dx1, sv["xh1"], sv["rs1"], row(ln1_g), FFN_RESIDUAL)
        small_grads["ln1_g"][l], small_grads["ln1_b"][l] = s8(pg), s8(pb)
        dgt, dut = _swiglu_bwd(drs1, sv["wd1"], sv["g1"], sv["u1"])
        dy = _proj_nt("ffn_dx", [dgt, dut], [sv["wg1"], sv["wu1"]], d, dr1, alpha, F32)
        big_grads["ffn1_w_down"][l] = _wgrad("wgrad_down", [sv["h1"]], [drs1], 4, sv["wd1"].shape[1:],
                                             lambda a, s: a[0][s], lambda b, s: b[0][...])
        big_grads["ffn1_w_gate"][l] = _wgrad("wgrad_up", [sv["xin"]], [dgt], 4, sv["wg1"].shape[1:],
                                             lambda a, s: a[0][...], lambda b, s: b[0][s])
        big_grads["ffn1_w_up"][l] = _wgrad("wgrad_up", [sv["xin"]], [dut], 4, sv["wu1"].shape[1:],
                                           lambda a, s: a[0][...], lambda b, s: b[0][s])
        big_grads["w_out"][l] = gw_out
    grad_x = dy.reshape(bl, seq, d)

    partial = [jnp.stack(big_grads[n]) for n in BIG]
    theirs = _swap_halves(partial)
    chip_sum = [_add_pair(a, b) for a, b in zip(partial, theirs)]
    got = _scatter_chips(chip_sum)
    grads = dict(zip(BIG, _join_halves([_add_chips(p, g) for p, g in zip(chip_sum, got)])))

    packed = _allreduce_small(_pack_small({n: jnp.stack(small_grads[n]) for n in SMALL}, depth, d))
    grads.update(_unpack_small(packed, depth, d, heads, n_rel))

    delta, new_m, new_v = {}, {}, {}
    for n in BIG:
        delta[n], new_m[n], new_v[n] = _adamw(w[n], grads[n], mom[n], var[n])
    pw, pm, pv = (_pack_small({n: src[n] for n in SMALL}, depth, d) for src in (w, mom, var))
    sd, sm, sv_ = _adamw(pw, packed, pm, pv)
    for dst, src in ((delta, sd), (new_m, sm), (new_v, sv_)):
        dst.update(_unpack_small(src, depth, d, heads, n_rel))

    return (loss, grad_x, *[grads[n] for n in ORDER], *[delta[n] for n in ORDER],
            *[new_m[n] for n in ORDER], *[new_v[n] for n in ORDER])
```

```python
import functools
import math

import jax
import jax.numpy as jnp
from jax import lax
from jax.experimental import pallas as pl
from jax.experimental.pallas import tpu as pltpu

F32 = jnp.float32
BF = jnp.bfloat16

CHUNK = 64
N_PREV_CHUNKS = 8
BAND = CHUNK * (N_PREV_CHUNKS + 1)
MAX_REL = 128
HEAD_DIM = 64
FFN_RESIDUAL = 0.5
LN_EPS = 1e-5
RMS_EPS = 1e-6
ADAM_LR = 0.001
ADAM_B1 = 0.9
ADAM_B2 = 0.999
ADAM_EPS = 1e-08
ADAM_WD = 0.01
ADAM_STEP = 10

LANES = 128
ATT_TILE = 4 * CHUNK
SB_TQ = 512
SB_TK = 256
SB_DEAD = -105.0
SB_UNSEEN = -1e30
NEG_BIAS = -1e30
VMEM_LIMIT = 56 * 1024 * 1024
ROW_TILE = 256
MESH_IDS = pl.DeviceIdType.MESH

_NT = (((1,), (1,)), ((), ()))
_TN = (((0,), (0,)), ((), ()))


def _dot(a, b):
    return jnp.dot(a, b, preferred_element_type=F32)


def _dot_nt(a, b):
    return lax.dot_general(a, b, _NT, preferred_element_type=F32)


def _dot_tn(a, b):
    return lax.dot_general(a, b, _TN, preferred_element_type=F32)


def _tile(n, pref):
    t = min(n, pref)
    while n % t:
        t //= 2
    return t


def _cparams(sem):
    return pltpu.CompilerParams(dimension_semantics=sem, vmem_limit_bytes=VMEM_LIMIT)


def _sds(shape, dtype):
    return jax.ShapeDtypeStruct(tuple(shape), dtype)


def _rows8(v):
    tm, c = v.shape
    return v.reshape(tm // 8, 8, c).sum(axis=0)


def _rowcall(name, body, n_rows, tm, row_ins, res_ins, row_outs, acc_outs=()):
    def rspec(shape):
        if len(shape) == 2:
            return pl.BlockSpec((tm, shape[1]), lambda m: (m, 0))
        return pl.BlockSpec((shape[0], tm, shape[2]), lambda m: (0, m, 0))

    def cspec(shape):
        zeros = (0,) * len(shape)
        return pl.BlockSpec(tuple(shape), lambda m: zeros)

    def split(items, mk):
        arrs, specs = [], []
        for it in items:
            if isinstance(it, tuple):
                arrs.append(it[0])
                specs.append(it[1])
            else:
                arrs.append(it)
                specs.append(mk(it.shape))
        return arrs, specs

    ra, rs = split(row_ins, rspec)
    ca, cs = split(res_ins, cspec)
    out_specs = [rspec(s.shape) for s in row_outs] + [cspec(s.shape) for s in acc_outs]
    sem = ("arbitrary",) if acc_outs else ("parallel",)
    return pl.pallas_call(
        body, name=name, grid=(n_rows // tm,), in_specs=rs + cs, out_specs=out_specs,
        out_shape=list(row_outs) + list(acc_outs), compiler_params=_cparams(sem),
    )(*ra, *ca)


def _ffn_up(x, wg, wu):
    t, _ = x.shape
    s_n, _, fs = wg.shape
    tm = _tile(t, ROW_TILE)

    def body(x_ref, wg_ref, wu_ref, g_ref, u_ref, h_ref):
        xb = x_ref[...].astype(BF)
        for s in range(s_n):
            g = _dot(xb, wg_ref[s])
            u = _dot(xb, wu_ref[s])
            g_ref[s] = g.astype(BF)
            u_ref[s] = u.astype(BF)
            h_ref[s] = (g * jax.nn.sigmoid(g) * u).astype(BF)

    out = _sds((s_n, t, fs), BF)
    return _rowcall("ffn_up", body, t, tm, [x], [wg, wu], [out, out, out])


def _proj_ln(name, a_list, b_list, xres, gain, bias, scale, alpha):
    t, d = xres.shape
    tm = _tile(t, ROW_TILE)
    na = len(a_list)
    a_nd = [a.ndim for a in a_list]
    a_s = [a.shape[0] for a in a_list]

    def body(*refs):
        a_refs = refs[:na]
        x_ref = refs[na]
        b_refs = refs[na + 1:2 * na + 1]
        g_ref, bb_ref = refs[2 * na + 1:2 * na + 3]
        y_ref, xh_ref, rs_ref = refs[2 * na + 3:]
        acc = None
        for i in range(na):
            if a_nd[i] == 3:
                parts = [_dot(a_refs[i][s].astype(BF), b_refs[i][s]) for s in range(a_s[i])]
            else:
                parts = [_dot(a_refs[i][...].astype(BF), b_refs[i][...])]
            for p in parts:
                acc = p if acc is None else acc + p
        if scale != 1.0:
            acc = scale * acc
        r = alpha * x_ref[...] + acc
        mu = jnp.mean(r, axis=-1, keepdims=True)
        xc = r - mu
        var = jnp.mean(xc * xc, axis=-1, keepdims=True)
        rstd = lax.rsqrt(var + LN_EPS)
        xh = xc * rstd
        y_ref[...] = xh * g_ref[...] + bb_ref[...]
        xh_ref[...] = xh
        rs_ref[...] = rstd

    outs = [_sds((t, d), F32), _sds((t, d), F32), _sds((t, 1), F32)]
    return _rowcall(name, body, t, tm, list(a_list) + [xres], list(b_list) + [gain, bias], outs)


def _proj_nn(name, x, w, out_dtype):
    t, _ = x.shape
    n = w.shape[1]
    tm = _tile(t, ROW_TILE)

    def body(x_ref, w_ref, o_ref):
        o_ref[...] = _dot(x_ref[...].astype(BF), w_ref[...]).astype(out_dtype)

    return _rowcall(name, body, t, tm, [x], [w], [_sds((t, n), out_dtype)])[0]


def _proj_nt(name, a_list, b_list, n_out, resid, alpha, out_dtype):
    first = a_list[0][0] if isinstance(a_list[0], tuple) else a_list[0]
    t = first.shape[-2]
    tm = _tile(t, ROW_TILE)
    na = len(a_list)
    a_nd = [(a[0] if isinstance(a, tuple) else a).ndim for a in a_list]
    a_s = [(a[0] if isinstance(a, tuple) else a).shape[0] for a in a_list]
    has_res = resid is not None

    def body(*refs):
        a_refs = refs[:na]
        pos = na
        if has_res:
            r_ref = refs[pos]
            pos += 1
        b_refs = refs[pos:pos + na]
        o_ref = refs[pos + na]
        acc = None
        for i in range(na):
            if a_nd[i] == 3:
                parts = [_dot_nt(a_refs[i][s].astype(BF), b_refs[i][s]) for s in range(a_s[i])]
            else:
                parts = [_dot_nt(a_refs[i][...].astype(BF), b_refs[i][...])]
            for p in parts:
                acc = p if acc is None else acc + p
        if has_res:
            acc = acc + alpha * r_ref[...]
        o_ref[...] = acc.astype(out_dtype)

    rows = list(a_list) + ([resid] if has_res else [])
    return _rowcall(name, body, t, tm, rows, list(b_list), [_sds((t, n_out), out_dtype)])[0]


def _ln_bwd(dy, xh, rstd, gain, scale):
    t, d = dy.shape
    tm = _tile(t, ROW_TILE)

    def body(dy_ref, xh_ref, rs_ref, g_ref, dr_ref, drs_ref, dg_ref, db_ref):
        m = pl.program_id(0)
        dyv = dy_ref[...]
        xhv = xh_ref[...]
        dxh = dyv * g_ref[...]
        m1 = jnp.mean(dxh, axis=-1, keepdims=True)
        m2 = jnp.mean(dxh * xhv, axis=-1, keepdims=True)
        dr = rs_ref[...] * (dxh - m1 - xhv * m2)
        dr_ref[...] = dr
        drs_ref[...] = (scale * dr).astype(BF)
        pg = _rows8(dyv * xhv)
        pb = _rows8(dyv)

        @pl.when(m == 0)
        def _():
            dg_ref[...] = pg
            db_ref[...] = pb

        @pl.when(m > 0)
        def _():
            dg_ref[...] += pg
            db_ref[...] += pb

    outs = [_sds((t, d), F32), _sds((t, d), BF)]
    accs = [_sds((8, d), F32), _sds((8, d), F32)]
    return _rowcall("ln_bwd", body, t, tm, [dy, xh, rstd], [gain], outs, accs)


def _swiglu_bwd(drs, wd, g, u):
    s_n, t, fs = g.shape
    tm = _tile(t, ROW_TILE)

    def body(d_ref, g_ref, u_ref, wd_ref, dg_ref, du_ref):
        dv = d_ref[...]
        for s in range(s_n):
            dh = _dot_nt(dv, wd_ref[s])
            gv = g_ref[s].astype(F32)
            uv = u_ref[s].astype(F32)
            sg = jax.nn.sigmoid(gv)
            dg_ref[s] = (dh * uv * (sg * (1.0 + gv * (1.0 - sg)))).astype(BF)
            du_ref[s] = (dh * (gv * sg)).astype(BF)

    out = _sds((s_n, t, fs), BF)
    return _rowcall("swiglu_bwd", body, t, tm, [drs, g, u], [wd], [out, out])


def _rms_bwd(dmix, half, o, gain):
    t, dh = o.shape
    tm = _tile(t, ROW_TILE)

    def body(dm_ref, o_ref, g_ref, do_ref, dg_ref):
        m = pl.program_id(0)
        lo = lax.broadcasted_iota(jnp.int32, (1, LANES), 1) < HEAD_DIM
        for gi in range(dh // LANES):
            sl = slice(gi * LANES, (gi + 1) * LANES)
            ov = o_ref[:, sl]
            dm = dm_ref[:, sl]
            ss = ov * ov
            s0 = jnp.sum(jnp.where(lo, ss, 0.0), axis=1, keepdims=True)
            s1 = jnp.sum(jnp.where(lo, 0.0, ss), axis=1, keepdims=True)
            r = jnp.where(lo, lax.rsqrt(s0 / HEAD_DIM + RMS_EPS), lax.rsqrt(s1 / HEAD_DIM + RMS_EPS))
            n = ov * r
            dn = dm * g_ref[:, sl]
            tt = dn * n
            t0 = jnp.sum(jnp.where(lo, tt, 0.0), axis=1, keepdims=True) / HEAD_DIM
            t1 = jnp.sum(jnp.where(lo, 0.0, tt), axis=1, keepdims=True) / HEAD_DIM
            do_ref[:, sl] = (r * (dn - n * jnp.where(lo, t0, t1))).astype(BF)
            part = _rows8(dm * n)

            @pl.when(m == 0)
            def _():
                dg_ref[:, sl] = part

            @pl.when(m > 0)
            def _():
                dg_ref[:, sl] += part

    dm_spec = pl.BlockSpec((tm, dh), lambda m: (m, half))
    return _rowcall("rms_bwd", body, t, tm, [(dmix, dm_spec), o], [gain], [_sds((t, dh), BF)], [_sds((8, dh), F32)])


def _loss_grad(y, target):
    t, d = y.shape
    tm = _tile(t, ROW_TILE)

    def body(y_ref, t_ref, dy_ref, acc_ref):
        m = pl.program_id(0)
        diff = y_ref[...] - t_ref[...]
        dy_ref[...] = diff / d
        part = _rows8(diff * diff)

        @pl.when(m == 0)
        def _():
            acc_ref[...] = part

        @pl.when(m > 0)
        def _():
            acc_ref[...] += part

    return _rowcall("loss_grad", body, t, tm, [y, target], [], [_sds((t, d), F32)], [_sds((8, d), F32)])


def _adamw(w, g, m, v):
    shape = w.shape
    c = shape[-1]
    rows = math.prod(shape[:-1])
    tm = _tile(rows, 512)

    def body(w_ref, g_ref, m_ref, v_ref, d_ref, nm_ref, nv_ref):
        gv = g_ref[...]
        nm = ADAM_B1 * m_ref[...] + (1.0 - ADAM_B1) * gv
        nv = ADAM_B2 * v_ref[...] + (1.0 - ADAM_B2) * (gv * gv)
        m_hat = nm / (1.0 - ADAM_B1 ** ADAM_STEP)
        v_hat = nv / (1.0 - ADAM_B2 ** ADAM_STEP)
        d_ref[...] = -ADAM_LR * (m_hat / (jnp.sqrt(v_hat) + ADAM_EPS) + ADAM_WD * w_ref[...])
        nm_ref[...] = nm
        nv_ref[...] = nv

    flat = [a.reshape(rows, c) for a in (w, g, m, v)]
    out = _sds((rows, c), F32)
    res = _rowcall("adamw", body, rows, tm, flat, [], [out, out, out])
    return tuple(r.reshape(shape) for r in res)


def _wgrad(name, a_list, b_list, out_sub, sub_shape, a_get, b_get):
    t = a_list[0].shape[-2]
    tk = _tile(t, 512)
    na, nb = len(a_list), len(b_list)
    nk = t // tk

    def tspec(shape):
        if len(shape) == 2:
            return pl.BlockSpec((tk, shape[1]), lambda k: (k, 0))
        return pl.BlockSpec((shape[0], tk, shape[2]), lambda k: (0, k, 0))

    def body(*refs):
        a_refs = refs[:na]
        b_refs = refs[na:na + nb]
        o_ref = refs[na + nb]
        acc_ref = refs[na + nb + 1]
        k = pl.program_id(0)

        @pl.when(k == 0)
        def _():
            acc_ref[...] = jnp.zeros_like(acc_ref)

        for s in range(out_sub):
            acc_ref[s] += _dot_tn(a_get(a_refs, s).astype(BF), b_get(b_refs, s).astype(BF))

        @pl.when(k == nk - 1)
        def _():
            o_ref[...] = acc_ref[...].astype(BF)

    full = (out_sub,) + tuple(sub_shape)
    return pl.pallas_call(
        body, name=name, grid=(nk,), in_specs=[tspec(a.shape) for a in a_list] + [tspec(b.shape) for b in b_list],
        out_specs=pl.BlockSpec(full, lambda k: (0, 0, 0)), out_shape=_sds(full, BF),
        scratch_shapes=[pltpu.VMEM(full, F32)], compiler_params=_cparams(("arbitrary",)),
    )(*a_list, *b_list)


def _head_masks(x):
    lo = lax.broadcasted_iota(jnp.int32, (1, LANES), 1) < HEAD_DIM
    zero = jnp.zeros_like(x)
    return lo, (jnp.where(lo, x, zero), jnp.where(lo, zero, x))


def _logistic_parts(z):
    nz = -z
    t = jnp.exp(jnp.minimum(z, nz))
    one_t = 1.0 + t
    return jnp.minimum(nz, 0.0) - jnp.log(one_t), t, one_t


def _split2(x):
    hi = x.astype(BF)
    lo = (x - hi.astype(F32)).astype(BF)
    return jnp.concatenate([hi, lo], axis=1)


def _tri2(cond):
    tri = jnp.where(cond, 1.0, 0.0).astype(BF)
    return jnp.concatenate([tri, tri], axis=0)


def _sb_specs(bl, seq, p_n, col0, tq):
    nq = seq // tq
    q_spec = pl.BlockSpec((tq, LANES), lambda b, p, i: (b * nq + i, col0 + p))
    k_spec = pl.BlockSpec((seq, LANES), lambda b, p, i: (b, col0 + p_n + p))
    v_spec = pl.BlockSpec((seq, LANES), lambda b, p, i: (b, col0 + 2 * p_n + p))
    tile_spec = pl.BlockSpec((tq, LANES), lambda b, p, i: (b * nq + i, p))
    seq_spec = pl.BlockSpec((seq, LANES), lambda b, p, i: (b, p))
    g_spec = pl.BlockSpec((1, LANES), lambda b, p, i: (0, p))
    return nq, q_spec, k_spec, v_spec, tile_spec, seq_spec, g_spec


def _rms_out(acc, lo, gain):
    ss = acc * acc
    s0 = jnp.sum(jnp.where(lo, ss, 0.0), axis=1, keepdims=True)
    s1 = jnp.sum(jnp.where(lo, 0.0, ss), axis=1, keepdims=True)
    r = jnp.where(lo, lax.rsqrt(s0 / HEAD_DIM + RMS_EPS), lax.rsqrt(s1 / HEAD_DIM + RMS_EPS))
    return acc * r * gain


def _sb_fwd(qkv, bl, seq, dg, gain):
    t = bl * seq
    p_n = dg // LANES
    tq, tk = min(SB_TQ, seq), SB_TK
    ratio = tq // tk
    scale = 1.0 / math.sqrt(HEAD_DIM)
    nq, q_spec, k_spec, v_spec, tile_spec, _, g_spec = _sb_specs(bl, seq, p_n, 0, tq)
    assert seq // tk <= LANES

    def body(q_ref, k_ref, v_ref, g_ref, o_ref, m_ref, c_ref):
        i = pl.program_id(2)
        row = lax.broadcasted_iota(jnp.int32, (tq, tk), 0)
        col = lax.broadcasted_iota(jnp.int32, (tq, tk), 1)
        krow = lax.broadcasted_iota(jnp.int32, (tk, tk), 0)
        kcol = lax.broadcasted_iota(jnp.int32, (tk, tk), 1)
        tri2 = _tri2(krow > kcol)
        lo, qh = _head_masks(q_ref[...] * jnp.asarray(scale, BF))
        lane = lax.broadcasted_iota(jnp.int32, (1, LANES), 1)

        def step(j, carry, diag):
            acc, c0, c1, s0, s1 = carry
            cars, seen = (c0, c1), (s0, s1)
            mask = None if diag is None else row > col + diag
            off = pl.multiple_of(j * tk, tk)
            kj = k_ref[pl.ds(off, tk), :]
            vj = v_ref[pl.ds(off, tk), :]
            zs = [_dot_nt(qh[h], kj) for h in range(2)]
            lks = [_logistic_parts(z)[0] for z in zs]
            if mask is not None:
                lks = [jnp.where(mask, lk, 0.0) for lk in lks]
            sins = [_dot(_split2(lk), tri2) for lk in lks]
            ws = []
            for h in range(2):
                w = jnp.exp((lks[h] + zs[h]) + (sins[h] + cars[h]))
                ws.append((w if mask is None else jnp.where(mask, w, 0.0)).astype(BF))
            pvs = [_dot(w, vj) for w in ws]
            new_seen = [jnp.where(lane == j, cars[h], seen[h]) for h in range(2)]
            new_cars = [cars[h] + jnp.sum(lks[h], axis=1, keepdims=True) for h in range(2)]
            return acc + jnp.where(lo, pvs[0], pvs[1]), new_cars[0], new_cars[1], new_seen[0], new_seen[1]

        zc = jnp.zeros((tq, 1), F32)
        zt = jnp.zeros((tq, LANES), F32)
        unseen = jnp.full((tq, LANES), SB_UNSEEN, F32)
        carry = (zt, zc, zc, unseen, unseen)
        for m in reversed(range(ratio)):
            carry = step(ratio * i + m, carry, m * tk)

        def alive(state):
            j, c = state
            return jnp.logical_and(j >= 0, jnp.max(jnp.maximum(c[1], c[2])) >= SB_DEAD)

        _, carry = lax.while_loop(alive, lambda st: (st[0] - 1, step(st[0], st[1], None)), (ratio * i - 1, carry))
        acc = carry[0]
        o_ref[...] = acc
        m_ref[...] = _rms_out(acc, lo, g_ref[...]).astype(BF)
        c_ref[:, :LANES] = carry[3]
        c_ref[:, LANES:] = carry[4]

    car_spec = pl.BlockSpec((tq, 2 * LANES), lambda b, p, i: (b * nq + i, p))
    return pl.pallas_call(
        body, name="sb_fwd", grid=(bl, p_n, nq), in_specs=[q_spec, k_spec, v_spec, g_spec],
        out_specs=[tile_spec, tile_spec, car_spec],
        out_shape=[_sds((t, dg), F32), _sds((t, dg), BF), _sds((t, 2 * dg), F32)],
        compiler_params=_cparams(("parallel", "parallel", "parallel")),
    )(qkv, qkv, qkv, gain)


def _sb_bwd(qkv, cars, do, bl, seq, dg):
    t = bl * seq
    p_n = dg // LANES
    tq, tk = min(SB_TQ, seq), SB_TK
    ratio = tq // tk
    scale = 1.0 / math.sqrt(HEAD_DIM)
    nq, q_spec, k_spec, v_spec, tile_spec, seq_spec, _ = _sb_specs(bl, seq, p_n, 0, tq)
    car_spec = pl.BlockSpec((tq, 2 * LANES), lambda b, p, i: (b * nq + i, p))

    def body(q_ref, k_ref, v_ref, c_ref, do_ref, dq_ref, dk_ref, dv_ref, dk_acc, dv_acc):
        i = pl.program_id(2)

        @pl.when(i == 0)
        def _():
            dk_acc[...] = jnp.zeros_like(dk_acc)
            dv_acc[...] = jnp.zeros_like(dv_acc)

        row = lax.broadcasted_iota(jnp.int32, (tq, tk), 0)
        col = lax.broadcasted_iota(jnp.int32, (tq, tk), 1)
        krow = lax.broadcasted_iota(jnp.int32, (tk, tk), 0)
        kcol = lax.broadcasted_iota(jnp.int32, (tk, tk), 1)
        tri2 = _tri2(krow > kcol)
        tri_pre2 = _tri2(krow < kcol)
        lane = lax.broadcasted_iota(jnp.int32, (1, LANES), 1)
        sc = jnp.asarray(scale, BF)
        qs = q_ref[...] * sc
        dob = do_ref[...]
        lo, qh = _head_masks(qs)
        _, doh = _head_masks(dob)
        seen = (c_ref[:, :LANES], c_ref[:, LANES:])

        def step(j, carry, diag):
            dq, e0, e1 = carry
            ecars = (e0, e1)
            mask = None if diag is None else row > col + diag
            off = pl.multiple_of(j * tk, tk)
            kj = k_ref[pl.ds(off, tk), :]
            vj = v_ref[pl.ds(off, tk), :]
            zs = [_dot_nt(qh[h], kj) for h in range(2)]
            dws = [_dot_nt(doh[h], vj) for h in range(2)]
            parts = [_logistic_parts(z) for z in zs]
            lks = [p[0] if mask is None else jnp.where(mask, p[0], 0.0) for p in parts]
            sins = [_dot(_split2(lk), tri2) for lk in lks]
            ws, es = [], []
            for h in range(2):
                car = jnp.sum(jnp.where(lane == j, seen[h], 0.0), axis=1, keepdims=True)
                w = jnp.exp((lks[h] + zs[h]) + (sins[h] + car))
                w = w if mask is None else jnp.where(mask, w, 0.0)
                ws.append(w.astype(BF))
                es.append(dws[h] * w)
            pres = [_dot(_split2(e), tri_pre2) for e in es]
            dzs = []
            for h in range(2):
                t_, e = parts[h][1], es[h]
                before = pres[h] + ecars[h]
                dz = pl.reciprocal(parts[h][2], approx=True) * jnp.where(zs[h] > 0.0, e * t_ - before, e - t_ * before)
                dzs.append((dz if mask is None else jnp.where(mask, dz, 0.0)).astype(BF))
            kjs = kj * sc
            dqs = [_dot(dz, kjs) for dz in dzs]
            dks = [_dot_tn(dz, qs) for dz in dzs]
            dvs = [_dot_tn(w, dob) for w in ws]
            dk_acc[pl.ds(off, tk), :] += jnp.where(lo, dks[0], dks[1])
            dv_acc[pl.ds(off, tk), :] += jnp.where(lo, dvs[0], dvs[1])
            new_e = [ecars[h] + jnp.sum(es[h], axis=1, keepdims=True) for h in range(2)]
            return dq + jnp.where(lo, dqs[0], dqs[1]), new_e[0], new_e[1]

        reach = jnp.max(jnp.maximum(seen[0], seen[1]), axis=0, keepdims=True)
        skipped = jnp.logical_and(reach < SB_DEAD, lane < ratio * i)
        first = jnp.minimum(jnp.sum(jnp.where(skipped, 1, 0)).astype(jnp.int32), ratio * i)
        zc = jnp.zeros((tq, 1), F32)
        carry = lax.fori_loop(first, ratio * i, lambda j, c: step(j, c, None), (jnp.zeros((tq, LANES), F32), zc, zc))
        for m in range(ratio):
            carry = step(ratio * i + m, carry, m * tk)
        dq_ref[...] = carry[0].astype(BF)

        @pl.when(i == nq - 1)
        def _():
            dk_ref[...] = dk_acc[...].astype(BF)
            dv_ref[...] = dv_acc[...].astype(BF)

    out = _sds((t, dg), BF)
    return pl.pallas_call(
        body, name="sb_bwd", grid=(bl, p_n, nq),
        in_specs=[q_spec, k_spec, v_spec, car_spec, tile_spec],
        out_specs=[tile_spec, seq_spec, seq_spec], out_shape=[out, out, out],
        scratch_shapes=[pltpu.VMEM((seq, LANES), F32), pltpu.VMEM((seq, LANES), F32)],
        compiler_params=_cparams(("parallel", "parallel", "arbitrary")),
    )(qkv, qkv, qkv, cars, do)


def _bias_expand(rb):
    h = rb.shape[0]
    pad = CHUNK * N_PREV_CHUNKS
    n_f = BAND + CHUNK - 1
    f = jnp.concatenate([rb[:, MAX_REL - (CHUNK - 1):2 * MAX_REL],
                         jnp.broadcast_to(rb[:, 2 * MAX_REL:], (h, pad + CHUNK - MAX_REL))], axis=1)
    fr = f[:, ::-1]
    skew = jnp.pad(jnp.broadcast_to(fr[:, None, :], (h, CHUNK, n_f)), ((0, 0), (0, 0), (0, 1)))
    skew = skew.reshape(h, CHUNK * (n_f + 1))[:, :CHUNK * n_f].reshape(h, CHUNK, n_f)
    table = skew[:, :, CHUNK - 1:CHUNK - 1 + BAND]
    n_c = ATT_TILE // CHUNK
    width = 3 * ATT_TILE
    rows = [jnp.pad(table, ((0, 0), (0, 0), (c * CHUNK, width - BAND - c * CHUNK)), constant_values=NEG_BIAS)
            for c in range(n_c)]
    return jnp.concatenate(rows, axis=1)


def _ca_scores(qh_h, k_ref, bias_ref, h, i, scale):
    tq = ATT_TILE
    ss, offs = [], []
    for kb in range(3):
        jb = i - 2 + kb
        off = pl.multiple_of(jnp.maximum(jb, 0) * tq, tq)
        s = _dot_nt(qh_h, k_ref[pl.ds(off, tq), :]) * scale + bias_ref[h, :, kb * tq:(kb + 1) * tq]
        ss.append(jnp.where(jb >= 0, s, NEG_BIAS))
        offs.append(off)
    m = jnp.maximum(jnp.maximum(jnp.max(ss[0], axis=1, keepdims=True), jnp.max(ss[1], axis=1, keepdims=True)),
                    jnp.max(ss[2], axis=1, keepdims=True))
    ps = [jnp.exp(s - m) for s in ss]
    den = ps[0].sum(axis=1, keepdims=True) + ps[1].sum(axis=1, keepdims=True) + ps[2].sum(axis=1, keepdims=True)
    inv = 1.0 / den
    return [p * inv for p in ps], offs


def _ca_fwd(qkv, bias4, bl, seq, dg, gain):
    t = bl * seq
    p_n = dg // LANES
    tq = ATT_TILE
    scale = 1.0 / math.sqrt(HEAD_DIM)
    nq, q_spec, k_spec, v_spec, tile_spec, _, g_spec = _sb_specs(bl, seq, p_n, 3 * p_n, tq)
    b_spec = pl.BlockSpec((2, tq, 3 * tq), lambda b, p, i: (p, 0, 0))

    def body(q_ref, k_ref, v_ref, bias_ref, g_ref, o_ref, m_ref):
        i = pl.program_id(2)
        lo, qh = _head_masks(q_ref[...])
        outs = []
        for h in range(2):
            ps, offs = _ca_scores(qh[h], k_ref, bias_ref, h, i, scale)
            acc = None
            for kb in range(3):
                pv = _dot(ps[kb].astype(BF), v_ref[pl.ds(offs[kb], tq), :])
                acc = pv if acc is None else acc + pv
            outs.append(acc)
        acc = jnp.where(lo, outs[0], outs[1])
        o_ref[...] = acc
        m_ref[...] = _rms_out(acc, lo, g_ref[...]).astype(BF)

    return pl.pallas_call(
        body, name="ca_fwd", grid=(bl, p_n, nq), in_specs=[q_spec, k_spec, v_spec, b_spec, g_spec],
        out_specs=[tile_spec, tile_spec], out_shape=[_sds((t, dg), F32), _sds((t, dg), BF)],
        compiler_params=_cparams(("parallel", "parallel", "parallel")),
    )(qkv, qkv, qkv, bias4, gain)


def _ca_bwd(qkv, bias4, o, do, bl, seq, dg):
    t = bl * seq
    p_n = dg // LANES
    tq = ATT_TILE
    scale = 1.0 / math.sqrt(HEAD_DIM)
    nq = seq // tq
    c0 = 3 * p_n
    q_spec = pl.BlockSpec((tq, LANES), lambda p, b, i: (b * nq + i, c0 + p))
    k_spec = pl.BlockSpec((seq, LANES), lambda p, b, i: (b, c0 + p_n + p))
    v_spec = pl.BlockSpec((seq, LANES), lambda p, b, i: (b, c0 + 2 * p_n + p))
    tile_spec = pl.BlockSpec((tq, LANES), lambda p, b, i: (b * nq + i, p))
    seq_spec = pl.BlockSpec((seq, LANES), lambda p, b, i: (b, p))
    b_spec = pl.BlockSpec((2, tq, 3 * tq), lambda p, b, i: (p, 0, 0))

    def body(q_ref, k_ref, v_ref, bias_ref, o_ref, do_ref, dq_ref, dk_ref, dv_ref, db_ref, dk_acc, dv_acc):
        b = pl.program_id(1)
        i = pl.program_id(2)

        @pl.when(i == 0)
        def _():
            dk_acc[...] = jnp.zeros_like(dk_acc)
            dv_acc[...] = jnp.zeros_like(dv_acc)

        @pl.when((i == 0) & (b == 0))
        def _():
            db_ref[...] = jnp.zeros_like(db_ref)

        q = q_ref[...]
        dob = do_ref[...]
        lo, qh = _head_masks(q)
        _, doh = _head_masks(dob)
        prod = dob.astype(F32) * o_ref[...]
        delta = (jnp.sum(jnp.where(lo, prod, 0.0), axis=1, keepdims=True),
                 jnp.sum(jnp.where(lo, 0.0, prod), axis=1, keepdims=True))
        dqs = []
        for h in range(2):
            ps, offs = _ca_scores(qh[h], k_ref, bias_ref, h, i, scale)
            dq = None
            dks, dvs = [], []
            for kb in range(3):
                dp = _dot_nt(doh[h], v_ref[pl.ds(offs[kb], tq), :])
                ds = ps[kb] * (dp - delta[h])
                db_ref[h, :, kb * tq:(kb + 1) * tq] += ds
                dsb = (ds * scale).astype(BF)
                part = _dot(dsb, k_ref[pl.ds(offs[kb], tq), :])
                dq = part if dq is None else dq + part
                dks.append(_dot_tn(dsb, q))
                dvs.append(_dot_tn(ps[kb].astype(BF), dob))
            dqs.append(dq)
            hm = lo if h == 0 else jnp.logical_not(lo)
            for kb in range(3):
                dk_acc[pl.ds(offs[kb], tq), :] += jnp.where(hm, dks[kb], 0.0)
                dv_acc[pl.ds(offs[kb], tq), :] += jnp.where(hm, dvs[kb], 0.0)
        dq_ref[...] = jnp.where(lo, dqs[0], dqs[1]).astype(BF)

        @pl.when(i == nq - 1)
        def _():
            dk_ref[...] = dk_acc[...].astype(BF)
            dv_ref[...] = dv_acc[...].astype(BF)

    out = _sds((t, dg), BF)
    return pl.pallas_call(
        body, name="ca_bwd", grid=(p_n, bl, nq),
        in_specs=[q_spec, k_spec, v_spec, b_spec, tile_spec, tile_spec],
        out_specs=[tile_spec, seq_spec, seq_spec, b_spec], out_shape=[out, out, out, _sds(bias4.shape, F32)],
        scratch_shapes=[pltpu.VMEM((seq, LANES), F32), pltpu.VMEM((seq, LANES), F32)],
        compiler_params=_cparams(("parallel", "arbitrary", "arbitrary")),
    )(qkv, qkv, qkv, bias4, o, do)


_ANY = pl.BlockSpec(memory_space=pl.ANY)


def _mesh_pos():
    x, y, c = lax.axis_index("x"), lax.axis_index("y"), lax.axis_index("c")
    chips = [(1 - x, y), (x, 1 - y), (1 - x, 1 - y)]
    return x, y, c, chips


def _half(ref_rows, c):
    return pl.ds(c * (ref_rows // 2), ref_rows // 2)


def _cast_into_slot(w, slot):
    l_n, r, c_n = w.shape
    tr = _tile(r, 512)

    def body(s_ref, w_ref, o_ref):
        o_ref[...] = w_ref[...].astype(BF)

    return pl.pallas_call(
        body, name="cast_into_slot", out_shape=_sds((l_n, 4, r, c_n), BF),
        grid_spec=pltpu.PrefetchScalarGridSpec(
            num_scalar_prefetch=1, grid=(l_n, r // tr),
            in_specs=[pl.BlockSpec((None, tr, c_n), lambda l, m, s: (l, m, 0))],
            out_specs=pl.BlockSpec((None, None, tr, c_n), lambda l, m, s: (l, s[0], m, 0))),
        compiler_params=_cparams(("parallel", "parallel")),
    )(slot, w)


def _gather_weights(bufs):
    nt = len(bufs)

    def body(*refs):
        outs = refs[nt:2 * nt]
        send_sems, recv_sems = refs[2 * nt:]
        x, y, c, chips = _mesh_pos()
        s_me = 2 * x + y

        def rows(t, cc):
            return _half(bufs[t].shape[2], cc)

        def copy(k, src, dst, to):
            return pltpu.make_async_remote_copy(src_ref=src, dst_ref=dst, send_sem=send_sems.at[k],
                                                recv_sem=recv_sems.at[k], device_id=to, device_id_type=MESH_IDS)

        first = []
        for j, (px, py) in enumerate(chips):
            for t in range(nt):
                blk = outs[t].at[:, s_me, rows(t, c)]
                first.append(copy(j * nt + t, blk, blk, (px, py, c)))
        for cp in first:
            cp.start()
        passed = []
        for j, (px, py) in enumerate(chips):
            s_p = 2 * px + py
            for t in range(nt):
                blk = outs[t].at[:, s_p, rows(t, c)]
                copy(j * nt + t, blk, blk, (px, py, c)).wait_recv()
                fwd = copy(3 * nt + j * nt + t, blk, blk, (x, y, 1 - c))
                fwd.start()
                passed.append(fwd)
        for j, (px, py) in enumerate(chips):
            s_p = 2 * px + py
            for t in range(nt):
                blk = outs[t].at[:, s_p, rows(t, 1 - c)]
                copy(3 * nt + j * nt + t, blk, blk, (x, y, 1 - c)).wait_recv()
        for cp in first + passed:
            cp.wait_send()

    return pl.pallas_call(
        body, name="gather_weights", in_specs=[_ANY] * nt, out_specs=[_ANY] * nt,
        out_shape=[_sds(b.shape, b.dtype) for b in bufs], input_output_aliases={t: t for t in range(nt)},
        scratch_shapes=[pltpu.SemaphoreType.DMA((6 * nt,)), pltpu.SemaphoreType.DMA((6 * nt,))],
    )(*bufs)


def _swap_halves(grads):
    nt = len(grads)

    def body(*refs):
        ins, theirs_o = refs[:nt], refs[nt:2 * nt]
        send_sems, recv_sems = refs[2 * nt:]
        x, y, c, _ = _mesh_pos()
        sends = [pltpu.make_async_remote_copy(
            src_ref=ins[t].at[:, :, _half(grads[t].shape[2], 1 - c)], dst_ref=theirs_o[t],
            send_sem=send_sems.at[t], recv_sem=recv_sems.at[t], device_id=(x, y, 1 - c), device_id_type=MESH_IDS)
            for t in range(nt)]
        for cp in sends:
            cp.start()
        for cp in sends:
            cp.wait_recv()
        for cp in sends:
            cp.wait_send()

    half = [_sds(g.shape[:2] + (g.shape[2] // 2, g.shape[3]), g.dtype) for g in grads]
    return pl.pallas_call(
        body, name="swap_halves", in_specs=[_ANY] * nt, out_specs=[_ANY] * nt, out_shape=half,
        scratch_shapes=[pltpu.SemaphoreType.DMA((nt,)), pltpu.SemaphoreType.DMA((nt,))],
    )(*grads)


def _add_pair(mine, theirs):
    l_n, s_n, r2, c_n = theirs.shape

    def body(a_ref, b_ref, o_ref):
        rows = pl.ds(pl.multiple_of(lax.axis_index("c") * r2, 16), r2)
        o_ref[...] = (a_ref[rows, :].astype(F32) + b_ref[...].astype(F32)).astype(BF)

    blk = (None, None, r2, c_n)
    return pl.pallas_call(
        body, name="add_pair", out_shape=_sds(theirs.shape, BF), grid=(l_n, s_n),
        in_specs=[pl.BlockSpec((None, None, 2 * r2, c_n), lambda l, s: (l, s, 0, 0)),
                  pl.BlockSpec(blk, lambda l, s: (l, s, 0, 0))],
        out_specs=pl.BlockSpec(blk, lambda l, s: (l, s, 0, 0)),
        compiler_params=_cparams(("parallel", "parallel")),
    )(mine, theirs)


def _scatter_chips(psums):
    nt = len(psums)

    def body(*refs):
        ins, got_o = refs[:nt], refs[nt:2 * nt]
        send_sems, recv_sems = refs[2 * nt:]
        x, y, c, chips = _mesh_pos()
        sends = []
        for j, (px, py) in enumerate(chips):
            for t in range(nt):
                sends.append(pltpu.make_async_remote_copy(
                    src_ref=ins[t].at[:, 2 * px + py], dst_ref=got_o[t].at[j],
                    send_sem=send_sems.at[j * nt + t], recv_sem=recv_sems.at[j * nt + t],
                    device_id=(px, py, c), device_id_type=MESH_IDS))
        for cp in sends:
            cp.start()
        for cp in sends:
            cp.wait_recv()
        for cp in sends:
            cp.wait_send()

    got = [_sds((3, p.shape[0]) + p.shape[2:], p.dtype) for p in psums]
    return pl.pallas_call(
        body, name="scatter_chips", in_specs=[_ANY] * nt, out_specs=[_ANY] * nt, out_shape=got,
        scratch_shapes=[pltpu.SemaphoreType.DMA((3 * nt,)), pltpu.SemaphoreType.DMA((3 * nt,))],
    )(*psums)


def _add_chips(psum, got):
    l_n, _, r2, c_n = psum.shape

    def body(a_ref, g_ref, o_ref):
        c = lax.axis_index("c")
        acc = a_ref[2 * lax.axis_index("x") + lax.axis_index("y")].astype(F32)
        for j in range(3):
            acc = acc + g_ref[j].astype(F32)
        o_ref[pl.ds(pl.multiple_of(c * r2, 8), r2), :] = acc
        o_ref[pl.ds(pl.multiple_of((1 - c) * r2, 8), r2), :] = jnp.zeros_like(acc)

    return pl.pallas_call(
        body, name="add_chips", out_shape=_sds((l_n, 2 * r2, c_n), F32), grid=(l_n,),
        in_specs=[pl.BlockSpec((None, 4, r2, c_n), lambda l: (l, 0, 0, 0)),
                  pl.BlockSpec((3, None, r2, c_n), lambda l: (0, l, 0, 0))],
        out_specs=pl.BlockSpec((None, 2 * r2, c_n), lambda l: (l, 0, 0)),
        compiler_params=_cparams(("parallel",)),
    )(psum, got)


def _join_halves(fulls):
    nt = len(fulls)

    def body(*refs):
        outs = refs[nt:2 * nt]
        send_sems, recv_sems = refs[2 * nt:]
        x, y, c, _ = _mesh_pos()

        def copy(t, cc):
            blk = outs[t].at[:, _half(fulls[t].shape[1], cc)]
            return pltpu.make_async_remote_copy(
                src_ref=blk, dst_ref=blk, send_sem=send_sems.at[t], recv_sem=recv_sems.at[t],
                device_id=(x, y, 1 - c), device_id_type=MESH_IDS)

        sends = [copy(t, c) for t in range(nt)]
        for cp in sends:
            cp.start()
        for t in range(nt):
            copy(t, 1 - c).wait_recv()
        for cp in sends:
            cp.wait_send()

    return pl.pallas_call(
        body, name="join_halves", in_specs=[_ANY] * nt, out_specs=[_ANY] * nt,
        out_shape=[_sds(f.shape, f.dtype) for f in fulls], input_output_aliases={t: t for t in range(nt)},
        scratch_shapes=[pltpu.SemaphoreType.DMA((nt,)), pltpu.SemaphoreType.DMA((nt,))],
    )(*fulls)


def _allreduce_small(part):
    r, d = part.shape
    flips = [(fx, fy, fc) for fx in (0, 1) for fy in (0, 1) for fc in (0, 1) if fx + fy + fc]

    def body(p_ref, o_ref, buf, send_sems, recv_sems):
        x, y, c, _ = _mesh_pos()
        me = 4 * x + 2 * y + c
        buf[me] = p_ref[...]
        sends = []
        for k, (fx, fy, fc) in enumerate(flips):
            sends.append(pltpu.make_async_remote_copy(
                src_ref=p_ref, dst_ref=buf.at[me], send_sem=send_sems.at[k], recv_sem=recv_sems.at[k],
                device_id=(x ^ fx, y ^ fy, c ^ fc), device_id_type=MESH_IDS))
        for cp in sends:
            cp.start()
        for k, (fx, fy, fc) in enumerate(flips):
            peer = 4 * (x ^ fx) + 2 * (y ^ fy) + (c ^ fc)
            pltpu.make_async_remote_copy(
                src_ref=p_ref, dst_ref=buf.at[peer], send_sem=send_sems.at[k], recv_sem=recv_sems.at[k],
                device_id=(x ^ fx, y ^ fy, c ^ fc), device_id_type=MESH_IDS).wait_recv()
        acc = buf[0]
        for dev in range(1, 8):
            acc = acc + buf[dev]
        o_ref[...] = acc
        for cp in sends:
            cp.wait_send()

    vmem = pl.BlockSpec(memory_space=pltpu.VMEM)
    return pl.pallas_call(
        body, name="allreduce_small", in_specs=[vmem], out_specs=vmem, out_shape=_sds((r, d), F32),
        scratch_shapes=[pltpu.VMEM((8, r, d), F32), pltpu.SemaphoreType.DMA((7,)), pltpu.SemaphoreType.DMA((7,))],
    )(part)


BIG = ("ffn1_w_gate", "ffn1_w_up", "ffn1_w_down", "w_in", "w_out", "ffn2_w_gate", "ffn2_w_up", "ffn2_w_down")
SMALL = ("ln1_g", "ln1_b", "rel_bias", "sb_out_g", "ca_out_g", "ln2_g", "ln2_b", "ln3_g", "ln3_b")
ORDER = ("ffn1_w_gate", "ffn1_w_up", "ffn1_w_down", "ln1_g", "ln1_b", "w_in", "rel_bias", "sb_out_g", "ca_out_g",
         "w_out", "ln2_g", "ln2_b", "ffn2_w_gate", "ffn2_w_up", "ffn2_w_down", "ln3_g", "ln3_b")


def _pack_small(vals, depth, d):
    rows = []
    for l in range(depth):
        for n in ("ln1_g", "ln1_b", "ln2_g", "ln2_b", "ln3_g", "ln3_b"):
            rows.append(vals[n][l][None, :])
        rows.append(jnp.concatenate([vals["sb_out_g"][l], vals["ca_out_g"][l]])[None, :])
        rb = vals["rel_bias"][l]
        rows.append(jnp.pad(rb, ((0, 0), (0, d - rb.shape[1]))))
    packed = jnp.concatenate(rows, axis=0)
    pad = (-packed.shape[0]) % 8
    return jnp.pad(packed, ((0, pad), (0, 0)))


def _unpack_small(packed, depth, d, heads, n_rel):
    per = 7 + heads
    out = {n: [] for n in SMALL}
    for l in range(depth):
        base = l * per
        for i, n in enumerate(("ln1_g", "ln1_b", "ln2_g", "ln2_b", "ln3_g", "ln3_b")):
            out[n].append(packed[base + i])
        out["sb_out_g"].append(packed[base + 6, :d // 2])
        out["ca_out_g"].append(packed[base + 6, d // 2:])
        out["rel_bias"].append(packed[base + 7:base + 7 + heads, :n_rel])
    return {n: jnp.stack(v) for n, v in out.items()}


def kernel(x, ffn1_w_gate, ffn1_w_up, ffn1_w_down, ln1_g, ln1_b, w_in, rel_bias, sb_out_g, ca_out_g, w_out, ln2_g, ln2_b, ffn2_w_gate, ffn2_w_up, ffn2_w_down, ln3_g, ln3_b, loss_target, m_ffn1_w_gate, m_ffn1_w_up, m_ffn1_w_down, m_ln1_g, m_ln1_b, m_w_in, m_rel_bias, m_sb_out_g, m_ca_out_g, m_w_out, m_ln2_g, m_ln2_b, m_ffn2_w_gate, m_ffn2_w_up, m_ffn2_w_down, m_ln3_g, m_ln3_b, v_ffn1_w_gate, v_ffn1_w_up, v_ffn1_w_down, v_ln1_g, v_ln1_b, v_w_in, v_rel_bias, v_sb_out_g, v_ca_out_g, v_w_out, v_ln2_g, v_ln2_b, v_ffn2_w_gate, v_ffn2_w_up, v_ffn2_w_down, v_ln3_g, v_ln3_b):
    w = dict(ffn1_w_gate=ffn1_w_gate, ffn1_w_up=ffn1_w_up, ffn1_w_down=ffn1_w_down, ln1_g=ln1_g, ln1_b=ln1_b, w_in=w_in, rel_bias=rel_bias, sb_out_g=sb_out_g, ca_out_g=ca_out_g, w_out=w_out, ln2_g=ln2_g, ln2_b=ln2_b, ffn2_w_gate=ffn2_w_gate, ffn2_w_up=ffn2_w_up, ffn2_w_down=ffn2_w_down, ln3_g=ln3_g, ln3_b=ln3_b)
    mom = dict(ffn1_w_gate=m_ffn1_w_gate, ffn1_w_up=m_ffn1_w_up, ffn1_w_down=m_ffn1_w_down, ln1_g=m_ln1_g, ln1_b=m_ln1_b, w_in=m_w_in, rel_bias=m_rel_bias, sb_out_g=m_sb_out_g, ca_out_g=m_ca_out_g, w_out=m_w_out, ln2_g=m_ln2_g, ln2_b=m_ln2_b, ffn2_w_gate=m_ffn2_w_gate, ffn2_w_up=m_ffn2_w_up, ffn2_w_down=m_ffn2_w_down, ln3_g=m_ln3_g, ln3_b=m_ln3_b)
    var = dict(ffn1_w_gate=v_ffn1_w_gate, ffn1_w_up=v_ffn1_w_up, ffn1_w_down=v_ffn1_w_down, ln1_g=v_ln1_g, ln1_b=v_ln1_b, w_in=v_w_in, rel_bias=v_rel_bias, sb_out_g=v_sb_out_g, ca_out_g=v_ca_out_g, w_out=v_w_out, ln2_g=v_ln2_g, ln2_b=v_ln2_b, ffn2_w_gate=v_ffn2_w_gate, ffn2_w_up=v_ffn2_w_up, ffn2_w_down=v_ffn2_w_down, ln3_g=v_ln3_g, ln3_b=v_ln3_b)

    bl, seq, d = x.shape
    t = bl * seq
    depth = w_in.shape[0]
    dg = d // 2
    heads = dg // HEAD_DIM
    alpha = (2 * depth) ** 0.25
    n_rel = rel_bias.shape[-1]

    chip = (2 * lax.axis_index("x") + lax.axis_index("y")).astype(jnp.int32).reshape(1)
    full = dict(zip(BIG, _gather_weights([_cast_into_slot(w[n], chip) for n in BIG])))

    xs = x.reshape(t, d)
    saved = []
    for l in range(depth):
        wg1, wu1, wd1 = full["ffn1_w_gate"][l], full["ffn1_w_up"][l], full["ffn1_w_down"][l]
        wg2, wu2, wd2 = full["ffn2_w_gate"][l], full["ffn2_w_up"][l], full["ffn2_w_down"][l]
        win = jnp.transpose(full["w_in"][l], (1, 0, 2)).reshape(d, 3 * d)
        wout = full["w_out"][l].reshape(2, dg, d)
        row = lambda a: a[l][None, :]
        g1, u1, h1 = _ffn_up(xs, wg1, wu1)
        x1, xh1, rs1 = _proj_ln("ffn_down_ln", [h1], [wd1], xs, row(ln1_g), row(ln1_b), FFN_RESIDUAL, alpha)
        qkv = _proj_nn("qkv_proj", x1, win, BF)
        bias4, bias_vjp = jax.vjp(_bias_expand, rel_bias[l])
        o_sb, m_sb, c_sb = _sb_fwd(qkv, bl, seq, dg, row(sb_out_g))
        o_ca, m_ca = _ca_fwd(qkv, bias4, bl, seq, dg, row(ca_out_g))
        x2, xh2, rs2 = _proj_ln("attn_out_ln", [m_sb, m_ca], [wout[0], wout[1]], x1, row(ln2_g), row(ln2_b), 1.0, alpha)
        g2, u2, h2 = _ffn_up(x2, wg2, wu2)
        x3, xh3, rs3 = _proj_ln("ffn_down_ln", [h2], [wd2], x2, row(ln3_g), row(ln3_b), FFN_RESIDUAL, alpha)
        saved.append(dict(xin=xs, g1=g1, u1=u1, h1=h1, x1=x1, xh1=xh1, rs1=rs1, qkv=qkv, bias4=bias4,
                          bias_vjp=bias_vjp, o_sb=o_sb, m_sb=m_sb, c_sb=c_sb, o_ca=o_ca, m_ca=m_ca, x2=x2, xh2=xh2, rs2=rs2,
                          g2=g2, u2=u2, h2=h2, xh3=xh3, rs3=rs3, wg1=wg1, wu1=wu1, wd1=wd1, wg2=wg2, wu2=wu2,
                          wd2=wd2, win=win, wout=wout))
        xs = x3

    dy, sq = _loss_grad(xs, loss_target.reshape(t, d))
    loss = lax.psum(0.5 * jnp.sum(sq) / d, ("x", "y", "c"))

    big_grads = {n: [None] * depth for n in BIG}
    small_grads = {n: [None] * depth for n in SMALL}
    s8 = lambda a: jnp.sum(a, axis=0)
    for l in reversed(range(depth)):
        sv = saved[l]
        row = lambda a: a[l][None, :]
        dr3, drs3, pg, pb = _ln_bwd(dy, sv["xh3"], sv["rs3"], row(ln3_g), FFN_RESIDUAL)
        small_grads["ln3_g"][l], small_grads["ln3_b"][l] = s8(pg), s8(pb)
        dgt, dut = _swiglu_bwd(drs3, sv["wd2"], sv["g2"], sv["u2"])
        dx2 = _proj_nt("ffn_dx", [dgt, dut], [sv["wg2"], sv["wu2"]], d, dr3, alpha, F32)
        big_grads["ffn2_w_down"][l] = _wgrad("wgrad_down", [sv["h2"]], [drs3], 4, sv["wd2"].shape[1:],
                                             lambda a, s: a[0][s], lambda b, s: b[0][...])
        big_grads["ffn2_w_gate"][l] = _wgrad("wgrad_up", [sv["x2"]], [dgt], 4, sv["wg2"].shape[1:],
                                             lambda a, s: a[0][...], lambda b, s: b[0][s])
        big_grads["ffn2_w_up"][l] = _wgrad("wgrad_up", [sv["x2"]], [dut], 4, sv["wu2"].shape[1:],
                                           lambda a, s: a[0][...], lambda b, s: b[0][s])
        dr2, drs2, pg, pb = _ln_bwd(dx2, sv["xh2"], sv["rs2"], row(ln2_g), 1.0)
        small_grads["ln2_g"][l], small_grads["ln2_b"][l] = s8(pg), s8(pb)
        dmix = _proj_nt("attn_dmix", [drs2], [sv["wout"].reshape(d, d)], d, None, alpha, F32)
        qd = dg // 2
        gw_out = _wgrad("wgrad_out", [sv["m_sb"], sv["m_ca"]], [drs2], 4, (qd, d),
                        lambda a, s: a[s // 2][:, (s % 2) * qd:(s % 2 + 1) * qd], lambda b, s: b[0][...])
        do_sb, pgs = _rms_bwd(dmix, 0, sv["o_sb"], row(sb_out_g))
        do_ca, pgc = _rms_bwd(dmix, 1, sv["o_ca"], row(ca_out_g))
        small_grads["sb_out_g"][l], small_grads["ca_out_g"][l] = s8(pgs), s8(pgc)
        dqa, dka, dva = _sb_bwd(sv["qkv"], sv["c_sb"], do_sb, bl, seq, dg)
        dqb, dkb, dvb, dbias4 = _ca_bwd(sv["qkv"], sv["bias4"], sv["o_ca"], do_ca, bl, seq, dg)
        small_grads["rel_bias"][l] = sv["bias_vjp"](dbias4)[0]
        dqkv = [dqa, dka, dva, dqb, dkb, dvb]
        win_cols = [(sv["win"], pl.BlockSpec((d, dg), functools.partial(lambda m, gi: (0, gi), gi=gi)))
                    for gi in range(6)]
        dx1 = _proj_nt("attn_dx", dqkv, win_cols, d, dr2, alpha, F32)
        gw_in = _wgrad("wgrad_in", [sv["x1"]], dqkv, 6, (d, dg), lambda a, s: a[0][...], lambda b, s: b[s][...])
        big_grads["w_in"][l] = jnp.transpose(
            jnp.transpose(gw_in, (1, 0, 2)).reshape(d, 4, 3 * d // 4), (1, 0, 2))
        dr1, drs1, pg, pb = _ln_bwd(dx1, sv["xh1"], sv["rs1"], row(ln1_g), FFN_RESIDUAL)
        small_grads["ln1_g"][l], small_grads["ln1_b"][l] = s8(pg), s8(pb)
        dgt, dut = _swiglu_bwd(drs1, sv["wd1"], sv["g1"], sv["u1"])
        dy = _proj_nt("ffn_dx", [dgt, dut], [sv["wg1"], sv["wu1"]], d, dr1, alpha, F32)
        big_grads["ffn1_w_down"][l] = _wgrad("wgrad_down", [sv["h1"]], [drs1], 4, sv["wd1"].shape[1:],
                                             lambda a, s: a[0][s], lambda b, s: b[0][...])
        big_grads["ffn1_w_gate"][l] = _wgrad("wgrad_up", [sv["xin"]], [dgt], 4, sv["wg1"].shape[1:],
                                             lambda a, s: a[0][...], lambda b, s: b[0][s])
        big_grads["ffn1_w_up"][l] = _wgrad("wgrad_up", [sv["xin"]], [dut], 4, sv["wu1"].shape[1:],
                                           lambda a, s: a[0][...], lambda b, s: b[0][s])
        big_grads["w_out"][l] = gw_out
    grad_x = dy.reshape(bl, seq, d)

    partial = [jnp.stack(big_grads[n]) for n in BIG]
    theirs = _swap_halves(partial)
    chip_sum = [_add_pair(a, b) for a, b in zip(partial, theirs)]
    got = _scatter_chips(chip_sum)
    grads = dict(zip(BIG, _join_halves([_add_chips(p, g) for p, g in zip(chip_sum, got)])))

    packed = _allreduce_small(_pack_small({n: jnp.stack(small_grads[n]) for n in SMALL}, depth, d))
    grads.update(_unpack_small(packed, depth, d, heads, n_rel))

    delta, new_m, new_v = {}, {}, {}
    for n in BIG:
        delta[n], new_m[n], new_v[n] = _adamw(w[n], grads[n], mom[n], var[n])
    pw, pm, pv = (_pack_small({n: src[n] for n in SMALL}, depth, d) for src in (w, mom, var))
    sd, sm, sv_ = _adamw(pw, packed, pm, pv)
    for dst, src in ((delta, sd), (new_m, sm), (new_v, sv_)):
        dst.update(_unpack_small(src, depth, d, heads, n_rel))

    return (loss, grad_x, *[grads[n] for n in ORDER], *[delta[n] for n in ORDER],
            *[new_m[n] for n in ORDER], *[new_v[n] for n in ORDER])
```

```python
import functools
import math

import jax
import jax.numpy as jnp
from jax import lax
from jax.experimental import pallas as pl
from jax.experimental.pallas import tpu as pltpu

F32 = jnp.float32
BF = jnp.bfloat16

CHUNK = 64
N_PREV_CHUNKS = 8
BAND = CHUNK * (N_PREV_CHUNKS + 1)
MAX_REL = 128
HEAD_DIM = 64
FFN_RESIDUAL = 0.5
LN_EPS = 1e-5
RMS_EPS = 1e-6
ADAM_LR = 0.001
ADAM_B1 = 0.9
ADAM_B2 = 0.999
ADAM_EPS = 1e-08
ADAM_WD = 0.01
ADAM_STEP = 10

LANES = 128
ATT_TILE = 4 * CHUNK
SB_TQ = 512
SB_TK = 256
SB_DEAD = -105.0
SB_UNSEEN = -1e30
NEG_BIAS = -1e30
VMEM_LIMIT = 56 * 1024 * 1024
ROW_TILE = 512
MESH_IDS = pl.DeviceIdType.MESH

_NT = (((1,), (1,)), ((), ()))
_TN = (((0,), (0,)), ((), ()))


def _dot(a, b):
    return jnp.dot(a, b, preferred_element_type=F32)


def _dot_nt(a, b):
    return lax.dot_general(a, b, _NT, preferred_element_type=F32)


def _dot_tn(a, b):
    return lax.dot_general(a, b, _TN, preferred_element_type=F32)


def _tile(n, pref):
    t = min(n, pref)
    while n % t:
        t //= 2
    return t


def _cparams(sem):
    return pltpu.CompilerParams(dimension_semantics=sem, vmem_limit_bytes=VMEM_LIMIT)


def _sds(shape, dtype):
    return jax.ShapeDtypeStruct(tuple(shape), dtype)


def _rows8(v):
    tm, c = v.shape
    return v.reshape(tm // 8, 8, c).sum(axis=0)


class _LayerOf(tuple):
    pass


def _lshape(item):
    return item[0].shape[1:] if isinstance(item, _LayerOf) else item.shape


def _accumulate(m, ref, part):
    @pl.when(m == 0)
    def _():
        ref[...] = part

    @pl.when(m > 0)
    def _():
        ref[...] += part


def _ln_bwd_tail(m, dy, xh_ref, rs_ref, g_ref, scale, dr_ref, drs_ref, dg_ref, db_ref):
    xhv = xh_ref[...]
    dxh = dy * g_ref[...]
    m1 = jnp.mean(dxh, axis=-1, keepdims=True)
    m2 = jnp.mean(dxh * xhv, axis=-1, keepdims=True)
    dr = rs_ref[...] * (dxh - m1 - xhv * m2)
    dr_ref[...] = dr
    drs_ref[...] = (scale * dr).astype(BF)
    _accumulate(m, dg_ref, _rows8(dy * xhv))
    _accumulate(m, db_ref, _rows8(dy))


def _rowcall(name, body, n_rows, tm, row_ins, res_ins, row_outs, acc_outs=()):
    def rspec(shape):
        if len(shape) == 2:
            return pl.BlockSpec((tm, shape[1]), lambda m: (m, 0))
        return pl.BlockSpec((shape[0], tm, shape[2]), lambda m: (0, m, 0))

    def cspec(shape, single=False):
        zeros = (0,) * len(shape)
        mode = pl.Buffered(1) if single else None
        return pl.BlockSpec(tuple(shape), lambda m: zeros, pipeline_mode=mode)

    def split(items, mk):
        arrs, specs = [], []
        for it in items:
            if isinstance(it, _LayerOf):
                arr, l = it
                tail = (0,) * (arr.ndim - 1)
                arrs.append(arr)
                specs.append(pl.BlockSpec((None,) + arr.shape[1:], functools.partial(lambda m, l, tail: (l,) + tail, l=l, tail=tail),
                                          pipeline_mode=pl.Buffered(1)))
            elif isinstance(it, tuple):
                arrs.append(it[0])
                specs.append(it[1])
            else:
                arrs.append(it)
                specs.append(mk(it.shape))
        return arrs, specs

    ra, rs = split(row_ins, rspec)
    ca, cs = split(res_ins, functools.partial(cspec, single=True))
    out_specs = [rspec(s.shape) for s in row_outs] + [cspec(s.shape) for s in acc_outs]
    sem = ("arbitrary",) if acc_outs else ("parallel",)
    return pl.pallas_call(
        body, name=name, grid=(n_rows // tm,), in_specs=rs + cs, out_specs=out_specs,
        out_shape=list(row_outs) + list(acc_outs), compiler_params=_cparams(sem),
    )(*ra, *ca)


def _ffn_up(x, wg, wu):
    t, _ = x.shape
    s_n, _, fs = _lshape(wg)
    tm = _tile(t, ROW_TILE)

    def body(x_ref, wg_ref, wu_ref, g_ref, u_ref, h_ref):
        xb = x_ref[...].astype(BF)
        for s in range(s_n):
            g = _dot(xb, wg_ref[s])
            u = _dot(xb, wu_ref[s])
            g_ref[s] = g.astype(BF)
            u_ref[s] = u.astype(BF)
            h_ref[s] = (g * jax.nn.sigmoid(g) * u).astype(BF)

    out = _sds((s_n, t, fs), BF)
    return _rowcall("ffn_up", body, t, tm, [x], [wg, wu], [out, out, out])


def _proj_ln(name, a_list, w, terms, xres, gain, bias, scale, alpha):
    t, d = xres.shape
    tm = _tile(t, ROW_TILE)
    na = len(a_list)

    def body(*refs):
        a_refs = refs[:na]
        x_ref, w_ref, g_ref, bb_ref = refs[na:na + 4]
        y_ref, xh_ref, rs_ref = refs[na + 4:]
        acc = None
        for i, s, cols, sw in terms:
            a = a_refs[i][s] if cols is None else a_refs[i][:, cols[0]:cols[1]]
            p = _dot(a.astype(BF), w_ref[sw])
            acc = p if acc is None else acc + p
        if scale != 1.0:
            acc = scale * acc
        r = alpha * x_ref[...] + acc
        mu = jnp.mean(r, axis=-1, keepdims=True)
        xc = r - mu
        var = jnp.mean(xc * xc, axis=-1, keepdims=True)
        rstd = lax.rsqrt(var + LN_EPS)
        xh = xc * rstd
        y_ref[...] = xh * g_ref[...] + bb_ref[...]
        xh_ref[...] = xh
        rs_ref[...] = rstd

    outs = [_sds((t, d), F32), _sds((t, d), F32), _sds((t, 1), F32)]
    return _rowcall(name, body, t, tm, list(a_list) + [xres], [w, gain, bias], outs)


def _shard_cols(groups, dg, s, width):
    pieces, lo = [], s * width
    while lo < (s + 1) * width:
        gi = lo // dg
        hi = min((gi + 1) * dg, (s + 1) * width)
        pieces.append((gi, lo - gi * dg, hi - gi * dg))
        lo = hi
    assert all(p[0] < groups for p in pieces)
    return pieces


def _qkv_proj(x, win):
    t, _ = x.shape
    s_n, _, wd_ = _lshape(win)
    tm = _tile(t, ROW_TILE)

    def body(x_ref, w_ref, o_ref):
        xb = x_ref[...].astype(BF)
        for s in range(s_n):
            o_ref[:, s * wd_:(s + 1) * wd_] = _dot(xb, w_ref[s]).astype(BF)

    return _rowcall("qkv_proj", body, t, tm, [x], [win], [_sds((t, s_n * wd_), BF)])[0]


def _dx_call(name, rows, res, matmul, resid, alpha, ln):
    t, d = resid.shape
    tm = _tile(t, ROW_TILE)
    nr, nc = len(rows), len(res)

    def body(*refs):
        r_refs, resid_ref = refs[:nr], refs[nr]
        if ln is None:
            c_refs, outs = refs[nr + 1:nr + 1 + nc], refs[nr + 1 + nc:]
            outs[0][...] = matmul(r_refs, c_refs) + alpha * resid_ref[...]
        else:
            xh_ref, rs_ref = refs[nr + 1:nr + 3]
            c_refs, g_ref = refs[nr + 3:nr + 3 + nc], refs[nr + 3 + nc]
            outs = refs[nr + 4 + nc:]
            dx = matmul(r_refs, c_refs) + alpha * resid_ref[...]
            _ln_bwd_tail(pl.program_id(0), dx, xh_ref, rs_ref, g_ref, ln[3], *outs)

    if ln is None:
        return _rowcall(name, body, t, tm, list(rows) + [resid], list(res), [_sds((t, d), F32)])[0]
    outs = [_sds((t, d), F32), _sds((t, d), BF)]
    accs = [_sds((8, d), F32), _sds((8, d), F32)]
    return _rowcall(name, body, t, tm, list(rows) + [resid, ln[0], ln[1]], list(res) + [ln[2]], outs, accs)


def _ffn_dx(dg_, du_, wg, wu, resid, alpha, ln):
    s_n = dg_.shape[0]

    def matmul(r, c):
        acc = None
        for i in range(2):
            for s in range(s_n):
                p = _dot_nt(r[i][s], c[i][s])
                acc = p if acc is None else acc + p
        return acc

    return _dx_call("ffn_dx", [dg_, du_], [wg, wu], matmul, resid, alpha, ln)


def _attn_dx(dqkv, win, resid, alpha, ln):
    s_n, _, wd_ = _lshape(win)
    dg = dqkv[0].shape[1]
    pieces = [_shard_cols(len(dqkv), dg, s, wd_) for s in range(s_n)]

    def matmul(r, c):
        acc = None
        for s in range(s_n):
            a = jnp.concatenate([r[gi][:, lo:hi] for gi, lo, hi in pieces[s]], axis=1)
            p = _dot_nt(a, c[0][s])
            acc = p if acc is None else acc + p
        return acc

    return _dx_call("attn_dx", list(dqkv), [win], matmul, resid, alpha, ln)


def _head_rms_bwd(dm, ov, gain, lo):
    ss = ov * ov
    s0 = jnp.sum(jnp.where(lo, ss, 0.0), axis=1, keepdims=True)
    s1 = jnp.sum(jnp.where(lo, 0.0, ss), axis=1, keepdims=True)
    r = jnp.where(lo, lax.rsqrt(s0 / HEAD_DIM + RMS_EPS), lax.rsqrt(s1 / HEAD_DIM + RMS_EPS))
    n = ov * r
    dn = dm * gain
    tt = dn * n
    t0 = jnp.sum(jnp.where(lo, tt, 0.0), axis=1, keepdims=True) / HEAD_DIM
    t1 = jnp.sum(jnp.where(lo, 0.0, tt), axis=1, keepdims=True) / HEAD_DIM
    return r * (dn - n * jnp.where(lo, t0, t1)), dm * n


def _attn_dmix(drs, wout, o_sb, o_ca, g_sb, g_ca):
    t, dg = o_sb.shape
    s_n, qd, _ = _lshape(wout)
    tm = _tile(t, ROW_TILE)
    per = dg // qd

    def body(d_ref, osb_ref, oca_ref, w_ref, gs_ref, gc_ref, dsb_ref, dca_ref, pgs_ref, pgc_ref):
        m = pl.program_id(0)
        dv = d_ref[...]
        lo = lax.broadcasted_iota(jnp.int32, (1, LANES), 1) < HEAD_DIM
        for grp, (o_ref, g_ref, do_ref, pg_ref) in enumerate(((osb_ref, gs_ref, dsb_ref, pgs_ref),
                                                               (oca_ref, gc_ref, dca_ref, pgc_ref))):
            dm = jnp.concatenate([_dot_nt(dv, w_ref[grp * per + k]) for k in range(per)], axis=1)
            parts = []
            for gi in range(dg // LANES):
                sl = slice(gi * LANES, (gi + 1) * LANES)
                do, pg = _head_rms_bwd(dm[:, sl], o_ref[:, sl], g_ref[:, sl], lo)
                do_ref[:, sl] = do.astype(BF)
                parts.append(_rows8(pg))
            _accumulate(m, pg_ref, jnp.concatenate(parts, axis=1))

    outs = [_sds((t, dg), BF), _sds((t, dg), BF)]
    accs = [_sds((8, dg), F32), _sds((8, dg), F32)]
    return _rowcall("attn_dmix", body, t, tm, [drs, o_sb, o_ca], [wout, g_sb, g_ca], outs, accs)


def _swiglu_bwd(drs, wd, g, u):
    s_n, t, fs = g.shape
    tm = _tile(t, ROW_TILE)

    def body(d_ref, g_ref, u_ref, wd_ref, dg_ref, du_ref):
        dv = d_ref[...]
        for s in range(s_n):
            dh = _dot_nt(dv, wd_ref[s])
            gv = g_ref[s].astype(F32)
            uv = u_ref[s].astype(F32)
            sg = jax.nn.sigmoid(gv)
            dg_ref[s] = (dh * uv * (sg * (1.0 + gv * (1.0 - sg)))).astype(BF)
            du_ref[s] = (dh * (gv * sg)).astype(BF)

    out = _sds((s_n, t, fs), BF)
    return _rowcall("swiglu_bwd", body, t, tm, [drs, g, u], [wd], [out, out])


def _loss_grad(y, target, xh, rstd, gain, scale):
    t, d = y.shape
    tm = _tile(t, ROW_TILE)

    def body(y_ref, t_ref, xh_ref, rs_ref, g_ref, dr_ref, drs_ref, dg_ref, db_ref, sq_ref):
        m = pl.program_id(0)
        diff = y_ref[...] - t_ref[...]
        _ln_bwd_tail(m, diff / d, xh_ref, rs_ref, g_ref, scale, dr_ref, drs_ref, dg_ref, db_ref)
        _accumulate(m, sq_ref, _rows8(diff * diff))

    outs = [_sds((t, d), F32), _sds((t, d), BF)]
    accs = [_sds((8, d), F32)] * 3
    return _rowcall("loss_grad", body, t, tm, [y, target, xh, rstd], [gain], outs, accs)


def _adamw(w, g, m, v):
    shape = w.shape
    c = shape[-1]
    rows = math.prod(shape[:-1])
    tm = _tile(rows, 512)

    def body(w_ref, g_ref, m_ref, v_ref, d_ref, nm_ref, nv_ref):
        gv = g_ref[...]
        nm = ADAM_B1 * m_ref[...] + (1.0 - ADAM_B1) * gv
        nv = ADAM_B2 * v_ref[...] + (1.0 - ADAM_B2) * (gv * gv)
        m_hat = nm / (1.0 - ADAM_B1 ** ADAM_STEP)
        v_hat = nv / (1.0 - ADAM_B2 ** ADAM_STEP)
        d_ref[...] = -ADAM_LR * (m_hat / (jnp.sqrt(v_hat) + ADAM_EPS) + ADAM_WD * w_ref[...])
        nm_ref[...] = nm
        nv_ref[...] = nv

    flat = [a.reshape(rows, c) for a in (w, g, m, v)]
    out = _sds((rows, c), F32)
    res = _rowcall("adamw", body, rows, tm, flat, [], [out, out, out])
    return tuple(r.reshape(shape) for r in res)


def _wgrad(name, a_list, b_list, out_sub, sub_shape, a_get, b_get):
    t = a_list[0].shape[-2]
    tk = _tile(t, 512)
    na, nb = len(a_list), len(b_list)
    nk = t // tk

    def tspec(shape):
        if len(shape) == 2:
            return pl.BlockSpec((tk, shape[1]), lambda k: (k, 0))
        return pl.BlockSpec((shape[0], tk, shape[2]), lambda k: (0, k, 0))

    def body(*refs):
        a_refs = refs[:na]
        b_refs = refs[na:na + nb]
        o_ref = refs[na + nb]
        acc_ref = refs[na + nb + 1]
        k = pl.program_id(0)

        @pl.when(k == 0)
        def _():
            acc_ref[...] = jnp.zeros_like(acc_ref)

        for s in range(out_sub):
            acc_ref[s] += _dot_tn(a_get(a_refs, s).astype(BF), b_get(b_refs, s).astype(BF))

        @pl.when(k == nk - 1)
        def _():
            o_ref[...] = acc_ref[...].astype(BF)

    full = (out_sub,) + tuple(sub_shape)
    return pl.pallas_call(
        body, name=name, grid=(nk,), in_specs=[tspec(a.shape) for a in a_list] + [tspec(b.shape) for b in b_list],
        out_specs=pl.BlockSpec(full, lambda k: (0, 0, 0)), out_shape=_sds(full, BF),
        scratch_shapes=[pltpu.VMEM(full, F32)], compiler_params=_cparams(("arbitrary",)),
    )(*a_list, *b_list)


def _head_masks(x):
    lo = lax.broadcasted_iota(jnp.int32, (1, LANES), 1) < HEAD_DIM
    zero = jnp.zeros_like(x)
    return lo, (jnp.where(lo, x, zero), jnp.where(lo, zero, x))


def _logistic_parts(z):
    nz = -z
    t = jnp.exp(jnp.minimum(z, nz))
    one_t = 1.0 + t
    return jnp.minimum(nz, 0.0) - jnp.log(one_t), t, one_t


def _split2(x):
    hi = x.astype(BF)
    lo = (x - hi.astype(F32)).astype(BF)
    return jnp.concatenate([hi, lo], axis=1)


def _tri2(cond):
    tri = jnp.where(cond, 1.0, 0.0).astype(BF)
    return jnp.concatenate([tri, tri], axis=0)


def _sb_specs(bl, seq, p_n, col0, tq):
    nq = seq // tq
    q_spec = pl.BlockSpec((tq, LANES), lambda b, p, i: (b * nq + i, col0 + p))
    k_spec = pl.BlockSpec((seq, LANES), lambda b, p, i: (b, col0 + p_n + p))
    v_spec = pl.BlockSpec((seq, LANES), lambda b, p, i: (b, col0 + 2 * p_n + p))
    tile_spec = pl.BlockSpec((tq, LANES), lambda b, p, i: (b * nq + i, p))
    seq_spec = pl.BlockSpec((seq, LANES), lambda b, p, i: (b, p))
    g_spec = pl.BlockSpec((1, LANES), lambda b, p, i: (0, p))
    return nq, q_spec, k_spec, v_spec, tile_spec, seq_spec, g_spec


def _rms_out(acc, lo, gain):
    ss = acc * acc
    s0 = jnp.sum(jnp.where(lo, ss, 0.0), axis=1, keepdims=True)
    s1 = jnp.sum(jnp.where(lo, 0.0, ss), axis=1, keepdims=True)
    r = jnp.where(lo, lax.rsqrt(s0 / HEAD_DIM + RMS_EPS), lax.rsqrt(s1 / HEAD_DIM + RMS_EPS))
    return acc * r * gain


def _sb_fwd(qkv, bl, seq, dg, gain):
    t = bl * seq
    p_n = dg // LANES
    tq, tk = min(SB_TQ, seq), SB_TK
    ratio = tq // tk
    scale = 1.0 / math.sqrt(HEAD_DIM)
    nq, q_spec, k_spec, v_spec, tile_spec, _, g_spec = _sb_specs(bl, seq, p_n, 0, tq)
    assert seq // tk <= LANES

    def body(q_ref, k_ref, v_ref, g_ref, o_ref, m_ref, c_ref):
        i = pl.program_id(2)
        row = lax.broadcasted_iota(jnp.int32, (tq, tk), 0)
        col = lax.broadcasted_iota(jnp.int32, (tq, tk), 1)
        krow = lax.broadcasted_iota(jnp.int32, (tk, tk), 0)
        kcol = lax.broadcasted_iota(jnp.int32, (tk, tk), 1)
        tri2 = _tri2(krow > kcol)
        lo, qh = _head_masks(q_ref[...] * jnp.asarray(scale, BF))
        lane = lax.broadcasted_iota(jnp.int32, (1, LANES), 1)

        def step(j, carry, diag):
            acc, c0, c1, s0, s1 = carry
            cars, seen = (c0, c1), (s0, s1)
            mask = None if diag is None else row > col + diag
            off = pl.multiple_of(j * tk, tk)
            kj = k_ref[pl.ds(off, tk), :]
            vj = v_ref[pl.ds(off, tk), :]
            zs = [_dot_nt(qh[h], kj) for h in range(2)]
            lks = [_logistic_parts(z)[0] for z in zs]
            if mask is not None:
                lks = [jnp.where(mask, lk, 0.0) for lk in lks]
            sins = [_dot(_split2(lk), tri2) for lk in lks]
            ws = []
            for h in range(2):
                w = jnp.exp((lks[h] + zs[h]) + (sins[h] + cars[h]))
                ws.append((w if mask is None else jnp.where(mask, w, 0.0)).astype(BF))
            pvs = [_dot(w, vj) for w in ws]
            new_seen = [jnp.where(lane == j, cars[h], seen[h]) for h in range(2)]
            new_cars = [cars[h] + jnp.sum(lks[h], axis=1, keepdims=True) for h in range(2)]
            return acc + jnp.where(lo, pvs[0], pvs[1]), new_cars[0], new_cars[1], new_seen[0], new_seen[1]

        zc = jnp.zeros((tq, 1), F32)
        zt = jnp.zeros((tq, LANES), F32)
        unseen = jnp.full((tq, LANES), SB_UNSEEN, F32)
        carry = (zt, zc, zc, unseen, unseen)
        for m in reversed(range(ratio)):
            carry = step(ratio * i + m, carry, m * tk)

        def alive(state):
            j, c = state
            return jnp.logical_and(j >= 0, jnp.max(jnp.maximum(c[1], c[2])) >= SB_DEAD)

        _, carry = lax.while_loop(alive, lambda st: (st[0] - 1, step(st[0], st[1], None)), (ratio * i - 1, carry))
        acc = carry[0]
        o_ref[...] = acc
        m_ref[...] = _rms_out(acc, lo, g_ref[...]).astype(BF)
        c_ref[:, :LANES] = carry[3]
        c_ref[:, LANES:] = carry[4]

    car_spec = pl.BlockSpec((tq, 2 * LANES), lambda b, p, i: (b * nq + i, p))
    return pl.pallas_call(
        body, name="sb_fwd", grid=(bl, p_n, nq), in_specs=[q_spec, k_spec, v_spec, g_spec],
        out_specs=[tile_spec, tile_spec, car_spec],
        out_shape=[_sds((t, dg), F32), _sds((t, dg), BF), _sds((t, 2 * dg), F32)],
        compiler_params=_cparams(("parallel", "parallel", "parallel")),
    )(qkv, qkv, qkv, gain)


def _sb_bwd(qkv, cars, do, bl, seq, dg):
    t = bl * seq
    p_n = dg // LANES
    tq, tk = min(SB_TQ, seq), SB_TK
    ratio = tq // tk
    scale = 1.0 / math.sqrt(HEAD_DIM)
    nq, q_spec, k_spec, v_spec, tile_spec, seq_spec, _ = _sb_specs(bl, seq, p_n, 0, tq)
    car_spec = pl.BlockSpec((tq, 2 * LANES), lambda b, p, i: (b * nq + i, p))

    def body(q_ref, k_ref, v_ref, c_ref, do_ref, dq_ref, dk_ref, dv_ref, dk_acc, dv_acc):
        i = pl.program_id(2)

        @pl.when(i == 0)
        def _():
            dk_acc[...] = jnp.zeros_like(dk_acc)
            dv_acc[...] = jnp.zeros_like(dv_acc)

        row = lax.broadcasted_iota(jnp.int32, (tq, tk), 0)
        col = lax.broadcasted_iota(jnp.int32, (tq, tk), 1)
        krow = lax.broadcasted_iota(jnp.int32, (tk, tk), 0)
        kcol = lax.broadcasted_iota(jnp.int32, (tk, tk), 1)
        tri2 = _tri2(krow > kcol)
        tri_pre2 = _tri2(krow < kcol)
        lane = lax.broadcasted_iota(jnp.int32, (1, LANES), 1)
        sc = jnp.asarray(scale, BF)
        qs = q_ref[...] * sc
        dob = do_ref[...]
        lo, qh = _head_masks(qs)
        _, doh = _head_masks(dob)
        seen = (c_ref[:, :LANES], c_ref[:, LANES:])

        def step(j, carry, diag):
            dq, e0, e1 = carry
            ecars = (e0, e1)
            mask = None if diag is None else row > col + diag
            off = pl.multiple_of(j * tk, tk)
            kj = k_ref[pl.ds(off, tk), :]
            vj = v_ref[pl.ds(off, tk), :]
            zs = [_dot_nt(qh[h], kj) for h in range(2)]
            dws = [_dot_nt(doh[h], vj) for h in range(2)]
            parts = [_logistic_parts(z) for z in zs]
            lks = [p[0] if mask is None else jnp.where(mask, p[0], 0.0) for p in parts]
            sins = [_dot(_split2(lk), tri2) for lk in lks]
            ws, es = [], []
            for h in range(2):
                car = jnp.sum(jnp.where(lane == j, seen[h], 0.0), axis=1, keepdims=True)
                w = jnp.exp((lks[h] + zs[h]) + (sins[h] + car))
                w = w if mask is None else jnp.where(mask, w, 0.0)
                ws.append(w.astype(BF))
                es.append(dws[h] * w)
            pres = [_dot(_split2(e), tri_pre2) for e in es]
            dzs = []
            for h in range(2):
                t_, e = parts[h][1], es[h]
                before = pres[h] + ecars[h]
                dz = pl.reciprocal(parts[h][2], approx=True) * jnp.where(zs[h] > 0.0, e * t_ - before, e - t_ * before)
                dzs.append((dz if mask is None else jnp.where(mask, dz, 0.0)).astype(BF))
            kjs = kj * sc
            dqs = [_dot(dz, kjs) for dz in dzs]
            dks = [_dot_tn(dz, qs) for dz in dzs]
            dvs = [_dot_tn(w, dob) for w in ws]
            dk_acc[pl.ds(off, tk), :] += jnp.where(lo, dks[0], dks[1])
            dv_acc[pl.ds(off, tk), :] += jnp.where(lo, dvs[0], dvs[1])
            new_e = [ecars[h] + jnp.sum(es[h], axis=1, keepdims=True) for h in range(2)]
            return dq + jnp.where(lo, dqs[0], dqs[1]), new_e[0], new_e[1]

        reach = jnp.max(jnp.maximum(seen[0], seen[1]), axis=0, keepdims=True)
        skipped = jnp.logical_and(reach < SB_DEAD, lane < ratio * i)
        first = jnp.minimum(jnp.sum(jnp.where(skipped, 1, 0)).astype(jnp.int32), ratio * i)
        zc = jnp.zeros((tq, 1), F32)
        carry = lax.fori_loop(first, ratio * i, lambda j, c: step(j, c, None), (jnp.zeros((tq, LANES), F32), zc, zc))
        for m in range(ratio):
            carry = step(ratio * i + m, carry, m * tk)
        dq_ref[...] = carry[0].astype(BF)

        @pl.when(i == nq - 1)
        def _():
            dk_ref[...] = dk_acc[...].astype(BF)
            dv_ref[...] = dv_acc[...].astype(BF)

    out = _sds((t, dg), BF)
    return pl.pallas_call(
        body, name="sb_bwd", grid=(bl, p_n, nq),
        in_specs=[q_spec, k_spec, v_spec, car_spec, tile_spec],
        out_specs=[tile_spec, seq_spec, seq_spec], out_shape=[out, out, out],
        scratch_shapes=[pltpu.VMEM((seq, LANES), F32), pltpu.VMEM((seq, LANES), F32)],
        compiler_params=_cparams(("parallel", "parallel", "arbitrary")),
    )(qkv, qkv, qkv, cars, do)


def _bias_expand(rb):
    h = rb.shape[0]
    pad = CHUNK * N_PREV_CHUNKS
    n_f = BAND + CHUNK - 1
    f = jnp.concatenate([rb[:, MAX_REL - (CHUNK - 1):2 * MAX_REL],
                         jnp.broadcast_to(rb[:, 2 * MAX_REL:], (h, pad + CHUNK - MAX_REL))], axis=1)
    fr = f[:, ::-1]
    skew = jnp.pad(jnp.broadcast_to(fr[:, None, :], (h, CHUNK, n_f)), ((0, 0), (0, 0), (0, 1)))
    skew = skew.reshape(h, CHUNK * (n_f + 1))[:, :CHUNK * n_f].reshape(h, CHUNK, n_f)
    table = skew[:, :, CHUNK - 1:CHUNK - 1 + BAND]
    n_c = ATT_TILE // CHUNK
    width = 3 * ATT_TILE
    rows = [jnp.pad(table, ((0, 0), (0, 0), (c * CHUNK, width - BAND - c * CHUNK)), constant_values=NEG_BIAS)
            for c in range(n_c)]
    return jnp.concatenate(rows, axis=1)


def _ca_scores(qh_h, k_ref, bias_ref, h, i, scale):
    tq = ATT_TILE
    ss, offs = [], []
    for kb in range(3):
        jb = i - 2 + kb
        off = pl.multiple_of(jnp.maximum(jb, 0) * tq, tq)
        s = _dot_nt(qh_h, k_ref[pl.ds(off, tq), :]) * scale + bias_ref[h, :, kb * tq:(kb + 1) * tq]
        ss.append(jnp.where(jb >= 0, s, NEG_BIAS))
        offs.append(off)
    m = jnp.max(jnp.maximum(jnp.maximum(ss[0], ss[1]), ss[2]), axis=1, keepdims=True)
    ps = [jnp.exp(s - m) for s in ss]
    den = jnp.sum(ps[0] + ps[1] + ps[2], axis=1, keepdims=True)
    inv = 1.0 / den
    return [p * inv for p in ps], offs


def _ca_fwd(qkv, bias4, bl, seq, dg, gain):
    t = bl * seq
    p_n = dg // LANES
    tq = ATT_TILE
    scale = 1.0 / math.sqrt(HEAD_DIM)
    nq, q_spec, k_spec, v_spec, tile_spec, _, g_spec = _sb_specs(bl, seq, p_n, 3 * p_n, tq)
    b_spec = pl.BlockSpec((2, tq, 3 * tq), lambda b, p, i: (p, 0, 0))

    def body(q_ref, k_ref, v_ref, bias_ref, g_ref, o_ref, m_ref):
        i = pl.program_id(2)
        lo, qh = _head_masks(q_ref[...])
        outs = []
        for h in range(2):
            ps, offs = _ca_scores(qh[h], k_ref, bias_ref, h, i, scale)
            acc = None
            for kb in range(3):
                pv = _dot(ps[kb].astype(BF), v_ref[pl.ds(offs[kb], tq), :])
                acc = pv if acc is None else acc + pv
            outs.append(acc)
        acc = jnp.where(lo, outs[0], outs[1])
        o_ref[...] = acc
        m_ref[...] = _rms_out(acc, lo, g_ref[...]).astype(BF)

    return pl.pallas_call(
        body, name="ca_fwd", grid=(bl, p_n, nq), in_specs=[q_spec, k_spec, v_spec, b_spec, g_spec],
        out_specs=[tile_spec, tile_spec], out_shape=[_sds((t, dg), F32), _sds((t, dg), BF)],
        compiler_params=_cparams(("parallel", "parallel", "parallel")),
    )(qkv, qkv, qkv, bias4, gain)


def _ca_bwd(qkv, bias4, o, do, bl, seq, dg):
    t = bl * seq
    p_n = dg // LANES
    tq = ATT_TILE
    scale = 1.0 / math.sqrt(HEAD_DIM)
    nq = seq // tq
    c0 = 3 * p_n
    q_spec = pl.BlockSpec((tq, LANES), lambda p, b, i: (b * nq + i, c0 + p))
    k_spec = pl.BlockSpec((seq, LANES), lambda p, b, i: (b, c0 + p_n + p))
    v_spec = pl.BlockSpec((seq, LANES), lambda p, b, i: (b, c0 + 2 * p_n + p))
    tile_spec = pl.BlockSpec((tq, LANES), lambda p, b, i: (b * nq + i, p))
    seq_spec = pl.BlockSpec((seq, LANES), lambda p, b, i: (b, p))
    b_spec = pl.BlockSpec((2, tq, 3 * tq), lambda p, b, i: (p, 0, 0))

    def body(q_ref, k_ref, v_ref, bias_ref, o_ref, do_ref, dq_ref, dk_ref, dv_ref, db_ref, dk_acc, dv_acc):
        b = pl.program_id(1)
        i = pl.program_id(2)

        @pl.when(i == 0)
        def _():
            dk_acc[...] = jnp.zeros_like(dk_acc)
            dv_acc[...] = jnp.zeros_like(dv_acc)

        @pl.when((i == 0) & (b == 0))
        def _():
            db_ref[...] = jnp.zeros_like(db_ref)

        q = q_ref[...]
        dob = do_ref[...]
        lo, qh = _head_masks(q)
        _, doh = _head_masks(dob)
        prod = dob.astype(F32) * o_ref[...]
        delta = (jnp.sum(jnp.where(lo, prod, 0.0), axis=1, keepdims=True),
                 jnp.sum(jnp.where(lo, 0.0, prod), axis=1, keepdims=True))
        dqs = []
        for h in range(2):
            ps, offs = _ca_scores(qh[h], k_ref, bias_ref, h, i, scale)
            dq = None
            dks, dvs = [], []
            for kb in range(3):
                dp = _dot_nt(doh[h], v_ref[pl.ds(offs[kb], tq), :])
                ds = ps[kb] * (dp - delta[h])
                db_ref[h, :, kb * tq:(kb + 1) * tq] += ds
                dsb = (ds * scale).astype(BF)
                part = _dot(dsb, k_ref[pl.ds(offs[kb], tq), :])
                dq = part if dq is None else dq + part
                dks.append(_dot_tn(dsb, q))
                dvs.append(_dot_tn(ps[kb].astype(BF), dob))
            dqs.append(dq)
            hm = lo if h == 0 else jnp.logical_not(lo)
            for kb in range(3):
                dk_acc[pl.ds(offs[kb], tq), :] += jnp.where(hm, dks[kb], 0.0)
                dv_acc[pl.ds(offs[kb], tq), :] += jnp.where(hm, dvs[kb], 0.0)
        dq_ref[...] = jnp.where(lo, dqs[0], dqs[1]).astype(BF)

        @pl.when(i == nq - 1)
        def _():
            dk_ref[...] = dk_acc[...].astype(BF)
            dv_ref[...] = dv_acc[...].astype(BF)

    out = _sds((t, dg), BF)
    return pl.pallas_call(
        body, name="ca_bwd", grid=(p_n, bl, nq),
        in_specs=[q_spec, k_spec, v_spec, b_spec, tile_spec, tile_spec],
        out_specs=[tile_spec, seq_spec, seq_spec, b_spec], out_shape=[out, out, out, _sds(bias4.shape, F32)],
        scratch_shapes=[pltpu.VMEM((seq, LANES), F32), pltpu.VMEM((seq, LANES), F32)],
        compiler_params=_cparams(("parallel", "arbitrary", "arbitrary")),
    )(qkv, qkv, qkv, bias4, o, do)


_ANY = pl.BlockSpec(memory_space=pl.ANY)


def _mesh_pos():
    x, y, c = lax.axis_index("x"), lax.axis_index("y"), lax.axis_index("c")
    chips = [(1 - x, y), (x, 1 - y), (1 - x, 1 - y)]
    return x, y, c, chips


def _half(ref_rows, c):
    return pl.ds(c * (ref_rows // 2), ref_rows // 2)


def _cast_into_slot(w, slot):
    l_n, r, c_n = w.shape
    tr = _tile(r, 512)

    def body(s_ref, w_ref, o_ref):
        o_ref[...] = w_ref[...].astype(BF)

    return pl.pallas_call(
        body, name="cast_into_slot", out_shape=_sds((l_n, 4, r, c_n), BF),
        grid_spec=pltpu.PrefetchScalarGridSpec(
            num_scalar_prefetch=1, grid=(l_n, r // tr),
            in_specs=[pl.BlockSpec((None, tr, c_n), lambda l, m, s: (l, m, 0))],
            out_specs=pl.BlockSpec((None, None, tr, c_n), lambda l, m, s: (l, s[0], m, 0))),
        compiler_params=_cparams(("parallel", "parallel")),
    )(slot, w)


def _gather_weights(bufs):
    nt = len(bufs)

    def body(*refs):
        outs = refs[nt:2 * nt]
        send_sems, recv_sems = refs[2 * nt:]
        x, y, c, chips = _mesh_pos()
        s_me = 2 * x + y

        def rows(t, cc):
            return _half(bufs[t].shape[2], cc)

        def copy(k, src, dst, to):
            return pltpu.make_async_remote_copy(src_ref=src, dst_ref=dst, send_sem=send_sems.at[k],
                                                recv_sem=recv_sems.at[k], device_id=to, device_id_type=MESH_IDS)

        first = []
        for j, (px, py) in enumerate(chips):
            for t in range(nt):
                blk = outs[t].at[:, s_me, rows(t, c)]
                first.append(copy(j * nt + t, blk, blk, (px, py, c)))
        for cp in first:
            cp.start()
        passed = []
        for j, (px, py) in enumerate(chips):
            s_p = 2 * px + py
            for t in range(nt):
                blk = outs[t].at[:, s_p, rows(t, c)]
                copy(j * nt + t, blk, blk, (px, py, c)).wait_recv()
                fwd = copy(3 * nt + j * nt + t, blk, blk, (x, y, 1 - c))
                fwd.start()
                passed.append(fwd)
        for j, (px, py) in enumerate(chips):
            s_p = 2 * px + py
            for t in range(nt):
                blk = outs[t].at[:, s_p, rows(t, 1 - c)]
                copy(3 * nt + j * nt + t, blk, blk, (x, y, 1 - c)).wait_recv()
        for cp in first + passed:
            cp.wait_send()

    return pl.pallas_call(
        body, name="gather_weights", in_specs=[_ANY] * nt, out_specs=[_ANY] * nt,
        out_shape=[_sds(b.shape, b.dtype) for b in bufs], input_output_aliases={t: t for t in range(nt)},
        scratch_shapes=[pltpu.SemaphoreType.DMA((6 * nt,)), pltpu.SemaphoreType.DMA((6 * nt,))],
    )(*bufs)


def _swap_halves(grads):
    nt = len(grads)

    def body(*refs):
        ins, theirs_o = refs[:nt], refs[nt:2 * nt]
        send_sems, recv_sems = refs[2 * nt:]
        x, y, c, _ = _mesh_pos()
        sends = [pltpu.make_async_remote_copy(
            src_ref=ins[t].at[:, :, _half(grads[t].shape[2], 1 - c)], dst_ref=theirs_o[t],
            send_sem=send_sems.at[t], recv_sem=recv_sems.at[t], device_id=(x, y, 1 - c), device_id_type=MESH_IDS)
            for t in range(nt)]
        for cp in sends:
            cp.start()
        for cp in sends:
            cp.wait_recv()
        for cp in sends:
            cp.wait_send()

    half = [_sds(g.shape[:2] + (g.shape[2] // 2, g.shape[3]), g.dtype) for g in grads]
    return pl.pallas_call(
        body, name="swap_halves", in_specs=[_ANY] * nt, out_specs=[_ANY] * nt, out_shape=half,
        scratch_shapes=[pltpu.SemaphoreType.DMA((nt,)), pltpu.SemaphoreType.DMA((nt,))],
    )(*grads)


def _add_pair(mine, theirs):
    l_n, s_n, r2, c_n = theirs.shape

    def body(a_ref, b_ref, o_ref):
        rows = pl.ds(pl.multiple_of(lax.axis_index("c") * r2, 16), r2)
        o_ref[...] = (a_ref[rows, :].astype(F32) + b_ref[...].astype(F32)).astype(BF)

    blk = (None, None, r2, c_n)
    return pl.pallas_call(
        body, name="add_pair", out_shape=_sds(theirs.shape, BF), grid=(l_n, s_n),
        in_specs=[pl.BlockSpec((None, None, 2 * r2, c_n), lambda l, s: (l, s, 0, 0)),
                  pl.BlockSpec(blk, lambda l, s: (l, s, 0, 0))],
        out_specs=pl.BlockSpec(blk, lambda l, s: (l, s, 0, 0)),
        compiler_params=_cparams(("parallel", "parallel")),
    )(mine, theirs)


def _scatter_chips(psums):
    nt = len(psums)

    def body(*refs):
        ins, got_o = refs[:nt], refs[nt:2 * nt]
        send_sems, recv_sems = refs[2 * nt:]
        x, y, c, chips = _mesh_pos()
        sends = []
        for j, (px, py) in enumerate(chips):
            for t in range(nt):
                sends.append(pltpu.make_async_remote_copy(
                    src_ref=ins[t].at[:, 2 * px + py], dst_ref=got_o[t].at[j],
                    send_sem=send_sems.at[j * nt + t], recv_sem=recv_sems.at[j * nt + t],
                    device_id=(px, py, c), device_id_type=MESH_IDS))
        for cp in sends:
            cp.start()
        for cp in sends:
            cp.wait_recv()
        for cp in sends:
            cp.wait_send()

    got = [_sds((3, p.shape[0]) + p.shape[2:], p.dtype) for p in psums]
    return pl.pallas_call(
        body, name="scatter_chips", in_specs=[_ANY] * nt, out_specs=[_ANY] * nt, out_shape=got,
        scratch_shapes=[pltpu.SemaphoreType.DMA((3 * nt,)), pltpu.SemaphoreType.DMA((3 * nt,))],
    )(*psums)


def _add_chips(psum, got):
    l_n, _, r2, c_n = psum.shape

    def body(a_ref, g_ref, o_ref):
        c = lax.axis_index("c")
        acc = a_ref[2 * lax.axis_index("x") + lax.axis_index("y")].astype(F32)
        for j in range(3):
            acc = acc + g_ref[j].astype(F32)
        o_ref[pl.ds(pl.multiple_of(c * r2, 8), r2), :] = acc
        o_ref[pl.ds(pl.multiple_of((1 - c) * r2, 8), r2), :] = jnp.zeros_like(acc)

    return pl.pallas_call(
        body, name="add_chips", out_shape=_sds((l_n, 2 * r2, c_n), F32), grid=(l_n,),
        in_specs=[pl.BlockSpec((None, 4, r2, c_n), lambda l: (l, 0, 0, 0)),
                  pl.BlockSpec((3, None, r2, c_n), lambda l: (0, l, 0, 0))],
        out_specs=pl.BlockSpec((None, 2 * r2, c_n), lambda l: (l, 0, 0)),
        compiler_params=_cparams(("parallel",)),
    )(psum, got)


def _join_halves(fulls):
    nt = len(fulls)

    def body(*refs):
        outs = refs[nt:2 * nt]
        send_sems, recv_sems = refs[2 * nt:]
        x, y, c, _ = _mesh_pos()

        def copy(t, cc):
            blk = outs[t].at[:, _half(fulls[t].shape[1], cc)]
            return pltpu.make_async_remote_copy(
                src_ref=blk, dst_ref=blk, send_sem=send_sems.at[t], recv_sem=recv_sems.at[t],
                device_id=(x, y, 1 - c), device_id_type=MESH_IDS)

        sends = [copy(t, c) for t in range(nt)]
        for cp in sends:
            cp.start()
        for t in range(nt):
            copy(t, 1 - c).wait_recv()
        for cp in sends:
            cp.wait_send()

    return pl.pallas_call(
        body, name="join_halves", in_specs=[_ANY] * nt, out_specs=[_ANY] * nt,
        out_shape=[_sds(f.shape, f.dtype) for f in fulls], input_output_aliases={t: t for t in range(nt)},
        scratch_shapes=[pltpu.SemaphoreType.DMA((nt,)), pltpu.SemaphoreType.DMA((nt,))],
    )(*fulls)


def _allreduce_small(part):
    r, d = part.shape
    flips = [(fx, fy, fc) for fx in (0, 1) for fy in (0, 1) for fc in (0, 1) if fx + fy + fc]

    def body(p_ref, o_ref, buf, send_sems, recv_sems):
        x, y, c, _ = _mesh_pos()
        me = 4 * x + 2 * y + c
        buf[me] = p_ref[...]
        sends = []
        for k, (fx, fy, fc) in enumerate(flips):
            sends.append(pltpu.make_async_remote_copy(
                src_ref=p_ref, dst_ref=buf.at[me], send_sem=send_sems.at[k], recv_sem=recv_sems.at[k],
                device_id=(x ^ fx, y ^ fy, c ^ fc), device_id_type=MESH_IDS))
        for cp in sends:
            cp.start()
        for k, (fx, fy, fc) in enumerate(flips):
            peer = 4 * (x ^ fx) + 2 * (y ^ fy) + (c ^ fc)
            pltpu.make_async_remote_copy(
                src_ref=p_ref, dst_ref=buf.at[peer], send_sem=send_sems.at[k], recv_sem=recv_sems.at[k],
                device_id=(x ^ fx, y ^ fy, c ^ fc), device_id_type=MESH_IDS).wait_recv()
        acc = buf[0]
        for dev in range(1, 8):
            acc = acc + buf[dev]
        o_ref[...] = acc
        for cp in sends:
            cp.wait_send()

    vmem = pl.BlockSpec(memory_space=pltpu.VMEM)
    return pl.pallas_call(
        body, name="allreduce_small", in_specs=[vmem], out_specs=vmem, out_shape=_sds((r, d), F32),
        scratch_shapes=[pltpu.VMEM((8, r, d), F32), pltpu.SemaphoreType.DMA((7,)), pltpu.SemaphoreType.DMA((7,))],
    )(part)


BIG = ("ffn1_w_gate", "ffn1_w_up", "ffn1_w_down", "w_in", "w_out", "ffn2_w_gate", "ffn2_w_up", "ffn2_w_down")
SMALL = ("ln1_g", "ln1_b", "rel_bias", "sb_out_g", "ca_out_g", "ln2_g", "ln2_b", "ln3_g", "ln3_b")
ORDER = ("ffn1_w_gate", "ffn1_w_up", "ffn1_w_down", "ln1_g", "ln1_b", "w_in", "rel_bias", "sb_out_g", "ca_out_g",
         "w_out", "ln2_g", "ln2_b", "ffn2_w_gate", "ffn2_w_up", "ffn2_w_down", "ln3_g", "ln3_b")


def _pack_small(vals, depth, d):
    rows = []
    for l in range(depth):
        for n in ("ln1_g", "ln1_b", "ln2_g", "ln2_b", "ln3_g", "ln3_b"):
            rows.append(vals[n][l][None, :])
        rows.append(jnp.concatenate([vals["sb_out_g"][l], vals["ca_out_g"][l]])[None, :])
        rb = vals["rel_bias"][l]
        rows.append(jnp.pad(rb, ((0, 0), (0, d - rb.shape[1]))))
    packed = jnp.concatenate(rows, axis=0)
    pad = (-packed.shape[0]) % 8
    return jnp.pad(packed, ((0, pad), (0, 0)))


def _unpack_small(packed, depth, d, heads, n_rel):
    per = 7 + heads
    out = {n: [] for n in SMALL}
    for l in range(depth):
        base = l * per
        for i, n in enumerate(("ln1_g", "ln1_b", "ln2_g", "ln2_b", "ln3_g", "ln3_b")):
            out[n].append(packed[base + i])
        out["sb_out_g"].append(packed[base + 6, :d // 2])
        out["ca_out_g"].append(packed[base + 6, d // 2:])
        out["rel_bias"].append(packed[base + 7:base + 7 + heads, :n_rel])
    return {n: jnp.stack(v) for n, v in out.items()}


def kernel(x, ffn1_w_gate, ffn1_w_up, ffn1_w_down, ln1_g, ln1_b, w_in, rel_bias, sb_out_g, ca_out_g, w_out, ln2_g, ln2_b, ffn2_w_gate, ffn2_w_up, ffn2_w_down, ln3_g, ln3_b, loss_target, m_ffn1_w_gate, m_ffn1_w_up, m_ffn1_w_down, m_ln1_g, m_ln1_b, m_w_in, m_rel_bias, m_sb_out_g, m_ca_out_g, m_w_out, m_ln2_g, m_ln2_b, m_ffn2_w_gate, m_ffn2_w_up, m_ffn2_w_down, m_ln3_g, m_ln3_b, v_ffn1_w_gate, v_ffn1_w_up, v_ffn1_w_down, v_ln1_g, v_ln1_b, v_w_in, v_rel_bias, v_sb_out_g, v_ca_out_g, v_w_out, v_ln2_g, v_ln2_b, v_ffn2_w_gate, v_ffn2_w_up, v_ffn2_w_down, v_ln3_g, v_ln3_b):
    w = dict(ffn1_w_gate=ffn1_w_gate, ffn1_w_up=ffn1_w_up, ffn1_w_down=ffn1_w_down, ln1_g=ln1_g, ln1_b=ln1_b, w_in=w_in, rel_bias=rel_bias, sb_out_g=sb_out_g, ca_out_g=ca_out_g, w_out=w_out, ln2_g=ln2_g, ln2_b=ln2_b, ffn2_w_gate=ffn2_w_gate, ffn2_w_up=ffn2_w_up, ffn2_w_down=ffn2_w_down, ln3_g=ln3_g, ln3_b=ln3_b)
    mom = dict(ffn1_w_gate=m_ffn1_w_gate, ffn1_w_up=m_ffn1_w_up, ffn1_w_down=m_ffn1_w_down, ln1_g=m_ln1_g, ln1_b=m_ln1_b, w_in=m_w_in, rel_bias=m_rel_bias, sb_out_g=m_sb_out_g, ca_out_g=m_ca_out_g, w_out=m_w_out, ln2_g=m_ln2_g, ln2_b=m_ln2_b, ffn2_w_gate=m_ffn2_w_gate, ffn2_w_up=m_ffn2_w_up, ffn2_w_down=m_ffn2_w_down, ln3_g=m_ln3_g, ln3_b=m_ln3_b)
    var = dict(ffn1_w_gate=v_ffn1_w_gate, ffn1_w_up=v_ffn1_w_up, ffn1_w_down=v_ffn1_w_down, ln1_g=v_ln1_g, ln1_b=v_ln1_b, w_in=v_w_in, rel_bias=v_rel_bias, sb_out_g=v_sb_out_g, ca_out_g=v_ca_out_g, w_out=v_w_out, ln2_g=v_ln2_g, ln2_b=v_ln2_b, ffn2_w_gate=v_ffn2_w_gate, ffn2_w_up=v_ffn2_w_up, ffn2_w_down=v_ffn2_w_down, ln3_g=v_ln3_g, ln3_b=v_ln3_b)

    bl, seq, d = x.shape
    t = bl * seq
    depth = w_in.shape[0]
    dg = d // 2
    heads = dg // HEAD_DIM
    alpha = (2 * depth) ** 0.25
    n_rel = rel_bias.shape[-1]

    chip = (2 * lax.axis_index("x") + lax.axis_index("y")).astype(jnp.int32).reshape(1)
    full = dict(zip(BIG, _gather_weights([_cast_into_slot(w[n], chip) for n in BIG])))

    qd = dg // 2
    shards = 4
    ffn_terms = [(0, s, None, s) for s in range(shards)]
    out_terms = [(i, None, (k * qd, (k + 1) * qd), 2 * i + k) for i in range(2) for k in range(2)]
    bias_all, bias_vjp = jax.vjp(jax.vmap(_bias_expand), rel_bias)

    xs = x.reshape(t, d)
    saved = []
    for l in range(depth):
        lw = {n: _LayerOf((full[n], l)) for n in BIG}
        row = lambda a: a[l][None, :]
        g1, u1, h1 = _ffn_up(xs, lw["ffn1_w_gate"], lw["ffn1_w_up"])
        x1, xh1, rs1 = _proj_ln("ffn_down_ln", [h1], lw["ffn1_w_down"], ffn_terms, xs, row(ln1_g), row(ln1_b),
                                FFN_RESIDUAL, alpha)
        qkv = _qkv_proj(x1, lw["w_in"])
        o_sb, m_sb, c_sb = _sb_fwd(qkv, bl, seq, dg, row(sb_out_g))
        o_ca, m_ca = _ca_fwd(qkv, bias_all[l], bl, seq, dg, row(ca_out_g))
        x2, xh2, rs2 = _proj_ln("attn_out_ln", [m_sb, m_ca], lw["w_out"], out_terms, x1, row(ln2_g), row(ln2_b),
                                1.0, alpha)
        g2, u2, h2 = _ffn_up(x2, lw["ffn2_w_gate"], lw["ffn2_w_up"])
        x3, xh3, rs3 = _proj_ln("ffn_down_ln", [h2], lw["ffn2_w_down"], ffn_terms, x2, row(ln3_g), row(ln3_b),
                                FFN_RESIDUAL, alpha)
        saved.append(dict(xin=xs, g1=g1, u1=u1, h1=h1, x1=x1, xh1=xh1, rs1=rs1, qkv=qkv, o_sb=o_sb, m_sb=m_sb,
                          c_sb=c_sb, o_ca=o_ca, m_ca=m_ca, x2=x2, xh2=xh2, rs2=rs2, g2=g2, u2=u2, h2=h2, xh3=xh3,
                          rs3=rs3, lw=lw))
        xs = x3

    last = saved[-1]
    dr3, drs3, pg, pb, sq = _loss_grad(xs, loss_target.reshape(t, d), last["xh3"], last["rs3"],
                                       ln3_g[depth - 1][None, :], FFN_RESIDUAL)
    loss = lax.psum(0.5 * jnp.sum(sq) / d, ("x", "y", "c"))

    big_grads = {n: [None] * depth for n in BIG}
    small_grads = {n: [None] * depth for n in SMALL}
    s8 = lambda a: jnp.sum(a, axis=0)
    wd_ = 3 * d // shards
    in_pieces = [_shard_cols(6, dg, s, wd_) for s in range(shards)]
    dbias = [None] * depth
    for l in reversed(range(depth)):
        sv, lw = saved[l], saved[l]["lw"]
        row = lambda a: a[l][None, :]
        small_grads["ln3_g"][l], small_grads["ln3_b"][l] = s8(pg), s8(pb)
        dgt, dut = _swiglu_bwd(drs3, lw["ffn2_w_down"], sv["g2"], sv["u2"])
        dr2, drs2, pg, pb = _ffn_dx(dgt, dut, lw["ffn2_w_gate"], lw["ffn2_w_up"], dr3, alpha,
                                    (sv["xh2"], sv["rs2"], row(ln2_g), 1.0))
        small_grads["ln2_g"][l], small_grads["ln2_b"][l] = s8(pg), s8(pb)
        big_grads["ffn2_w_down"][l] = _wgrad("wgrad_down", [sv["h2"]], [drs3], shards, _lshape(lw["ffn2_w_down"])[1:],
                                             lambda a, s: a[0][s], lambda b, s: b[0][...])
        big_grads["ffn2_w_gate"][l] = _wgrad("wgrad_up", [sv["x2"]], [dgt], shards, _lshape(lw["ffn2_w_gate"])[1:],
                                             lambda a, s: a[0][...], lambda b, s: b[0][s])
        big_grads["ffn2_w_up"][l] = _wgrad("wgrad_up", [sv["x2"]], [dut], shards, _lshape(lw["ffn2_w_up"])[1:],
                                           lambda a, s: a[0][...], lambda b, s: b[0][s])
        do_sb, do_ca, pgs, pgc = _attn_dmix(drs2, lw["w_out"], sv["o_sb"], sv["o_ca"], row(sb_out_g), row(ca_out_g))
        small_grads["sb_out_g"][l], small_grads["ca_out_g"][l] = s8(pgs), s8(pgc)
        big_grads["w_out"][l] = _wgrad(
            "wgrad_out", [sv["m_sb"], sv["m_ca"]], [drs2], shards, (qd, d),
            lambda a, s: a[s // 2][:, (s % 2) * qd:(s % 2 + 1) * qd], lambda b, s: b[0][...])
        dqa, dka, dva = _sb_bwd(sv["qkv"], sv["c_sb"], do_sb, bl, seq, dg)
        dqb, dkb, dvb, dbias[l] = _ca_bwd(sv["qkv"], bias_all[l], sv["o_ca"], do_ca, bl, seq, dg)
        dqkv = [dqa, dka, dva, dqb, dkb, dvb]
        dr1, drs1, pg, pb = _attn_dx(dqkv, lw["w_in"], dr2, alpha, (sv["xh1"], sv["rs1"], row(ln1_g), FFN_RESIDUAL))
        small_grads["ln1_g"][l], small_grads["ln1_b"][l] = s8(pg), s8(pb)
        big_grads["w_in"][l] = _wgrad(
            "wgrad_in", [sv["x1"]], dqkv, shards, (d, wd_), lambda a, s: a[0][...],
            lambda b, s: jnp.concatenate([b[gi][:, lo:hi] for gi, lo, hi in in_pieces[s]], axis=1))
        dgt, dut = _swiglu_bwd(drs1, lw["ffn1_w_down"], sv["g1"], sv["u1"])
        if l > 0:
            prev = saved[l - 1]
            dr3, drs3, pg, pb = _ffn_dx(dgt, dut, lw["ffn1_w_gate"], lw["ffn1_w_up"], dr1, alpha,
                                        (prev["xh3"], prev["rs3"], ln3_g[l - 1][None, :], FFN_RESIDUAL))
        else:
            grad_x = _ffn_dx(dgt, dut, lw["ffn1_w_gate"], lw["ffn1_w_up"], dr1, alpha, None).reshape(bl, seq, d)
        big_grads["ffn1_w_down"][l] = _wgrad("wgrad_down", [sv["h1"]], [drs1], shards, _lshape(lw["ffn1_w_down"])[1:],
                                             lambda a, s: a[0][s], lambda b, s: b[0][...])
        big_grads["ffn1_w_gate"][l] = _wgrad("wgrad_up", [sv["xin"]], [dgt], shards, _lshape(lw["ffn1_w_gate"])[1:],
                                             lambda a, s: a[0][...], lambda b, s: b[0][s])
        big_grads["ffn1_w_up"][l] = _wgrad("wgrad_up", [sv["xin"]], [dut], shards, _lshape(lw["ffn1_w_up"])[1:],
                                           lambda a, s: a[0][...], lambda b, s: b[0][s])
    rel_grads = bias_vjp(jnp.stack(dbias))[0]
    for l in range(depth):
        small_grads["rel_bias"][l] = rel_grads[l]

    partial = [jnp.stack(big_grads[n]) for n in BIG]
    theirs = _swap_halves(partial)
    chip_sum = [_add_pair(a, b) for a, b in zip(partial, theirs)]
    got = _scatter_chips(chip_sum)
    grads = dict(zip(BIG, _join_halves([_add_chips(p, g) for p, g in zip(chip_sum, got)])))

    packed = _allreduce_small(_pack_small({n: jnp.stack(small_grads[n]) for n in SMALL}, depth, d))
    grads.update(_unpack_small(packed, depth, d, heads, n_rel))

    delta, new_m, new_v = {}, {}, {}
    for n in BIG:
        delta[n], new_m[n], new_v[n] = _adamw(w[n], grads[n], mom[n], var[n])
    pw, pm, pv = (_pack_small({n: src[n] for n in SMALL}, depth, d) for src in (w, mom, var))
    sd, sm, sv_ = _adamw(pw, packed, pm, pv)
    for dst, src in ((delta, sd), (new_m, sm), (new_v, sv_)):
        dst.update(_unpack_small(src, depth, d, heads, n_rel))

    return (loss, grad_x, *[grads[n] for n in ORDER], *[delta[n] for n in ORDER],
            *[new_m[n] for n in ORDER], *[new_v[n] for n in ORDER])
```

```python
import functools
import math

import jax
import jax.numpy as jnp
from jax import lax
from jax.experimental import pallas as pl
from jax.experimental.pallas import tpu as pltpu

F32 = jnp.float32
BF = jnp.bfloat16

CHUNK = 64
N_PREV_CHUNKS = 8
BAND = CHUNK * (N_PREV_CHUNKS + 1)
MAX_REL = 128
HEAD_DIM = 64
FFN_RESIDUAL = 0.5
LN_EPS = 1e-5
RMS_EPS = 1e-6
ADAM_LR = 0.001
ADAM_B1 = 0.9
ADAM_B2 = 0.999
ADAM_EPS = 1e-08
ADAM_WD = 0.01
ADAM_STEP = 10

LANES = 128
ATT_TILE = 4 * CHUNK
SB_TQ = 512
SB_TK = 256
SB_DEAD = -105.0
SB_UNSEEN = -1e30
NEG_BIAS = -1e30
VMEM_LIMIT = 56 * 1024 * 1024
ROW_TILE = 512
MESH_IDS = pl.DeviceIdType.MESH

_NT = (((1,), (1,)), ((), ()))
_TN = (((0,), (0,)), ((), ()))


def _dot(a, b):
    return jnp.dot(a, b, preferred_element_type=F32)


def _dot_nt(a, b):
    return lax.dot_general(a, b, _NT, preferred_element_type=F32)


def _dot_tn(a, b):
    return lax.dot_general(a, b, _TN, preferred_element_type=F32)


def _tile(n, pref):
    t = min(n, pref)
    while n % t:
        t //= 2
    return t


def _cparams(sem):
    return pltpu.CompilerParams(dimension_semantics=sem, vmem_limit_bytes=VMEM_LIMIT)


def _sds(shape, dtype):
    return jax.ShapeDtypeStruct(tuple(shape), dtype)


def _sigmoid(v):
    return pl.reciprocal(1.0 + jnp.exp(-v), approx=True)


def _rows8(v):
    tm, c = v.shape
    return v.reshape(tm // 8, 8, c).sum(axis=0)


class _LayerOf(tuple):
    pass


def _lshape(item):
    return item[0].shape[1:] if isinstance(item, _LayerOf) else item.shape


def _accumulate(m, ref, part):
    @pl.when(m == 0)
    def _():
        ref[...] = part

    @pl.when(m > 0)
    def _():
        ref[...] += part


def _ln_bwd_tail(m, dy, xh_ref, rs_ref, g_ref, scale, dr_ref, drs_ref, dg_ref, db_ref):
    xhv = xh_ref[...]
    dxh = dy * g_ref[...]
    m1 = jnp.mean(dxh, axis=-1, keepdims=True)
    m2 = jnp.mean(dxh * xhv, axis=-1, keepdims=True)
    dr = rs_ref[...] * (dxh - m1 - xhv * m2)
    dr_ref[...] = dr
    drs_ref[...] = (scale * dr).astype(BF)
    _accumulate(m, dg_ref, _rows8(dy * xhv))
    _accumulate(m, db_ref, _rows8(dy))


def _rowcall(name, body, n_rows, tm, row_ins, res_ins, row_outs, acc_outs=()):
    def rspec(shape):
        if len(shape) == 2:
            return pl.BlockSpec((tm, shape[1]), lambda m: (m, 0))
        return pl.BlockSpec((shape[0], tm, shape[2]), lambda m: (0, m, 0))

    def cspec(shape, single=False):
        zeros = (0,) * len(shape)
        mode = pl.Buffered(1) if single else None
        return pl.BlockSpec(tuple(shape), lambda m: zeros, pipeline_mode=mode)

    def split(items, mk):
        arrs, specs = [], []
        for it in items:
            if isinstance(it, _LayerOf):
                arr, l = it
                tail = (0,) * (arr.ndim - 1)
                arrs.append(arr)
                specs.append(pl.BlockSpec((None,) + arr.shape[1:], functools.partial(lambda m, l, tail: (l,) + tail, l=l, tail=tail),
                                          pipeline_mode=pl.Buffered(1)))
            elif isinstance(it, tuple):
                arrs.append(it[0])
                specs.append(it[1])
            else:
                arrs.append(it)
                specs.append(mk(it.shape))
        return arrs, specs

    ra, rs = split(row_ins, rspec)
    ca, cs = split(res_ins, functools.partial(cspec, single=True))
    out_specs = [rspec(s.shape) for s in row_outs] + [cspec(s.shape) for s in acc_outs]
    sem = ("arbitrary",) if acc_outs else ("parallel",)
    return pl.pallas_call(
        body, name=name, grid=(n_rows // tm,), in_specs=rs + cs, out_specs=out_specs,
        out_shape=list(row_outs) + list(acc_outs), compiler_params=_cparams(sem),
    )(*ra, *ca)


def _ffn_up(x, wg, wu):
    t, _ = x.shape
    s_n, _, fs = _lshape(wg)
    tm = _tile(t, ROW_TILE)

    def body(x_ref, wg_ref, wu_ref, g_ref, u_ref, h_ref):
        xb = x_ref[...].astype(BF)
        for s in range(s_n):
            g = _dot(xb, wg_ref[s])
            u = _dot(xb, wu_ref[s])
            g_ref[s] = g.astype(BF)
            u_ref[s] = u.astype(BF)
            h_ref[s] = (g * _sigmoid(g) * u).astype(BF)

    out = _sds((s_n, t, fs), BF)
    return _rowcall("ffn_up", body, t, tm, [x], [wg, wu], [out, out, out])


def _proj_ln(name, a_list, w, terms, xres, gain, bias, scale, alpha):
    t, d = xres.shape
    tm = _tile(t, ROW_TILE)
    na = len(a_list)

    def body(*refs):
        a_refs = refs[:na]
        x_ref, w_ref, g_ref, bb_ref = refs[na:na + 4]
        y_ref, xh_ref, rs_ref = refs[na + 4:]
        acc = None
        for i, s, cols, sw in terms:
            a = a_refs[i][s] if cols is None else a_refs[i][:, cols[0]:cols[1]]
            p = _dot(a.astype(BF), w_ref[sw])
            acc = p if acc is None else acc + p
        if scale != 1.0:
            acc = scale * acc
        r = alpha * x_ref[...] + acc
        mu = jnp.mean(r, axis=-1, keepdims=True)
        xc = r - mu
        var = jnp.mean(xc * xc, axis=-1, keepdims=True)
        rstd = lax.rsqrt(var + LN_EPS)
        xh = xc * rstd
        y_ref[...] = xh * g_ref[...] + bb_ref[...]
        xh_ref[...] = xh
        rs_ref[...] = rstd

    outs = [_sds((t, d), F32), _sds((t, d), F32), _sds((t, 1), F32)]
    return _rowcall(name, body, t, tm, list(a_list) + [xres], [w, gain, bias], outs)


def _shard_cols(groups, dg, s, width):
    pieces, lo = [], s * width
    while lo < (s + 1) * width:
        gi = lo // dg
        hi = min((gi + 1) * dg, (s + 1) * width)
        pieces.append((gi, lo - gi * dg, hi - gi * dg))
        lo = hi
    assert all(p[0] < groups for p in pieces)
    return pieces


def _qkv_proj(x, win):
    t, _ = x.shape
    s_n, _, wd_ = _lshape(win)
    tm = _tile(t, ROW_TILE)

    def body(x_ref, w_ref, o_ref):
        xb = x_ref[...].astype(BF)
        for s in range(s_n):
            o_ref[:, s * wd_:(s + 1) * wd_] = _dot(xb, w_ref[s]).astype(BF)

    return _rowcall("qkv_proj", body, t, tm, [x], [win], [_sds((t, s_n * wd_), BF)])[0]


def _dx_call(name, rows, res, matmul, resid, alpha, ln):
    t, d = resid.shape
    tm = _tile(t, ROW_TILE)
    nr, nc = len(rows), len(res)

    def body(*refs):
        r_refs, resid_ref = refs[:nr], refs[nr]
        if ln is None:
            c_refs, outs = refs[nr + 1:nr + 1 + nc], refs[nr + 1 + nc:]
            outs[0][...] = matmul(r_refs, c_refs) + alpha * resid_ref[...]
        else:
            xh_ref, rs_ref = refs[nr + 1:nr + 3]
            c_refs, g_ref = refs[nr + 3:nr + 3 + nc], refs[nr + 3 + nc]
            outs = refs[nr + 4 + nc:]
            dx = matmul(r_refs, c_refs) + alpha * resid_ref[...]
            _ln_bwd_tail(pl.program_id(0), dx, xh_ref, rs_ref, g_ref, ln[3], *outs)

    if ln is None:
        return _rowcall(name, body, t, tm, list(rows) + [resid], list(res), [_sds((t, d), F32)])[0]
    outs = [_sds((t, d), F32), _sds((t, d), BF)]
    accs = [_sds((8, d), F32), _sds((8, d), F32)]
    return _rowcall(name, body, t, tm, list(rows) + [resid, ln[0], ln[1]], list(res) + [ln[2]], outs, accs)


def _ffn_dx(dg_, du_, wg, wu, resid, alpha, ln):
    s_n = dg_.shape[0]

    def matmul(r, c):
        acc = None
        for i in range(2):
            for s in range(s_n):
                p = _dot_nt(r[i][s], c[i][s])
                acc = p if acc is None else acc + p
        return acc

    return _dx_call("ffn_dx", [dg_, du_], [wg, wu], matmul, resid, alpha, ln)


def _attn_dx(dqkv, win, resid, alpha, ln):
    s_n, _, wd_ = _lshape(win)
    dg = dqkv[0].shape[1]
    pieces = [_shard_cols(len(dqkv), dg, s, wd_) for s in range(s_n)]

    def matmul(r, c):
        acc = None
        for s in range(s_n):
            a = jnp.concatenate([r[gi][:, lo:hi] for gi, lo, hi in pieces[s]], axis=1)
            p = _dot_nt(a, c[0][s])
            acc = p if acc is None else acc + p
        return acc

    return _dx_call("attn_dx", list(dqkv), [win], matmul, resid, alpha, ln)


def _head_rms_bwd(dm, ov, gain, lo):
    ss = ov * ov
    s0 = jnp.sum(jnp.where(lo, ss, 0.0), axis=1, keepdims=True)
    s1 = jnp.sum(jnp.where(lo, 0.0, ss), axis=1, keepdims=True)
    r = jnp.where(lo, lax.rsqrt(s0 / HEAD_DIM + RMS_EPS), lax.rsqrt(s1 / HEAD_DIM + RMS_EPS))
    n = ov * r
    dn = dm * gain
    tt = dn * n
    t0 = jnp.sum(jnp.where(lo, tt, 0.0), axis=1, keepdims=True) / HEAD_DIM
    t1 = jnp.sum(jnp.where(lo, 0.0, tt), axis=1, keepdims=True) / HEAD_DIM
    return r * (dn - n * jnp.where(lo, t0, t1)), dm * n


def _attn_dmix(drs, wout, o_sb, o_ca, g_sb, g_ca):
    t, dg = o_sb.shape
    s_n, qd, _ = _lshape(wout)
    tm = _tile(t, ROW_TILE)
    per = dg // qd

    def body(d_ref, osb_ref, oca_ref, w_ref, gs_ref, gc_ref, dsb_ref, dca_ref, pgs_ref, pgc_ref):
        m = pl.program_id(0)
        dv = d_ref[...]
        lo = lax.broadcasted_iota(jnp.int32, (1, LANES), 1) < HEAD_DIM
        for grp, (o_ref, g_ref, do_ref, pg_ref) in enumerate(((osb_ref, gs_ref, dsb_ref, pgs_ref),
                                                               (oca_ref, gc_ref, dca_ref, pgc_ref))):
            dm = jnp.concatenate([_dot_nt(dv, w_ref[grp * per + k]) for k in range(per)], axis=1)
            parts = []
            for gi in range(dg // LANES):
                sl = slice(gi * LANES, (gi + 1) * LANES)
                do, pg = _head_rms_bwd(dm[:, sl], o_ref[:, sl], g_ref[:, sl], lo)
                do_ref[:, sl] = do.astype(BF)
                parts.append(_rows8(pg))
            _accumulate(m, pg_ref, jnp.concatenate(parts, axis=1))

    outs = [_sds((t, dg), BF), _sds((t, dg), BF)]
    accs = [_sds((8, dg), F32), _sds((8, dg), F32)]
    return _rowcall("attn_dmix", body, t, tm, [drs, o_sb, o_ca], [wout, g_sb, g_ca], outs, accs)


def _swiglu_bwd(drs, wd, g, u):
    s_n, t, fs = g.shape
    tm = _tile(t, ROW_TILE)

    def body(d_ref, g_ref, u_ref, wd_ref, dg_ref, du_ref):
        dv = d_ref[...]
        for s in range(s_n):
            dh = _dot_nt(dv, wd_ref[s])
            gv = g_ref[s].astype(F32)
            uv = u_ref[s].astype(F32)
            sg = _sigmoid(gv)
            gs = gv * sg
            dg_ref[s] = (dh * uv * (sg + gs * (1.0 - sg))).astype(BF)
            du_ref[s] = (dh * gs).astype(BF)

    out = _sds((s_n, t, fs), BF)
    return _rowcall("swiglu_bwd", body, t, tm, [drs, g, u], [wd], [out, out])


def _loss_grad(y, target, xh, rstd, gain, scale):
    t, d = y.shape
    tm = _tile(t, ROW_TILE)

    def body(y_ref, t_ref, xh_ref, rs_ref, g_ref, dr_ref, drs_ref, dg_ref, db_ref, sq_ref):
        m = pl.program_id(0)
        diff = y_ref[...] - t_ref[...]
        _ln_bwd_tail(m, diff / d, xh_ref, rs_ref, g_ref, scale, dr_ref, drs_ref, dg_ref, db_ref)
        _accumulate(m, sq_ref, _rows8(diff * diff))

    outs = [_sds((t, d), F32), _sds((t, d), BF)]
    accs = [_sds((8, d), F32)] * 3
    return _rowcall("loss_grad", body, t, tm, [y, target, xh, rstd], [gain], outs, accs)


def _adamw(w, g, m, v):
    shape = w.shape
    c = shape[-1]
    rows = math.prod(shape[:-1])
    tm = _tile(rows, 512)

    def body(w_ref, g_ref, m_ref, v_ref, d_ref, nm_ref, nv_ref):
        gv = g_ref[...]
        nm = ADAM_B1 * m_ref[...] + (1.0 - ADAM_B1) * gv
        nv = ADAM_B2 * v_ref[...] + (1.0 - ADAM_B2) * (gv * gv)
        m_hat = nm / (1.0 - ADAM_B1 ** ADAM_STEP)
        v_hat = nv / (1.0 - ADAM_B2 ** ADAM_STEP)
        d_ref[...] = -ADAM_LR * (m_hat / (jnp.sqrt(v_hat) + ADAM_EPS) + ADAM_WD * w_ref[...])
        nm_ref[...] = nm
        nv_ref[...] = nv

    flat = [a.reshape(rows, c) for a in (w, g, m, v)]
    out = _sds((rows, c), F32)
    res = _rowcall("adamw", body, rows, tm, flat, [], [out, out, out])
    return tuple(r.reshape(shape) for r in res)


def _wgrad(name, a_list, b_list, out_sub, sub_shape, a_get, b_get):
    t = a_list[0].shape[-2]
    tk = _tile(t, 512)
    na, nb = len(a_list), len(b_list)
    nk = t // tk

    def tspec(shape):
        if len(shape) == 2:
            return pl.BlockSpec((tk, shape[1]), lambda k: (k, 0))
        return pl.BlockSpec((shape[0], tk, shape[2]), lambda k: (0, k, 0))

    def body(*refs):
        a_refs = refs[:na]
        b_refs = refs[na:na + nb]
        o_ref = refs[na + nb]
        acc_ref = refs[na + nb + 1]
        k = pl.program_id(0)

        @pl.when(k == 0)
        def _():
            acc_ref[...] = jnp.zeros_like(acc_ref)

        for s in range(out_sub):
            acc_ref[s] += _dot_tn(a_get(a_refs, s).astype(BF), b_get(b_refs, s).astype(BF))

        @pl.when(k == nk - 1)
        def _():
            o_ref[...] = acc_ref[...].astype(BF)

    full = (out_sub,) + tuple(sub_shape)
    return pl.pallas_call(
        body, name=name, grid=(nk,), in_specs=[tspec(a.shape) for a in a_list] + [tspec(b.shape) for b in b_list],
        out_specs=pl.BlockSpec(full, lambda k: (0, 0, 0)), out_shape=_sds(full, BF),
        scratch_shapes=[pltpu.VMEM(full, F32)], compiler_params=_cparams(("arbitrary",)),
    )(*a_list, *b_list)


def _head_masks(x):
    lo = lax.broadcasted_iota(jnp.int32, (1, LANES), 1) < HEAD_DIM
    zero = jnp.zeros_like(x)
    return lo, (jnp.where(lo, x, zero), jnp.where(lo, zero, x))


def _logistic_parts(z):
    nz = -z
    t = jnp.exp(jnp.minimum(z, nz))
    one_t = 1.0 + t
    return jnp.minimum(nz, 0.0) - jnp.log(one_t), t, one_t


def _split2(x):
    hi = x.astype(BF)
    lo = (x - hi.astype(F32)).astype(BF)
    return jnp.concatenate([hi, lo], axis=1)


def _tri2(cond):
    tri = jnp.where(cond, 1.0, 0.0).astype(BF)
    return jnp.concatenate([tri, tri], axis=0)


def _sb_specs(bl, seq, p_n, col0, tq):
    nq = seq // tq
    q_spec = pl.BlockSpec((tq, LANES), lambda b, p, i: (b * nq + i, col0 + p))
    k_spec = pl.BlockSpec((seq, LANES), lambda b, p, i: (b, col0 + p_n + p))
    v_spec = pl.BlockSpec((seq, LANES), lambda b, p, i: (b, col0 + 2 * p_n + p))
    tile_spec = pl.BlockSpec((tq, LANES), lambda b, p, i: (b * nq + i, p))
    seq_spec = pl.BlockSpec((seq, LANES), lambda b, p, i: (b, p))
    g_spec = pl.BlockSpec((1, LANES), lambda b, p, i: (0, p))
    return nq, q_spec, k_spec, v_spec, tile_spec, seq_spec, g_spec


def _rms_out(acc, lo, gain):
    ss = acc * acc
    s0 = jnp.sum(jnp.where(lo, ss, 0.0), axis=1, keepdims=True)
    s1 = jnp.sum(jnp.where(lo, 0.0, ss), axis=1, keepdims=True)
    r = jnp.where(lo, lax.rsqrt(s0 / HEAD_DIM + RMS_EPS), lax.rsqrt(s1 / HEAD_DIM + RMS_EPS))
    return acc * r * gain


def _sb_fwd(qkv, bl, seq, dg, gain):
    t = bl * seq
    p_n = dg // LANES
    tq, tk = min(SB_TQ, seq), SB_TK
    ratio = tq // tk
    scale = 1.0 / math.sqrt(HEAD_DIM)
    nq, q_spec, k_spec, v_spec, tile_spec, _, g_spec = _sb_specs(bl, seq, p_n, 0, tq)
    assert seq // tk <= LANES

    def body(q_ref, k_ref, v_ref, g_ref, o_ref, m_ref, c_ref):
        i = pl.program_id(2)
        row = lax.broadcasted_iota(jnp.int32, (tq, tk), 0)
        col = lax.broadcasted_iota(jnp.int32, (tq, tk), 1)
        krow = lax.broadcasted_iota(jnp.int32, (tk, tk), 0)
        kcol = lax.broadcasted_iota(jnp.int32, (tk, tk), 1)
        tri2 = _tri2(krow > kcol)
        lo, qh = _head_masks(q_ref[...] * jnp.asarray(scale, BF))
        lane = lax.broadcasted_iota(jnp.int32, (1, LANES), 1)

        def step(j, carry, diag):
            acc, c0, c1, s0, s1 = carry
            cars, seen = (c0, c1), (s0, s1)
            mask = None if diag is None else row > col + diag
            off = pl.multiple_of(j * tk, tk)
            kj = k_ref[pl.ds(off, tk), :]
            vj = v_ref[pl.ds(off, tk), :]
            zs = [_dot_nt(qh[h], kj) for h in range(2)]
            lks = [_logistic_parts(z)[0] for z in zs]
            if mask is not None:
                lks = [jnp.where(mask, lk, 0.0) for lk in lks]
            sins = [_dot(_split2(lk), tri2) for lk in lks]
            ws = []
            for h in range(2):
                w = jnp.exp((lks[h] + zs[h]) + (sins[h] + cars[h]))
                ws.append((w if mask is None else jnp.where(mask, w, 0.0)).astype(BF))
            pvs = [_dot(w, vj) for w in ws]
            new_seen = [jnp.where(lane == j, cars[h], seen[h]) for h in range(2)]
            new_cars = [cars[h] + jnp.sum(lks[h], axis=1, keepdims=True) for h in range(2)]
            return acc + jnp.where(lo, pvs[0], pvs[1]), new_cars[0], new_cars[1], new_seen[0], new_seen[1]

        zc = jnp.zeros((tq, 1), F32)
        zt = jnp.zeros((tq, LANES), F32)
        unseen = jnp.full((tq, LANES), SB_UNSEEN, F32)
        carry = (zt, zc, zc, unseen, unseen)
        for m in reversed(range(ratio)):
            carry = step(ratio * i + m, carry, m * tk)

        def alive(state):
            j, c = state
            return jnp.logical_and(j >= 0, jnp.max(jnp.maximum(c[1], c[2])) >= SB_DEAD)

        _, carry = lax.while_loop(alive, lambda st: (st[0] - 1, step(st[0], st[1], None)), (ratio * i - 1, carry))
        acc = carry[0]
        o_ref[...] = acc
        m_ref[...] = _rms_out(acc, lo, g_ref[...]).astype(BF)
        c_ref[:, :LANES] = carry[3]
        c_ref[:, LANES:] = carry[4]

    car_spec = pl.BlockSpec((tq, 2 * LANES), lambda b, p, i: (b * nq + i, p))
    return pl.pallas_call(
        body, name="sb_fwd", grid=(bl, p_n, nq), in_specs=[q_spec, k_spec, v_spec, g_spec],
        out_specs=[tile_spec, tile_spec, car_spec],
        out_shape=[_sds((t, dg), F32), _sds((t, dg), BF), _sds((t, 2 * dg), F32)],
        compiler_params=_cparams(("parallel", "parallel", "parallel")),
    )(qkv, qkv, qkv, gain)


def _sb_bwd(qkv, cars, do, bl, seq, dg):
    t = bl * seq
    p_n = dg // LANES
    tq, tk = min(SB_TQ, seq), SB_TK
    ratio = tq // tk
    scale = 1.0 / math.sqrt(HEAD_DIM)
    nq, q_spec, k_spec, v_spec, tile_spec, seq_spec, _ = _sb_specs(bl, seq, p_n, 0, tq)
    car_spec = pl.BlockSpec((tq, 2 * LANES), lambda b, p, i: (b * nq + i, p))

    def body(q_ref, k_ref, v_ref, c_ref, do_ref, dq_ref, dk_ref, dv_ref, dk_acc, dv_acc):
        i = pl.program_id(2)

        @pl.when(i == 0)
        def _():
            dk_acc[...] = jnp.zeros_like(dk_acc)
            dv_acc[...] = jnp.zeros_like(dv_acc)

        row = lax.broadcasted_iota(jnp.int32, (tq, tk), 0)
        col = lax.broadcasted_iota(jnp.int32, (tq, tk), 1)
        krow = lax.broadcasted_iota(jnp.int32, (tk, tk), 0)
        kcol = lax.broadcasted_iota(jnp.int32, (tk, tk), 1)
        tri2 = _tri2(krow > kcol)
        tri_pre2 = _tri2(krow < kcol)
        lane = lax.broadcasted_iota(jnp.int32, (1, LANES), 1)
        sc = jnp.asarray(scale, BF)
        qs = q_ref[...] * sc
        dob = do_ref[...]
        lo, qh = _head_masks(qs)
        _, doh = _head_masks(dob)
        seen = (c_ref[:, :LANES], c_ref[:, LANES:])

        def step(j, carry, diag):
            dq, e0, e1 = carry
            ecars = (e0, e1)
            mask = None if diag is None else row > col + diag
            off = pl.multiple_of(j * tk, tk)
            kj = k_ref[pl.ds(off, tk), :]
            vj = v_ref[pl.ds(off, tk), :]
            zs = [_dot_nt(qh[h], kj) for h in range(2)]
            dws = [_dot_nt(doh[h], vj) for h in range(2)]
            parts = [_logistic_parts(z) for z in zs]
            lks = [p[0] if mask is None else jnp.where(mask, p[0], 0.0) for p in parts]
            sins = [_dot(_split2(lk), tri2) for lk in lks]
            ws, es = [], []
            for h in range(2):
                car = jnp.sum(jnp.where(lane == j, seen[h], 0.0), axis=1, keepdims=True)
                w = jnp.exp((lks[h] + zs[h]) + (sins[h] + car))
                w = w if mask is None else jnp.where(mask, w, 0.0)
                ws.append(w.astype(BF))
                es.append(dws[h] * w)
            pres = [_dot(_split2(e), tri_pre2) for e in es]
            dzs = []
            for h in range(2):
                t_, e = parts[h][1], es[h]
                before = pres[h] + ecars[h]
                dz = pl.reciprocal(parts[h][2], approx=True) * jnp.where(zs[h] > 0.0, e * t_ - before, e - t_ * before)
                dzs.append((dz if mask is None else jnp.where(mask, dz, 0.0)).astype(BF))
            kjs = kj * sc
            dqs = [_dot(dz, kjs) for dz in dzs]
            dks = [_dot_tn(dz, qs) for dz in dzs]
            dvs = [_dot_tn(w, dob) for w in ws]
            dk_acc[pl.ds(off, tk), :] += jnp.where(lo, dks[0], dks[1])
            dv_acc[pl.ds(off, tk), :] += jnp.where(lo, dvs[0], dvs[1])
            new_e = [ecars[h] + jnp.sum(es[h], axis=1, keepdims=True) for h in range(2)]
            return dq + jnp.where(lo, dqs[0], dqs[1]), new_e[0], new_e[1]

        reach = jnp.max(jnp.maximum(seen[0], seen[1]), axis=0, keepdims=True)
        skipped = jnp.logical_and(reach < SB_DEAD, lane < ratio * i)
        first = jnp.minimum(jnp.sum(jnp.where(skipped, 1, 0)).astype(jnp.int32), ratio * i)
        zc = jnp.zeros((tq, 1), F32)
        carry = lax.fori_loop(first, ratio * i, lambda j, c: step(j, c, None), (jnp.zeros((tq, LANES), F32), zc, zc))
        for m in range(ratio):
            carry = step(ratio * i + m, carry, m * tk)
        dq_ref[...] = carry[0].astype(BF)

        @pl.when(i == nq - 1)
        def _():
            dk_ref[...] = dk_acc[...].astype(BF)
            dv_ref[...] = dv_acc[...].astype(BF)

    out = _sds((t, dg), BF)
    return pl.pallas_call(
        body, name="sb_bwd", grid=(bl, p_n, nq),
        in_specs=[q_spec, k_spec, v_spec, car_spec, tile_spec],
        out_specs=[tile_spec, seq_spec, seq_spec], out_shape=[out, out, out],
        scratch_shapes=[pltpu.VMEM((seq, LANES), F32), pltpu.VMEM((seq, LANES), F32)],
        compiler_params=_cparams(("parallel", "parallel", "arbitrary")),
    )(qkv, qkv, qkv, cars, do)


def _bias_expand(rb):
    h = rb.shape[0]
    pad = CHUNK * N_PREV_CHUNKS
    n_f = BAND + CHUNK - 1
    f = jnp.concatenate([rb[:, MAX_REL - (CHUNK - 1):2 * MAX_REL],
                         jnp.broadcast_to(rb[:, 2 * MAX_REL:], (h, pad + CHUNK - MAX_REL))], axis=1)
    fr = f[:, ::-1]
    skew = jnp.pad(jnp.broadcast_to(fr[:, None, :], (h, CHUNK, n_f)), ((0, 0), (0, 0), (0, 1)))
    skew = skew.reshape(h, CHUNK * (n_f + 1))[:, :CHUNK * n_f].reshape(h, CHUNK, n_f)
    table = skew[:, :, CHUNK - 1:CHUNK - 1 + BAND]
    n_c = ATT_TILE // CHUNK
    width = 3 * ATT_TILE
    rows = [jnp.pad(table, ((0, 0), (0, 0), (c * CHUNK, width - BAND - c * CHUNK)), constant_values=NEG_BIAS)
            for c in range(n_c)]
    return jnp.concatenate(rows, axis=1)


def _ca_scores(qh_h, k_ref, bias_ref, h, i, scale):
    tq = ATT_TILE
    ss, offs = [], []
    for kb in range(3):
        jb = i - 2 + kb
        off = pl.multiple_of(jnp.maximum(jb, 0) * tq, tq)
        s = _dot_nt(qh_h, k_ref[pl.ds(off, tq), :]) * scale + bias_ref[h, :, kb * tq:(kb + 1) * tq]
        ss.append(jnp.where(jb >= 0, s, NEG_BIAS))
        offs.append(off)
    m = jnp.max(jnp.maximum(jnp.maximum(ss[0], ss[1]), ss[2]), axis=1, keepdims=True)
    ps = [jnp.exp(s - m) for s in ss]
    den = jnp.sum(ps[0] + ps[1] + ps[2], axis=1, keepdims=True)
    inv = 1.0 / den
    return [p * inv for p in ps], offs


def _ca_fwd(qkv, bias4, bl, seq, dg, gain, bufs=(), layer=None):
    t = bl * seq
    p_n = dg // LANES
    tq = ATT_TILE
    scale = 1.0 / math.sqrt(HEAD_DIM)
    nq, q_spec, k_spec, v_spec, tile_spec, _, g_spec = _sb_specs(bl, seq, p_n, 3 * p_n, tq)
    b_spec = pl.BlockSpec((2, tq, 3 * tq), lambda b, p, i: (p, 0, 0))
    nt = len(bufs)

    def body(q_ref, k_ref, v_ref, bias_ref, g_ref, *rest):
        o_ref, m_ref = rest[nt:nt + 2]
        i = pl.program_id(2)
        if nt:
            start, finish = _gather_plan(rest[nt + 2:2 * nt + 2], [b.shape[2] for b in bufs], rest[2 * nt + 2],
                                         rest[2 * nt + 3], layer)
            step = (pl.program_id(0) * p_n + pl.program_id(1)) * nq + i
            pl.when(step == 0)(start)
        lo, qh = _head_masks(q_ref[...])
        outs = []
        for h in range(2):
            ps, offs = _ca_scores(qh[h], k_ref, bias_ref, h, i, scale)
            acc = None
            for kb in range(3):
                pv = _dot(ps[kb].astype(BF), v_ref[pl.ds(offs[kb], tq), :])
                acc = pv if acc is None else acc + pv
            outs.append(acc)
        acc = jnp.where(lo, outs[0], outs[1])
        o_ref[...] = acc
        m_ref[...] = _rms_out(acc, lo, g_ref[...]).astype(BF)
        if nt:
            pl.when(step == bl * p_n * nq - 1)(finish)

    if not nt:
        return pl.pallas_call(
            body, name="ca_fwd", grid=(bl, p_n, nq), in_specs=[q_spec, k_spec, v_spec, b_spec, g_spec],
            out_specs=[tile_spec, tile_spec], out_shape=[_sds((t, dg), F32), _sds((t, dg), BF)],
            compiler_params=_cparams(("parallel", "parallel", "parallel")),
        )(qkv, qkv, qkv, bias4, gain)
    res = pl.pallas_call(
        body, name="ca_fwd_gather", grid=(bl, p_n, nq),
        in_specs=[q_spec, k_spec, v_spec, b_spec, g_spec] + [_ANY] * nt,
        out_specs=[tile_spec, tile_spec] + [_ANY] * nt,
        out_shape=[_sds((t, dg), F32), _sds((t, dg), BF)] + [_sds(b.shape, b.dtype) for b in bufs],
        input_output_aliases={5 + k: 2 + k for k in range(nt)},
        scratch_shapes=[pltpu.SemaphoreType.DMA((6 * nt,)), pltpu.SemaphoreType.DMA((6 * nt,))],
        compiler_params=_cparams(("arbitrary", "arbitrary", "arbitrary")),
    )(qkv, qkv, qkv, bias4, gain, *bufs)
    return res[0], res[1], list(res[2:])


def _ca_bwd(qkv, bias4, o, do, bl, seq, dg, psums=()):
    t = bl * seq
    p_n = dg // LANES
    tq = ATT_TILE
    scale = 1.0 / math.sqrt(HEAD_DIM)
    nq = seq // tq
    c0 = 3 * p_n
    q_spec = pl.BlockSpec((tq, LANES), lambda p, b, i: (b * nq + i, c0 + p))
    k_spec = pl.BlockSpec((seq, LANES), lambda p, b, i: (b, c0 + p_n + p))
    v_spec = pl.BlockSpec((seq, LANES), lambda p, b, i: (b, c0 + 2 * p_n + p))
    tile_spec = pl.BlockSpec((tq, LANES), lambda p, b, i: (b * nq + i, p))
    seq_spec = pl.BlockSpec((seq, LANES), lambda p, b, i: (b, p))
    b_spec = pl.BlockSpec((2, tq, 3 * tq), lambda p, b, i: (p, 0, 0))

    nt = len(psums)

    def body(q_ref, k_ref, v_ref, bias_ref, o_ref, do_ref, *rest):
        dq_ref, dk_ref, dv_ref, db_ref = rest[nt:nt + 4]
        dk_acc, dv_acc = rest[2 * nt + 4:2 * nt + 6]
        b = pl.program_id(1)
        i = pl.program_id(2)
        if nt:
            start, finish = _scatter_plan(rest[:nt], rest[nt + 4:2 * nt + 4], rest[2 * nt + 6], rest[2 * nt + 7])
            step = (pl.program_id(0) * bl + b) * nq + i
            pl.when(step == 0)(start)

        @pl.when(i == 0)
        def _():
            dk_acc[...] = jnp.zeros_like(dk_acc)
            dv_acc[...] = jnp.zeros_like(dv_acc)

        @pl.when((i == 0) & (b == 0))
        def _():
            db_ref[...] = jnp.zeros_like(db_ref)

        q = q_ref[...]
        dob = do_ref[...]
        lo, qh = _head_masks(q)
        _, doh = _head_masks(dob)
        prod = dob.astype(F32) * o_ref[...]
        delta = (jnp.sum(jnp.where(lo, prod, 0.0), axis=1, keepdims=True),
                 jnp.sum(jnp.where(lo, 0.0, prod), axis=1, keepdims=True))
        dqs = []
        for h in range(2):
            ps, offs = _ca_scores(qh[h], k_ref, bias_ref, h, i, scale)
            dq = None
            dks, dvs = [], []
            for kb in range(3):
                dp = _dot_nt(doh[h], v_ref[pl.ds(offs[kb], tq), :])
                ds = ps[kb] * (dp - delta[h])
                db_ref[h, :, kb * tq:(kb + 1) * tq] += ds
                dsb = (ds * scale).astype(BF)
                part = _dot(dsb, k_ref[pl.ds(offs[kb], tq), :])
                dq = part if dq is None else dq + part
                dks.append(_dot_tn(dsb, q))
                dvs.append(_dot_tn(ps[kb].astype(BF), dob))
            dqs.append(dq)
            hm = lo if h == 0 else jnp.logical_not(lo)
            for kb in range(3):
                dk_acc[pl.ds(offs[kb], tq), :] += jnp.where(hm, dks[kb], 0.0)
                dv_acc[pl.ds(offs[kb], tq), :] += jnp.where(hm, dvs[kb], 0.0)
        dq_ref[...] = jnp.where(lo, dqs[0], dqs[1]).astype(BF)

        @pl.when(i == nq - 1)
        def _():
            dk_ref[...] = dk_acc[...].astype(BF)
            dv_ref[...] = dv_acc[...].astype(BF)

        if nt:
            pl.when(step == p_n * bl * nq - 1)(finish)

    out = _sds((t, dg), BF)
    acc = [pltpu.VMEM((seq, LANES), F32), pltpu.VMEM((seq, LANES), F32)]
    if not nt:
        return pl.pallas_call(
            body, name="ca_bwd", grid=(p_n, bl, nq),
            in_specs=[q_spec, k_spec, v_spec, b_spec, tile_spec, tile_spec],
            out_specs=[tile_spec, seq_spec, seq_spec, b_spec], out_shape=[out, out, out, _sds(bias4.shape, F32)],
            scratch_shapes=acc, compiler_params=_cparams(("parallel", "arbitrary", "arbitrary")),
        )(qkv, qkv, qkv, bias4, o, do)
    res = pl.pallas_call(
        body, name="ca_bwd_scatter", grid=(p_n, bl, nq),
        in_specs=[q_spec, k_spec, v_spec, b_spec, tile_spec, tile_spec] + [_ANY] * nt,
        out_specs=[tile_spec, seq_spec, seq_spec, b_spec] + [_ANY] * nt,
        out_shape=[out, out, out, _sds(bias4.shape, F32)] + _scatter_shapes(psums),
        scratch_shapes=acc + [pltpu.SemaphoreType.DMA((3 * nt,)), pltpu.SemaphoreType.DMA((3 * nt,))],
        compiler_params=_cparams(("arbitrary", "arbitrary", "arbitrary")),
    )(qkv, qkv, qkv, bias4, o, do, *psums)
    return res[0], res[1], res[2], res[3], list(res[4:])


_ANY = pl.BlockSpec(memory_space=pl.ANY)


def _mesh_pos():
    x, y, c = lax.axis_index("x"), lax.axis_index("y"), lax.axis_index("c")
    chips = [(1 - x, y), (x, 1 - y), (1 - x, 1 - y)]
    return x, y, c, chips


def _half(ref_rows, c):
    return pl.ds(c * (ref_rows // 2), ref_rows // 2)


def _cast_into_slot(w, slot):
    l_n, r, c_n = w.shape
    tr = _tile(r, 512)

    def body(s_ref, w_ref, o_ref):
        o_ref[...] = w_ref[...].astype(BF)

    return pl.pallas_call(
        body, name="cast_into_slot", out_shape=_sds((l_n, 4, r, c_n), BF),
        grid_spec=pltpu.PrefetchScalarGridSpec(
            num_scalar_prefetch=1, grid=(l_n, r // tr),
            in_specs=[pl.BlockSpec((None, tr, c_n), lambda l, m, s: (l, m, 0))],
            out_specs=pl.BlockSpec((None, None, tr, c_n), lambda l, m, s: (l, s[0], m, 0))),
        compiler_params=_cparams(("parallel", "parallel")),
    )(slot, w)


def _gather_plan(outs, n_rows, send_sems, recv_sems, layer):
    nt = len(outs)

    def blk(t, s, cc):
        return outs[t].at[pl.ds(layer, 1), s, _half(n_rows[t], cc)]

    def copy(k, ref, to):
        return pltpu.make_async_remote_copy(src_ref=ref, dst_ref=ref, send_sem=send_sems.at[k],
                                            recv_sem=recv_sems.at[k], device_id=to, device_id_type=MESH_IDS)

    def over_ici():
        x, y, c, chips = _mesh_pos()
        return [copy(j * nt + t, blk(t, 2 * x + y, c), (px, py, c)) for j, (px, py) in enumerate(chips) for t in range(nt)]

    def start():
        for cp in over_ici():
            cp.start()

    def finish():
        x, y, c, chips = _mesh_pos()
        passed = []
        for j, (px, py) in enumerate(chips):
            for t in range(nt):
                got = blk(t, 2 * px + py, c)
                copy(j * nt + t, got, (px, py, c)).wait_recv()
                relay = copy(3 * nt + j * nt + t, got, (x, y, 1 - c))
                relay.start()
                passed.append(relay)
        for j, (px, py) in enumerate(chips):
            for t in range(nt):
                copy(3 * nt + j * nt + t, blk(t, 2 * px + py, 1 - c), (x, y, 1 - c)).wait_recv()
        for cp in over_ici() + passed:
            cp.wait_send()

    return start, finish


def _gather_weights(bufs, layer):
    nt = len(bufs)

    def body(*refs):
        start, finish = _gather_plan(refs[nt:2 * nt], [b.shape[2] for b in bufs], refs[2 * nt], refs[2 * nt + 1], layer)
        start()
        finish()

    return pl.pallas_call(
        body, name="gather_weights", in_specs=[_ANY] * nt, out_specs=[_ANY] * nt,
        out_shape=[_sds(b.shape, b.dtype) for b in bufs], input_output_aliases={t: t for t in range(nt)},
        scratch_shapes=[pltpu.SemaphoreType.DMA((6 * nt,)), pltpu.SemaphoreType.DMA((6 * nt,))],
    )(*bufs)


def _swap_halves(grads):
    nt = len(grads)

    def body(*refs):
        ins, theirs_o = refs[:nt], refs[nt:2 * nt]
        send_sems, recv_sems = refs[2 * nt:]
        x, y, c, _ = _mesh_pos()
        sends = [pltpu.make_async_remote_copy(
            src_ref=ins[t].at[:, :, _half(grads[t].shape[2], 1 - c)], dst_ref=theirs_o[t],
            send_sem=send_sems.at[t], recv_sem=recv_sems.at[t], device_id=(x, y, 1 - c), device_id_type=MESH_IDS)
            for t in range(nt)]
        for cp in sends:
            cp.start()
        for cp in sends:
            cp.wait_recv()
        for cp in sends:
            cp.wait_send()

    half = [_sds(g.shape[:2] + (g.shape[2] // 2, g.shape[3]), g.dtype) for g in grads]
    return pl.pallas_call(
        body, name="swap_halves", in_specs=[_ANY] * nt, out_specs=[_ANY] * nt, out_shape=half,
        scratch_shapes=[pltpu.SemaphoreType.DMA((nt,)), pltpu.SemaphoreType.DMA((nt,))],
    )(*grads)


def _add_pair(mine, theirs):
    l_n, s_n, r2, c_n = theirs.shape

    def body(a_ref, b_ref, o_ref):
        rows = pl.ds(pl.multiple_of(lax.axis_index("c") * r2, 16), r2)
        o_ref[...] = (a_ref[rows, :].astype(F32) + b_ref[...].astype(F32)).astype(BF)

    blk = (None, None, r2, c_n)
    return pl.pallas_call(
        body, name="add_pair", out_shape=_sds(theirs.shape, BF), grid=(l_n, s_n),
        in_specs=[pl.BlockSpec((None, None, 2 * r2, c_n), lambda l, s: (l, s, 0, 0)),
                  pl.BlockSpec(blk, lambda l, s: (l, s, 0, 0))],
        out_specs=pl.BlockSpec(blk, lambda l, s: (l, s, 0, 0)),
        compiler_params=_cparams(("parallel", "parallel")),
    )(mine, theirs)


def _scatter_chips(psums):
    nt = len(psums)

    def body(*refs):
        start, finish = _scatter_plan(refs[:nt], refs[nt:2 * nt], refs[2 * nt], refs[2 * nt + 1])
        start()
        finish()

    return pl.pallas_call(
        body, name="scatter_chips", in_specs=[_ANY] * nt, out_specs=[_ANY] * nt, out_shape=_scatter_shapes(psums),
        scratch_shapes=[pltpu.SemaphoreType.DMA((3 * nt,)), pltpu.SemaphoreType.DMA((3 * nt,))],
    )(*psums)


def _scatter_shapes(psums):
    return [_sds((3, p.shape[0]) + p.shape[2:], p.dtype) for p in psums]


def _scatter_plan(ins, got, send_sems, recv_sems):
    nt = len(ins)

    def sends():
        _, _, c, chips = _mesh_pos()
        return [pltpu.make_async_remote_copy(
            src_ref=ins[t].at[:, 2 * px + py], dst_ref=got[t].at[j], send_sem=send_sems.at[j * nt + t],
            recv_sem=recv_sems.at[j * nt + t], device_id=(px, py, c), device_id_type=MESH_IDS)
            for j, (px, py) in enumerate(chips) for t in range(nt)]

    def start():
        for cp in sends():
            cp.start()

    def finish():
        for cp in sends():
            cp.wait_recv()
        for cp in sends():
            cp.wait_send()

    return start, finish


def _add_chips(psum, got):
    l_n, _, r2, c_n = psum.shape

    def body(a_ref, g_ref, o_ref):
        c = lax.axis_index("c")
        acc = a_ref[2 * lax.axis_index("x") + lax.axis_index("y")].astype(F32)
        for j in range(3):
            acc = acc + g_ref[j].astype(F32)
        o_ref[pl.ds(pl.multiple_of(c * r2, 8), r2), :] = acc
        o_ref[pl.ds(pl.multiple_of((1 - c) * r2, 8), r2), :] = jnp.zeros_like(acc)

    return pl.pallas_call(
        body, name="add_chips", out_shape=_sds((l_n, 2 * r2, c_n), F32), grid=(l_n,),
        in_specs=[pl.BlockSpec((None, 4, r2, c_n), lambda l: (l, 0, 0, 0)),
                  pl.BlockSpec((3, None, r2, c_n), lambda l: (0, l, 0, 0))],
        out_specs=pl.BlockSpec((None, 2 * r2, c_n), lambda l: (l, 0, 0)),
        compiler_params=_cparams(("parallel",)),
    )(psum, got)


def _join_halves(fulls):
    nt = len(fulls)

    def body(*refs):
        outs = refs[nt:2 * nt]
        send_sems, recv_sems = refs[2 * nt:]
        x, y, c, _ = _mesh_pos()

        def copy(t, cc):
            blk = outs[t].at[:, _half(fulls[t].shape[1], cc)]
            return pltpu.make_async_remote_copy(
                src_ref=blk, dst_ref=blk, send_sem=send_sems.at[t], recv_sem=recv_sems.at[t],
                device_id=(x, y, 1 - c), device_id_type=MESH_IDS)

        sends = [copy(t, c) for t in range(nt)]
        for cp in sends:
            cp.start()
        for t in range(nt):
            copy(t, 1 - c).wait_recv()
        for cp in sends:
            cp.wait_send()

    return pl.pallas_call(
        body, name="join_halves", in_specs=[_ANY] * nt, out_specs=[_ANY] * nt,
        out_shape=[_sds(f.shape, f.dtype) for f in fulls], input_output_aliases={t: t for t in range(nt)},
        scratch_shapes=[pltpu.SemaphoreType.DMA((nt,)), pltpu.SemaphoreType.DMA((nt,))],
    )(*fulls)


def _allreduce_small(part):
    r, d = part.shape
    flips = [(fx, fy, fc) for fx in (0, 1) for fy in (0, 1) for fc in (0, 1) if fx + fy + fc]

    def body(p_ref, o_ref, buf, send_sems, recv_sems):
        x, y, c, _ = _mesh_pos()
        me = 4 * x + 2 * y + c
        buf[me] = p_ref[...]
        sends = []
        for k, (fx, fy, fc) in enumerate(flips):
            sends.append(pltpu.make_async_remote_copy(
                src_ref=p_ref, dst_ref=buf.at[me], send_sem=send_sems.at[k], recv_sem=recv_sems.at[k],
                device_id=(x ^ fx, y ^ fy, c ^ fc), device_id_type=MESH_IDS))
        for cp in sends:
            cp.start()
        for k, (fx, fy, fc) in enumerate(flips):
            peer = 4 * (x ^ fx) + 2 * (y ^ fy) + (c ^ fc)
            pltpu.make_async_remote_copy(
                src_ref=p_ref, dst_ref=buf.at[peer], send_sem=send_sems.at[k], recv_sem=recv_sems.at[k],
                device_id=(x ^ fx, y ^ fy, c ^ fc), device_id_type=MESH_IDS).wait_recv()
        acc = buf[0]
        for dev in range(1, 8):
            acc = acc + buf[dev]
        o_ref[...] = acc
        for cp in sends:
            cp.wait_send()

    vmem = pl.BlockSpec(memory_space=pltpu.VMEM)
    return pl.pallas_call(
        body, name="allreduce_small", in_specs=[vmem], out_specs=vmem, out_shape=_sds((r, d), F32),
        scratch_shapes=[pltpu.VMEM((8, r, d), F32), pltpu.SemaphoreType.DMA((7,)), pltpu.SemaphoreType.DMA((7,))],
    )(part)


BIG = ("ffn1_w_gate", "ffn1_w_up", "ffn1_w_down", "w_in", "w_out", "ffn2_w_gate", "ffn2_w_up", "ffn2_w_down")
SMALL = ("ln1_g", "ln1_b", "rel_bias", "sb_out_g", "ca_out_g", "ln2_g", "ln2_b", "ln3_g", "ln3_b")
ORDER = ("ffn1_w_gate", "ffn1_w_up", "ffn1_w_down", "ln1_g", "ln1_b", "w_in", "rel_bias", "sb_out_g", "ca_out_g",
         "w_out", "ln2_g", "ln2_b", "ffn2_w_gate", "ffn2_w_up", "ffn2_w_down", "ln3_g", "ln3_b")


def _pack_small(vals, depth, d):
    rows = []
    for l in range(depth):
        for n in ("ln1_g", "ln1_b", "ln2_g", "ln2_b", "ln3_g", "ln3_b"):
            rows.append(vals[n][l][None, :])
        rows.append(jnp.concatenate([vals["sb_out_g"][l], vals["ca_out_g"][l]])[None, :])
        rb = vals["rel_bias"][l]
        rows.append(jnp.pad(rb, ((0, 0), (0, d - rb.shape[1]))))
    packed = jnp.concatenate(rows, axis=0)
    pad = (-packed.shape[0]) % 8
    return jnp.pad(packed, ((0, pad), (0, 0)))


def _unpack_small(packed, depth, d, heads, n_rel):
    per = 7 + heads
    out = {n: [] for n in SMALL}
    for l in range(depth):
        base = l * per
        for i, n in enumerate(("ln1_g", "ln1_b", "ln2_g", "ln2_b", "ln3_g", "ln3_b")):
            out[n].append(packed[base + i])
        out["sb_out_g"].append(packed[base + 6, :d // 2])
        out["ca_out_g"].append(packed[base + 6, d // 2:])
        out["rel_bias"].append(packed[base + 7:base + 7 + heads, :n_rel])
    return {n: jnp.stack(v) for n, v in out.items()}


def kernel(x, ffn1_w_gate, ffn1_w_up, ffn1_w_down, ln1_g, ln1_b, w_in, rel_bias, sb_out_g, ca_out_g, w_out, ln2_g, ln2_b, ffn2_w_gate, ffn2_w_up, ffn2_w_down, ln3_g, ln3_b, loss_target, m_ffn1_w_gate, m_ffn1_w_up, m_ffn1_w_down, m_ln1_g, m_ln1_b, m_w_in, m_rel_bias, m_sb_out_g, m_ca_out_g, m_w_out, m_ln2_g, m_ln2_b, m_ffn2_w_gate, m_ffn2_w_up, m_ffn2_w_down, m_ln3_g, m_ln3_b, v_ffn1_w_gate, v_ffn1_w_up, v_ffn1_w_down, v_ln1_g, v_ln1_b, v_w_in, v_rel_bias, v_sb_out_g, v_ca_out_g, v_w_out, v_ln2_g, v_ln2_b, v_ffn2_w_gate, v_ffn2_w_up, v_ffn2_w_down, v_ln3_g, v_ln3_b):
    w = dict(ffn1_w_gate=ffn1_w_gate, ffn1_w_up=ffn1_w_up, ffn1_w_down=ffn1_w_down, ln1_g=ln1_g, ln1_b=ln1_b, w_in=w_in, rel_bias=rel_bias, sb_out_g=sb_out_g, ca_out_g=ca_out_g, w_out=w_out, ln2_g=ln2_g, ln2_b=ln2_b, ffn2_w_gate=ffn2_w_gate, ffn2_w_up=ffn2_w_up, ffn2_w_down=ffn2_w_down, ln3_g=ln3_g, ln3_b=ln3_b)
    mom = dict(ffn1_w_gate=m_ffn1_w_gate, ffn1_w_up=m_ffn1_w_up, ffn1_w_down=m_ffn1_w_down, ln1_g=m_ln1_g, ln1_b=m_ln1_b, w_in=m_w_in, rel_bias=m_rel_bias, sb_out_g=m_sb_out_g, ca_out_g=m_ca_out_g, w_out=m_w_out, ln2_g=m_ln2_g, ln2_b=m_ln2_b, ffn2_w_gate=m_ffn2_w_gate, ffn2_w_up=m_ffn2_w_up, ffn2_w_down=m_ffn2_w_down, ln3_g=m_ln3_g, ln3_b=m_ln3_b)
    var = dict(ffn1_w_gate=v_ffn1_w_gate, ffn1_w_up=v_ffn1_w_up, ffn1_w_down=v_ffn1_w_down, ln1_g=v_ln1_g, ln1_b=v_ln1_b, w_in=v_w_in, rel_bias=v_rel_bias, sb_out_g=v_sb_out_g, ca_out_g=v_ca_out_g, w_out=v_w_out, ln2_g=v_ln2_g, ln2_b=v_ln2_b, ffn2_w_gate=v_ffn2_w_gate, ffn2_w_up=v_ffn2_w_up, ffn2_w_down=v_ffn2_w_down, ln3_g=v_ln3_g, ln3_b=v_ln3_b)

    bl, seq, d = x.shape
    t = bl * seq
    depth = w_in.shape[0]
    dg = d // 2
    heads = dg // HEAD_DIM
    alpha = (2 * depth) ** 0.25
    n_rel = rel_bias.shape[-1]

    chip = (2 * lax.axis_index("x") + lax.axis_index("y")).astype(jnp.int32).reshape(1)
    bufs = _gather_weights([_cast_into_slot(w[n], chip) for n in BIG], 0)

    qd = dg // 2
    shards = 4
    ffn_terms = [(0, s, None, s) for s in range(shards)]
    out_terms = [(i, None, (k * qd, (k + 1) * qd), 2 * i + k) for i in range(2) for k in range(2)]
    bias_all, bias_vjp = jax.vjp(jax.vmap(_bias_expand), rel_bias)

    xs = x.reshape(t, d)
    saved = []
    for l in range(depth):
        lw = {n: _LayerOf((b, l)) for n, b in zip(BIG, bufs)}
        row = lambda a: a[l][None, :]
        g1, u1, h1 = _ffn_up(xs, lw["ffn1_w_gate"], lw["ffn1_w_up"])
        x1, xh1, rs1 = _proj_ln("ffn_down_ln", [h1], lw["ffn1_w_down"], ffn_terms, xs, row(ln1_g), row(ln1_b),
                                FFN_RESIDUAL, alpha)
        qkv = _qkv_proj(x1, lw["w_in"])
        o_sb, m_sb, c_sb = _sb_fwd(qkv, bl, seq, dg, row(sb_out_g))
        if l + 1 < depth:
            o_ca, m_ca, bufs = _ca_fwd(qkv, bias_all[l], bl, seq, dg, row(ca_out_g), bufs, l + 1)
            lw = {n: _LayerOf((b, l)) for n, b in zip(BIG, bufs)}
        else:
            o_ca, m_ca = _ca_fwd(qkv, bias_all[l], bl, seq, dg, row(ca_out_g))
        x2, xh2, rs2 = _proj_ln("attn_out_ln", [m_sb, m_ca], lw["w_out"], out_terms, x1, row(ln2_g), row(ln2_b),
                                1.0, alpha)
        g2, u2, h2 = _ffn_up(x2, lw["ffn2_w_gate"], lw["ffn2_w_up"])
        x3, xh3, rs3 = _proj_ln("ffn_down_ln", [h2], lw["ffn2_w_down"], ffn_terms, x2, row(ln3_g), row(ln3_b),
                                FFN_RESIDUAL, alpha)
        saved.append(dict(xin=xs, g1=g1, u1=u1, h1=h1, x1=x1, xh1=xh1, rs1=rs1, qkv=qkv, o_sb=o_sb, m_sb=m_sb,
                          c_sb=c_sb, o_ca=o_ca, m_ca=m_ca, x2=x2, xh2=xh2, rs2=rs2, g2=g2, u2=u2, h2=h2, xh3=xh3,
                          rs3=rs3))
        xs = x3

    last = saved[-1]
    dr3, drs3, pg, pb, sq = _loss_grad(xs, loss_target.reshape(t, d), last["xh3"], last["rs3"],
                                       ln3_g[depth - 1][None, :], FFN_RESIDUAL)
    loss = lax.psum(0.5 * jnp.sum(sq) / d, ("x", "y", "c"))

    big_grads = {n: [None] * depth for n in BIG}
    small_grads = {n: [None] * depth for n in SMALL}
    s8 = lambda a: jnp.sum(a, axis=0)
    wd_ = 3 * d // shards
    in_pieces = [_shard_cols(6, dg, s, wd_) for s in range(shards)]
    dbias = [None] * depth
    chip_sums, arrived = [None] * depth, [None] * depth
    for l in reversed(range(depth)):
        sv = saved[l]
        lw = {n: _LayerOf((b, l)) for n, b in zip(BIG, bufs)}
        row = lambda a: a[l][None, :]
        small_grads["ln3_g"][l], small_grads["ln3_b"][l] = s8(pg), s8(pb)
        dgt, dut = _swiglu_bwd(drs3, lw["ffn2_w_down"], sv["g2"], sv["u2"])
        dr2, drs2, pg, pb = _ffn_dx(dgt, dut, lw["ffn2_w_gate"], lw["ffn2_w_up"], dr3, alpha,
                                    (sv["xh2"], sv["rs2"], row(ln2_g), 1.0))
        small_grads["ln2_g"][l], small_grads["ln2_b"][l] = s8(pg), s8(pb)
        big_grads["ffn2_w_down"][l] = _wgrad("wgrad_down", [sv["h2"]], [drs3], shards, _lshape(lw["ffn2_w_down"])[1:],
                                             lambda a, s: a[0][s], lambda b, s: b[0][...])
        big_grads["ffn2_w_gate"][l] = _wgrad("wgrad_up", [sv["x2"]], [dgt], shards, _lshape(lw["ffn2_w_gate"])[1:],
                                             lambda a, s: a[0][...], lambda b, s: b[0][s])
        big_grads["ffn2_w_up"][l] = _wgrad("wgrad_up", [sv["x2"]], [dut], shards, _lshape(lw["ffn2_w_up"])[1:],
                                           lambda a, s: a[0][...], lambda b, s: b[0][s])
        do_sb, do_ca, pgs, pgc = _attn_dmix(drs2, lw["w_out"], sv["o_sb"], sv["o_ca"], row(sb_out_g), row(ca_out_g))
        small_grads["sb_out_g"][l], small_grads["ca_out_g"][l] = s8(pgs), s8(pgc)
        big_grads["w_out"][l] = _wgrad(
            "wgrad_out", [sv["m_sb"], sv["m_ca"]], [drs2], shards, (qd, d),
            lambda a, s: a[s // 2][:, (s % 2) * qd:(s % 2 + 1) * qd], lambda b, s: b[0][...])
        dqa, dka, dva = _sb_bwd(sv["qkv"], sv["c_sb"], do_sb, bl, seq, dg)
        if l + 1 < depth:
            dqb, dkb, dvb, dbias[l], arrived[l + 1] = _ca_bwd(sv["qkv"], bias_all[l], sv["o_ca"], do_ca, bl, seq, dg,
                                                             chip_sums[l + 1])
        else:
            dqb, dkb, dvb, dbias[l] = _ca_bwd(sv["qkv"], bias_all[l], sv["o_ca"], do_ca, bl, seq, dg)
        dqkv = [dqa, dka, dva, dqb, dkb, dvb]
        dr1, drs1, pg, pb = _attn_dx(dqkv, lw["w_in"], dr2, alpha, (sv["xh1"], sv["rs1"], row(ln1_g), FFN_RESIDUAL))
        small_grads["ln1_g"][l], small_grads["ln1_b"][l] = s8(pg), s8(pb)
        big_grads["w_in"][l] = _wgrad(
            "wgrad_in", [sv["x1"]], dqkv, shards, (d, wd_), lambda a, s: a[0][...],
            lambda b, s: jnp.concatenate([b[gi][:, lo:hi] for gi, lo, hi in in_pieces[s]], axis=1))
        dgt, dut = _swiglu_bwd(drs1, lw["ffn1_w_down"], sv["g1"], sv["u1"])
        if l > 0:
            prev = saved[l - 1]
            dr3, drs3, pg, pb = _ffn_dx(dgt, dut, lw["ffn1_w_gate"], lw["ffn1_w_up"], dr1, alpha,
                                        (prev["xh3"], prev["rs3"], ln3_g[l - 1][None, :], FFN_RESIDUAL))
        else:
            grad_x = _ffn_dx(dgt, dut, lw["ffn1_w_gate"], lw["ffn1_w_up"], dr1, alpha, None).reshape(bl, seq, d)
        big_grads["ffn1_w_down"][l] = _wgrad("wgrad_down", [sv["h1"]], [drs1], shards, _lshape(lw["ffn1_w_down"])[1:],
                                             lambda a, s: a[0][s], lambda b, s: b[0][...])
        big_grads["ffn1_w_gate"][l] = _wgrad("wgrad_up", [sv["xin"]], [dgt], shards, _lshape(lw["ffn1_w_gate"])[1:],
                                             lambda a, s: a[0][...], lambda b, s: b[0][s])
        big_grads["ffn1_w_up"][l] = _wgrad("wgrad_up", [sv["xin"]], [dut], shards, _lshape(lw["ffn1_w_up"])[1:],
                                           lambda a, s: a[0][...], lambda b, s: b[0][s])
        partial = [big_grads[n][l][None] for n in BIG]
        chip_sums[l] = [_add_pair(a, b) for a, b in zip(partial, _swap_halves(partial))]
    arrived[0] = _scatter_chips(chip_sums[0])
    rel_grads = bias_vjp(jnp.stack(dbias))[0]
    for l in range(depth):
        small_grads["rel_bias"][l] = rel_grads[l]

    halves = [jnp.concatenate([_add_chips(chip_sums[l][k], arrived[l][k]) for l in range(depth)], axis=0)
              for k in range(len(BIG))]
    grads = dict(zip(BIG, _join_halves(halves)))

    packed = _allreduce_small(_pack_small({n: jnp.stack(small_grads[n]) for n in SMALL}, depth, d))
    grads.update(_unpack_small(packed, depth, d, heads, n_rel))

    delta, new_m, new_v = {}, {}, {}
    for n in BIG:
        delta[n], new_m[n], new_v[n] = _adamw(w[n], grads[n], mom[n], var[n])
    pw, pm, pv = (_pack_small({n: src[n] for n in SMALL}, depth, d) for src in (w, mom, var))
    sd, sm, sv_ = _adamw(pw, packed, pm, pv)
    for dst, src in ((delta, sd), (new_m, sm), (new_v, sv_)):
        dst.update(_unpack_small(src, depth, d, heads, n_rel))

    return (loss, grad_x, *[grads[n] for n in ORDER], *[delta[n] for n in ORDER],
            *[new_m[n] for n in ORDER], *[new_v[n] for n in ORDER])
```

```python
import functools
import math

import jax
import jax.numpy as jnp
from jax import lax
from jax.experimental import pallas as pl
from jax.experimental.pallas import tpu as pltpu

F32 = jnp.float32
BF = jnp.bfloat16

CHUNK = 64
N_PREV_CHUNKS = 8
BAND = CHUNK * (N_PREV_CHUNKS + 1)
MAX_REL = 128
HEAD_DIM = 64
FFN_RESIDUAL = 0.5
LN_EPS = 1e-5
RMS_EPS = 1e-6
ADAM_LR = 0.001
ADAM_B1 = 0.9
ADAM_B2 = 0.999
ADAM_EPS = 1e-08
ADAM_WD = 0.01
ADAM_STEP = 10

LANES = 128
ATT_TILE = 4 * CHUNK
SB_TQ = 512
SB_TK = 256
SB_DEAD = -105.0
SB_UNSEEN = -1e30
NEG_BIAS = -1e30
VMEM_LIMIT = 56 * 1024 * 1024
ROW_TILE = 512
MESH_IDS = pl.DeviceIdType.MESH

_NT = (((1,), (1,)), ((), ()))
_TN = (((0,), (0,)), ((), ()))


def _dot(a, b):
    return jnp.dot(a, b, preferred_element_type=F32)


def _dot_nt(a, b):
    return lax.dot_general(a, b, _NT, preferred_element_type=F32)


def _dot_tn(a, b):
    return lax.dot_general(a, b, _TN, preferred_element_type=F32)


def _tile(n, pref):
    t = min(n, pref)
    while n % t:
        t //= 2
    return t


def _cparams(sem):
    return pltpu.CompilerParams(dimension_semantics=sem, vmem_limit_bytes=VMEM_LIMIT)


def _sds(shape, dtype):
    return jax.ShapeDtypeStruct(tuple(shape), dtype)


def _sigmoid(v):
    return pl.reciprocal(1.0 + jnp.exp(-v), approx=True)


def _rows8(v):
    tm, c = v.shape
    return v.reshape(tm // 8, 8, c).sum(axis=0)


class _LayerOf(tuple):
    pass


def _lshape(item):
    return item[0].shape[1:] if isinstance(item, _LayerOf) else item.shape


def _accumulate(m, ref, part):
    @pl.when(m == 0)
    def _():
        ref[...] = part

    @pl.when(m > 0)
    def _():
        ref[...] += part


def _ln_bwd_tail(m, dy, xh_ref, rs_ref, g_ref, scale, dr_ref, drs_ref, dg_ref, db_ref):
    xhv = xh_ref[...]
    dxh = dy * g_ref[...]
    m1 = jnp.mean(dxh, axis=-1, keepdims=True)
    m2 = jnp.mean(dxh * xhv, axis=-1, keepdims=True)
    dr = rs_ref[...] * (dxh - m1 - xhv * m2)
    dr_ref[...] = dr
    drs_ref[...] = (scale * dr).astype(BF)
    _accumulate(m, dg_ref, _rows8(dy * xhv))
    _accumulate(m, db_ref, _rows8(dy))


def _rowcall(name, body, n_rows, tm, row_ins, res_ins, row_outs, acc_outs=()):
    def rspec(shape):
        if len(shape) == 2:
            return pl.BlockSpec((tm, shape[1]), lambda m: (m, 0))
        return pl.BlockSpec((shape[0], tm, shape[2]), lambda m: (0, m, 0))

    def cspec(shape, single=False):
        zeros = (0,) * len(shape)
        mode = pl.Buffered(1) if single else None
        return pl.BlockSpec(tuple(shape), lambda m: zeros, pipeline_mode=mode)

    def split(items, mk):
        arrs, specs = [], []
        for it in items:
            if isinstance(it, _LayerOf):
                arr, l = it
                tail = (0,) * (arr.ndim - 1)
                arrs.append(arr)
                specs.append(pl.BlockSpec((None,) + arr.shape[1:], functools.partial(lambda m, l, tail: (l,) + tail, l=l, tail=tail),
                                          pipeline_mode=pl.Buffered(1)))
            elif isinstance(it, tuple):
                arrs.append(it[0])
                specs.append(it[1])
            else:
                arrs.append(it)
                specs.append(mk(it.shape))
        return arrs, specs

    ra, rs = split(row_ins, rspec)
    ca, cs = split(res_ins, functools.partial(cspec, single=True))
    out_specs = [rspec(s.shape) for s in row_outs] + [cspec(s.shape) for s in acc_outs]
    sem = ("arbitrary",) if acc_outs else ("parallel",)
    return pl.pallas_call(
        body, name=name, grid=(n_rows // tm,), in_specs=rs + cs, out_specs=out_specs,
        out_shape=list(row_outs) + list(acc_outs), compiler_params=_cparams(sem),
    )(*ra, *ca)


def _ffn_up(x, wg, wu):
    t, _ = x.shape
    s_n, _, fs = _lshape(wg)
    tm = _tile(t, ROW_TILE)

    def body(x_ref, wg_ref, wu_ref, g_ref, u_ref, h_ref):
        xb = x_ref[...].astype(BF)
        for s in range(s_n):
            g = _dot(xb, wg_ref[s])
            u = _dot(xb, wu_ref[s])
            g_ref[s] = g.astype(BF)
            u_ref[s] = u.astype(BF)
            h_ref[s] = (g * _sigmoid(g) * u).astype(BF)

    out = _sds((s_n, t, fs), BF)
    return _rowcall("ffn_up", body, t, tm, [x], [wg, wu], [out, out, out])


def _proj_ln(name, a_list, w, terms, xres, gain, bias, scale, alpha):
    t, d = xres.shape
    tm = _tile(t, ROW_TILE)
    na = len(a_list)

    def body(*refs):
        a_refs = refs[:na]
        x_ref, w_ref, g_ref, bb_ref = refs[na:na + 4]
        y_ref, xh_ref, rs_ref = refs[na + 4:]
        acc = None
        for i, s, cols, sw in terms:
            a = a_refs[i][s] if cols is None else a_refs[i][:, cols[0]:cols[1]]
            p = _dot(a.astype(BF), w_ref[sw])
            acc = p if acc is None else acc + p
        if scale != 1.0:
            acc = scale * acc
        r = alpha * x_ref[...] + acc
        mu = jnp.mean(r, axis=-1, keepdims=True)
        xc = r - mu
        var = jnp.mean(xc * xc, axis=-1, keepdims=True)
        rstd = lax.rsqrt(var + LN_EPS)
        xh = xc * rstd
        y_ref[...] = xh * g_ref[...] + bb_ref[...]
        xh_ref[...] = xh
        rs_ref[...] = rstd

    outs = [_sds((t, d), F32), _sds((t, d), F32), _sds((t, 1), F32)]
    return _rowcall(name, body, t, tm, list(a_list) + [xres], [w, gain, bias], outs)


def _shard_cols(groups, dg, s, width):
    pieces, lo = [], s * width
    while lo < (s + 1) * width:
        gi = lo // dg
        hi = min((gi + 1) * dg, (s + 1) * width)
        pieces.append((gi, lo - gi * dg, hi - gi * dg))
        lo = hi
    assert all(p[0] < groups for p in pieces)
    return pieces


def _qkv_proj(x, win):
    t, _ = x.shape
    s_n, _, wd_ = _lshape(win)
    tm = _tile(t, ROW_TILE)

    def body(x_ref, w_ref, o_ref):
        xb = x_ref[...].astype(BF)
        for s in range(s_n):
            o_ref[:, s * wd_:(s + 1) * wd_] = _dot(xb, w_ref[s]).astype(BF)

    return _rowcall("qkv_proj", body, t, tm, [x], [win], [_sds((t, s_n * wd_), BF)])[0]


def _dx_call(name, rows, res, matmul, resid, alpha, ln):
    t, d = resid.shape
    tm = _tile(t, ROW_TILE)
    nr, nc = len(rows), len(res)

    def body(*refs):
        r_refs, resid_ref = refs[:nr], refs[nr]
        if ln is None:
            c_refs, outs = refs[nr + 1:nr + 1 + nc], refs[nr + 1 + nc:]
            outs[0][...] = matmul(r_refs, c_refs) + alpha * resid_ref[...]
        else:
            xh_ref, rs_ref = refs[nr + 1:nr + 3]
            c_refs, g_ref = refs[nr + 3:nr + 3 + nc], refs[nr + 3 + nc]
            outs = refs[nr + 4 + nc:]
            dx = matmul(r_refs, c_refs) + alpha * resid_ref[...]
            _ln_bwd_tail(pl.program_id(0), dx, xh_ref, rs_ref, g_ref, ln[3], *outs)

    if ln is None:
        return _rowcall(name, body, t, tm, list(rows) + [resid], list(res), [_sds((t, d), F32)])[0]
    outs = [_sds((t, d), F32), _sds((t, d), BF)]
    accs = [_sds((8, d), F32), _sds((8, d), F32)]
    return _rowcall(name, body, t, tm, list(rows) + [resid, ln[0], ln[1]], list(res) + [ln[2]], outs, accs)


def _ffn_dx(dg_, du_, wg, wu, resid, alpha, ln):
    s_n = dg_.shape[0]

    def matmul(r, c):
        acc = None
        for i in range(2):
            for s in range(s_n):
                p = _dot_nt(r[i][s], c[i][s])
                acc = p if acc is None else acc + p
        return acc

    return _dx_call("ffn_dx", [dg_, du_], [wg, wu], matmul, resid, alpha, ln)


def _attn_dx(dqkv, win, resid, alpha, ln):
    s_n, _, wd_ = _lshape(win)
    dg = dqkv[0].shape[1]
    pieces = [_shard_cols(len(dqkv), dg, s, wd_) for s in range(s_n)]

    def matmul(r, c):
        acc = None
        for s in range(s_n):
            a = jnp.concatenate([r[gi][:, lo:hi] for gi, lo, hi in pieces[s]], axis=1)
            p = _dot_nt(a, c[0][s])
            acc = p if acc is None else acc + p
        return acc

    return _dx_call("attn_dx", list(dqkv), [win], matmul, resid, alpha, ln)


def _head_rms_bwd(dm, ov, gain, lo):
    ss = ov * ov
    s0 = jnp.sum(jnp.where(lo, ss, 0.0), axis=1, keepdims=True)
    s1 = jnp.sum(jnp.where(lo, 0.0, ss), axis=1, keepdims=True)
    r = jnp.where(lo, lax.rsqrt(s0 / HEAD_DIM + RMS_EPS), lax.rsqrt(s1 / HEAD_DIM + RMS_EPS))
    n = ov * r
    dn = dm * gain
    tt = dn * n
    t0 = jnp.sum(jnp.where(lo, tt, 0.0), axis=1, keepdims=True) / HEAD_DIM
    t1 = jnp.sum(jnp.where(lo, 0.0, tt), axis=1, keepdims=True) / HEAD_DIM
    return r * (dn - n * jnp.where(lo, t0, t1)), dm * n


def _attn_dmix(drs, wout, o_sb, o_ca, g_sb, g_ca):
    t, dg = o_sb.shape
    s_n, qd, _ = _lshape(wout)
    tm = _tile(t, ROW_TILE)
    per = dg // qd

    def body(d_ref, osb_ref, oca_ref, w_ref, gs_ref, gc_ref, dsb_ref, dca_ref, pgs_ref, pgc_ref):
        m = pl.program_id(0)
        dv = d_ref[...]
        lo = lax.broadcasted_iota(jnp.int32, (1, LANES), 1) < HEAD_DIM
        for grp, (o_ref, g_ref, do_ref, pg_ref) in enumerate(((osb_ref, gs_ref, dsb_ref, pgs_ref),
                                                               (oca_ref, gc_ref, dca_ref, pgc_ref))):
            dm = jnp.concatenate([_dot_nt(dv, w_ref[grp * per + k]) for k in range(per)], axis=1)
            parts = []
            for gi in range(dg // LANES):
                sl = slice(gi * LANES, (gi + 1) * LANES)
                do, pg = _head_rms_bwd(dm[:, sl], o_ref[:, sl], g_ref[:, sl], lo)
                do_ref[:, sl] = do.astype(BF)
                parts.append(_rows8(pg))
            _accumulate(m, pg_ref, jnp.concatenate(parts, axis=1))

    outs = [_sds((t, dg), BF), _sds((t, dg), BF)]
    accs = [_sds((8, dg), F32), _sds((8, dg), F32)]
    return _rowcall("attn_dmix", body, t, tm, [drs, o_sb, o_ca], [wout, g_sb, g_ca], outs, accs)


def _swiglu_bwd(drs, wd, g, u):
    s_n, t, fs = g.shape
    tm = _tile(t, ROW_TILE)

    def body(d_ref, g_ref, u_ref, wd_ref, dg_ref, du_ref):
        dv = d_ref[...]
        rc = _tile(tm, 128)
        for s in range(s_n):
            dh_all = _dot_nt(dv, wd_ref[s])
            for r0 in range(0, tm, rc):
                dh = dh_all[r0:r0 + rc]
                gv = g_ref[s, r0:r0 + rc, :].astype(F32)
                uv = u_ref[s, r0:r0 + rc, :].astype(F32)
                sg = _sigmoid(gv)
                gs = gv * sg
                dg_ref[s, r0:r0 + rc, :] = (dh * uv * (sg + gs * (1.0 - sg))).astype(BF)
                du_ref[s, r0:r0 + rc, :] = (dh * gs).astype(BF)

    out = _sds((s_n, t, fs), BF)
    return _rowcall("swiglu_bwd", body, t, tm, [drs, g, u], [wd], [out, out])


def _loss_grad(y, target, xh, rstd, gain, scale):
    t, d = y.shape
    tm = _tile(t, ROW_TILE)

    def body(y_ref, t_ref, xh_ref, rs_ref, g_ref, dr_ref, drs_ref, dg_ref, db_ref, sq_ref):
        m = pl.program_id(0)
        diff = y_ref[...] - t_ref[...]
        _ln_bwd_tail(m, diff / d, xh_ref, rs_ref, g_ref, scale, dr_ref, drs_ref, dg_ref, db_ref)
        _accumulate(m, sq_ref, _rows8(diff * diff))

    outs = [_sds((t, d), F32), _sds((t, d), BF)]
    accs = [_sds((8, d), F32)] * 3
    return _rowcall("loss_grad", body, t, tm, [y, target, xh, rstd], [gain], outs, accs)


def _adamw(w, g, m, v):
    shape = w.shape
    c = shape[-1]
    rows = math.prod(shape[:-1])
    tm = _tile(rows, 512)

    def body(w_ref, g_ref, m_ref, v_ref, d_ref, nm_ref, nv_ref):
        gv = g_ref[...]
        nm = ADAM_B1 * m_ref[...] + (1.0 - ADAM_B1) * gv
        nv = ADAM_B2 * v_ref[...] + (1.0 - ADAM_B2) * (gv * gv)
        m_hat = nm / (1.0 - ADAM_B1 ** ADAM_STEP)
        v_hat = nv / (1.0 - ADAM_B2 ** ADAM_STEP)
        d_ref[...] = -ADAM_LR * (m_hat / (jnp.sqrt(v_hat) + ADAM_EPS) + ADAM_WD * w_ref[...])
        nm_ref[...] = nm
        nv_ref[...] = nv

    flat = [a.reshape(rows, c) for a in (w, g, m, v)]
    out = _sds((rows, c), F32)
    res = _rowcall("adamw", body, rows, tm, flat, [], [out, out, out])
    return tuple(r.reshape(shape) for r in res)


def _wgrad(name, a_list, b_list, out_sub, sub_shape, a_get, b_get):
    t = a_list[0].shape[-2]
    tk = _tile(t, 1024)
    na, nb = len(a_list), len(b_list)
    nk = t // tk

    def tspec(shape):
        if len(shape) == 2:
            return pl.BlockSpec((tk, shape[1]), lambda k: (k, 0))
        return pl.BlockSpec((shape[0], tk, shape[2]), lambda k: (0, k, 0))

    def body(*refs):
        a_refs = refs[:na]
        b_refs = refs[na:na + nb]
        o_ref = refs[na + nb]
        acc_ref = refs[na + nb + 1]
        k = pl.program_id(0)

        @pl.when(k == 0)
        def _():
            acc_ref[...] = jnp.zeros_like(acc_ref)

        for s in range(out_sub):
            acc_ref[s] += _dot_tn(a_get(a_refs, s).astype(BF), b_get(b_refs, s).astype(BF))

        @pl.when(k == nk - 1)
        def _():
            o_ref[...] = acc_ref[...].astype(BF)

    full = (out_sub,) + tuple(sub_shape)
    return pl.pallas_call(
        body, name=name, grid=(nk,), in_specs=[tspec(a.shape) for a in a_list] + [tspec(b.shape) for b in b_list],
        out_specs=pl.BlockSpec(full, lambda k: (0, 0, 0)), out_shape=_sds(full, BF),
        scratch_shapes=[pltpu.VMEM(full, F32)], compiler_params=_cparams(("arbitrary",)),
    )(*a_list, *b_list)


def _head_masks(x):
    lo = lax.broadcasted_iota(jnp.int32, (1, LANES), 1) < HEAD_DIM
    zero = jnp.zeros_like(x)
    return lo, (jnp.where(lo, x, zero), jnp.where(lo, zero, x))


def _logistic_parts(z):
    nz = -z
    t = jnp.exp(jnp.minimum(z, nz))
    one_t = 1.0 + t
    return jnp.minimum(nz, 0.0) - jnp.log(one_t), t, one_t


def _split2(x):
    hi = x.astype(BF)
    lo = (x - hi.astype(F32)).astype(BF)
    return jnp.concatenate([hi, lo], axis=1)


def _tri2(cond):
    tri = jnp.where(cond, 1.0, 0.0).astype(BF)
    return jnp.concatenate([tri, tri], axis=0)


def _sb_specs(bl, seq, p_n, col0, tq):
    nq = seq // tq
    q_spec = pl.BlockSpec((tq, LANES), lambda b, p, i: (b * nq + i, col0 + p))
    k_spec = pl.BlockSpec((seq, LANES), lambda b, p, i: (b, col0 + p_n + p))
    v_spec = pl.BlockSpec((seq, LANES), lambda b, p, i: (b, col0 + 2 * p_n + p))
    tile_spec = pl.BlockSpec((tq, LANES), lambda b, p, i: (b * nq + i, p))
    seq_spec = pl.BlockSpec((seq, LANES), lambda b, p, i: (b, p))
    g_spec = pl.BlockSpec((1, LANES), lambda b, p, i: (0, p))
    return nq, q_spec, k_spec, v_spec, tile_spec, seq_spec, g_spec


def _rms_out(acc, lo, gain):
    ss = acc * acc
    s0 = jnp.sum(jnp.where(lo, ss, 0.0), axis=1, keepdims=True)
    s1 = jnp.sum(jnp.where(lo, 0.0, ss), axis=1, keepdims=True)
    r = jnp.where(lo, lax.rsqrt(s0 / HEAD_DIM + RMS_EPS), lax.rsqrt(s1 / HEAD_DIM + RMS_EPS))
    return acc * r * gain


def _sb_fwd(qkv, bl, seq, dg, gain):
    t = bl * seq
    p_n = dg // LANES
    tq, tk = min(SB_TQ, seq), SB_TK
    ratio = tq // tk
    scale = 1.0 / math.sqrt(HEAD_DIM)
    nq, q_spec, k_spec, v_spec, tile_spec, _, g_spec = _sb_specs(bl, seq, p_n, 0, tq)
    assert seq // tk <= LANES

    def body(q_ref, k_ref, v_ref, g_ref, o_ref, m_ref, c_ref):
        i = pl.program_id(2)
        row = lax.broadcasted_iota(jnp.int32, (tq, tk), 0)
        col = lax.broadcasted_iota(jnp.int32, (tq, tk), 1)
        krow = lax.broadcasted_iota(jnp.int32, (tk, tk), 0)
        kcol = lax.broadcasted_iota(jnp.int32, (tk, tk), 1)
        tri2 = _tri2(krow > kcol)
        lo, qh = _head_masks(q_ref[...] * jnp.asarray(scale, BF))
        lane = lax.broadcasted_iota(jnp.int32, (1, LANES), 1)

        def step(j, carry, diag):
            acc, c0, c1, s0, s1 = carry
            cars, seen = (c0, c1), (s0, s1)
            mask = None if diag is None else row > col + diag
            off = pl.multiple_of(j * tk, tk)
            kj = k_ref[pl.ds(off, tk), :]
            vj = v_ref[pl.ds(off, tk), :]
            zs = [_dot_nt(qh[h], kj) for h in range(2)]
            lks = [_logistic_parts(z)[0] for z in zs]
            if mask is not None:
                lks = [jnp.where(mask, lk, 0.0) for lk in lks]
            sins = [_dot(_split2(lk), tri2) for lk in lks]
            ws = []
            for h in range(2):
                w = jnp.exp((lks[h] + zs[h]) + (sins[h] + cars[h]))
                ws.append((w if mask is None else jnp.where(mask, w, 0.0)).astype(BF))
            pvs = [_dot(w, vj) for w in ws]
            new_seen = [jnp.where(lane == j, cars[h], seen[h]) for h in range(2)]
            new_cars = [cars[h] + jnp.sum(lks[h], axis=1, keepdims=True) for h in range(2)]
            return acc + jnp.where(lo, pvs[0], pvs[1]), new_cars[0], new_cars[1], new_seen[0], new_seen[1]

        def corner_step(j):
            top = tq - tk
            mask = krow > kcol
            kj = k_ref[pl.ds(pl.multiple_of(j * tk, tk), tk), :]
            vj = v_ref[pl.ds(pl.multiple_of(j * tk, tk), tk), :]
            zs = [_dot_nt(qh[h][top:], kj) for h in range(2)]
            lks = [jnp.where(mask, _logistic_parts(z)[0], 0.0) for z in zs]
            sins = [_dot(_split2(lk), tri2) for lk in lks]
            ws = [jnp.where(mask, jnp.exp((lks[h] + zs[h]) + sins[h]), 0.0).astype(BF) for h in range(2)]
            pvs = [_dot(w, vj) for w in ws]
            pad = lambda low: jnp.concatenate([jnp.zeros((top,) + low.shape[1:], F32), low], axis=0)
            cars = [pad(jnp.sum(lk, axis=1, keepdims=True)) for lk in lks]
            seen = jnp.where(lane == j, 0.0, jnp.full((tq, LANES), SB_UNSEEN, F32))
            return pad(jnp.where(lo, pvs[0], pvs[1])), cars[0], cars[1], seen, seen

        if ratio == 2:
            carry = step(ratio * i, corner_step(ratio * i + 1), 0)
        else:
            zc = jnp.zeros((tq, 1), F32)
            unseen = jnp.full((tq, LANES), SB_UNSEEN, F32)
            carry = (jnp.zeros((tq, LANES), F32), zc, zc, unseen, unseen)
            for m in reversed(range(ratio)):
                carry = step(ratio * i + m, carry, m * tk)

        def alive(state):
            j, c = state
            return jnp.logical_and(j >= 0, jnp.max(jnp.maximum(c[1], c[2])) >= SB_DEAD)

        _, carry = lax.while_loop(alive, lambda st: (st[0] - 1, step(st[0], st[1], None)), (ratio * i - 1, carry))
        acc = carry[0]
        o_ref[...] = acc
        m_ref[...] = _rms_out(acc, lo, g_ref[...]).astype(BF)
        c_ref[:, :LANES] = carry[3]
        c_ref[:, LANES:] = carry[4]

    car_spec = pl.BlockSpec((tq, 2 * LANES), lambda b, p, i: (b * nq + i, p))
    return pl.pallas_call(
        body, name="sb_fwd", grid=(bl, p_n, nq), in_specs=[q_spec, k_spec, v_spec, g_spec],
        out_specs=[tile_spec, tile_spec, car_spec],
        out_shape=[_sds((t, dg), F32), _sds((t, dg), BF), _sds((t, 2 * dg), F32)],
        compiler_params=_cparams(("parallel", "parallel", "parallel")),
    )(qkv, qkv, qkv, gain)


def _sb_bwd(qkv, cars, do, bl, seq, dg):
    t = bl * seq
    p_n = dg // LANES
    tq, tk = min(SB_TQ, seq), SB_TK
    ratio = tq // tk
    scale = 1.0 / math.sqrt(HEAD_DIM)
    nq, q_spec, k_spec, v_spec, tile_spec, seq_spec, _ = _sb_specs(bl, seq, p_n, 0, tq)
    car_spec = pl.BlockSpec((tq, 2 * LANES), lambda b, p, i: (b * nq + i, p))

    def body(q_ref, k_ref, v_ref, c_ref, do_ref, dq_ref, dk_ref, dv_ref, dk_acc, dv_acc):
        i = pl.program_id(2)

        @pl.when(i == 0)
        def _():
            dk_acc[...] = jnp.zeros_like(dk_acc)
            dv_acc[...] = jnp.zeros_like(dv_acc)

        row = lax.broadcasted_iota(jnp.int32, (tq, tk), 0)
        col = lax.broadcasted_iota(jnp.int32, (tq, tk), 1)
        krow = lax.broadcasted_iota(jnp.int32, (tk, tk), 0)
        kcol = lax.broadcasted_iota(jnp.int32, (tk, tk), 1)
        tri2 = _tri2(krow > kcol)
        tri_pre2 = _tri2(krow < kcol)
        lane = lax.broadcasted_iota(jnp.int32, (1, LANES), 1)
        sc = jnp.asarray(scale, BF)
        qs = q_ref[...] * sc
        dob = do_ref[...]
        lo, qh = _head_masks(qs)
        _, doh = _head_masks(dob)
        seen = (c_ref[:, :LANES], c_ref[:, LANES:])

        def step(j, carry, diag):
            dq, e0, e1 = carry
            ecars = (e0, e1)
            mask = None if diag is None else row > col + diag
            off = pl.multiple_of(j * tk, tk)
            kj = k_ref[pl.ds(off, tk), :]
            vj = v_ref[pl.ds(off, tk), :]
            zs = [_dot_nt(qh[h], kj) for h in range(2)]
            dws = [_dot_nt(doh[h], vj) for h in range(2)]
            parts = [_logistic_parts(z) for z in zs]
            lks = [p[0] if mask is None else jnp.where(mask, p[0], 0.0) for p in parts]
            sins = [_dot(_split2(lk), tri2) for lk in lks]
            ws, es = [], []
            for h in range(2):
                car = jnp.sum(jnp.where(lane == j, seen[h], 0.0), axis=1, keepdims=True)
                w = jnp.exp((lks[h] + zs[h]) + (sins[h] + car))
                w = w if mask is None else jnp.where(mask, w, 0.0)
                ws.append(w.astype(BF))
                es.append(dws[h] * w)
            pres = [_dot(_split2(e), tri_pre2) for e in es]
            dzs = []
            for h in range(2):
                t_, e = parts[h][1], es[h]
                before = pres[h] + ecars[h]
                dz = pl.reciprocal(parts[h][2], approx=True) * jnp.where(zs[h] > 0.0, e * t_ - before, e - t_ * before)
                dzs.append((dz if mask is None else jnp.where(mask, dz, 0.0)).astype(BF))
            kjs = kj * sc
            dqs = [_dot(dz, kjs) for dz in dzs]
            dks = [_dot_tn(dz, qs) for dz in dzs]
            dvs = [_dot_tn(w, dob) for w in ws]
            dk_acc[pl.ds(off, tk), :] += jnp.where(lo, dks[0], dks[1])
            dv_acc[pl.ds(off, tk), :] += jnp.where(lo, dvs[0], dvs[1])
            new_e = [ecars[h] + jnp.sum(es[h], axis=1, keepdims=True) for h in range(2)]
            return dq + jnp.where(lo, dqs[0], dqs[1]), new_e[0], new_e[1]

        reach = jnp.max(jnp.maximum(seen[0], seen[1]), axis=0, keepdims=True)
        skipped = jnp.logical_and(reach < SB_DEAD, lane < ratio * i)
        first = jnp.minimum(jnp.sum(jnp.where(skipped, 1, 0)).astype(jnp.int32), ratio * i)
        zc = jnp.zeros((tq, 1), F32)
        def corner_step(j, carry):
            dq, e0, e1 = carry
            top = tq - tk
            mask = krow > kcol
            off = pl.multiple_of(j * tk, tk)
            kj = k_ref[pl.ds(off, tk), :]
            vj = v_ref[pl.ds(off, tk), :]
            zs = [_dot_nt(qh[h][top:], kj) for h in range(2)]
            dws = [_dot_nt(doh[h][top:], vj) for h in range(2)]
            parts = [_logistic_parts(z) for z in zs]
            lks = [jnp.where(mask, p[0], 0.0) for p in parts]
            sins = [_dot(_split2(lk), tri2) for lk in lks]
            ws, es = [], []
            for h in range(2):
                car = jnp.sum(jnp.where(lane == j, seen[h][top:], 0.0), axis=1, keepdims=True)
                w = jnp.where(mask, jnp.exp((lks[h] + zs[h]) + (sins[h] + car)), 0.0)
                ws.append(w.astype(BF))
                es.append(dws[h] * w)
            pres = [_dot(_split2(e), tri_pre2) for e in es]
            dzs = []
            for h, ecar in enumerate((e0, e1)):
                t_, e = parts[h][1], es[h]
                before = pres[h] + ecar[top:]
                dz = pl.reciprocal(parts[h][2], approx=True) * jnp.where(zs[h] > 0.0, e * t_ - before, e - t_ * before)
                dzs.append(jnp.where(mask, dz, 0.0).astype(BF))
            kjs = kj * sc
            dqs = [_dot(dz, kjs) for dz in dzs]
            dks = [_dot_tn(dz, qs[top:]) for dz in dzs]
            dvs = [_dot_tn(w, dob[top:]) for w in ws]
            dk_acc[pl.ds(off, tk), :] += jnp.where(lo, dks[0], dks[1])
            dv_acc[pl.ds(off, tk), :] += jnp.where(lo, dvs[0], dvs[1])
            return dq + jnp.concatenate([jnp.zeros((top, LANES), F32), jnp.where(lo, dqs[0], dqs[1])], axis=0)

        carry = lax.fori_loop(first, ratio * i, lambda j, c: step(j, c, None), (jnp.zeros((tq, LANES), F32), zc, zc))
        if ratio == 2:
            dq = corner_step(ratio * i + 1, step(ratio * i, carry, 0))
        else:
            for m in range(ratio):
                carry = step(ratio * i + m, carry, m * tk)
            dq = carry[0]
        dq_ref[...] = dq.astype(BF)

        @pl.when(i == nq - 1)
        def _():
            dk_ref[...] = dk_acc[...].astype(BF)
            dv_ref[...] = dv_acc[...].astype(BF)

    out = _sds((t, dg), BF)
    return pl.pallas_call(
        body, name="sb_bwd", grid=(bl, p_n, nq),
        in_specs=[q_spec, k_spec, v_spec, car_spec, tile_spec],
        out_specs=[tile_spec, seq_spec, seq_spec], out_shape=[out, out, out],
        scratch_shapes=[pltpu.VMEM((seq, LANES), F32), pltpu.VMEM((seq, LANES), F32)],
        compiler_params=_cparams(("parallel", "parallel", "arbitrary")),
    )(qkv, qkv, qkv, cars, do)


def _bias_expand(rb):
    h = rb.shape[0]
    pad = CHUNK * N_PREV_CHUNKS
    n_f = BAND + CHUNK - 1
    f = jnp.concatenate([rb[:, MAX_REL - (CHUNK - 1):2 * MAX_REL],
                         jnp.broadcast_to(rb[:, 2 * MAX_REL:], (h, pad + CHUNK - MAX_REL))], axis=1)
    fr = f[:, ::-1]
    skew = jnp.pad(jnp.broadcast_to(fr[:, None, :], (h, CHUNK, n_f)), ((0, 0), (0, 0), (0, 1)))
    skew = skew.reshape(h, CHUNK * (n_f + 1))[:, :CHUNK * n_f].reshape(h, CHUNK, n_f)
    table = skew[:, :, CHUNK - 1:CHUNK - 1 + BAND]
    n_c = ATT_TILE // CHUNK
    width = 3 * ATT_TILE
    rows = [jnp.pad(table, ((0, 0), (0, 0), (c * CHUNK, width - BAND - c * CHUNK)), constant_values=NEG_BIAS)
            for c in range(n_c)]
    return jnp.concatenate(rows, axis=1)


def _ca_scores(qh_h, k_ref, bias_ref, h, i, scale):
    tq = ATT_TILE
    ss, offs = [], []
    for kb in range(3):
        jb = i - 2 + kb
        off = pl.multiple_of(jnp.maximum(jb, 0) * tq, tq)
        s = _dot_nt(qh_h, k_ref[pl.ds(off, tq), :]) * scale + bias_ref[h, :, kb * tq:(kb + 1) * tq]
        ss.append(jnp.where(jb >= 0, s, NEG_BIAS))
        offs.append(off)
    m = jnp.max(jnp.maximum(jnp.maximum(ss[0], ss[1]), ss[2]), axis=1, keepdims=True)
    ps = [jnp.exp(s - m) for s in ss]
    den = jnp.sum(ps[0] + ps[1] + ps[2], axis=1, keepdims=True)
    inv = 1.0 / den
    return [p * inv for p in ps], offs


def _ca_fwd(qkv, bias4, bl, seq, dg, gain, bufs=(), layer=None):
    t = bl * seq
    p_n = dg // LANES
    tq = ATT_TILE
    scale = 1.0 / math.sqrt(HEAD_DIM)
    nq, q_spec, k_spec, v_spec, tile_spec, _, g_spec = _sb_specs(bl, seq, p_n, 3 * p_n, tq)
    b_spec = pl.BlockSpec((2, tq, 3 * tq), lambda b, p, i: (p, 0, 0))
    nt = len(bufs)

    def body(q_ref, k_ref, v_ref, bias_ref, g_ref, *rest):
        o_ref, m_ref = rest[nt:nt + 2]
        i = pl.program_id(2)
        if nt:
            start, finish = _gather_plan(rest[nt + 2:2 * nt + 2], [b.shape[2] for b in bufs], rest[2 * nt + 2],
                                         rest[2 * nt + 3], layer)
            step = (pl.program_id(0) * p_n + pl.program_id(1)) * nq + i
            pl.when(step == 0)(start)
        lo, qh = _head_masks(q_ref[...])
        outs = []
        for h in range(2):
            ps, offs = _ca_scores(qh[h], k_ref, bias_ref, h, i, scale)
            acc = None
            for kb in range(3):
                pv = _dot(ps[kb].astype(BF), v_ref[pl.ds(offs[kb], tq), :])
                acc = pv if acc is None else acc + pv
            outs.append(acc)
        acc = jnp.where(lo, outs[0], outs[1])
        o_ref[...] = acc
        m_ref[...] = _rms_out(acc, lo, g_ref[...]).astype(BF)
        if nt:
            pl.when(step == bl * p_n * nq - 1)(finish)

    if not nt:
        return pl.pallas_call(
            body, name="ca_fwd", grid=(bl, p_n, nq), in_specs=[q_spec, k_spec, v_spec, b_spec, g_spec],
            out_specs=[tile_spec, tile_spec], out_shape=[_sds((t, dg), F32), _sds((t, dg), BF)],
            compiler_params=_cparams(("parallel", "parallel", "parallel")),
        )(qkv, qkv, qkv, bias4, gain)
    res = pl.pallas_call(
        body, name="ca_fwd_gather", grid=(bl, p_n, nq),
        in_specs=[q_spec, k_spec, v_spec, b_spec, g_spec] + [_ANY] * nt,
        out_specs=[tile_spec, tile_spec] + [_ANY] * nt,
        out_shape=[_sds((t, dg), F32), _sds((t, dg), BF)] + [_sds(b.shape, b.dtype) for b in bufs],
        input_output_aliases={5 + k: 2 + k for k in range(nt)},
        scratch_shapes=[pltpu.SemaphoreType.DMA((6 * nt,)), pltpu.SemaphoreType.DMA((6 * nt,))],
        compiler_params=_cparams(("arbitrary", "arbitrary", "arbitrary")),
    )(qkv, qkv, qkv, bias4, gain, *bufs)
    return res[0], res[1], list(res[2:])


def _ca_bwd(qkv, bias4, o, do, bl, seq, dg, psums=()):
    t = bl * seq
    p_n = dg // LANES
    tq = ATT_TILE
    scale = 1.0 / math.sqrt(HEAD_DIM)
    nq = seq // tq
    c0 = 3 * p_n
    q_spec = pl.BlockSpec((tq, LANES), lambda p, b, i: (b * nq + i, c0 + p))
    k_spec = pl.BlockSpec((seq, LANES), lambda p, b, i: (b, c0 + p_n + p))
    v_spec = pl.BlockSpec((seq, LANES), lambda p, b, i: (b, c0 + 2 * p_n + p))
    tile_spec = pl.BlockSpec((tq, LANES), lambda p, b, i: (b * nq + i, p))
    seq_spec = pl.BlockSpec((seq, LANES), lambda p, b, i: (b, p))
    b_spec = pl.BlockSpec((2, tq, 3 * tq), lambda p, b, i: (p, 0, 0))

    nt = len(psums)

    def body(q_ref, k_ref, v_ref, bias_ref, o_ref, do_ref, *rest):
        dq_ref, dk_ref, dv_ref, db_ref = rest[nt:nt + 4]
        dk_acc, dv_acc = rest[2 * nt + 4:2 * nt + 6]
        b = pl.program_id(1)
        i = pl.program_id(2)
        if nt:
            start, finish = _scatter_plan(rest[:nt], rest[nt + 4:2 * nt + 4], rest[2 * nt + 6], rest[2 * nt + 7])
            step = (pl.program_id(0) * bl + b) * nq + i
            pl.when(step == 0)(start)

        @pl.when(i == 0)
        def _():
            dk_acc[...] = jnp.zeros_like(dk_acc)
            dv_acc[...] = jnp.zeros_like(dv_acc)

        @pl.when((i == 0) & (b == 0))
        def _():
            db_ref[...] = jnp.zeros_like(db_ref)

        q = q_ref[...]
        dob = do_ref[...]
        lo, qh = _head_masks(q)
        _, doh = _head_masks(dob)
        prod = dob.astype(F32) * o_ref[...]
        delta = (jnp.sum(jnp.where(lo, prod, 0.0), axis=1, keepdims=True),
                 jnp.sum(jnp.where(lo, 0.0, prod), axis=1, keepdims=True))
        dqs = []
        for h in range(2):
            ps, offs = _ca_scores(qh[h], k_ref, bias_ref, h, i, scale)
            dq = None
            dks, dvs = [], []
            for kb in range(3):
                dp = _dot_nt(doh[h], v_ref[pl.ds(offs[kb], tq), :])
                ds = ps[kb] * (dp - delta[h])
                db_ref[h, :, kb * tq:(kb + 1) * tq] += ds
                dsb = (ds * scale).astype(BF)
                part = _dot(dsb, k_ref[pl.ds(offs[kb], tq), :])
                dq = part if dq is None else dq + part
                dks.append(_dot_tn(dsb, q))
                dvs.append(_dot_tn(ps[kb].astype(BF), dob))
            dqs.append(dq)
            hm = lo if h == 0 else jnp.logical_not(lo)
            for kb in range(3):
                dk_acc[pl.ds(offs[kb], tq), :] += jnp.where(hm, dks[kb], 0.0)
                dv_acc[pl.ds(offs[kb], tq), :] += jnp.where(hm, dvs[kb], 0.0)
        dq_ref[...] = jnp.where(lo, dqs[0], dqs[1]).astype(BF)

        @pl.when(i == nq - 1)
        def _():
            dk_ref[...] = dk_acc[...].astype(BF)
            dv_ref[...] = dv_acc[...].astype(BF)

        if nt:
            pl.when(step == p_n * bl * nq - 1)(finish)

    out = _sds((t, dg), BF)
    acc = [pltpu.VMEM((seq, LANES), F32), pltpu.VMEM((seq, LANES), F32)]
    if not nt:
        return pl.pallas_call(
            body, name="ca_bwd", grid=(p_n, bl, nq),
            in_specs=[q_spec, k_spec, v_spec, b_spec, tile_spec, tile_spec],
            out_specs=[tile_spec, seq_spec, seq_spec, b_spec], out_shape=[out, out, out, _sds(bias4.shape, F32)],
            scratch_shapes=acc, compiler_params=_cparams(("parallel", "arbitrary", "arbitrary")),
        )(qkv, qkv, qkv, bias4, o, do)
    res = pl.pallas_call(
        body, name="ca_bwd_scatter", grid=(p_n, bl, nq),
        in_specs=[q_spec, k_spec, v_spec, b_spec, tile_spec, tile_spec] + [_ANY] * nt,
        out_specs=[tile_spec, seq_spec, seq_spec, b_spec] + [_ANY] * nt,
        out_shape=[out, out, out, _sds(bias4.shape, F32)] + _scatter_shapes(psums),
        scratch_shapes=acc + [pltpu.SemaphoreType.DMA((3 * nt,)), pltpu.SemaphoreType.DMA((3 * nt,))],
        compiler_params=_cparams(("arbitrary", "arbitrary", "arbitrary")),
    )(qkv, qkv, qkv, bias4, o, do, *psums)
    return res[0], res[1], res[2], res[3], list(res[4:])


_ANY = pl.BlockSpec(memory_space=pl.ANY)


def _mesh_pos():
    x, y, c = lax.axis_index("x"), lax.axis_index("y"), lax.axis_index("c")
    chips = [(1 - x, y), (x, 1 - y), (1 - x, 1 - y)]
    return x, y, c, chips


def _half(ref_rows, c):
    return pl.ds(c * (ref_rows // 2), ref_rows // 2)


def _cast_into_slot(w, slot):
    l_n, r, c_n = w.shape
    tr = _tile(r, 512)

    def body(s_ref, w_ref, o_ref):
        o_ref[...] = w_ref[...].astype(BF)

    return pl.pallas_call(
        body, name="cast_into_slot", out_shape=_sds((l_n, 4, r, c_n), BF),
        grid_spec=pltpu.PrefetchScalarGridSpec(
            num_scalar_prefetch=1, grid=(l_n, r // tr),
            in_specs=[pl.BlockSpec((None, tr, c_n), lambda l, m, s: (l, m, 0))],
            out_specs=pl.BlockSpec((None, None, tr, c_n), lambda l, m, s: (l, s[0], m, 0))),
        compiler_params=_cparams(("parallel", "parallel")),
    )(slot, w)


def _gather_plan(outs, n_rows, send_sems, recv_sems, layer):
    nt = len(outs)

    def blk(t, s, cc):
        return outs[t].at[pl.ds(layer, 1), s, _half(n_rows[t], cc)]

    def copy(k, ref, to):
        return pltpu.make_async_remote_copy(src_ref=ref, dst_ref=ref, send_sem=send_sems.at[k],
                                            recv_sem=recv_sems.at[k], device_id=to, device_id_type=MESH_IDS)

    def over_ici():
        x, y, c, chips = _mesh_pos()
        return [copy(j * nt + t, blk(t, 2 * x + y, c), (px, py, c)) for j, (px, py) in enumerate(chips) for t in range(nt)]

    def start():
        for cp in over_ici():
            cp.start()

    def finish():
        x, y, c, chips = _mesh_pos()
        passed = []
        for j, (px, py) in enumerate(chips):
            for t in range(nt):
                got = blk(t, 2 * px + py, c)
                copy(j * nt + t, got, (px, py, c)).wait_recv()
                relay = copy(3 * nt + j * nt + t, got, (x, y, 1 - c))
                relay.start()
                passed.append(relay)
        for j, (px, py) in enumerate(chips):
            for t in range(nt):
                copy(3 * nt + j * nt + t, blk(t, 2 * px + py, 1 - c), (x, y, 1 - c)).wait_recv()
        for cp in over_ici() + passed:
            cp.wait_send()

    return start, finish


def _gather_weights(bufs, layer):
    nt = len(bufs)

    def body(*refs):
        start, finish = _gather_plan(refs[nt:2 * nt], [b.shape[2] for b in bufs], refs[2 * nt], refs[2 * nt + 1], layer)
        start()
        finish()

    return pl.pallas_call(
        body, name="gather_weights", in_specs=[_ANY] * nt, out_specs=[_ANY] * nt,
        out_shape=[_sds(b.shape, b.dtype) for b in bufs], input_output_aliases={t: t for t in range(nt)},
        scratch_shapes=[pltpu.SemaphoreType.DMA((6 * nt,)), pltpu.SemaphoreType.DMA((6 * nt,))],
    )(*bufs)


def _swap_halves(grads):
    nt = len(grads)

    def body(*refs):
        ins, theirs_o = refs[:nt], refs[nt:2 * nt]
        send_sems, recv_sems = refs[2 * nt:]
        x, y, c, _ = _mesh_pos()
        sends = [pltpu.make_async_remote_copy(
            src_ref=ins[t].at[:, :, _half(grads[t].shape[2], 1 - c)], dst_ref=theirs_o[t],
            send_sem=send_sems.at[t], recv_sem=recv_sems.at[t], device_id=(x, y, 1 - c), device_id_type=MESH_IDS)
            for t in range(nt)]
        for cp in sends:
            cp.start()
        for cp in sends:
            cp.wait_recv()
        for cp in sends:
            cp.wait_send()

    half = [_sds(g.shape[:2] + (g.shape[2] // 2, g.shape[3]), g.dtype) for g in grads]
    return pl.pallas_call(
        body, name="swap_halves", in_specs=[_ANY] * nt, out_specs=[_ANY] * nt, out_shape=half,
        scratch_shapes=[pltpu.SemaphoreType.DMA((nt,)), pltpu.SemaphoreType.DMA((nt,))],
    )(*grads)


def _add_pair(mine, theirs):
    l_n, s_n, r2, c_n = theirs.shape

    def body(a_ref, b_ref, o_ref):
        rows = pl.ds(pl.multiple_of(lax.axis_index("c") * r2, 16), r2)
        o_ref[...] = (a_ref[rows, :].astype(F32) + b_ref[...].astype(F32)).astype(BF)

    blk = (None, None, r2, c_n)
    return pl.pallas_call(
        body, name="add_pair", out_shape=_sds(theirs.shape, BF), grid=(l_n, s_n),
        in_specs=[pl.BlockSpec((None, None, 2 * r2, c_n), lambda l, s: (l, s, 0, 0)),
                  pl.BlockSpec(blk, lambda l, s: (l, s, 0, 0))],
        out_specs=pl.BlockSpec(blk, lambda l, s: (l, s, 0, 0)),
        compiler_params=_cparams(("parallel", "parallel")),
    )(mine, theirs)


def _scatter_chips(psums):
    nt = len(psums)

    def body(*refs):
        start, finish = _scatter_plan(refs[:nt], refs[nt:2 * nt], refs[2 * nt], refs[2 * nt + 1])
        start()
        finish()

    return pl.pallas_call(
        body, name="scatter_chips", in_specs=[_ANY] * nt, out_specs=[_ANY] * nt, out_shape=_scatter_shapes(psums),
        scratch_shapes=[pltpu.SemaphoreType.DMA((3 * nt,)), pltpu.SemaphoreType.DMA((3 * nt,))],
    )(*psums)


def _scatter_shapes(psums):
    return [_sds((3, p.shape[0]) + p.shape[2:], p.dtype) for p in psums]


def _scatter_plan(ins, got, send_sems, recv_sems):
    nt = len(ins)

    def sends():
        _, _, c, chips = _mesh_pos()
        return [pltpu.make_async_remote_copy(
            src_ref=ins[t].at[:, 2 * px + py], dst_ref=got[t].at[j], send_sem=send_sems.at[j * nt + t],
            recv_sem=recv_sems.at[j * nt + t], device_id=(px, py, c), device_id_type=MESH_IDS)
            for j, (px, py) in enumerate(chips) for t in range(nt)]

    def start():
        for cp in sends():
            cp.start()

    def finish():
        for cp in sends():
            cp.wait_recv()
        for cp in sends():
            cp.wait_send()

    return start, finish


def _add_chips(psum, got):
    l_n, _, r2, c_n = psum.shape

    def body(a_ref, g_ref, o_ref):
        c = lax.axis_index("c")
        acc = a_ref[2 * lax.axis_index("x") + lax.axis_index("y")].astype(F32)
        for j in range(3):
            acc = acc + g_ref[j].astype(F32)
        o_ref[pl.ds(pl.multiple_of(c * r2, 8), r2), :] = acc
        o_ref[pl.ds(pl.multiple_of((1 - c) * r2, 8), r2), :] = jnp.zeros_like(acc)

    return pl.pallas_call(
        body, name="add_chips", out_shape=_sds((l_n, 2 * r2, c_n), F32), grid=(l_n,),
        in_specs=[pl.BlockSpec((None, 4, r2, c_n), lambda l: (l, 0, 0, 0)),
                  pl.BlockSpec((3, None, r2, c_n), lambda l: (0, l, 0, 0))],
        out_specs=pl.BlockSpec((None, 2 * r2, c_n), lambda l: (l, 0, 0)),
        compiler_params=_cparams(("parallel",)),
    )(psum, got)


def _join_halves(fulls):
    nt = len(fulls)

    def body(*refs):
        outs = refs[nt:2 * nt]
        send_sems, recv_sems = refs[2 * nt:]
        x, y, c, _ = _mesh_pos()

        def copy(t, cc):
            blk = outs[t].at[:, _half(fulls[t].shape[1], cc)]
            return pltpu.make_async_remote_copy(
                src_ref=blk, dst_ref=blk, send_sem=send_sems.at[t], recv_sem=recv_sems.at[t],
                device_id=(x, y, 1 - c), device_id_type=MESH_IDS)

        sends = [copy(t, c) for t in range(nt)]
        for cp in sends:
            cp.start()
        for t in range(nt):
            copy(t, 1 - c).wait_recv()
        for cp in sends:
            cp.wait_send()

    return pl.pallas_call(
        body, name="join_halves", in_specs=[_ANY] * nt, out_specs=[_ANY] * nt,
        out_shape=[_sds(f.shape, f.dtype) for f in fulls], input_output_aliases={t: t for t in range(nt)},
        scratch_shapes=[pltpu.SemaphoreType.DMA((nt,)), pltpu.SemaphoreType.DMA((nt,))],
    )(*fulls)


def _allreduce_small(part):
    r, d = part.shape
    flips = [(fx, fy, fc) for fx in (0, 1) for fy in (0, 1) for fc in (0, 1) if fx + fy + fc]

    def body(p_ref, o_ref, buf, send_sems, recv_sems):
        x, y, c, _ = _mesh_pos()
        me = 4 * x + 2 * y + c
        buf[me] = p_ref[...]
        sends = []
        for k, (fx, fy, fc) in enumerate(flips):
            sends.append(pltpu.make_async_remote_copy(
                src_ref=p_ref, dst_ref=buf.at[me], send_sem=send_sems.at[k], recv_sem=recv_sems.at[k],
                device_id=(x ^ fx, y ^ fy, c ^ fc), device_id_type=MESH_IDS))
        for cp in sends:
            cp.start()
        for k, (fx, fy, fc) in enumerate(flips):
            peer = 4 * (x ^ fx) + 2 * (y ^ fy) + (c ^ fc)
            pltpu.make_async_remote_copy(
                src_ref=p_ref, dst_ref=buf.at[peer], send_sem=send_sems.at[k], recv_sem=recv_sems.at[k],
                device_id=(x ^ fx, y ^ fy, c ^ fc), device_id_type=MESH_IDS).wait_recv()
        acc = buf[0]
        for dev in range(1, 8):
            acc = acc + buf[dev]
        o_ref[...] = acc
        for cp in sends:
            cp.wait_send()

    vmem = pl.BlockSpec(memory_space=pltpu.VMEM)
    return pl.pallas_call(
        body, name="allreduce_small", in_specs=[vmem], out_specs=vmem, out_shape=_sds((r, d), F32),
        scratch_shapes=[pltpu.VMEM((8, r, d), F32), pltpu.SemaphoreType.DMA((7,)), pltpu.SemaphoreType.DMA((7,))],
    )(part)


BIG = ("ffn1_w_gate", "ffn1_w_up", "ffn1_w_down", "w_in", "w_out", "ffn2_w_gate", "ffn2_w_up", "ffn2_w_down")
SMALL = ("ln1_g", "ln1_b", "rel_bias", "sb_out_g", "ca_out_g", "ln2_g", "ln2_b", "ln3_g", "ln3_b")
ORDER = ("ffn1_w_gate", "ffn1_w_up", "ffn1_w_down", "ln1_g", "ln1_b", "w_in", "rel_bias", "sb_out_g", "ca_out_g",
         "w_out", "ln2_g", "ln2_b", "ffn2_w_gate", "ffn2_w_up", "ffn2_w_down", "ln3_g", "ln3_b")


def _pack_small(vals, depth, d):
    rows = []
    for l in range(depth):
        for n in ("ln1_g", "ln1_b", "ln2_g", "ln2_b", "ln3_g", "ln3_b"):
            rows.append(vals[n][l][None, :])
        rows.append(jnp.concatenate([vals["sb_out_g"][l], vals["ca_out_g"][l]])[None, :])
        rb = vals["rel_bias"][l]
        rows.append(jnp.pad(rb, ((0, 0), (0, d - rb.shape[1]))))
    packed = jnp.concatenate(rows, axis=0)
    pad = (-packed.shape[0]) % 8
    return jnp.pad(packed, ((0, pad), (0, 0)))


def _unpack_small(packed, depth, d, heads, n_rel):
    per = 7 + heads
    out = {n: [] for n in SMALL}
    for l in range(depth):
        base = l * per
        for i, n in enumerate(("ln1_g", "ln1_b", "ln2_g", "ln2_b", "ln3_g", "ln3_b")):
            out[n].append(packed[base + i])
        out["sb_out_g"].append(packed[base + 6, :d // 2])
        out["ca_out_g"].append(packed[base + 6, d // 2:])
        out["rel_bias"].append(packed[base + 7:base + 7 + heads, :n_rel])
    return {n: jnp.stack(v) for n, v in out.items()}


def kernel(x, ffn1_w_gate, ffn1_w_up, ffn1_w_down, ln1_g, ln1_b, w_in, rel_bias, sb_out_g, ca_out_g, w_out, ln2_g, ln2_b, ffn2_w_gate, ffn2_w_up, ffn2_w_down, ln3_g, ln3_b, loss_target, m_ffn1_w_gate, m_ffn1_w_up, m_ffn1_w_down, m_ln1_g, m_ln1_b, m_w_in, m_rel_bias, m_sb_out_g, m_ca_out_g, m_w_out, m_ln2_g, m_ln2_b, m_ffn2_w_gate, m_ffn2_w_up, m_ffn2_w_down, m_ln3_g, m_ln3_b, v_ffn1_w_gate, v_ffn1_w_up, v_ffn1_w_down, v_ln1_g, v_ln1_b, v_w_in, v_rel_bias, v_sb_out_g, v_ca_out_g, v_w_out, v_ln2_g, v_ln2_b, v_ffn2_w_gate, v_ffn2_w_up, v_ffn2_w_down, v_ln3_g, v_ln3_b):
    w = dict(ffn1_w_gate=ffn1_w_gate, ffn1_w_up=ffn1_w_up, ffn1_w_down=ffn1_w_down, ln1_g=ln1_g, ln1_b=ln1_b, w_in=w_in, rel_bias=rel_bias, sb_out_g=sb_out_g, ca_out_g=ca_out_g, w_out=w_out, ln2_g=ln2_g, ln2_b=ln2_b, ffn2_w_gate=ffn2_w_gate, ffn2_w_up=ffn2_w_up, ffn2_w_down=ffn2_w_down, ln3_g=ln3_g, ln3_b=ln3_b)
    mom = dict(ffn1_w_gate=m_ffn1_w_gate, ffn1_w_up=m_ffn1_w_up, ffn1_w_down=m_ffn1_w_down, ln1_g=m_ln1_g, ln1_b=m_ln1_b, w_in=m_w_in, rel_bias=m_rel_bias, sb_out_g=m_sb_out_g, ca_out_g=m_ca_out_g, w_out=m_w_out, ln2_g=m_ln2_g, ln2_b=m_ln2_b, ffn2_w_gate=m_ffn2_w_gate, ffn2_w_up=m_ffn2_w_up, ffn2_w_down=m_ffn2_w_down, ln3_g=m_ln3_g, ln3_b=m_ln3_b)
    var = dict(ffn1_w_gate=v_ffn1_w_gate, ffn1_w_up=v_ffn1_w_up, ffn1_w_down=v_ffn1_w_down, ln1_g=v_ln1_g, ln1_b=v_ln1_b, w_in=v_w_in, rel_bias=v_rel_bias, sb_out_g=v_sb_out_g, ca_out_g=v_ca_out_g, w_out=v_w_out, ln2_g=v_ln2_g, ln2_b=v_ln2_b, ffn2_w_gate=v_ffn2_w_gate, ffn2_w_up=v_ffn2_w_up, ffn2_w_down=v_ffn2_w_down, ln3_g=v_ln3_g, ln3_b=v_ln3_b)

    bl, seq, d = x.shape
    t = bl * seq
    depth = w_in.shape[0]
    dg = d // 2
    heads = dg // HEAD_DIM
    alpha = (2 * depth) ** 0.25
    n_rel = rel_bias.shape[-1]

    chip = (2 * lax.axis_index("x") + lax.axis_index("y")).astype(jnp.int32).reshape(1)
    bufs = _gather_weights([_cast_into_slot(w[n], chip) for n in BIG], 0)

    qd = dg // 2
    shards = 4
    ffn_terms = [(0, s, None, s) for s in range(shards)]
    out_terms = [(i, None, (k * qd, (k + 1) * qd), 2 * i + k) for i in range(2) for k in range(2)]
    bias_all, bias_vjp = jax.vjp(jax.vmap(_bias_expand), rel_bias)

    xs = x.reshape(t, d)
    saved = []
    for l in range(depth):
        lw = {n: _LayerOf((b, l)) for n, b in zip(BIG, bufs)}
        row = lambda a: a[l][None, :]
        g1, u1, h1 = _ffn_up(xs, lw["ffn1_w_gate"], lw["ffn1_w_up"])
        x1, xh1, rs1 = _proj_ln("ffn_down_ln", [h1], lw["ffn1_w_down"], ffn_terms, xs, row(ln1_g), row(ln1_b),
                                FFN_RESIDUAL, alpha)
        qkv = _qkv_proj(x1, lw["w_in"])
        o_sb, m_sb, c_sb = _sb_fwd(qkv, bl, seq, dg, row(sb_out_g))
        if l + 1 < depth:
            o_ca, m_ca, bufs = _ca_fwd(qkv, bias_all[l], bl, seq, dg, row(ca_out_g), bufs, l + 1)
            lw = {n: _LayerOf((b, l)) for n, b in zip(BIG, bufs)}
        else:
            o_ca, m_ca = _ca_fwd(qkv, bias_all[l], bl, seq, dg, row(ca_out_g))
        x2, xh2, rs2 = _proj_ln("attn_out_ln", [m_sb, m_ca], lw["w_out"], out_terms, x1, row(ln2_g), row(ln2_b),
                                1.0, alpha)
        g2, u2, h2 = _ffn_up(x2, lw["ffn2_w_gate"], lw["ffn2_w_up"])
        x3, xh3, rs3 = _proj_ln("ffn_down_ln", [h2], lw["ffn2_w_down"], ffn_terms, x2, row(ln3_g), row(ln3_b),
                                FFN_RESIDUAL, alpha)
        saved.append(dict(xin=xs, g1=g1, u1=u1, h1=h1, x1=x1, xh1=xh1, rs1=rs1, qkv=qkv, o_sb=o_sb, m_sb=m_sb,
                          c_sb=c_sb, o_ca=o_ca, m_ca=m_ca, x2=x2, xh2=xh2, rs2=rs2, g2=g2, u2=u2, h2=h2, xh3=xh3,
                          rs3=rs3))
        xs = x3

    last = saved[-1]
    dr3, drs3, pg, pb, sq = _loss_grad(xs, loss_target.reshape(t, d), last["xh3"], last["rs3"],
                                       ln3_g[depth - 1][None, :], FFN_RESIDUAL)
    loss = lax.psum(0.5 * jnp.sum(sq) / d, ("x", "y", "c"))

    big_grads = {n: [None] * depth for n in BIG}
    small_grads = {n: [None] * depth for n in SMALL}
    s8 = lambda a: jnp.sum(a, axis=0)
    wd_ = 3 * d // shards
    in_pieces = [_shard_cols(6, dg, s, wd_) for s in range(shards)]
    dbias = [None] * depth
    chip_sums, arrived = [None] * depth, [None] * depth
    for l in reversed(range(depth)):
        sv = saved[l]
        lw = {n: _LayerOf((b, l)) for n, b in zip(BIG, bufs)}
        row = lambda a: a[l][None, :]
        small_grads["ln3_g"][l], small_grads["ln3_b"][l] = s8(pg), s8(pb)
        dgt, dut = _swiglu_bwd(drs3, lw["ffn2_w_down"], sv["g2"], sv["u2"])
        dr2, drs2, pg, pb = _ffn_dx(dgt, dut, lw["ffn2_w_gate"], lw["ffn2_w_up"], dr3, alpha,
                                    (sv["xh2"], sv["rs2"], row(ln2_g), 1.0))
        small_grads["ln2_g"][l], small_grads["ln2_b"][l] = s8(pg), s8(pb)
        big_grads["ffn2_w_down"][l] = _wgrad("wgrad_down", [sv["h2"]], [drs3], shards, _lshape(lw["ffn2_w_down"])[1:],
                                             lambda a, s: a[0][s], lambda b, s: b[0][...])
        big_grads["ffn2_w_gate"][l] = _wgrad("wgrad_up", [sv["x2"]], [dgt], shards, _lshape(lw["ffn2_w_gate"])[1:],
                                             lambda a, s: a[0][...], lambda b, s: b[0][s])
        big_grads["ffn2_w_up"][l] = _wgrad("wgrad_up", [sv["x2"]], [dut], shards, _lshape(lw["ffn2_w_up"])[1:],
                                           lambda a, s: a[0][...], lambda b, s: b[0][s])
        do_sb, do_ca, pgs, pgc = _attn_dmix(drs2, lw["w_out"], sv["o_sb"], sv["o_ca"], row(sb_out_g), row(ca_out_g))
        small_grads["sb_out_g"][l], small_grads["ca_out_g"][l] = s8(pgs), s8(pgc)
        big_grads["w_out"][l] = _wgrad(
            "wgrad_out", [sv["m_sb"], sv["m_ca"]], [drs2], shards, (qd, d),
            lambda a, s: a[s // 2][:, (s % 2) * qd:(s % 2 + 1) * qd], lambda b, s: b[0][...])
        dqa, dka, dva = _sb_bwd(sv["qkv"], sv["c_sb"], do_sb, bl, seq, dg)
        if l + 1 < depth:
            dqb, dkb, dvb, dbias[l], arrived[l + 1] = _ca_bwd(sv["qkv"], bias_all[l], sv["o_ca"], do_ca, bl, seq, dg,
                                                             chip_sums[l + 1])
        else:
            dqb, dkb, dvb, dbias[l] = _ca_bwd(sv["qkv"], bias_all[l], sv["o_ca"], do_ca, bl, seq, dg)
        dqkv = [dqa, dka, dva, dqb, dkb, dvb]
        dr1, drs1, pg, pb = _attn_dx(dqkv, lw["w_in"], dr2, alpha, (sv["xh1"], sv["rs1"], row(ln1_g), FFN_RESIDUAL))
        small_grads["ln1_g"][l], small_grads["ln1_b"][l] = s8(pg), s8(pb)
        big_grads["w_in"][l] = _wgrad(
            "wgrad_in", [sv["x1"]], dqkv, shards, (d, wd_), lambda a, s: a[0][...],
            lambda b, s: jnp.concatenate([b[gi][:, lo:hi] for gi, lo, hi in in_pieces[s]], axis=1))
        dgt, dut = _swiglu_bwd(drs1, lw["ffn1_w_down"], sv["g1"], sv["u1"])
        if l > 0:
            prev = saved[l - 1]
            dr3, drs3, pg, pb = _ffn_dx(dgt, dut, lw["ffn1_w_gate"], lw["ffn1_w_up"], dr1, alpha,
                                        (prev["xh3"], prev["rs3"], ln3_g[l - 1][None, :], FFN_RESIDUAL))
        else:
            grad_x = _ffn_dx(dgt, dut, lw["ffn1_w_gate"], lw["ffn1_w_up"], dr1, alpha, None).reshape(bl, seq, d)
        big_grads["ffn1_w_down"][l] = _wgrad("wgrad_down", [sv["h1"]], [drs1], shards, _lshape(lw["ffn1_w_down"])[1:],
                                             lambda a, s: a[0][s], lambda b, s: b[0][...])
        big_grads["ffn1_w_gate"][l] = _wgrad("wgrad_up", [sv["xin"]], [dgt], shards, _lshape(lw["ffn1_w_gate"])[1:],
                                             lambda a, s: a[0][...], lambda b, s: b[0][s])
        big_grads["ffn1_w_up"][l] = _wgrad("wgrad_up", [sv["xin"]], [dut], shards, _lshape(lw["ffn1_w_up"])[1:],
                                           lambda a, s: a[0][...], lambda b, s: b[0][s])
        partial = [big_grads[n][l][None] for n in BIG]
        chip_sums[l] = [_add_pair(a, b) for a, b in zip(partial, _swap_halves(partial))]
    arrived[0] = _scatter_chips(chip_sums[0])
    rel_grads = bias_vjp(jnp.stack(dbias))[0]
    for l in range(depth):
        small_grads["rel_bias"][l] = rel_grads[l]

    halves = [jnp.concatenate([_add_chips(chip_sums[l][k], arrived[l][k]) for l in range(depth)], axis=0)
              for k in range(len(BIG))]
    grads = dict(zip(BIG, _join_halves(halves)))

    packed = _allreduce_small(_pack_small({n: jnp.stack(small_grads[n]) for n in SMALL}, depth, d))
    grads.update(_unpack_small(packed, depth, d, heads, n_rel))

    delta, new_m, new_v = {}, {}, {}
    for n in BIG:
        delta[n], new_m[n], new_v[n] = _adamw(w[n], grads[n], mom[n], var[n])
    pw, pm, pv = (_pack_small({n: src[n] for n in SMALL}, depth, d) for src in (w, mom, var))
    sd, sm, sv_ = _adamw(pw, packed, pm, pv)
    for dst, src in ((delta, sd), (new_m, sm), (new_v, sv_)):
        dst.update(_unpack_small(src, depth, d, heads, n_rel))

    return (loss, grad_x, *[grads[n] for n in ORDER], *[delta[n] for n in ORDER],
            *[new_m[n] for n in ORDER], *[new_v[n] for n in ORDER])
```

```python
import functools
import math

import jax
import jax.numpy as jnp
from jax import lax
from jax.experimental import pallas as pl
from jax.experimental.pallas import tpu as pltpu

F32 = jnp.float32
BF = jnp.bfloat16

CHUNK = 64
N_PREV_CHUNKS = 8
BAND = CHUNK * (N_PREV_CHUNKS + 1)
MAX_REL = 128
HEAD_DIM = 64
FFN_RESIDUAL = 0.5
LN_EPS = 1e-5
RMS_EPS = 1e-6
ADAM_LR = 0.001
ADAM_B1 = 0.9
ADAM_B2 = 0.999
ADAM_EPS = 1e-08
ADAM_WD = 0.01
ADAM_STEP = 10

LANES = 128
ATT_TILE = 4 * CHUNK
SB_TQ = 512
SB_TK = 256
SB_DEAD = -105.0
SB_UNSEEN = -1e30
NEG_BIAS = -1e30
VMEM_LIMIT = 56 * 1024 * 1024
ROW_TILE = 512
MESH_IDS = pl.DeviceIdType.MESH

_NT = (((1,), (1,)), ((), ()))
_TN = (((0,), (0,)), ((), ()))


def _dot(a, b):
    return jnp.dot(a, b, preferred_element_type=F32)


def _dot_nt(a, b):
    return lax.dot_general(a, b, _NT, preferred_element_type=F32)


def _dot_tn(a, b):
    return lax.dot_general(a, b, _TN, preferred_element_type=F32)


def _tile(n, pref):
    t = min(n, pref)
    while n % t:
        t //= 2
    return t


def _cparams(sem):
    return pltpu.CompilerParams(dimension_semantics=sem, vmem_limit_bytes=VMEM_LIMIT)


def _sds(shape, dtype):
    return jax.ShapeDtypeStruct(tuple(shape), dtype)


def _sigmoid(v):
    return pl.reciprocal(1.0 + jnp.exp(-v), approx=True)


def _rows8(v):
    tm, c = v.shape
    return v.reshape(tm // 8, 8, c).sum(axis=0)


class _LayerOf(tuple):
    pass


def _lshape(item):
    return item[0].shape[1:] if isinstance(item, _LayerOf) else item.shape


def _accumulate(m, ref, part):
    @pl.when(m == 0)
    def _():
        ref[...] = part

    @pl.when(m > 0)
    def _():
        ref[...] += part


def _ln_bwd_tail(m, dy, xh_ref, rs_ref, g_ref, scale, dr_ref, drs_ref, dg_ref, db_ref):
    xhv = xh_ref[...]
    dxh = dy * g_ref[...]
    m1 = jnp.mean(dxh, axis=-1, keepdims=True)
    m2 = jnp.mean(dxh * xhv, axis=-1, keepdims=True)
    dr = rs_ref[...] * (dxh - m1 - xhv * m2)
    dr_ref[...] = dr
    drs_ref[...] = (scale * dr).astype(BF)
    _accumulate(m, dg_ref, _rows8(dy * xhv))
    _accumulate(m, db_ref, _rows8(dy))


def _rowcall(name, body, n_rows, tm, row_ins, res_ins, row_outs, acc_outs=()):
    def rspec(shape):
        if len(shape) == 2:
            return pl.BlockSpec((tm, shape[1]), lambda m: (m, 0))
        return pl.BlockSpec((shape[0], tm, shape[2]), lambda m: (0, m, 0))

    def cspec(shape, single=False):
        zeros = (0,) * len(shape)
        mode = pl.Buffered(1) if single else None
        return pl.BlockSpec(tuple(shape), lambda m: zeros, pipeline_mode=mode)

    def split(items, mk):
        arrs, specs = [], []
        for it in items:
            if isinstance(it, _LayerOf):
                arr, l = it
                tail = (0,) * (arr.ndim - 1)
                arrs.append(arr)
                specs.append(pl.BlockSpec((None,) + arr.shape[1:], functools.partial(lambda m, l, tail: (l,) + tail, l=l, tail=tail),
                                          pipeline_mode=pl.Buffered(1)))
            elif isinstance(it, tuple):
                arrs.append(it[0])
                specs.append(it[1])
            else:
                arrs.append(it)
                specs.append(mk(it.shape))
        return arrs, specs

    ra, rs = split(row_ins, rspec)
    ca, cs = split(res_ins, functools.partial(cspec, single=True))
    out_specs = [rspec(s.shape) for s in row_outs] + [cspec(s.shape) for s in acc_outs]
    sem = ("arbitrary",) if acc_outs else ("parallel",)
    return pl.pallas_call(
        body, name=name, grid=(n_rows // tm,), in_specs=rs + cs, out_specs=out_specs,
        out_shape=list(row_outs) + list(acc_outs), compiler_params=_cparams(sem),
    )(*ra, *ca)


def _ffn_up(x, wg, wu):
    t, _ = x.shape
    s_n, _, fs = _lshape(wg)
    tm = _tile(t, ROW_TILE)

    def body(x_ref, wg_ref, wu_ref, g_ref, u_ref, h_ref):
        xb = x_ref[...].astype(BF)
        for s in range(s_n):
            g = _dot(xb, wg_ref[s])
            u = _dot(xb, wu_ref[s])
            g_ref[s] = g.astype(BF)
            u_ref[s] = u.astype(BF)
            h_ref[s] = (g * _sigmoid(g) * u).astype(BF)

    out = _sds((s_n, t, fs), BF)
    return _rowcall("ffn_up", body, t, tm, [x], [wg, wu], [out, out, out])


def _proj_ln(name, a_list, w, terms, xres, gain, bias, scale, alpha):
    t, d = xres.shape
    tm = _tile(t, ROW_TILE)
    na = len(a_list)

    def body(*refs):
        a_refs = refs[:na]
        x_ref, w_ref, g_ref, bb_ref = refs[na:na + 4]
        y_ref, xh_ref, rs_ref = refs[na + 4:]
        acc = None
        for i, s, cols, sw in terms:
            a = a_refs[i][s] if cols is None else a_refs[i][:, cols[0]:cols[1]]
            p = _dot(a.astype(BF), w_ref[sw])
            acc = p if acc is None else acc + p
        if scale != 1.0:
            acc = scale * acc
        r = alpha * x_ref[...] + acc
        mu = jnp.mean(r, axis=-1, keepdims=True)
        xc = r - mu
        var = jnp.mean(xc * xc, axis=-1, keepdims=True)
        rstd = lax.rsqrt(var + LN_EPS)
        xh = xc * rstd
        y_ref[...] = xh * g_ref[...] + bb_ref[...]
        xh_ref[...] = xh
        rs_ref[...] = rstd

    outs = [_sds((t, d), F32), _sds((t, d), F32), _sds((t, 1), F32)]
    return _rowcall(name, body, t, tm, list(a_list) + [xres], [w, gain, bias], outs)


def _shard_cols(groups, dg, s, width):
    pieces, lo = [], s * width
    while lo < (s + 1) * width:
        gi = lo // dg
        hi = min((gi + 1) * dg, (s + 1) * width)
        pieces.append((gi, lo - gi * dg, hi - gi * dg))
        lo = hi
    assert all(p[0] < groups for p in pieces)
    return pieces


def _qkv_proj(x, win):
    t, _ = x.shape
    s_n, _, wd_ = _lshape(win)
    tm = _tile(t, ROW_TILE)

    def body(x_ref, w_ref, o_ref):
        xb = x_ref[...].astype(BF)
        for s in range(s_n):
            o_ref[:, s * wd_:(s + 1) * wd_] = _dot(xb, w_ref[s]).astype(BF)

    return _rowcall("qkv_proj", body, t, tm, [x], [win], [_sds((t, s_n * wd_), BF)])[0]


def _dx_call(name, rows, res, matmul, resid, alpha, ln):
    t, d = resid.shape
    tm = _tile(t, ROW_TILE)
    nr, nc = len(rows), len(res)

    def body(*refs):
        r_refs, resid_ref = refs[:nr], refs[nr]
        if ln is None:
            c_refs, outs = refs[nr + 1:nr + 1 + nc], refs[nr + 1 + nc:]
            outs[0][...] = matmul(r_refs, c_refs) + alpha * resid_ref[...]
        else:
            xh_ref, rs_ref = refs[nr + 1:nr + 3]
            c_refs, g_ref = refs[nr + 3:nr + 3 + nc], refs[nr + 3 + nc]
            outs = refs[nr + 4 + nc:]
            dx = matmul(r_refs, c_refs) + alpha * resid_ref[...]
            _ln_bwd_tail(pl.program_id(0), dx, xh_ref, rs_ref, g_ref, ln[3], *outs)

    if ln is None:
        return _rowcall(name, body, t, tm, list(rows) + [resid], list(res), [_sds((t, d), F32)])[0]
    outs = [_sds((t, d), F32), _sds((t, d), BF)]
    accs = [_sds((8, d), F32), _sds((8, d), F32)]
    return _rowcall(name, body, t, tm, list(rows) + [resid, ln[0], ln[1]], list(res) + [ln[2]], outs, accs)


def _ffn_dx(dg_, du_, wg, wu, resid, alpha, ln):
    s_n = dg_.shape[0]

    def matmul(r, c):
        acc = None
        for i in range(2):
            for s in range(s_n):
                p = _dot_nt(r[i][s], c[i][s])
                acc = p if acc is None else acc + p
        return acc

    return _dx_call("ffn_dx", [dg_, du_], [wg, wu], matmul, resid, alpha, ln)


def _attn_dx(dqkv, win, resid, alpha, ln):
    s_n, _, wd_ = _lshape(win)
    dg = dqkv[0].shape[1]
    pieces = [_shard_cols(len(dqkv), dg, s, wd_) for s in range(s_n)]

    def matmul(r, c):
        acc = None
        for s in range(s_n):
            a = jnp.concatenate([r[gi][:, lo:hi] for gi, lo, hi in pieces[s]], axis=1)
            p = _dot_nt(a, c[0][s])
            acc = p if acc is None else acc + p
        return acc

    return _dx_call("attn_dx", list(dqkv), [win], matmul, resid, alpha, ln)


def _head_rms_bwd(dm, ov, gain, lo):
    ss = ov * ov
    s0 = jnp.sum(jnp.where(lo, ss, 0.0), axis=1, keepdims=True)
    s1 = jnp.sum(jnp.where(lo, 0.0, ss), axis=1, keepdims=True)
    r = jnp.where(lo, lax.rsqrt(s0 / HEAD_DIM + RMS_EPS), lax.rsqrt(s1 / HEAD_DIM + RMS_EPS))
    n = ov * r
    dn = dm * gain
    tt = dn * n
    t0 = jnp.sum(jnp.where(lo, tt, 0.0), axis=1, keepdims=True) / HEAD_DIM
    t1 = jnp.sum(jnp.where(lo, 0.0, tt), axis=1, keepdims=True) / HEAD_DIM
    return r * (dn - n * jnp.where(lo, t0, t1)), dm * n


def _attn_dmix(drs, wout, o_sb, o_ca, g_sb, g_ca):
    t, dg = o_sb.shape
    s_n, qd, _ = _lshape(wout)
    tm = _tile(t, ROW_TILE)
    per = dg // qd

    def body(d_ref, osb_ref, oca_ref, w_ref, gs_ref, gc_ref, dsb_ref, dca_ref, pgs_ref, pgc_ref):
        m = pl.program_id(0)
        dv = d_ref[...]
        lo = lax.broadcasted_iota(jnp.int32, (1, LANES), 1) < HEAD_DIM
        for grp, (o_ref, g_ref, do_ref, pg_ref) in enumerate(((osb_ref, gs_ref, dsb_ref, pgs_ref),
                                                               (oca_ref, gc_ref, dca_ref, pgc_ref))):
            dm = jnp.concatenate([_dot_nt(dv, w_ref[grp * per + k]) for k in range(per)], axis=1)
            parts = []
            for gi in range(dg // LANES):
                sl = slice(gi * LANES, (gi + 1) * LANES)
                do, pg = _head_rms_bwd(dm[:, sl], o_ref[:, sl], g_ref[:, sl], lo)
                do_ref[:, sl] = do.astype(BF)
                parts.append(_rows8(pg))
            _accumulate(m, pg_ref, jnp.concatenate(parts, axis=1))

    outs = [_sds((t, dg), BF), _sds((t, dg), BF)]
    accs = [_sds((8, dg), F32), _sds((8, dg), F32)]
    return _rowcall("attn_dmix", body, t, tm, [drs, o_sb, o_ca], [wout, g_sb, g_ca], outs, accs)


def _swiglu_bwd(drs, wd, g, u):
    s_n, t, fs = g.shape
    tm = _tile(t, ROW_TILE)

    def body(d_ref, g_ref, u_ref, wd_ref, dg_ref, du_ref):
        dv = d_ref[...]
        rc = _tile(tm, 128)
        for s in range(s_n):
            dh_all = _dot_nt(dv, wd_ref[s])
            for r0 in range(0, tm, rc):
                dh = dh_all[r0:r0 + rc]
                gv = g_ref[s, r0:r0 + rc, :].astype(F32)
                uv = u_ref[s, r0:r0 + rc, :].astype(F32)
                sg = _sigmoid(gv)
                gs = gv * sg
                dg_ref[s, r0:r0 + rc, :] = (dh * uv * (sg + gs * (1.0 - sg))).astype(BF)
                du_ref[s, r0:r0 + rc, :] = (dh * gs).astype(BF)

    out = _sds((s_n, t, fs), BF)
    return _rowcall("swiglu_bwd", body, t, tm, [drs, g, u], [wd], [out, out])


def _loss_grad(y, target, xh, rstd, gain, scale):
    t, d = y.shape
    tm = _tile(t, ROW_TILE)

    def body(y_ref, t_ref, xh_ref, rs_ref, g_ref, dr_ref, drs_ref, dg_ref, db_ref, sq_ref):
        m = pl.program_id(0)
        diff = y_ref[...] - t_ref[...]
        _ln_bwd_tail(m, diff / d, xh_ref, rs_ref, g_ref, scale, dr_ref, drs_ref, dg_ref, db_ref)
        _accumulate(m, sq_ref, _rows8(diff * diff))

    outs = [_sds((t, d), F32), _sds((t, d), BF)]
    accs = [_sds((8, d), F32)] * 3
    return _rowcall("loss_grad", body, t, tm, [y, target, xh, rstd], [gain], outs, accs)


def _adamw(w, g, m, v):
    shape = w.shape
    c = shape[-1]
    rows = math.prod(shape[:-1])
    tm = _tile(rows, 512)

    def body(w_ref, g_ref, m_ref, v_ref, go_ref, d_ref, nm_ref, nv_ref):
        gv = g_ref[...]
        go_ref[...] = gv
        nm = ADAM_B1 * m_ref[...] + (1.0 - ADAM_B1) * gv
        nv = ADAM_B2 * v_ref[...] + (1.0 - ADAM_B2) * (gv * gv)
        m_hat = nm / (1.0 - ADAM_B1 ** ADAM_STEP)
        v_hat = nv / (1.0 - ADAM_B2 ** ADAM_STEP)
        d_ref[...] = -ADAM_LR * (m_hat / (jnp.sqrt(v_hat) + ADAM_EPS) + ADAM_WD * w_ref[...])
        nm_ref[...] = nm
        nv_ref[...] = nv

    flat = [a.reshape(rows, c) for a in (w, g, m, v)]
    out = _sds((rows, c), F32)
    res = _rowcall("adamw", body, rows, tm, flat, [], [out, out, out, out])
    return tuple(r.reshape(shape) for r in res)


def _wgrad(name, a_list, b_list, out_sub, sub_shape, a_get, b_get):
    t = a_list[0].shape[-2]
    tk = _tile(t, 1024)
    na, nb = len(a_list), len(b_list)
    nk = t // tk

    def tspec(shape):
        if len(shape) == 2:
            return pl.BlockSpec((tk, shape[1]), lambda k: (k, 0))
        return pl.BlockSpec((shape[0], tk, shape[2]), lambda k: (0, k, 0))

    def body(*refs):
        a_refs = refs[:na]
        b_refs = refs[na:na + nb]
        o_ref = refs[na + nb]
        acc_ref = refs[na + nb + 1]
        k = pl.program_id(0)

        @pl.when(k == 0)
        def _():
            acc_ref[...] = jnp.zeros_like(acc_ref)

        for s in range(out_sub):
            acc_ref[s] += _dot_tn(a_get(a_refs, s).astype(BF), b_get(b_refs, s).astype(BF))

        @pl.when(k == nk - 1)
        def _():
            o_ref[...] = acc_ref[...].astype(BF)

    full = (out_sub,) + tuple(sub_shape)
    return pl.pallas_call(
        body, name=name, grid=(nk,), in_specs=[tspec(a.shape) for a in a_list] + [tspec(b.shape) for b in b_list],
        out_specs=pl.BlockSpec(full, lambda k: (0, 0, 0)), out_shape=_sds(full, BF),
        scratch_shapes=[pltpu.VMEM(full, F32)], compiler_params=_cparams(("arbitrary",)),
    )(*a_list, *b_list)


def _head_masks(x):
    lo = lax.broadcasted_iota(jnp.int32, (1, LANES), 1) < HEAD_DIM
    zero = jnp.zeros_like(x)
    return lo, (jnp.where(lo, x, zero), jnp.where(lo, zero, x))


def _logistic_parts(z):
    nz = -z
    t = jnp.exp(jnp.minimum(z, nz))
    one_t = 1.0 + t
    return jnp.minimum(nz, 0.0) - jnp.log(one_t), t, one_t


def _split2(x):
    hi = x.astype(BF)
    lo = (x - hi.astype(F32)).astype(BF)
    return jnp.concatenate([hi, lo], axis=1)


def _tri2(cond):
    tri = jnp.where(cond, 1.0, 0.0).astype(BF)
    return jnp.concatenate([tri, tri], axis=0)


def _sb_specs(bl, seq, p_n, col0, tq):
    nq = seq // tq
    q_spec = pl.BlockSpec((tq, LANES), lambda b, p, i: (b * nq + i, col0 + p))
    k_spec = pl.BlockSpec((seq, LANES), lambda b, p, i: (b, col0 + p_n + p))
    v_spec = pl.BlockSpec((seq, LANES), lambda b, p, i: (b, col0 + 2 * p_n + p))
    tile_spec = pl.BlockSpec((tq, LANES), lambda b, p, i: (b * nq + i, p))
    seq_spec = pl.BlockSpec((seq, LANES), lambda b, p, i: (b, p))
    g_spec = pl.BlockSpec((1, LANES), lambda b, p, i: (0, p))
    return nq, q_spec, k_spec, v_spec, tile_spec, seq_spec, g_spec


def _rms_out(acc, lo, gain):
    ss = acc * acc
    s0 = jnp.sum(jnp.where(lo, ss, 0.0), axis=1, keepdims=True)
    s1 = jnp.sum(jnp.where(lo, 0.0, ss), axis=1, keepdims=True)
    r = jnp.where(lo, lax.rsqrt(s0 / HEAD_DIM + RMS_EPS), lax.rsqrt(s1 / HEAD_DIM + RMS_EPS))
    return acc * r * gain


def _sb_fwd(qkv, bl, seq, dg, gain, bufs=(), layer=None):
    t = bl * seq
    p_n = dg // LANES
    tq, tk = min(SB_TQ, seq), SB_TK
    ratio = tq // tk
    scale = 1.0 / math.sqrt(HEAD_DIM)
    nq, q_spec, k_spec, v_spec, tile_spec, _, g_spec = _sb_specs(bl, seq, p_n, 0, tq)
    assert seq // tk <= LANES
    nt = len(bufs)

    def body(q_ref, k_ref, v_ref, g_ref, *rest):
        o_ref, m_ref, c_ref = rest[nt:nt + 3]
        i = pl.program_id(2)
        if nt:
            start, finish = _gather_plan(rest[nt + 3:2 * nt + 3], [b.shape[2] for b in bufs], rest[2 * nt + 3],
                                         rest[2 * nt + 4], layer)
            grid_step = (pl.program_id(0) * p_n + pl.program_id(1)) * nq + i
            pl.when(grid_step == 0)(start)
        row = lax.broadcasted_iota(jnp.int32, (tq, tk), 0)
        col = lax.broadcasted_iota(jnp.int32, (tq, tk), 1)
        krow = lax.broadcasted_iota(jnp.int32, (tk, tk), 0)
        kcol = lax.broadcasted_iota(jnp.int32, (tk, tk), 1)
        tri2 = _tri2(krow > kcol)
        lo, qh = _head_masks(q_ref[...] * jnp.asarray(scale, BF))
        lane = lax.broadcasted_iota(jnp.int32, (1, LANES), 1)

        def step(j, carry, diag):
            acc, c0, c1, s0, s1 = carry
            cars, seen = (c0, c1), (s0, s1)
            mask = None if diag is None else row > col + diag
            off = pl.multiple_of(j * tk, tk)
            kj = k_ref[pl.ds(off, tk), :]
            vj = v_ref[pl.ds(off, tk), :]
            zs = [_dot_nt(qh[h], kj) for h in range(2)]
            lks = [_logistic_parts(z)[0] for z in zs]
            if mask is not None:
                lks = [jnp.where(mask, lk, 0.0) for lk in lks]
            sins = [_dot(_split2(lk), tri2) for lk in lks]
            ws = []
            for h in range(2):
                w = jnp.exp((lks[h] + zs[h]) + (sins[h] + cars[h]))
                ws.append((w if mask is None else jnp.where(mask, w, 0.0)).astype(BF))
            pvs = [_dot(w, vj) for w in ws]
            new_seen = [jnp.where(lane == j, cars[h], seen[h]) for h in range(2)]
            new_cars = [cars[h] + jnp.sum(lks[h], axis=1, keepdims=True) for h in range(2)]
            return acc + jnp.where(lo, pvs[0], pvs[1]), new_cars[0], new_cars[1], new_seen[0], new_seen[1]

        def corner_step(j):
            top = tq - tk
            mask = krow > kcol
            kj = k_ref[pl.ds(pl.multiple_of(j * tk, tk), tk), :]
            vj = v_ref[pl.ds(pl.multiple_of(j * tk, tk), tk), :]
            zs = [_dot_nt(qh[h][top:], kj) for h in range(2)]
            lks = [jnp.where(mask, _logistic_parts(z)[0], 0.0) for z in zs]
            sins = [_dot(_split2(lk), tri2) for lk in lks]
            ws = [jnp.where(mask, jnp.exp((lks[h] + zs[h]) + sins[h]), 0.0).astype(BF) for h in range(2)]
            pvs = [_dot(w, vj) for w in ws]
            pad = lambda low: jnp.concatenate([jnp.zeros((top,) + low.shape[1:], F32), low], axis=0)
            cars = [pad(jnp.sum(lk, axis=1, keepdims=True)) for lk in lks]
            seen = jnp.where(lane == j, 0.0, jnp.full((tq, LANES), SB_UNSEEN, F32))
            return pad(jnp.where(lo, pvs[0], pvs[1])), cars[0], cars[1], seen, seen

        if ratio == 2:
            carry = step(ratio * i, corner_step(ratio * i + 1), 0)
        else:
            zc = jnp.zeros((tq, 1), F32)
            unseen = jnp.full((tq, LANES), SB_UNSEEN, F32)
            carry = (jnp.zeros((tq, LANES), F32), zc, zc, unseen, unseen)
            for m in reversed(range(ratio)):
                carry = step(ratio * i + m, carry, m * tk)

        def alive(state):
            j, c = state
            return jnp.logical_and(j >= 0, jnp.max(jnp.maximum(c[1], c[2])) >= SB_DEAD)

        _, carry = lax.while_loop(alive, lambda st: (st[0] - 1, step(st[0], st[1], None)), (ratio * i - 1, carry))
        acc = carry[0]
        o_ref[...] = acc
        m_ref[...] = _rms_out(acc, lo, g_ref[...]).astype(BF)
        c_ref[:, :LANES] = carry[3]
        c_ref[:, LANES:] = carry[4]
        if nt:
            pl.when(grid_step == bl * p_n * nq - 1)(finish)

    car_spec = pl.BlockSpec((tq, 2 * LANES), lambda b, p, i: (b * nq + i, p))
    outs = [_sds((t, dg), F32), _sds((t, dg), BF), _sds((t, 2 * dg), F32)]
    if not nt:
        return pl.pallas_call(
            body, name="sb_fwd", grid=(bl, p_n, nq), in_specs=[q_spec, k_spec, v_spec, g_spec],
            out_specs=[tile_spec, tile_spec, car_spec], out_shape=outs,
            compiler_params=_cparams(("parallel", "parallel", "parallel")),
        )(qkv, qkv, qkv, gain)
    res = pl.pallas_call(
        body, name="sb_fwd_gather", grid=(bl, p_n, nq), in_specs=[q_spec, k_spec, v_spec, g_spec] + [_ANY] * nt,
        out_specs=[tile_spec, tile_spec, car_spec] + [_ANY] * nt,
        out_shape=outs + [_sds(b.shape, b.dtype) for b in bufs],
        input_output_aliases={4 + k: 3 + k for k in range(nt)},
        scratch_shapes=[pltpu.SemaphoreType.DMA((6 * nt,)), pltpu.SemaphoreType.DMA((6 * nt,))],
        compiler_params=_cparams(("arbitrary", "arbitrary", "arbitrary")),
    )(qkv, qkv, qkv, gain, *bufs)
    return res[0], res[1], res[2], list(res[3:])


def _sb_bwd(qkv, cars, do, bl, seq, dg):
    t = bl * seq
    p_n = dg // LANES
    tq, tk = min(SB_TQ, seq), SB_TK
    ratio = tq // tk
    scale = 1.0 / math.sqrt(HEAD_DIM)
    nq, q_spec, k_spec, v_spec, tile_spec, seq_spec, _ = _sb_specs(bl, seq, p_n, 0, tq)
    car_spec = pl.BlockSpec((tq, 2 * LANES), lambda b, p, i: (b * nq + i, p))

    def body(q_ref, k_ref, v_ref, c_ref, do_ref, dq_ref, dk_ref, dv_ref, dk_acc, dv_acc):
        i = pl.program_id(2)

        @pl.when(i == 0)
        def _():
            dk_acc[...] = jnp.zeros_like(dk_acc)
            dv_acc[...] = jnp.zeros_like(dv_acc)

        row = lax.broadcasted_iota(jnp.int32, (tq, tk), 0)
        col = lax.broadcasted_iota(jnp.int32, (tq, tk), 1)
        krow = lax.broadcasted_iota(jnp.int32, (tk, tk), 0)
        kcol = lax.broadcasted_iota(jnp.int32, (tk, tk), 1)
        tri2 = _tri2(krow > kcol)
        tri_pre2 = _tri2(krow < kcol)
        lane = lax.broadcasted_iota(jnp.int32, (1, LANES), 1)
        sc = jnp.asarray(scale, BF)
        qs = q_ref[...] * sc
        dob = do_ref[...]
        lo, qh = _head_masks(qs)
        _, doh = _head_masks(dob)
        seen = (c_ref[:, :LANES], c_ref[:, LANES:])

        def step(j, carry, diag):
            dq, e0, e1 = carry
            ecars = (e0, e1)
            mask = None if diag is None else row > col + diag
            off = pl.multiple_of(j * tk, tk)
            kj = k_ref[pl.ds(off, tk), :]
            vj = v_ref[pl.ds(off, tk), :]
            zs = [_dot_nt(qh[h], kj) for h in range(2)]
            dws = [_dot_nt(doh[h], vj) for h in range(2)]
            parts = [_logistic_parts(z) for z in zs]
            lks = [p[0] if mask is None else jnp.where(mask, p[0], 0.0) for p in parts]
            sins = [_dot(_split2(lk), tri2) for lk in lks]
            ws, es = [], []
            for h in range(2):
                car = jnp.sum(jnp.where(lane == j, seen[h], 0.0), axis=1, keepdims=True)
                w = jnp.exp((lks[h] + zs[h]) + (sins[h] + car))
                w = w if mask is None else jnp.where(mask, w, 0.0)
                ws.append(w.astype(BF))
                es.append(dws[h] * w)
            pres = [_dot(_split2(e), tri_pre2) for e in es]
            dzs = []
            for h in range(2):
                t_, e = parts[h][1], es[h]
                before = pres[h] + ecars[h]
                dz = pl.reciprocal(parts[h][2], approx=True) * jnp.where(zs[h] > 0.0, e * t_ - before, e - t_ * before)
                dzs.append((dz if mask is None else jnp.where(mask, dz, 0.0)).astype(BF))
            kjs = kj * sc
            dqs = [_dot(dz, kjs) for dz in dzs]
            dks = [_dot_tn(dz, qs) for dz in dzs]
            dvs = [_dot_tn(w, dob) for w in ws]
            dk_acc[pl.ds(off, tk), :] += jnp.where(lo, dks[0], dks[1])
            dv_acc[pl.ds(off, tk), :] += jnp.where(lo, dvs[0], dvs[1])
            new_e = [ecars[h] + jnp.sum(es[h], axis=1, keepdims=True) for h in range(2)]
            return dq + jnp.where(lo, dqs[0], dqs[1]), new_e[0], new_e[1]

        reach = jnp.max(jnp.maximum(seen[0], seen[1]), axis=0, keepdims=True)
        skipped = jnp.logical_and(reach < SB_DEAD, lane < ratio * i)
        first = jnp.minimum(jnp.sum(jnp.where(skipped, 1, 0)).astype(jnp.int32), ratio * i)
        zc = jnp.zeros((tq, 1), F32)
        def corner_step(j, carry):
            dq, e0, e1 = carry
            top = tq - tk
            mask = krow > kcol
            off = pl.multiple_of(j * tk, tk)
            kj = k_ref[pl.ds(off, tk), :]
            vj = v_ref[pl.ds(off, tk), :]
            zs = [_dot_nt(qh[h][top:], kj) for h in range(2)]
            dws = [_dot_nt(doh[h][top:], vj) for h in range(2)]
            parts = [_logistic_parts(z) for z in zs]
            lks = [jnp.where(mask, p[0], 0.0) for p in parts]
            sins = [_dot(_split2(lk), tri2) for lk in lks]
            ws, es = [], []
            for h in range(2):
                car = jnp.sum(jnp.where(lane == j, seen[h][top:], 0.0), axis=1, keepdims=True)
                w = jnp.where(mask, jnp.exp((lks[h] + zs[h]) + (sins[h] + car)), 0.0)
                ws.append(w.astype(BF))
                es.append(dws[h] * w)
            pres = [_dot(_split2(e), tri_pre2) for e in es]
            dzs = []
            for h, ecar in enumerate((e0, e1)):
                t_, e = parts[h][1], es[h]
                before = pres[h] + ecar[top:]
                dz = pl.reciprocal(parts[h][2], approx=True) * jnp.where(zs[h] > 0.0, e * t_ - before, e - t_ * before)
                dzs.append(jnp.where(mask, dz, 0.0).astype(BF))
            kjs = kj * sc
            dqs = [_dot(dz, kjs) for dz in dzs]
            dks = [_dot_tn(dz, qs[top:]) for dz in dzs]
            dvs = [_dot_tn(w, dob[top:]) for w in ws]
            dk_acc[pl.ds(off, tk), :] += jnp.where(lo, dks[0], dks[1])
            dv_acc[pl.ds(off, tk), :] += jnp.where(lo, dvs[0], dvs[1])
            return dq + jnp.concatenate([jnp.zeros((top, LANES), F32), jnp.where(lo, dqs[0], dqs[1])], axis=0)

        carry = lax.fori_loop(first, ratio * i, lambda j, c: step(j, c, None), (jnp.zeros((tq, LANES), F32), zc, zc))
        if ratio == 2:
            dq = corner_step(ratio * i + 1, step(ratio * i, carry, 0))
        else:
            for m in range(ratio):
                carry = step(ratio * i + m, carry, m * tk)
            dq = carry[0]
        dq_ref[...] = dq.astype(BF)

        @pl.when(i == nq - 1)
        def _():
            dk_ref[...] = dk_acc[...].astype(BF)
            dv_ref[...] = dv_acc[...].astype(BF)

    out = _sds((t, dg), BF)
    return pl.pallas_call(
        body, name="sb_bwd", grid=(bl, p_n, nq),
        in_specs=[q_spec, k_spec, v_spec, car_spec, tile_spec],
        out_specs=[tile_spec, seq_spec, seq_spec], out_shape=[out, out, out],
        scratch_shapes=[pltpu.VMEM((seq, LANES), F32), pltpu.VMEM((seq, LANES), F32)],
        compiler_params=_cparams(("parallel", "parallel", "arbitrary")),
    )(qkv, qkv, qkv, cars, do)


def _bias_expand(rb):
    h = rb.shape[0]
    pad = CHUNK * N_PREV_CHUNKS
    n_f = BAND + CHUNK - 1
    f = jnp.concatenate([rb[:, MAX_REL - (CHUNK - 1):2 * MAX_REL],
                         jnp.broadcast_to(rb[:, 2 * MAX_REL:], (h, pad + CHUNK - MAX_REL))], axis=1)
    fr = f[:, ::-1]
    skew = jnp.pad(jnp.broadcast_to(fr[:, None, :], (h, CHUNK, n_f)), ((0, 0), (0, 0), (0, 1)))
    skew = skew.reshape(h, CHUNK * (n_f + 1))[:, :CHUNK * n_f].reshape(h, CHUNK, n_f)
    table = skew[:, :, CHUNK - 1:CHUNK - 1 + BAND]
    n_c = ATT_TILE // CHUNK
    width = 3 * ATT_TILE
    rows = [jnp.pad(table, ((0, 0), (0, 0), (c * CHUNK, width - BAND - c * CHUNK)), constant_values=NEG_BIAS)
            for c in range(n_c)]
    return jnp.concatenate(rows, axis=1)


def _ca_scores(qh_h, k_ref, bias_ref, h, i, scale):
    tq = ATT_TILE
    ss, offs = [], []
    for kb in range(3):
        jb = i - 2 + kb
        off = pl.multiple_of(jnp.maximum(jb, 0) * tq, tq)
        s = _dot_nt(qh_h, k_ref[pl.ds(off, tq), :]) * scale + bias_ref[h, :, kb * tq:(kb + 1) * tq]
        ss.append(jnp.where(jb >= 0, s, NEG_BIAS))
        offs.append(off)
    m = jnp.max(jnp.maximum(jnp.maximum(ss[0], ss[1]), ss[2]), axis=1, keepdims=True)
    ps = [jnp.exp(s - m) for s in ss]
    den = jnp.sum(ps[0] + ps[1] + ps[2], axis=1, keepdims=True)
    inv = 1.0 / den
    return [p * inv for p in ps], offs


def _ca_fwd(qkv, bias4, bl, seq, dg, gain, bufs=(), layer=None):
    t = bl * seq
    p_n = dg // LANES
    tq = ATT_TILE
    scale = 1.0 / math.sqrt(HEAD_DIM)
    nq, q_spec, k_spec, v_spec, tile_spec, _, g_spec = _sb_specs(bl, seq, p_n, 3 * p_n, tq)
    b_spec = pl.BlockSpec((2, tq, 3 * tq), lambda b, p, i: (p, 0, 0))
    nt = len(bufs)

    def body(q_ref, k_ref, v_ref, bias_ref, g_ref, *rest):
        o_ref, m_ref = rest[nt:nt + 2]
        i = pl.program_id(2)
        if nt:
            start, finish = _gather_plan(rest[nt + 2:2 * nt + 2], [b.shape[2] for b in bufs], rest[2 * nt + 2],
                                         rest[2 * nt + 3], layer)
            step = (pl.program_id(0) * p_n + pl.program_id(1)) * nq + i
            pl.when(step == 0)(start)
        lo, qh = _head_masks(q_ref[...])
        outs = []
        for h in range(2):
            ps, offs = _ca_scores(qh[h], k_ref, bias_ref, h, i, scale)
            acc = None
            for kb in range(3):
                pv = _dot(ps[kb].astype(BF), v_ref[pl.ds(offs[kb], tq), :])
                acc = pv if acc is None else acc + pv
            outs.append(acc)
        acc = jnp.where(lo, outs[0], outs[1])
        o_ref[...] = acc
        m_ref[...] = _rms_out(acc, lo, g_ref[...]).astype(BF)
        if nt:
            pl.when(step == bl * p_n * nq - 1)(finish)

    if not nt:
        return pl.pallas_call(
            body, name="ca_fwd", grid=(bl, p_n, nq), in_specs=[q_spec, k_spec, v_spec, b_spec, g_spec],
            out_specs=[tile_spec, tile_spec], out_shape=[_sds((t, dg), F32), _sds((t, dg), BF)],
            compiler_params=_cparams(("parallel", "parallel", "parallel")),
        )(qkv, qkv, qkv, bias4, gain)
    res = pl.pallas_call(
        body, name="ca_fwd_gather", grid=(bl, p_n, nq),
        in_specs=[q_spec, k_spec, v_spec, b_spec, g_spec] + [_ANY] * nt,
        out_specs=[tile_spec, tile_spec] + [_ANY] * nt,
        out_shape=[_sds((t, dg), F32), _sds((t, dg), BF)] + [_sds(b.shape, b.dtype) for b in bufs],
        input_output_aliases={5 + k: 2 + k for k in range(nt)},
        scratch_shapes=[pltpu.SemaphoreType.DMA((6 * nt,)), pltpu.SemaphoreType.DMA((6 * nt,))],
        compiler_params=_cparams(("arbitrary", "arbitrary", "arbitrary")),
    )(qkv, qkv, qkv, bias4, gain, *bufs)
    return res[0], res[1], list(res[2:])


def _ca_bwd(qkv, bias4, o, do, bl, seq, dg, psums=()):
    t = bl * seq
    p_n = dg // LANES
    tq = ATT_TILE
    scale = 1.0 / math.sqrt(HEAD_DIM)
    nq = seq // tq
    c0 = 3 * p_n
    q_spec = pl.BlockSpec((tq, LANES), lambda p, b, i: (b * nq + i, c0 + p))
    k_spec = pl.BlockSpec((seq, LANES), lambda p, b, i: (b, c0 + p_n + p))
    v_spec = pl.BlockSpec((seq, LANES), lambda p, b, i: (b, c0 + 2 * p_n + p))
    tile_spec = pl.BlockSpec((tq, LANES), lambda p, b, i: (b * nq + i, p))
    seq_spec = pl.BlockSpec((seq, LANES), lambda p, b, i: (b, p))
    b_spec = pl.BlockSpec((2, tq, 3 * tq), lambda p, b, i: (p, 0, 0))

    nt = len(psums)

    def body(q_ref, k_ref, v_ref, bias_ref, o_ref, do_ref, *rest):
        dq_ref, dk_ref, dv_ref, db_ref = rest[nt:nt + 4]
        dk_acc, dv_acc = rest[2 * nt + 4:2 * nt + 6]
        b = pl.program_id(1)
        i = pl.program_id(2)
        if nt:
            start, finish = _scatter_plan(rest[:nt], rest[nt + 4:2 * nt + 4], rest[2 * nt + 6], rest[2 * nt + 7])
            step = (pl.program_id(0) * bl + b) * nq + i
            pl.when(step == 0)(start)

        @pl.when(i == 0)
        def _():
            dk_acc[...] = jnp.zeros_like(dk_acc)
            dv_acc[...] = jnp.zeros_like(dv_acc)

        @pl.when((i == 0) & (b == 0))
        def _():
            db_ref[...] = jnp.zeros_like(db_ref)

        q = q_ref[...]
        dob = do_ref[...]
        lo, qh = _head_masks(q)
        _, doh = _head_masks(dob)
        prod = dob.astype(F32) * o_ref[...]
        delta = (jnp.sum(jnp.where(lo, prod, 0.0), axis=1, keepdims=True),
                 jnp.sum(jnp.where(lo, 0.0, prod), axis=1, keepdims=True))
        dqs = []
        for h in range(2):
            ps, offs = _ca_scores(qh[h], k_ref, bias_ref, h, i, scale)
            dq = None
            dks, dvs = [], []
            for kb in range(3):
                dp = _dot_nt(doh[h], v_ref[pl.ds(offs[kb], tq), :])
                ds = ps[kb] * (dp - delta[h])
                db_ref[h, :, kb * tq:(kb + 1) * tq] += ds
                dsb = (ds * scale).astype(BF)
                part = _dot(dsb, k_ref[pl.ds(offs[kb], tq), :])
                dq = part if dq is None else dq + part
                dks.append(_dot_tn(dsb, q))
                dvs.append(_dot_tn(ps[kb].astype(BF), dob))
            dqs.append(dq)
            hm = lo if h == 0 else jnp.logical_not(lo)
            for kb in range(3):
                dk_acc[pl.ds(offs[kb], tq), :] += jnp.where(hm, dks[kb], 0.0)
                dv_acc[pl.ds(offs[kb], tq), :] += jnp.where(hm, dvs[kb], 0.0)
        dq_ref[...] = jnp.where(lo, dqs[0], dqs[1]).astype(BF)

        @pl.when(i == nq - 1)
        def _():
            dk_ref[...] = dk_acc[...].astype(BF)
            dv_ref[...] = dv_acc[...].astype(BF)

        if nt:
            pl.when(step == p_n * bl * nq - 1)(finish)

    out = _sds((t, dg), BF)
    acc = [pltpu.VMEM((seq, LANES), F32), pltpu.VMEM((seq, LANES), F32)]
    if not nt:
        return pl.pallas_call(
            body, name="ca_bwd", grid=(p_n, bl, nq),
            in_specs=[q_spec, k_spec, v_spec, b_spec, tile_spec, tile_spec],
            out_specs=[tile_spec, seq_spec, seq_spec, b_spec], out_shape=[out, out, out, _sds(bias4.shape, F32)],
            scratch_shapes=acc, compiler_params=_cparams(("parallel", "arbitrary", "arbitrary")),
        )(qkv, qkv, qkv, bias4, o, do)
    res = pl.pallas_call(
        body, name="ca_bwd_scatter", grid=(p_n, bl, nq),
        in_specs=[q_spec, k_spec, v_spec, b_spec, tile_spec, tile_spec] + [_ANY] * nt,
        out_specs=[tile_spec, seq_spec, seq_spec, b_spec] + [_ANY] * nt,
        out_shape=[out, out, out, _sds(bias4.shape, F32)] + _scatter_shapes(psums),
        scratch_shapes=acc + [pltpu.SemaphoreType.DMA((3 * nt,)), pltpu.SemaphoreType.DMA((3 * nt,))],
        compiler_params=_cparams(("arbitrary", "arbitrary", "arbitrary")),
    )(qkv, qkv, qkv, bias4, o, do, *psums)
    return res[0], res[1], res[2], res[3], list(res[4:])


_ANY = pl.BlockSpec(memory_space=pl.ANY)


def _mesh_pos():
    x, y, c = lax.axis_index("x"), lax.axis_index("y"), lax.axis_index("c")
    chips = [(1 - x, y), (x, 1 - y), (1 - x, 1 - y)]
    return x, y, c, chips


def _half(ref_rows, c):
    return pl.ds(c * (ref_rows // 2), ref_rows // 2)


def _cast_into_slot(w, slot):
    l_n, r, c_n = w.shape
    tr = _tile(r, 512)

    def body(s_ref, w_ref, o_ref):
        o_ref[...] = w_ref[...].astype(BF)

    return pl.pallas_call(
        body, name="cast_into_slot", out_shape=_sds((l_n, 4, r, c_n), BF),
        grid_spec=pltpu.PrefetchScalarGridSpec(
            num_scalar_prefetch=1, grid=(l_n, r // tr),
            in_specs=[pl.BlockSpec((None, tr, c_n), lambda l, m, s: (l, m, 0))],
            out_specs=pl.BlockSpec((None, None, tr, c_n), lambda l, m, s: (l, s[0], m, 0))),
        compiler_params=_cparams(("parallel", "parallel")),
    )(slot, w)


def _gather_plan(outs, n_rows, send_sems, recv_sems, layer):
    nt = len(outs)

    def blk(t, s, cc):
        return outs[t].at[pl.ds(layer, 1), s, _half(n_rows[t], cc)]

    def copy(k, ref, to):
        return pltpu.make_async_remote_copy(src_ref=ref, dst_ref=ref, send_sem=send_sems.at[k],
                                            recv_sem=recv_sems.at[k], device_id=to, device_id_type=MESH_IDS)

    def over_ici():
        x, y, c, chips = _mesh_pos()
        return [copy(j * nt + t, blk(t, 2 * x + y, c), (px, py, c)) for j, (px, py) in enumerate(chips) for t in range(nt)]

    def start():
        for cp in over_ici():
            cp.start()

    def finish():
        x, y, c, chips = _mesh_pos()
        passed = []
        for j, (px, py) in enumerate(chips):
            for t in range(nt):
                got = blk(t, 2 * px + py, c)
                copy(j * nt + t, got, (px, py, c)).wait_recv()
                relay = copy(3 * nt + j * nt + t, got, (x, y, 1 - c))
                relay.start()
                passed.append(relay)
        for j, (px, py) in enumerate(chips):
            for t in range(nt):
                copy(3 * nt + j * nt + t, blk(t, 2 * px + py, 1 - c), (x, y, 1 - c)).wait_recv()
        for cp in over_ici() + passed:
            cp.wait_send()

    return start, finish


def _gather_weights(bufs, layer):
    nt = len(bufs)

    def body(*refs):
        start, finish = _gather_plan(refs[nt:2 * nt], [b.shape[2] for b in bufs], refs[2 * nt], refs[2 * nt + 1], layer)
        start()
        finish()

    return pl.pallas_call(
        body, name="gather_weights", in_specs=[_ANY] * nt, out_specs=[_ANY] * nt,
        out_shape=[_sds(b.shape, b.dtype) for b in bufs], input_output_aliases={t: t for t in range(nt)},
        scratch_shapes=[pltpu.SemaphoreType.DMA((6 * nt,)), pltpu.SemaphoreType.DMA((6 * nt,))],
    )(*bufs)


def _swap_halves(grads):
    nt = len(grads)

    def body(*refs):
        ins, theirs_o = refs[:nt], refs[nt:2 * nt]
        send_sems, recv_sems = refs[2 * nt:]
        x, y, c, _ = _mesh_pos()
        sends = [pltpu.make_async_remote_copy(
            src_ref=ins[t].at[:, :, _half(grads[t].shape[2], 1 - c)], dst_ref=theirs_o[t],
            send_sem=send_sems.at[t], recv_sem=recv_sems.at[t], device_id=(x, y, 1 - c), device_id_type=MESH_IDS)
            for t in range(nt)]
        for cp in sends:
            cp.start()
        for cp in sends:
            cp.wait_recv()
        for cp in sends:
            cp.wait_send()

    half = [_sds(g.shape[:2] + (g.shape[2] // 2, g.shape[3]), g.dtype) for g in grads]
    return pl.pallas_call(
        body, name="swap_halves", in_specs=[_ANY] * nt, out_specs=[_ANY] * nt, out_shape=half,
        scratch_shapes=[pltpu.SemaphoreType.DMA((nt,)), pltpu.SemaphoreType.DMA((nt,))],
    )(*grads)


def _add_pair(mine, theirs):
    l_n, s_n, r2, c_n = theirs.shape

    def body(a_ref, b_ref, o_ref):
        rows = pl.ds(pl.multiple_of(lax.axis_index("c") * r2, 16), r2)
        o_ref[...] = (a_ref[rows, :].astype(F32) + b_ref[...].astype(F32)).astype(BF)

    blk = (None, None, r2, c_n)
    return pl.pallas_call(
        body, name="add_pair", out_shape=_sds(theirs.shape, BF), grid=(l_n, s_n),
        in_specs=[pl.BlockSpec((None, None, 2 * r2, c_n), lambda l, s: (l, s, 0, 0)),
                  pl.BlockSpec(blk, lambda l, s: (l, s, 0, 0))],
        out_specs=pl.BlockSpec(blk, lambda l, s: (l, s, 0, 0)),
        compiler_params=_cparams(("parallel", "parallel")),
    )(mine, theirs)


def _scatter_chips(psums):
    nt = len(psums)

    def body(*refs):
        start, finish = _scatter_plan(refs[:nt], refs[nt:2 * nt], refs[2 * nt], refs[2 * nt + 1])
        start()
        finish()

    return pl.pallas_call(
        body, name="scatter_chips", in_specs=[_ANY] * nt, out_specs=[_ANY] * nt, out_shape=_scatter_shapes(psums),
        scratch_shapes=[pltpu.SemaphoreType.DMA((3 * nt,)), pltpu.SemaphoreType.DMA((3 * nt,))],
    )(*psums)


def _scatter_shapes(psums):
    return [_sds((3, p.shape[0]) + p.shape[2:], p.dtype) for p in psums]


def _scatter_plan(ins, got, send_sems, recv_sems):
    nt = len(ins)

    def sends():
        _, _, c, chips = _mesh_pos()
        return [pltpu.make_async_remote_copy(
            src_ref=ins[t].at[:, 2 * px + py], dst_ref=got[t].at[j], send_sem=send_sems.at[j * nt + t],
            recv_sem=recv_sems.at[j * nt + t], device_id=(px, py, c), device_id_type=MESH_IDS)
            for j, (px, py) in enumerate(chips) for t in range(nt)]

    def start():
        for cp in sends():
            cp.start()

    def finish():
        for cp in sends():
            cp.wait_recv()
        for cp in sends():
            cp.wait_send()

    return start, finish


def _add_chips(psum, got):
    l_n, _, r2, c_n = psum.shape

    def body(a_ref, g_ref, o_ref):
        c = lax.axis_index("c")
        acc = a_ref[2 * lax.axis_index("x") + lax.axis_index("y")].astype(F32)
        for j in range(3):
            acc = acc + g_ref[j].astype(F32)
        o_ref[pl.ds(pl.multiple_of(c * r2, 8), r2), :] = acc
        o_ref[pl.ds(pl.multiple_of((1 - c) * r2, 8), r2), :] = jnp.zeros_like(acc)

    return pl.pallas_call(
        body, name="add_chips", out_shape=_sds((l_n, 2 * r2, c_n), F32), grid=(l_n,),
        in_specs=[pl.BlockSpec((None, 4, r2, c_n), lambda l: (l, 0, 0, 0)),
                  pl.BlockSpec((3, None, r2, c_n), lambda l: (0, l, 0, 0))],
        out_specs=pl.BlockSpec((None, 2 * r2, c_n), lambda l: (l, 0, 0)),
        compiler_params=_cparams(("parallel",)),
    )(psum, got)


def _join_halves(fulls):
    nt = len(fulls)

    def body(*refs):
        outs = refs[nt:2 * nt]
        send_sems, recv_sems = refs[2 * nt:]
        x, y, c, _ = _mesh_pos()

        def copy(t, cc):
            blk = outs[t].at[:, _half(fulls[t].shape[1], cc)]
            return pltpu.make_async_remote_copy(
                src_ref=blk, dst_ref=blk, send_sem=send_sems.at[t], recv_sem=recv_sems.at[t],
                device_id=(x, y, 1 - c), device_id_type=MESH_IDS)

        sends = [copy(t, c) for t in range(nt)]
        for cp in sends:
            cp.start()
        for t in range(nt):
            copy(t, 1 - c).wait_recv()
        for cp in sends:
            cp.wait_send()

    return pl.pallas_call(
        body, name="join_halves", in_specs=[_ANY] * nt, out_specs=[_ANY] * nt,
        out_shape=[_sds(f.shape, f.dtype) for f in fulls], input_output_aliases={t: t for t in range(nt)},
        scratch_shapes=[pltpu.SemaphoreType.DMA((nt,)), pltpu.SemaphoreType.DMA((nt,))],
    )(*fulls)


def _allreduce_small(part):
    r, d = part.shape
    flips = [(fx, fy, fc) for fx in (0, 1) for fy in (0, 1) for fc in (0, 1) if fx + fy + fc]

    def body(p_ref, o_ref, buf, send_sems, recv_sems):
        x, y, c, _ = _mesh_pos()
        me = 4 * x + 2 * y + c
        buf[me] = p_ref[...]
        sends = []
        for k, (fx, fy, fc) in enumerate(flips):
            sends.append(pltpu.make_async_remote_copy(
                src_ref=p_ref, dst_ref=buf.at[me], send_sem=send_sems.at[k], recv_sem=recv_sems.at[k],
                device_id=(x ^ fx, y ^ fy, c ^ fc), device_id_type=MESH_IDS))
        for cp in sends:
            cp.start()
        for k, (fx, fy, fc) in enumerate(flips):
            peer = 4 * (x ^ fx) + 2 * (y ^ fy) + (c ^ fc)
            pltpu.make_async_remote_copy(
                src_ref=p_ref, dst_ref=buf.at[peer], send_sem=send_sems.at[k], recv_sem=recv_sems.at[k],
                device_id=(x ^ fx, y ^ fy, c ^ fc), device_id_type=MESH_IDS).wait_recv()
        acc = buf[0]
        for dev in range(1, 8):
            acc = acc + buf[dev]
        o_ref[...] = acc
        for cp in sends:
            cp.wait_send()

    vmem = pl.BlockSpec(memory_space=pltpu.VMEM)
    return pl.pallas_call(
        body, name="allreduce_small", in_specs=[vmem], out_specs=vmem, out_shape=_sds((r, d), F32),
        scratch_shapes=[pltpu.VMEM((8, r, d), F32), pltpu.SemaphoreType.DMA((7,)), pltpu.SemaphoreType.DMA((7,))],
    )(part)


BIG = ("ffn1_w_gate", "ffn1_w_up", "ffn1_w_down", "w_in", "w_out", "ffn2_w_gate", "ffn2_w_up", "ffn2_w_down")
SMALL = ("ln1_g", "ln1_b", "rel_bias", "sb_out_g", "ca_out_g", "ln2_g", "ln2_b", "ln3_g", "ln3_b")
ORDER = ("ffn1_w_gate", "ffn1_w_up", "ffn1_w_down", "ln1_g", "ln1_b", "w_in", "rel_bias", "sb_out_g", "ca_out_g",
         "w_out", "ln2_g", "ln2_b", "ffn2_w_gate", "ffn2_w_up", "ffn2_w_down", "ln3_g", "ln3_b")


def _pack_small(vals, depth, d):
    rows = []
    for l in range(depth):
        for n in ("ln1_g", "ln1_b", "ln2_g", "ln2_b", "ln3_g", "ln3_b"):
            rows.append(vals[n][l][None, :])
        rows.append(jnp.concatenate([vals["sb_out_g"][l], vals["ca_out_g"][l]])[None, :])
        rb = vals["rel_bias"][l]
        rows.append(jnp.pad(rb, ((0, 0), (0, d - rb.shape[1]))))
    packed = jnp.concatenate(rows, axis=0)
    pad = (-packed.shape[0]) % 8
    return jnp.pad(packed, ((0, pad), (0, 0)))


def _unpack_small(packed, depth, d, heads, n_rel):
    per = 7 + heads
    out = {n: [] for n in SMALL}
    for l in range(depth):
        base = l * per
        for i, n in enumerate(("ln1_g", "ln1_b", "ln2_g", "ln2_b", "ln3_g", "ln3_b")):
            out[n].append(packed[base + i])
        out["sb_out_g"].append(packed[base + 6, :d // 2])
        out["ca_out_g"].append(packed[base + 6, d // 2:])
        out["rel_bias"].append(packed[base + 7:base + 7 + heads, :n_rel])
    return {n: jnp.stack(v) for n, v in out.items()}


def kernel(x, ffn1_w_gate, ffn1_w_up, ffn1_w_down, ln1_g, ln1_b, w_in, rel_bias, sb_out_g, ca_out_g, w_out, ln2_g, ln2_b, ffn2_w_gate, ffn2_w_up, ffn2_w_down, ln3_g, ln3_b, loss_target, m_ffn1_w_gate, m_ffn1_w_up, m_ffn1_w_down, m_ln1_g, m_ln1_b, m_w_in, m_rel_bias, m_sb_out_g, m_ca_out_g, m_w_out, m_ln2_g, m_ln2_b, m_ffn2_w_gate, m_ffn2_w_up, m_ffn2_w_down, m_ln3_g, m_ln3_b, v_ffn1_w_gate, v_ffn1_w_up, v_ffn1_w_down, v_ln1_g, v_ln1_b, v_w_in, v_rel_bias, v_sb_out_g, v_ca_out_g, v_w_out, v_ln2_g, v_ln2_b, v_ffn2_w_gate, v_ffn2_w_up, v_ffn2_w_down, v_ln3_g, v_ln3_b):
    w = dict(ffn1_w_gate=ffn1_w_gate, ffn1_w_up=ffn1_w_up, ffn1_w_down=ffn1_w_down, ln1_g=ln1_g, ln1_b=ln1_b, w_in=w_in, rel_bias=rel_bias, sb_out_g=sb_out_g, ca_out_g=ca_out_g, w_out=w_out, ln2_g=ln2_g, ln2_b=ln2_b, ffn2_w_gate=ffn2_w_gate, ffn2_w_up=ffn2_w_up, ffn2_w_down=ffn2_w_down, ln3_g=ln3_g, ln3_b=ln3_b)
    mom = dict(ffn1_w_gate=m_ffn1_w_gate, ffn1_w_up=m_ffn1_w_up, ffn1_w_down=m_ffn1_w_down, ln1_g=m_ln1_g, ln1_b=m_ln1_b, w_in=m_w_in, rel_bias=m_rel_bias, sb_out_g=m_sb_out_g, ca_out_g=m_ca_out_g, w_out=m_w_out, ln2_g=m_ln2_g, ln2_b=m_ln2_b, ffn2_w_gate=m_ffn2_w_gate, ffn2_w_up=m_ffn2_w_up, ffn2_w_down=m_ffn2_w_down, ln3_g=m_ln3_g, ln3_b=m_ln3_b)
    var = dict(ffn1_w_gate=v_ffn1_w_gate, ffn1_w_up=v_ffn1_w_up, ffn1_w_down=v_ffn1_w_down, ln1_g=v_ln1_g, ln1_b=v_ln1_b, w_in=v_w_in, rel_bias=v_rel_bias, sb_out_g=v_sb_out_g, ca_out_g=v_ca_out_g, w_out=v_w_out, ln2_g=v_ln2_g, ln2_b=v_ln2_b, ffn2_w_gate=v_ffn2_w_gate, ffn2_w_up=v_ffn2_w_up, ffn2_w_down=v_ffn2_w_down, ln3_g=v_ln3_g, ln3_b=v_ln3_b)

    bl, seq, d = x.shape
    t = bl * seq
    depth = w_in.shape[0]
    dg = d // 2
    heads = dg // HEAD_DIM
    alpha = (2 * depth) ** 0.25
    n_rel = rel_bias.shape[-1]

    chip = (2 * lax.axis_index("x") + lax.axis_index("y")).astype(jnp.int32).reshape(1)
    bufs = list(_gather_weights([_cast_into_slot(w[n], chip) for n in BIG], 0))
    with_sb = [BIG.index(n) for n in ("w_in", "ffn2_w_gate", "ffn2_w_up", "ffn2_w_down")]
    with_ca = [k for k in range(len(BIG)) if k not in with_sb]

    qd = dg // 2
    shards = 4
    ffn_terms = [(0, s, None, s) for s in range(shards)]
    out_terms = [(i, None, (k * qd, (k + 1) * qd), 2 * i + k) for i in range(2) for k in range(2)]
    bias_all, bias_vjp = jax.vjp(jax.vmap(_bias_expand), rel_bias)

    xs = x.reshape(t, d)
    saved = []
    for l in range(depth):
        lw = {n: _LayerOf((b, l)) for n, b in zip(BIG, bufs)}
        row = lambda a: a[l][None, :]
        g1, u1, h1 = _ffn_up(xs, lw["ffn1_w_gate"], lw["ffn1_w_up"])
        x1, xh1, rs1 = _proj_ln("ffn_down_ln", [h1], lw["ffn1_w_down"], ffn_terms, xs, row(ln1_g), row(ln1_b),
                                FFN_RESIDUAL, alpha)
        qkv = _qkv_proj(x1, lw["w_in"])
        if l + 1 < depth:
            o_sb, m_sb, c_sb, new = _sb_fwd(qkv, bl, seq, dg, row(sb_out_g), [bufs[k] for k in with_sb], l + 1)
            for k, b in zip(with_sb, new):
                bufs[k] = b
            o_ca, m_ca, new = _ca_fwd(qkv, bias_all[l], bl, seq, dg, row(ca_out_g), [bufs[k] for k in with_ca], l + 1)
            for k, b in zip(with_ca, new):
                bufs[k] = b
            lw = {n: _LayerOf((b, l)) for n, b in zip(BIG, bufs)}
        else:
            o_sb, m_sb, c_sb = _sb_fwd(qkv, bl, seq, dg, row(sb_out_g))
            o_ca, m_ca = _ca_fwd(qkv, bias_all[l], bl, seq, dg, row(ca_out_g))
        x2, xh2, rs2 = _proj_ln("attn_out_ln", [m_sb, m_ca], lw["w_out"], out_terms, x1, row(ln2_g), row(ln2_b),
                                1.0, alpha)
        g2, u2, h2 = _ffn_up(x2, lw["ffn2_w_gate"], lw["ffn2_w_up"])
        x3, xh3, rs3 = _proj_ln("ffn_down_ln", [h2], lw["ffn2_w_down"], ffn_terms, x2, row(ln3_g), row(ln3_b),
                                FFN_RESIDUAL, alpha)
        saved.append(dict(xin=xs, g1=g1, u1=u1, h1=h1, x1=x1, xh1=xh1, rs1=rs1, qkv=qkv, o_sb=o_sb, m_sb=m_sb,
                          c_sb=c_sb, o_ca=o_ca, m_ca=m_ca, x2=x2, xh2=xh2, rs2=rs2, g2=g2, u2=u2, h2=h2, xh3=xh3,
                          rs3=rs3))
        xs = x3

    last = saved[-1]
    dr3, drs3, pg, pb, sq = _loss_grad(xs, loss_target.reshape(t, d), last["xh3"], last["rs3"],
                                       ln3_g[depth - 1][None, :], FFN_RESIDUAL)
    loss = lax.psum(0.5 * jnp.sum(sq) / d, ("x", "y", "c"))

    big_grads = {n: [None] * depth for n in BIG}
    small_grads = {n: [None] * depth for n in SMALL}
    s8 = lambda a: jnp.sum(a, axis=0)
    wd_ = 3 * d // shards
    in_pieces = [_shard_cols(6, dg, s, wd_) for s in range(shards)]
    dbias = [None] * depth
    chip_sums, arrived = [None] * depth, [None] * depth
    for l in reversed(range(depth)):
        sv = saved[l]
        lw = {n: _LayerOf((b, l)) for n, b in zip(BIG, bufs)}
        row = lambda a: a[l][None, :]
        small_grads["ln3_g"][l], small_grads["ln3_b"][l] = s8(pg), s8(pb)
        dgt, dut = _swiglu_bwd(drs3, lw["ffn2_w_down"], sv["g2"], sv["u2"])
        dr2, drs2, pg, pb = _ffn_dx(dgt, dut, lw["ffn2_w_gate"], lw["ffn2_w_up"], dr3, alpha,
                                    (sv["xh2"], sv["rs2"], row(ln2_g), 1.0))
        small_grads["ln2_g"][l], small_grads["ln2_b"][l] = s8(pg), s8(pb)
        big_grads["ffn2_w_down"][l] = _wgrad("wgrad_down", [sv["h2"]], [drs3], shards, _lshape(lw["ffn2_w_down"])[1:],
                                             lambda a, s: a[0][s], lambda b, s: b[0][...])
        big_grads["ffn2_w_gate"][l] = _wgrad("wgrad_up", [sv["x2"]], [dgt], shards, _lshape(lw["ffn2_w_gate"])[1:],
                                             lambda a, s: a[0][...], lambda b, s: b[0][s])
        big_grads["ffn2_w_up"][l] = _wgrad("wgrad_up", [sv["x2"]], [dut], shards, _lshape(lw["ffn2_w_up"])[1:],
                                           lambda a, s: a[0][...], lambda b, s: b[0][s])
        do_sb, do_ca, pgs, pgc = _attn_dmix(drs2, lw["w_out"], sv["o_sb"], sv["o_ca"], row(sb_out_g), row(ca_out_g))
        small_grads["sb_out_g"][l], small_grads["ca_out_g"][l] = s8(pgs), s8(pgc)
        big_grads["w_out"][l] = _wgrad(
            "wgrad_out", [sv["m_sb"], sv["m_ca"]], [drs2], shards, (qd, d),
            lambda a, s: a[s // 2][:, (s % 2) * qd:(s % 2 + 1) * qd], lambda b, s: b[0][...])
        dqa, dka, dva = _sb_bwd(sv["qkv"], sv["c_sb"], do_sb, bl, seq, dg)
        if l + 1 < depth:
            dqb, dkb, dvb, dbias[l], arrived[l + 1] = _ca_bwd(sv["qkv"], bias_all[l], sv["o_ca"], do_ca, bl, seq, dg,
                                                             chip_sums[l + 1])
        else:
            dqb, dkb, dvb, dbias[l] = _ca_bwd(sv["qkv"], bias_all[l], sv["o_ca"], do_ca, bl, seq, dg)
        dqkv = [dqa, dka, dva, dqb, dkb, dvb]
        dr1, drs1, pg, pb = _attn_dx(dqkv, lw["w_in"], dr2, alpha, (sv["xh1"], sv["rs1"], row(ln1_g), FFN_RESIDUAL))
        small_grads["ln1_g"][l], small_grads["ln1_b"][l] = s8(pg), s8(pb)
        big_grads["w_in"][l] = _wgrad(
            "wgrad_in", [sv["x1"]], dqkv, shards, (d, wd_), lambda a, s: a[0][...],
            lambda b, s: jnp.concatenate([b[gi][:, lo:hi] for gi, lo, hi in in_pieces[s]], axis=1))
        dgt, dut = _swiglu_bwd(drs1, lw["ffn1_w_down"], sv["g1"], sv["u1"])
        if l > 0:
            prev = saved[l - 1]
            dr3, drs3, pg, pb = _ffn_dx(dgt, dut, lw["ffn1_w_gate"], lw["ffn1_w_up"], dr1, alpha,
                                        (prev["xh3"], prev["rs3"], ln3_g[l - 1][None, :], FFN_RESIDUAL))
        else:
            grad_x = _ffn_dx(dgt, dut, lw["ffn1_w_gate"], lw["ffn1_w_up"], dr1, alpha, None).reshape(bl, seq, d)
        big_grads["ffn1_w_down"][l] = _wgrad("wgrad_down", [sv["h1"]], [drs1], shards, _lshape(lw["ffn1_w_down"])[1:],
                                             lambda a, s: a[0][s], lambda b, s: b[0][...])
        big_grads["ffn1_w_gate"][l] = _wgrad("wgrad_up", [sv["xin"]], [dgt], shards, _lshape(lw["ffn1_w_gate"])[1:],
                                             lambda a, s: a[0][...], lambda b, s: b[0][s])
        big_grads["ffn1_w_up"][l] = _wgrad("wgrad_up", [sv["xin"]], [dut], shards, _lshape(lw["ffn1_w_up"])[1:],
                                           lambda a, s: a[0][...], lambda b, s: b[0][s])
        partial = [big_grads[n][l][None] for n in BIG]
        chip_sums[l] = [_add_pair(a, b) for a, b in zip(partial, _swap_halves(partial))]
    arrived[0] = _scatter_chips(chip_sums[0])
    rel_grads = bias_vjp(jnp.stack(dbias))[0]
    for l in range(depth):
        small_grads["rel_bias"][l] = rel_grads[l]

    halves = [jnp.concatenate([_add_chips(chip_sums[l][k], arrived[l][k]) for l in range(depth)], axis=0)
              for k in range(len(BIG))]
    joined = dict(zip(BIG, _join_halves(halves)))
    packed = _allreduce_small(_pack_small({n: jnp.stack(small_grads[n]) for n in SMALL}, depth, d))

    grads, delta, new_m, new_v = {}, {}, {}, {}
    for n in BIG:
        grads[n], delta[n], new_m[n], new_v[n] = _adamw(w[n], joined[n], mom[n], var[n])
    pw, pm, pv = (_pack_small({n: src[n] for n in SMALL}, depth, d) for src in (w, mom, var))
    sg, sd, sm, sv_ = _adamw(pw, packed, pm, pv)
    for dst, src in ((grads, sg), (delta, sd), (new_m, sm), (new_v, sv_)):
        dst.update(_unpack_small(src, depth, d, heads, n_rel))

    return (loss, grad_x, *[grads[n] for n in ORDER], *[delta[n] for n in ORDER],
            *[new_m[n] for n in ORDER], *[new_v[n] for n in ORDER])
```

```python
import functools
import math

import jax
import jax.numpy as jnp
from jax import lax
from jax.experimental import pallas as pl
from jax.experimental.pallas import tpu as pltpu

F32 = jnp.float32
BF = jnp.bfloat16

CHUNK = 64
N_PREV_CHUNKS = 8
BAND = CHUNK * (N_PREV_CHUNKS + 1)
MAX_REL = 128
HEAD_DIM = 64
FFN_RESIDUAL = 0.5
LN_EPS = 1e-5
RMS_EPS = 1e-6
ADAM_LR = 0.001
ADAM_B1 = 0.9
ADAM_B2 = 0.999
ADAM_EPS = 1e-08
ADAM_WD = 0.01
ADAM_STEP = 10

LANES = 128
ATT_TILE = 4 * CHUNK
SB_TQ = 512
SB_TK = 256
SB_DEAD = -105.0
SB_UNSEEN = -1e30
NEG_BIAS = -1e30
VMEM_LIMIT = 56 * 1024 * 1024
ROW_TILE = 512
MESH_IDS = pl.DeviceIdType.MESH

_NT = (((1,), (1,)), ((), ()))
_TN = (((0,), (0,)), ((), ()))


def _dot(a, b):
    return jnp.dot(a, b, preferred_element_type=F32)


def _dot_nt(a, b):
    return lax.dot_general(a, b, _NT, preferred_element_type=F32)


def _dot_tn(a, b):
    return lax.dot_general(a, b, _TN, preferred_element_type=F32)


def _tile(n, pref):
    t = min(n, pref)
    while n % t:
        t //= 2
    return t


def _cparams(sem):
    return pltpu.CompilerParams(dimension_semantics=sem, vmem_limit_bytes=VMEM_LIMIT)


def _sds(shape, dtype):
    return jax.ShapeDtypeStruct(tuple(shape), dtype)


def _sigmoid(v):
    return pl.reciprocal(1.0 + jnp.exp(-v), approx=True)


def _rows8(v):
    tm, c = v.shape
    return v.reshape(tm // 8, 8, c).sum(axis=0)


class _LayerOf(tuple):
    pass


def _lshape(item):
    return item[0].shape[1:] if isinstance(item, _LayerOf) else item.shape


def _accumulate(m, ref, part):
    @pl.when(m == 0)
    def _():
        ref[...] = part

    @pl.when(m > 0)
    def _():
        ref[...] += part


def _ln_bwd_tail(m, dy, xh_ref, rs_ref, g_ref, scale, dr_ref, drs_ref, dg_ref, db_ref):
    xhv = xh_ref[...]
    dxh = dy * g_ref[...]
    m1 = jnp.mean(dxh, axis=-1, keepdims=True)
    m2 = jnp.mean(dxh * xhv, axis=-1, keepdims=True)
    dr = rs_ref[...] * (dxh - m1 - xhv * m2)
    dr_ref[...] = dr
    drs_ref[...] = (scale * dr).astype(BF)
    _accumulate(m, dg_ref, _rows8(dy * xhv))
    _accumulate(m, db_ref, _rows8(dy))


def _rowcall(name, body, n_rows, tm, row_ins, res_ins, row_outs, acc_outs=()):
    def rspec(shape):
        if len(shape) == 2:
            return pl.BlockSpec((tm, shape[1]), lambda m: (m, 0))
        return pl.BlockSpec((shape[0], tm, shape[2]), lambda m: (0, m, 0))

    def cspec(shape, single=False):
        zeros = (0,) * len(shape)
        mode = pl.Buffered(1) if single else None
        return pl.BlockSpec(tuple(shape), lambda m: zeros, pipeline_mode=mode)

    def split(items, mk):
        arrs, specs = [], []
        for it in items:
            if isinstance(it, _LayerOf):
                arr, l = it
                tail = (0,) * (arr.ndim - 1)
                arrs.append(arr)
                specs.append(pl.BlockSpec((None,) + arr.shape[1:], functools.partial(lambda m, l, tail: (l,) + tail, l=l, tail=tail),
                                          pipeline_mode=pl.Buffered(1)))
            elif isinstance(it, tuple):
                arrs.append(it[0])
                specs.append(it[1])
            else:
                arrs.append(it)
                specs.append(mk(it.shape))
        return arrs, specs

    ra, rs = split(row_ins, rspec)
    ca, cs = split(res_ins, functools.partial(cspec, single=True))
    out_specs = [rspec(s.shape) for s in row_outs] + [cspec(s.shape) for s in acc_outs]
    sem = ("arbitrary",) if acc_outs else ("parallel",)
    return pl.pallas_call(
        body, name=name, grid=(n_rows // tm,), in_specs=rs + cs, out_specs=out_specs,
        out_shape=list(row_outs) + list(acc_outs), compiler_params=_cparams(sem),
    )(*ra, *ca)


def _ffn_up(x, wg, wu):
    t, _ = x.shape
    s_n, _, fs = _lshape(wg)
    tm = _tile(t, ROW_TILE)

    def body(x_ref, wg_ref, wu_ref, g_ref, u_ref, h_ref):
        xb = x_ref[...].astype(BF)
        for s in range(s_n):
            g = _dot(xb, wg_ref[s])
            u = _dot(xb, wu_ref[s])
            g_ref[s] = g.astype(BF)
            u_ref[s] = u.astype(BF)
            h_ref[s] = (g * _sigmoid(g) * u).astype(BF)

    out = _sds((s_n, t, fs), BF)
    return _rowcall("ffn_up", body, t, tm, [x], [wg, wu], [out, out, out])


def _proj_ln(name, a_list, w, terms, xres, gain, bias, scale, alpha):
    t, d = xres.shape
    tm = _tile(t, ROW_TILE)
    na = len(a_list)

    def body(*refs):
        a_refs = refs[:na]
        x_ref, w_ref, g_ref, bb_ref = refs[na:na + 4]
        y_ref, xh_ref, rs_ref = refs[na + 4:]
        acc = None
        for i, s, cols, sw in terms:
            a = a_refs[i][s] if cols is None else a_refs[i][:, cols[0]:cols[1]]
            p = _dot(a.astype(BF), w_ref[sw])
            acc = p if acc is None else acc + p
        if scale != 1.0:
            acc = scale * acc
        r = alpha * x_ref[...] + acc
        mu = jnp.mean(r, axis=-1, keepdims=True)
        xc = r - mu
        var = jnp.mean(xc * xc, axis=-1, keepdims=True)
        rstd = lax.rsqrt(var + LN_EPS)
        xh = xc * rstd
        y_ref[...] = xh * g_ref[...] + bb_ref[...]
        xh_ref[...] = xh
        rs_ref[...] = rstd

    outs = [_sds((t, d), F32), _sds((t, d), F32), _sds((t, 1), F32)]
    return _rowcall(name, body, t, tm, list(a_list) + [xres], [w, gain, bias], outs)


def _shard_cols(groups, dg, s, width):
    pieces, lo = [], s * width
    while lo < (s + 1) * width:
        gi = lo // dg
        hi = min((gi + 1) * dg, (s + 1) * width)
        pieces.append((gi, lo - gi * dg, hi - gi * dg))
        lo = hi
    assert all(p[0] < groups for p in pieces)
    return pieces


def _qkv_proj(x, win):
    t, _ = x.shape
    s_n, _, wd_ = _lshape(win)
    tm = _tile(t, ROW_TILE)

    def body(x_ref, w_ref, o_ref):
        xb = x_ref[...].astype(BF)
        for s in range(s_n):
            o_ref[:, s * wd_:(s + 1) * wd_] = _dot(xb, w_ref[s]).astype(BF)

    return _rowcall("qkv_proj", body, t, tm, [x], [win], [_sds((t, s_n * wd_), BF)])[0]


def _dx_call(name, rows, res, matmul, resid, alpha, ln):
    t, d = resid.shape
    tm = _tile(t, ROW_TILE)
    nr, nc = len(rows), len(res)

    def body(*refs):
        r_refs, resid_ref = refs[:nr], refs[nr]
        if ln is None:
            c_refs, outs = refs[nr + 1:nr + 1 + nc], refs[nr + 1 + nc:]
            outs[0][...] = matmul(r_refs, c_refs) + alpha * resid_ref[...]
        else:
            xh_ref, rs_ref = refs[nr + 1:nr + 3]
            c_refs, g_ref = refs[nr + 3:nr + 3 + nc], refs[nr + 3 + nc]
            outs = refs[nr + 4 + nc:]
            dx = matmul(r_refs, c_refs) + alpha * resid_ref[...]
            _ln_bwd_tail(pl.program_id(0), dx, xh_ref, rs_ref, g_ref, ln[3], *outs)

    if ln is None:
        return _rowcall(name, body, t, tm, list(rows) + [resid], list(res), [_sds((t, d), F32)])[0]
    outs = [_sds((t, d), F32), _sds((t, d), BF)]
    accs = [_sds((8, d), F32), _sds((8, d), F32)]
    return _rowcall(name, body, t, tm, list(rows) + [resid, ln[0], ln[1]], list(res) + [ln[2]], outs, accs)


def _ffn_dx(dg_, du_, wg, wu, resid, alpha, ln):
    s_n = dg_.shape[0]

    def matmul(r, c):
        acc = None
        for i in range(2):
            for s in range(s_n):
                p = _dot_nt(r[i][s], c[i][s])
                acc = p if acc is None else acc + p
        return acc

    return _dx_call("ffn_dx", [dg_, du_], [wg, wu], matmul, resid, alpha, ln)


def _attn_dx(dqkv, win, resid, alpha, ln):
    s_n, _, wd_ = _lshape(win)
    dg = dqkv[0].shape[1]
    pieces = [_shard_cols(len(dqkv), dg, s, wd_) for s in range(s_n)]

    def matmul(r, c):
        acc = None
        for s in range(s_n):
            a = jnp.concatenate([r[gi][:, lo:hi] for gi, lo, hi in pieces[s]], axis=1)
            p = _dot_nt(a, c[0][s])
            acc = p if acc is None else acc + p
        return acc

    return _dx_call("attn_dx", list(dqkv), [win], matmul, resid, alpha, ln)


def _head_rms_bwd(dm, ov, gain, lo):
    ss = ov * ov
    s0 = jnp.sum(jnp.where(lo, ss, 0.0), axis=1, keepdims=True)
    s1 = jnp.sum(jnp.where(lo, 0.0, ss), axis=1, keepdims=True)
    r = jnp.where(lo, lax.rsqrt(s0 / HEAD_DIM + RMS_EPS), lax.rsqrt(s1 / HEAD_DIM + RMS_EPS))
    n = ov * r
    dn = dm * gain
    tt = dn * n
    t0 = jnp.sum(jnp.where(lo, tt, 0.0), axis=1, keepdims=True) / HEAD_DIM
    t1 = jnp.sum(jnp.where(lo, 0.0, tt), axis=1, keepdims=True) / HEAD_DIM
    return r * (dn - n * jnp.where(lo, t0, t1)), dm * n


def _attn_dmix(drs, wout, o_sb, o_ca, g_sb, g_ca):
    t, dg = o_sb.shape
    s_n, qd, _ = _lshape(wout)
    tm = _tile(t, ROW_TILE)
    per = dg // qd

    def body(d_ref, osb_ref, oca_ref, w_ref, gs_ref, gc_ref, dsb_ref, dca_ref, pgs_ref, pgc_ref):
        m = pl.program_id(0)
        dv = d_ref[...]
        lo = lax.broadcasted_iota(jnp.int32, (1, LANES), 1) < HEAD_DIM
        for grp, (o_ref, g_ref, do_ref, pg_ref) in enumerate(((osb_ref, gs_ref, dsb_ref, pgs_ref),
                                                               (oca_ref, gc_ref, dca_ref, pgc_ref))):
            dm = jnp.concatenate([_dot_nt(dv, w_ref[grp * per + k]) for k in range(per)], axis=1)
            parts = []
            for gi in range(dg // LANES):
                sl = slice(gi * LANES, (gi + 1) * LANES)
                do, pg = _head_rms_bwd(dm[:, sl], o_ref[:, sl], g_ref[:, sl], lo)
                do_ref[:, sl] = do.astype(BF)
                parts.append(_rows8(pg))
            _accumulate(m, pg_ref, jnp.concatenate(parts, axis=1))

    outs = [_sds((t, dg), BF), _sds((t, dg), BF)]
    accs = [_sds((8, dg), F32), _sds((8, dg), F32)]
    return _rowcall("attn_dmix", body, t, tm, [drs, o_sb, o_ca], [wout, g_sb, g_ca], outs, accs)


def _swiglu_bwd(drs, wd, g, u):
    s_n, t, fs = g.shape
    tm = _tile(t, ROW_TILE)

    def body(d_ref, g_ref, u_ref, wd_ref, dg_ref, du_ref):
        dv = d_ref[...]
        rc = _tile(tm, 128)
        for s in range(s_n):
            dh_all = _dot_nt(dv, wd_ref[s])
            for r0 in range(0, tm, rc):
                dh = dh_all[r0:r0 + rc]
                gv = g_ref[s, r0:r0 + rc, :].astype(F32)
                uv = u_ref[s, r0:r0 + rc, :].astype(F32)
                sg = _sigmoid(gv)
                gs = gv * sg
                dg_ref[s, r0:r0 + rc, :] = (dh * uv * (sg + gs * (1.0 - sg))).astype(BF)
                du_ref[s, r0:r0 + rc, :] = (dh * gs).astype(BF)

    out = _sds((s_n, t, fs), BF)
    return _rowcall("swiglu_bwd", body, t, tm, [drs, g, u], [wd], [out, out])


def _loss_grad(y, target, xh, rstd, gain, scale):
    t, d = y.shape
    tm = _tile(t, ROW_TILE)

    def body(y_ref, t_ref, xh_ref, rs_ref, g_ref, dr_ref, drs_ref, dg_ref, db_ref, sq_ref):
        m = pl.program_id(0)
        diff = y_ref[...] - t_ref[...]
        _ln_bwd_tail(m, diff / d, xh_ref, rs_ref, g_ref, scale, dr_ref, drs_ref, dg_ref, db_ref)
        _accumulate(m, sq_ref, _rows8(diff * diff))

    outs = [_sds((t, d), F32), _sds((t, d), BF)]
    accs = [_sds((8, d), F32)] * 3
    return _rowcall("loss_grad", body, t, tm, [y, target, xh, rstd], [gain], outs, accs)


def _adamw(w, g, m, v):
    shape = w.shape
    c = shape[-1]
    rows = math.prod(shape[:-1])
    tm = _tile(rows, 512)

    def body(w_ref, g_ref, m_ref, v_ref, go_ref, d_ref, nm_ref, nv_ref):
        gv = g_ref[...]
        go_ref[...] = gv
        nm = ADAM_B1 * m_ref[...] + (1.0 - ADAM_B1) * gv
        nv = ADAM_B2 * v_ref[...] + (1.0 - ADAM_B2) * (gv * gv)
        m_hat = nm / (1.0 - ADAM_B1 ** ADAM_STEP)
        v_hat = nv / (1.0 - ADAM_B2 ** ADAM_STEP)
        d_ref[...] = -ADAM_LR * (m_hat / (jnp.sqrt(v_hat) + ADAM_EPS) + ADAM_WD * w_ref[...])
        nm_ref[...] = nm
        nv_ref[...] = nv

    flat = [a.reshape(rows, c) for a in (w, g, m, v)]
    out = _sds((rows, c), F32)
    res = _rowcall("adamw", body, rows, tm, flat, [], [out, out, out, out])
    return tuple(r.reshape(shape) for r in res)


def _wgrad(name, a_list, b_list, out_sub, sub_shape, a_get, b_get):
    t = a_list[0].shape[-2]
    tk = _tile(t, 1024)
    na, nb = len(a_list), len(b_list)
    nk = t // tk

    def tspec(shape):
        if len(shape) == 2:
            return pl.BlockSpec((tk, shape[1]), lambda k: (k, 0))
        return pl.BlockSpec((shape[0], tk, shape[2]), lambda k: (0, k, 0))

    def body(*refs):
        a_refs = refs[:na]
        b_refs = refs[na:na + nb]
        o_ref = refs[na + nb]
        acc_ref = refs[na + nb + 1]
        k = pl.program_id(0)

        @pl.when(k == 0)
        def _():
            acc_ref[...] = jnp.zeros_like(acc_ref)

        for s in range(out_sub):
            acc_ref[s] += _dot_tn(a_get(a_refs, s).astype(BF), b_get(b_refs, s).astype(BF))

        @pl.when(k == nk - 1)
        def _():
            o_ref[...] = acc_ref[...].astype(BF)

    full = (out_sub,) + tuple(sub_shape)
    return pl.pallas_call(
        body, name=name, grid=(nk,), in_specs=[tspec(a.shape) for a in a_list] + [tspec(b.shape) for b in b_list],
        out_specs=pl.BlockSpec(full, lambda k: (0, 0, 0)), out_shape=_sds(full, BF),
        scratch_shapes=[pltpu.VMEM(full, F32)], compiler_params=_cparams(("arbitrary",)),
    )(*a_list, *b_list)


def _head_masks(x):
    lo = lax.broadcasted_iota(jnp.int32, (1, LANES), 1) < HEAD_DIM
    zero = jnp.zeros_like(x)
    return lo, (jnp.where(lo, x, zero), jnp.where(lo, zero, x))


def _logistic_parts(z):
    nz = -z
    t = jnp.exp(jnp.minimum(z, nz))
    one_t = 1.0 + t
    return jnp.minimum(nz, 0.0) - jnp.log(one_t), t, one_t


def _split2(x):
    hi = x.astype(BF)
    lo = (x - hi.astype(F32)).astype(BF)
    return jnp.concatenate([hi, lo], axis=1)


def _tri2(cond):
    tri = jnp.where(cond, 1.0, 0.0).astype(BF)
    return jnp.concatenate([tri, tri], axis=0)


def _sb_specs(bl, seq, p_n, col0, tq):
    nq = seq // tq
    q_spec = pl.BlockSpec((tq, LANES), lambda b, p, i: (b * nq + i, col0 + p))
    k_spec = pl.BlockSpec((seq, LANES), lambda b, p, i: (b, col0 + p_n + p))
    v_spec = pl.BlockSpec((seq, LANES), lambda b, p, i: (b, col0 + 2 * p_n + p))
    tile_spec = pl.BlockSpec((tq, LANES), lambda b, p, i: (b * nq + i, p))
    seq_spec = pl.BlockSpec((seq, LANES), lambda b, p, i: (b, p))
    g_spec = pl.BlockSpec((1, LANES), lambda b, p, i: (0, p))
    return nq, q_spec, k_spec, v_spec, tile_spec, seq_spec, g_spec


def _rms_out(acc, lo, gain):
    ss = acc * acc
    s0 = jnp.sum(jnp.where(lo, ss, 0.0), axis=1, keepdims=True)
    s1 = jnp.sum(jnp.where(lo, 0.0, ss), axis=1, keepdims=True)
    r = jnp.where(lo, lax.rsqrt(s0 / HEAD_DIM + RMS_EPS), lax.rsqrt(s1 / HEAD_DIM + RMS_EPS))
    return acc * r * gain


def _sb_fwd(qkv, bl, seq, dg, gain, bufs=(), layer=None):
    t = bl * seq
    p_n = dg // LANES
    tq, tk = min(SB_TQ, seq), SB_TK
    ratio = tq // tk
    scale = 1.0 / math.sqrt(HEAD_DIM)
    nq, q_spec, k_spec, v_spec, tile_spec, _, g_spec = _sb_specs(bl, seq, p_n, 0, tq)
    assert seq // tk <= LANES
    nt = len(bufs)

    def body(q_ref, k_ref, v_ref, g_ref, *rest):
        o_ref, m_ref, c_ref = rest[nt:nt + 3]
        i = pl.program_id(2)
        if nt:
            start, finish = _gather_plan(rest[nt + 3:2 * nt + 3], [b.shape[2] for b in bufs], rest[2 * nt + 3],
                                         rest[2 * nt + 4], layer)
            grid_step = (pl.program_id(0) * p_n + pl.program_id(1)) * nq + i
            pl.when(grid_step == 0)(start)
        row = lax.broadcasted_iota(jnp.int32, (tq, tk), 0)
        col = lax.broadcasted_iota(jnp.int32, (tq, tk), 1)
        krow = lax.broadcasted_iota(jnp.int32, (tk, tk), 0)
        kcol = lax.broadcasted_iota(jnp.int32, (tk, tk), 1)
        tri2 = _tri2(krow > kcol)
        lo, qh = _head_masks(q_ref[...] * jnp.asarray(scale, BF))
        lane = lax.broadcasted_iota(jnp.int32, (1, LANES), 1)

        def step(j, carry, diag):
            acc, c0, c1, s0, s1 = carry
            cars, seen = (c0, c1), (s0, s1)
            mask = None if diag is None else row > col + diag
            off = pl.multiple_of(j * tk, tk)
            kj = k_ref[pl.ds(off, tk), :]
            vj = v_ref[pl.ds(off, tk), :]
            zs = [_dot_nt(qh[h], kj) for h in range(2)]
            lks = [_logistic_parts(z)[0] for z in zs]
            if mask is not None:
                lks = [jnp.where(mask, lk, 0.0) for lk in lks]
            sins = [_dot(_split2(lk), tri2) for lk in lks]
            ws = []
            for h in range(2):
                w = jnp.exp((lks[h] + zs[h]) + (sins[h] + cars[h]))
                ws.append((w if mask is None else jnp.where(mask, w, 0.0)).astype(BF))
            pvs = [_dot(w, vj) for w in ws]
            new_seen = [jnp.where(lane == j, cars[h], seen[h]) for h in range(2)]
            new_cars = [cars[h] + jnp.sum(lks[h], axis=1, keepdims=True) for h in range(2)]
            return acc + jnp.where(lo, pvs[0], pvs[1]), new_cars[0], new_cars[1], new_seen[0], new_seen[1]

        def corner_step(j):
            top = tq - tk
            mask = krow > kcol
            kj = k_ref[pl.ds(pl.multiple_of(j * tk, tk), tk), :]
            vj = v_ref[pl.ds(pl.multiple_of(j * tk, tk), tk), :]
            zs = [_dot_nt(qh[h][top:], kj) for h in range(2)]
            lks = [jnp.where(mask, _logistic_parts(z)[0], 0.0) for z in zs]
            sins = [_dot(_split2(lk), tri2) for lk in lks]
            ws = [jnp.where(mask, jnp.exp((lks[h] + zs[h]) + sins[h]), 0.0).astype(BF) for h in range(2)]
            pvs = [_dot(w, vj) for w in ws]
            pad = lambda low: jnp.concatenate([jnp.zeros((top,) + low.shape[1:], F32), low], axis=0)
            cars = [pad(jnp.sum(lk, axis=1, keepdims=True)) for lk in lks]
            seen = jnp.where(lane == j, 0.0, jnp.full((tq, LANES), SB_UNSEEN, F32))
            return pad(jnp.where(lo, pvs[0], pvs[1])), cars[0], cars[1], seen, seen

        if ratio == 2:
            carry = step(ratio * i, corner_step(ratio * i + 1), 0)
        else:
            zc = jnp.zeros((tq, 1), F32)
            unseen = jnp.full((tq, LANES), SB_UNSEEN, F32)
            carry = (jnp.zeros((tq, LANES), F32), zc, zc, unseen, unseen)
            for m in reversed(range(ratio)):
                carry = step(ratio * i + m, carry, m * tk)

        def alive(state):
            j, c = state
            return jnp.logical_and(j >= 0, jnp.max(jnp.maximum(c[1], c[2])) >= SB_DEAD)

        _, carry = lax.while_loop(alive, lambda st: (st[0] - 1, step(st[0], st[1], None)), (ratio * i - 1, carry))
        acc = carry[0]
        o_ref[...] = acc
        m_ref[...] = _rms_out(acc, lo, g_ref[...]).astype(BF)
        c_ref[:, :LANES] = carry[3]
        c_ref[:, LANES:] = carry[4]
        if nt:
            pl.when(grid_step == bl * p_n * nq - 1)(finish)

    car_spec = pl.BlockSpec((tq, 2 * LANES), lambda b, p, i: (b * nq + i, p))
    outs = [_sds((t, dg), F32), _sds((t, dg), BF), _sds((t, 2 * dg), F32)]
    if not nt:
        return pl.pallas_call(
            body, name="sb_fwd", grid=(bl, p_n, nq), in_specs=[q_spec, k_spec, v_spec, g_spec],
            out_specs=[tile_spec, tile_spec, car_spec], out_shape=outs,
            compiler_params=_cparams(("parallel", "parallel", "parallel")),
        )(qkv, qkv, qkv, gain)
    res = pl.pallas_call(
        body, name="sb_fwd_gather", grid=(bl, p_n, nq), in_specs=[q_spec, k_spec, v_spec, g_spec] + [_ANY] * nt,
        out_specs=[tile_spec, tile_spec, car_spec] + [_ANY] * nt,
        out_shape=outs + [_sds(b.shape, b.dtype) for b in bufs],
        input_output_aliases={4 + k: 3 + k for k in range(nt)},
        scratch_shapes=[pltpu.SemaphoreType.DMA((6 * nt,)), pltpu.SemaphoreType.DMA((6 * nt,))],
        compiler_params=_cparams(("arbitrary", "arbitrary", "arbitrary")),
    )(qkv, qkv, qkv, gain, *bufs)
    return res[0], res[1], res[2], list(res[3:])


def _sb_bwd(qkv, cars, do, bl, seq, dg):
    t = bl * seq
    p_n = dg // LANES
    tq, tk = min(SB_TQ, seq), SB_TK
    ratio = tq // tk
    scale = 1.0 / math.sqrt(HEAD_DIM)
    nq, q_spec, k_spec, v_spec, tile_spec, seq_spec, _ = _sb_specs(bl, seq, p_n, 0, tq)
    car_spec = pl.BlockSpec((tq, 2 * LANES), lambda b, p, i: (b * nq + i, p))

    def body(q_ref, k_ref, v_ref, c_ref, do_ref, dq_ref, dk_ref, dv_ref, dk_acc, dv_acc):
        i = pl.program_id(2)

        @pl.when(i == 0)
        def _():
            dk_acc[...] = jnp.zeros_like(dk_acc)
            dv_acc[...] = jnp.zeros_like(dv_acc)

        row = lax.broadcasted_iota(jnp.int32, (tq, tk), 0)
        col = lax.broadcasted_iota(jnp.int32, (tq, tk), 1)
        krow = lax.broadcasted_iota(jnp.int32, (tk, tk), 0)
        kcol = lax.broadcasted_iota(jnp.int32, (tk, tk), 1)
        tri2 = _tri2(krow > kcol)
        tri_pre2 = _tri2(krow < kcol)
        lane = lax.broadcasted_iota(jnp.int32, (1, LANES), 1)
        sc = jnp.asarray(scale, BF)
        qs = q_ref[...] * sc
        dob = do_ref[...]
        lo, qh = _head_masks(qs)
        _, doh = _head_masks(dob)
        seen = (c_ref[:, :LANES], c_ref[:, LANES:])

        def step(j, carry, diag):
            dq, e0, e1 = carry
            ecars = (e0, e1)
            mask = None if diag is None else row > col + diag
            off = pl.multiple_of(j * tk, tk)
            kj = k_ref[pl.ds(off, tk), :]
            vj = v_ref[pl.ds(off, tk), :]
            zs = [_dot_nt(qh[h], kj) for h in range(2)]
            dws = [_dot_nt(doh[h], vj) for h in range(2)]
            parts = [_logistic_parts(z) for z in zs]
            lks = [p[0] if mask is None else jnp.where(mask, p[0], 0.0) for p in parts]
            sins = [_dot(_split2(lk), tri2) for lk in lks]
            ws, es = [], []
            for h in range(2):
                car = jnp.sum(jnp.where(lane == j, seen[h], 0.0), axis=1, keepdims=True)
                w = jnp.exp((lks[h] + zs[h]) + (sins[h] + car))
                w = w if mask is None else jnp.where(mask, w, 0.0)
                ws.append(w.astype(BF))
                es.append(dws[h] * w)
            pres = [_dot(_split2(e), tri_pre2) for e in es]
            dzs = []
            for h in range(2):
                t_, e = parts[h][1], es[h]
                before = pres[h] + ecars[h]
                dz = pl.reciprocal(parts[h][2], approx=True) * jnp.where(zs[h] > 0.0, e * t_ - before, e - t_ * before)
                dzs.append((dz if mask is None else jnp.where(mask, dz, 0.0)).astype(BF))
            kjs = kj * sc
            dqs = [_dot(dz, kjs) for dz in dzs]
            dks = [_dot_tn(dz, qs) for dz in dzs]
            dvs = [_dot_tn(w, dob) for w in ws]
            dk_acc[pl.ds(off, tk), :] += jnp.where(lo, dks[0], dks[1])
            dv_acc[pl.ds(off, tk), :] += jnp.where(lo, dvs[0], dvs[1])
            new_e = [ecars[h] + jnp.sum(es[h], axis=1, keepdims=True) for h in range(2)]
            return dq + jnp.where(lo, dqs[0], dqs[1]), new_e[0], new_e[1]

        reach = jnp.max(jnp.maximum(seen[0], seen[1]), axis=0, keepdims=True)
        skipped = jnp.logical_and(reach < SB_DEAD, lane < ratio * i)
        first = jnp.minimum(jnp.sum(jnp.where(skipped, 1, 0)).astype(jnp.int32), ratio * i)
        zc = jnp.zeros((tq, 1), F32)
        def corner_step(j, carry):
            dq, e0, e1 = carry
            top = tq - tk
            mask = krow > kcol
            off = pl.multiple_of(j * tk, tk)
            kj = k_ref[pl.ds(off, tk), :]
            vj = v_ref[pl.ds(off, tk), :]
            zs = [_dot_nt(qh[h][top:], kj) for h in range(2)]
            dws = [_dot_nt(doh[h][top:], vj) for h in range(2)]
            parts = [_logistic_parts(z) for z in zs]
            lks = [jnp.where(mask, p[0], 0.0) for p in parts]
            sins = [_dot(_split2(lk), tri2) for lk in lks]
            ws, es = [], []
            for h in range(2):
                car = jnp.sum(jnp.where(lane == j, seen[h][top:], 0.0), axis=1, keepdims=True)
                w = jnp.where(mask, jnp.exp((lks[h] + zs[h]) + (sins[h] + car)), 0.0)
                ws.append(w.astype(BF))
                es.append(dws[h] * w)
            pres = [_dot(_split2(e), tri_pre2) for e in es]
            dzs = []
            for h, ecar in enumerate((e0, e1)):
                t_, e = parts[h][1], es[h]
                before = pres[h] + ecar[top:]
                dz = pl.reciprocal(parts[h][2], approx=True) * jnp.where(zs[h] > 0.0, e * t_ - before, e - t_ * before)
                dzs.append(jnp.where(mask, dz, 0.0).astype(BF))
            kjs = kj * sc
            dqs = [_dot(dz, kjs) for dz in dzs]
            dks = [_dot_tn(dz, qs[top:]) for dz in dzs]
            dvs = [_dot_tn(w, dob[top:]) for w in ws]
            dk_acc[pl.ds(off, tk), :] += jnp.where(lo, dks[0], dks[1])
            dv_acc[pl.ds(off, tk), :] += jnp.where(lo, dvs[0], dvs[1])
            return dq + jnp.concatenate([jnp.zeros((top, LANES), F32), jnp.where(lo, dqs[0], dqs[1])], axis=0)

        carry = lax.fori_loop(first, ratio * i, lambda j, c: step(j, c, None), (jnp.zeros((tq, LANES), F32), zc, zc))
        if ratio == 2:
            dq = corner_step(ratio * i + 1, step(ratio * i, carry, 0))
        else:
            for m in range(ratio):
                carry = step(ratio * i + m, carry, m * tk)
            dq = carry[0]
        dq_ref[...] = dq.astype(BF)

        @pl.when(i == nq - 1)
        def _():
            dk_ref[...] = dk_acc[...].astype(BF)
            dv_ref[...] = dv_acc[...].astype(BF)

    out = _sds((t, dg), BF)
    return pl.pallas_call(
        body, name="sb_bwd", grid=(bl, p_n, nq),
        in_specs=[q_spec, k_spec, v_spec, car_spec, tile_spec],
        out_specs=[tile_spec, seq_spec, seq_spec], out_shape=[out, out, out],
        scratch_shapes=[pltpu.VMEM((seq, LANES), F32), pltpu.VMEM((seq, LANES), F32)],
        compiler_params=_cparams(("parallel", "parallel", "arbitrary")),
    )(qkv, qkv, qkv, cars, do)


def _bias_expand(rb):
    h = rb.shape[0]
    pad = CHUNK * N_PREV_CHUNKS
    n_f = BAND + CHUNK - 1
    f = jnp.concatenate([rb[:, MAX_REL - (CHUNK - 1):2 * MAX_REL],
                         jnp.broadcast_to(rb[:, 2 * MAX_REL:], (h, pad + CHUNK - MAX_REL))], axis=1)
    fr = f[:, ::-1]
    skew = jnp.pad(jnp.broadcast_to(fr[:, None, :], (h, CHUNK, n_f)), ((0, 0), (0, 0), (0, 1)))
    skew = skew.reshape(h, CHUNK * (n_f + 1))[:, :CHUNK * n_f].reshape(h, CHUNK, n_f)
    table = skew[:, :, CHUNK - 1:CHUNK - 1 + BAND]
    n_c = ATT_TILE // CHUNK
    width = 3 * ATT_TILE
    rows = [jnp.pad(table, ((0, 0), (0, 0), (c * CHUNK, width - BAND - c * CHUNK)), constant_values=NEG_BIAS)
            for c in range(n_c)]
    return jnp.concatenate(rows, axis=1)


def _ca_scores(qh_h, k_ref, bias_ref, h, i, scale):
    tq = ATT_TILE
    ss, offs = [], []
    for kb in range(3):
        jb = i - 2 + kb
        off = pl.multiple_of(jnp.maximum(jb, 0) * tq, tq)
        s = _dot_nt(qh_h, k_ref[pl.ds(off, tq), :]) * scale + bias_ref[h, :, kb * tq:(kb + 1) * tq]
        ss.append(jnp.where(jb >= 0, s, NEG_BIAS))
        offs.append(off)
    m = jnp.max(jnp.maximum(jnp.maximum(ss[0], ss[1]), ss[2]), axis=1, keepdims=True)
    ps = [jnp.exp(s - m) for s in ss]
    den = jnp.sum(ps[0] + ps[1] + ps[2], axis=1, keepdims=True)
    inv = 1.0 / den
    return [p * inv for p in ps], offs


def _ca_fwd(qkv, bias4, bl, seq, dg, gain, bufs=(), layer=None):
    t = bl * seq
    p_n = dg // LANES
    tq = ATT_TILE
    scale = 1.0 / math.sqrt(HEAD_DIM)
    nq, q_spec, k_spec, v_spec, tile_spec, _, g_spec = _sb_specs(bl, seq, p_n, 3 * p_n, tq)
    b_spec = pl.BlockSpec((2, tq, 3 * tq), lambda b, p, i: (p, 0, 0))
    nt = len(bufs)

    def body(q_ref, k_ref, v_ref, bias_ref, g_ref, *rest):
        o_ref, m_ref = rest[nt:nt + 2]
        i = pl.program_id(2)
        if nt:
            start, finish = _gather_plan(rest[nt + 2:2 * nt + 2], [b.shape[2] for b in bufs], rest[2 * nt + 2],
                                         rest[2 * nt + 3], layer)
            step = (pl.program_id(0) * p_n + pl.program_id(1)) * nq + i
            pl.when(step == 0)(start)
        lo, qh = _head_masks(q_ref[...])
        outs = []
        for h in range(2):
            ps, offs = _ca_scores(qh[h], k_ref, bias_ref, h, i, scale)
            acc = None
            for kb in range(3):
                pv = _dot(ps[kb].astype(BF), v_ref[pl.ds(offs[kb], tq), :])
                acc = pv if acc is None else acc + pv
            outs.append(acc)
        acc = jnp.where(lo, outs[0], outs[1])
        o_ref[...] = acc
        m_ref[...] = _rms_out(acc, lo, g_ref[...]).astype(BF)
        if nt:
            pl.when(step == bl * p_n * nq - 1)(finish)

    if not nt:
        return pl.pallas_call(
            body, name="ca_fwd", grid=(bl, p_n, nq), in_specs=[q_spec, k_spec, v_spec, b_spec, g_spec],
            out_specs=[tile_spec, tile_spec], out_shape=[_sds((t, dg), F32), _sds((t, dg), BF)],
            compiler_params=_cparams(("parallel", "parallel", "parallel")),
        )(qkv, qkv, qkv, bias4, gain)
    res = pl.pallas_call(
        body, name="ca_fwd_gather", grid=(bl, p_n, nq),
        in_specs=[q_spec, k_spec, v_spec, b_spec, g_spec] + [_ANY] * nt,
        out_specs=[tile_spec, tile_spec] + [_ANY] * nt,
        out_shape=[_sds((t, dg), F32), _sds((t, dg), BF)] + [_sds(b.shape, b.dtype) for b in bufs],
        input_output_aliases={5 + k: 2 + k for k in range(nt)},
        scratch_shapes=[pltpu.SemaphoreType.DMA((6 * nt,)), pltpu.SemaphoreType.DMA((6 * nt,))],
        compiler_params=_cparams(("arbitrary", "arbitrary", "arbitrary")),
    )(qkv, qkv, qkv, bias4, gain, *bufs)
    return res[0], res[1], list(res[2:])


def _ca_bwd(qkv, bias4, o, do, bl, seq, dg, psums=()):
    t = bl * seq
    p_n = dg // LANES
    tq = ATT_TILE
    scale = 1.0 / math.sqrt(HEAD_DIM)
    nq = seq // tq
    c0 = 3 * p_n
    q_spec = pl.BlockSpec((tq, LANES), lambda p, b, i: (b * nq + i, c0 + p))
    k_spec = pl.BlockSpec((seq, LANES), lambda p, b, i: (b, c0 + p_n + p))
    v_spec = pl.BlockSpec((seq, LANES), lambda p, b, i: (b, c0 + 2 * p_n + p))
    tile_spec = pl.BlockSpec((tq, LANES), lambda p, b, i: (b * nq + i, p))
    seq_spec = pl.BlockSpec((seq, LANES), lambda p, b, i: (b, p))
    b_spec = pl.BlockSpec((2, tq, 3 * tq), lambda p, b, i: (p, 0, 0))

    nt = len(psums)

    def body(q_ref, k_ref, v_ref, bias_ref, o_ref, do_ref, *rest):
        dq_ref, dk_ref, dv_ref, db_ref = rest[nt:nt + 4]
        dk_acc, dv_acc = rest[2 * nt + 4:2 * nt + 6]
        b = pl.program_id(1)
        i = pl.program_id(2)
        if nt:
            start, finish = _scatter_plan(rest[:nt], rest[nt + 4:2 * nt + 4], rest[2 * nt + 6], rest[2 * nt + 7])
            step = (pl.program_id(0) * bl + b) * nq + i
            pl.when(step == 0)(start)

        @pl.when(i == 0)
        def _():
            dk_acc[...] = jnp.zeros_like(dk_acc)
            dv_acc[...] = jnp.zeros_like(dv_acc)

        @pl.when((i == 0) & (b == 0))
        def _():
            db_ref[...] = jnp.zeros_like(db_ref)

        q = q_ref[...]
        dob = do_ref[...]
        lo, qh = _head_masks(q)
        _, doh = _head_masks(dob)
        prod = dob.astype(F32) * o_ref[...]
        delta = (jnp.sum(jnp.where(lo, prod, 0.0), axis=1, keepdims=True),
                 jnp.sum(jnp.where(lo, 0.0, prod), axis=1, keepdims=True))
        dqs = []
        for h in range(2):
            ps, offs = _ca_scores(qh[h], k_ref, bias_ref, h, i, scale)
            dq = None
            dks, dvs = [], []
            for kb in range(3):
                dp = _dot_nt(doh[h], v_ref[pl.ds(offs[kb], tq), :])
                ds = ps[kb] * (dp - delta[h])
                db_ref[h, :, kb * tq:(kb + 1) * tq] += ds
                dsb = (ds * scale).astype(BF)
                part = _dot(dsb, k_ref[pl.ds(offs[kb], tq), :])
                dq = part if dq is None else dq + part
                dks.append(_dot_tn(dsb, q))
                dvs.append(_dot_tn(ps[kb].astype(BF), dob))
            dqs.append(dq)
            hm = lo if h == 0 else jnp.logical_not(lo)
            for kb in range(3):
                dk_acc[pl.ds(offs[kb], tq), :] += jnp.where(hm, dks[kb], 0.0)
                dv_acc[pl.ds(offs[kb], tq), :] += jnp.where(hm, dvs[kb], 0.0)
        dq_ref[...] = jnp.where(lo, dqs[0], dqs[1]).astype(BF)

        @pl.when(i == nq - 1)
        def _():
            dk_ref[...] = dk_acc[...].astype(BF)
            dv_ref[...] = dv_acc[...].astype(BF)

        if nt:
            pl.when(step == p_n * bl * nq - 1)(finish)

    out = _sds((t, dg), BF)
    acc = [pltpu.VMEM((seq, LANES), F32), pltpu.VMEM((seq, LANES), F32)]
    if not nt:
        return pl.pallas_call(
            body, name="ca_bwd", grid=(p_n, bl, nq),
            in_specs=[q_spec, k_spec, v_spec, b_spec, tile_spec, tile_spec],
            out_specs=[tile_spec, seq_spec, seq_spec, b_spec], out_shape=[out, out, out, _sds(bias4.shape, F32)],
            scratch_shapes=acc, compiler_params=_cparams(("parallel", "arbitrary", "arbitrary")),
        )(qkv, qkv, qkv, bias4, o, do)
    res = pl.pallas_call(
        body, name="ca_bwd_scatter", grid=(p_n, bl, nq),
        in_specs=[q_spec, k_spec, v_spec, b_spec, tile_spec, tile_spec] + [_ANY] * nt,
        out_specs=[tile_spec, seq_spec, seq_spec, b_spec] + [_ANY] * nt,
        out_shape=[out, out, out, _sds(bias4.shape, F32)] + _scatter_shapes(psums),
        scratch_shapes=acc + [pltpu.SemaphoreType.DMA((3 * nt,)), pltpu.SemaphoreType.DMA((3 * nt,))],
        compiler_params=_cparams(("arbitrary", "arbitrary", "arbitrary")),
    )(qkv, qkv, qkv, bias4, o, do, *psums)
    return res[0], res[1], res[2], res[3], list(res[4:])


_ANY = pl.BlockSpec(memory_space=pl.ANY)


def _mesh_pos():
    x, y, c = lax.axis_index("x"), lax.axis_index("y"), lax.axis_index("c")
    chips = [(1 - x, y), (x, 1 - y), (1 - x, 1 - y)]
    return x, y, c, chips


def _half(ref_rows, c):
    return pl.ds(c * (ref_rows // 2), ref_rows // 2)


def _cast_into_slot(w, slot):
    l_n, r, c_n = w.shape
    tr = _tile(r, 512)

    def body(s_ref, w_ref, o_ref):
        o_ref[...] = w_ref[...].astype(BF)

    return pl.pallas_call(
        body, name="cast_into_slot", out_shape=_sds((l_n, 4, r, c_n), BF),
        grid_spec=pltpu.PrefetchScalarGridSpec(
            num_scalar_prefetch=1, grid=(l_n, r // tr),
            in_specs=[pl.BlockSpec((None, tr, c_n), lambda l, m, s: (l, m, 0))],
            out_specs=pl.BlockSpec((None, None, tr, c_n), lambda l, m, s: (l, s[0], m, 0))),
        compiler_params=_cparams(("parallel", "parallel")),
    )(slot, w)


def _gather_plan(outs, n_rows, send_sems, recv_sems, layer):
    nt = len(outs)

    def blk(t, s, cc):
        return outs[t].at[pl.ds(layer, 1), s, _half(n_rows[t], cc)]

    def copy(k, ref, to):
        return pltpu.make_async_remote_copy(src_ref=ref, dst_ref=ref, send_sem=send_sems.at[k],
                                            recv_sem=recv_sems.at[k], device_id=to, device_id_type=MESH_IDS)

    def over_ici():
        x, y, c, chips = _mesh_pos()
        return [copy(j * nt + t, blk(t, 2 * x + y, c), (px, py, c)) for j, (px, py) in enumerate(chips) for t in range(nt)]

    def start():
        for cp in over_ici():
            cp.start()

    def finish():
        x, y, c, chips = _mesh_pos()
        passed = []
        for j, (px, py) in enumerate(chips):
            for t in range(nt):
                got = blk(t, 2 * px + py, c)
                copy(j * nt + t, got, (px, py, c)).wait_recv()
                relay = copy(3 * nt + j * nt + t, got, (x, y, 1 - c))
                relay.start()
                passed.append(relay)
        for j, (px, py) in enumerate(chips):
            for t in range(nt):
                copy(3 * nt + j * nt + t, blk(t, 2 * px + py, 1 - c), (x, y, 1 - c)).wait_recv()
        for cp in over_ici() + passed:
            cp.wait_send()

    return start, finish


def _gather_weights(bufs, layer):
    nt = len(bufs)

    def body(*refs):
        start, finish = _gather_plan(refs[nt:2 * nt], [b.shape[2] for b in bufs], refs[2 * nt], refs[2 * nt + 1], layer)
        start()
        finish()

    return pl.pallas_call(
        body, name="gather_weights", in_specs=[_ANY] * nt, out_specs=[_ANY] * nt,
        out_shape=[_sds(b.shape, b.dtype) for b in bufs], input_output_aliases={t: t for t in range(nt)},
        scratch_shapes=[pltpu.SemaphoreType.DMA((6 * nt,)), pltpu.SemaphoreType.DMA((6 * nt,))],
    )(*bufs)


def _swap_halves(grads):
    nt = len(grads)

    def body(*refs):
        ins, theirs_o = refs[:nt], refs[nt:2 * nt]
        send_sems, recv_sems = refs[2 * nt:]
        x, y, c, _ = _mesh_pos()
        sends = [pltpu.make_async_remote_copy(
            src_ref=ins[t].at[:, :, _half(grads[t].shape[2], 1 - c)], dst_ref=theirs_o[t],
            send_sem=send_sems.at[t], recv_sem=recv_sems.at[t], device_id=(x, y, 1 - c), device_id_type=MESH_IDS)
            for t in range(nt)]
        for cp in sends:
            cp.start()
        for cp in sends:
            cp.wait_recv()
        for cp in sends:
            cp.wait_send()

    half = [_sds(g.shape[:2] + (g.shape[2] // 2, g.shape[3]), g.dtype) for g in grads]
    return pl.pallas_call(
        body, name="swap_halves", in_specs=[_ANY] * nt, out_specs=[_ANY] * nt, out_shape=half,
        scratch_shapes=[pltpu.SemaphoreType.DMA((nt,)), pltpu.SemaphoreType.DMA((nt,))],
    )(*grads)


def _add_pair(mine, theirs):
    nt = len(mine)
    r2s = [th.shape[2] for th in theirs]

    def body(*refs):
        c = lax.axis_index("c")
        for k in range(nt):
            rows = pl.ds(pl.multiple_of(c * r2s[k], 16), r2s[k])
            refs[2 * nt + k][...] = (refs[k][rows, :].astype(F32) + refs[nt + k][...].astype(F32)).astype(BF)

    def spec(shape):
        return pl.BlockSpec((None, None) + tuple(shape[2:]), lambda s: (0, s, 0, 0))

    return pl.pallas_call(
        body, name="add_pair", out_shape=[_sds(th.shape, BF) for th in theirs], grid=(mine[0].shape[1],),
        in_specs=[spec(a.shape) for a in mine] + [spec(b.shape) for b in theirs],
        out_specs=[spec(b.shape) for b in theirs], compiler_params=_cparams(("parallel",)),
    )(*mine, *theirs)


def _scatter_chips(psums):
    nt = len(psums)

    def body(*refs):
        start, finish = _scatter_plan(refs[:nt], refs[nt:2 * nt], refs[2 * nt], refs[2 * nt + 1])
        start()
        finish()

    return pl.pallas_call(
        body, name="scatter_chips", in_specs=[_ANY] * nt, out_specs=[_ANY] * nt, out_shape=_scatter_shapes(psums),
        scratch_shapes=[pltpu.SemaphoreType.DMA((3 * nt,)), pltpu.SemaphoreType.DMA((3 * nt,))],
    )(*psums)


def _scatter_shapes(psums):
    return [_sds((3, p.shape[0]) + p.shape[2:], p.dtype) for p in psums]


def _scatter_plan(ins, got, send_sems, recv_sems):
    nt = len(ins)

    def sends():
        _, _, c, chips = _mesh_pos()
        return [pltpu.make_async_remote_copy(
            src_ref=ins[t].at[:, 2 * px + py], dst_ref=got[t].at[j], send_sem=send_sems.at[j * nt + t],
            recv_sem=recv_sems.at[j * nt + t], device_id=(px, py, c), device_id_type=MESH_IDS)
            for j, (px, py) in enumerate(chips) for t in range(nt)]

    def start():
        for cp in sends():
            cp.start()

    def finish():
        for cp in sends():
            cp.wait_recv()
        for cp in sends():
            cp.wait_send()

    return start, finish


def _add_chips(psums, gots, fulls, layer, depth):
    nt = len(psums)
    r2s = [p.shape[2] for p in psums]
    n_alias = nt if fulls is not None else 0

    def body(*refs):
        c = lax.axis_index("c")
        chip = 2 * lax.axis_index("x") + lax.axis_index("y")
        outs = refs[n_alias + 2 * nt:]
        for k in range(nt):
            acc = refs[n_alias + k][chip].astype(F32)
            for j in range(3):
                acc = acc + refs[n_alias + nt + k][j].astype(F32)
            outs[k][pl.ds(pl.multiple_of(c * r2s[k], 8), r2s[k]), :] = acc
            outs[k][pl.ds(pl.multiple_of((1 - c) * r2s[k], 8), r2s[k]), :] = jnp.zeros_like(acc)

    once = pl.Buffered(1)
    in_specs = [_ANY] * n_alias
    in_specs += [pl.BlockSpec((None,) + p.shape[1:], lambda i: (0, 0, 0, 0), pipeline_mode=once) for p in psums]
    in_specs += [pl.BlockSpec((3, None) + g.shape[2:], lambda i: (0, 0, 0, 0), pipeline_mode=once) for g in gots]
    out_shape = [_sds((depth, 2 * p.shape[2], p.shape[3]), F32) for p in psums]
    out_specs = [pl.BlockSpec((None,) + s.shape[1:], lambda i: (layer, 0, 0), pipeline_mode=once) for s in out_shape]
    return pl.pallas_call(
        body, name="add_chips", out_shape=out_shape, grid=(1,), in_specs=in_specs, out_specs=out_specs,
        input_output_aliases={k: k for k in range(n_alias)}, compiler_params=_cparams(("arbitrary",)),
    )(*(fulls or []), *psums, *gots)


def _join_halves(fulls):
    nt = len(fulls)

    def body(*refs):
        outs = refs[nt:2 * nt]
        send_sems, recv_sems = refs[2 * nt:]
        x, y, c, _ = _mesh_pos()

        def copy(t, cc):
            blk = outs[t].at[:, _half(fulls[t].shape[1], cc)]
            return pltpu.make_async_remote_copy(
                src_ref=blk, dst_ref=blk, send_sem=send_sems.at[t], recv_sem=recv_sems.at[t],
                device_id=(x, y, 1 - c), device_id_type=MESH_IDS)

        sends = [copy(t, c) for t in range(nt)]
        for cp in sends:
            cp.start()
        for t in range(nt):
            copy(t, 1 - c).wait_recv()
        for cp in sends:
            cp.wait_send()

    return pl.pallas_call(
        body, name="join_halves", in_specs=[_ANY] * nt, out_specs=[_ANY] * nt,
        out_shape=[_sds(f.shape, f.dtype) for f in fulls], input_output_aliases={t: t for t in range(nt)},
        scratch_shapes=[pltpu.SemaphoreType.DMA((nt,)), pltpu.SemaphoreType.DMA((nt,))],
    )(*fulls)


def _allreduce_small(part):
    r, d = part.shape
    flips = [(fx, fy, fc) for fx in (0, 1) for fy in (0, 1) for fc in (0, 1) if fx + fy + fc]

    def body(p_ref, o_ref, buf, send_sems, recv_sems):
        x, y, c, _ = _mesh_pos()
        me = 4 * x + 2 * y + c
        buf[me] = p_ref[...]
        sends = []
        for k, (fx, fy, fc) in enumerate(flips):
            sends.append(pltpu.make_async_remote_copy(
                src_ref=p_ref, dst_ref=buf.at[me], send_sem=send_sems.at[k], recv_sem=recv_sems.at[k],
                device_id=(x ^ fx, y ^ fy, c ^ fc), device_id_type=MESH_IDS))
        for cp in sends:
            cp.start()
        for k, (fx, fy, fc) in enumerate(flips):
            peer = 4 * (x ^ fx) + 2 * (y ^ fy) + (c ^ fc)
            pltpu.make_async_remote_copy(
                src_ref=p_ref, dst_ref=buf.at[peer], send_sem=send_sems.at[k], recv_sem=recv_sems.at[k],
                device_id=(x ^ fx, y ^ fy, c ^ fc), device_id_type=MESH_IDS).wait_recv()
        acc = buf[0]
        for dev in range(1, 8):
            acc = acc + buf[dev]
        o_ref[...] = acc
        for cp in sends:
            cp.wait_send()

    vmem = pl.BlockSpec(memory_space=pltpu.VMEM)
    return pl.pallas_call(
        body, name="allreduce_small", in_specs=[vmem], out_specs=vmem, out_shape=_sds((r, d), F32),
        scratch_shapes=[pltpu.VMEM((8, r, d), F32), pltpu.SemaphoreType.DMA((7,)), pltpu.SemaphoreType.DMA((7,))],
    )(part)


BIG = ("ffn1_w_gate", "ffn1_w_up", "ffn1_w_down", "w_in", "w_out", "ffn2_w_gate", "ffn2_w_up", "ffn2_w_down")
SMALL = ("ln1_g", "ln1_b", "rel_bias", "sb_out_g", "ca_out_g", "ln2_g", "ln2_b", "ln3_g", "ln3_b")
ORDER = ("ffn1_w_gate", "ffn1_w_up", "ffn1_w_down", "ln1_g", "ln1_b", "w_in", "rel_bias", "sb_out_g", "ca_out_g",
         "w_out", "ln2_g", "ln2_b", "ffn2_w_gate", "ffn2_w_up", "ffn2_w_down", "ln3_g", "ln3_b")


def _pack_small(vals, depth, d):
    rows = []
    for l in range(depth):
        for n in ("ln1_g", "ln1_b", "ln2_g", "ln2_b", "ln3_g", "ln3_b"):
            rows.append(vals[n][l][None, :])
        rows.append(jnp.concatenate([vals["sb_out_g"][l], vals["ca_out_g"][l]])[None, :])
        rb = vals["rel_bias"][l]
        rows.append(jnp.pad(rb, ((0, 0), (0, d - rb.shape[1]))))
    packed = jnp.concatenate(rows, axis=0)
    pad = (-packed.shape[0]) % 8
    return jnp.pad(packed, ((0, pad), (0, 0)))


def _unpack_small(packed, depth, d, heads, n_rel):
    per = 7 + heads
    out = {n: [] for n in SMALL}
    for l in range(depth):
        base = l * per
        for i, n in enumerate(("ln1_g", "ln1_b", "ln2_g", "ln2_b", "ln3_g", "ln3_b")):
            out[n].append(packed[base + i])
        out["sb_out_g"].append(packed[base + 6, :d // 2])
        out["ca_out_g"].append(packed[base + 6, d // 2:])
        out["rel_bias"].append(packed[base + 7:base + 7 + heads, :n_rel])
    return {n: jnp.stack(v) for n, v in out.items()}


def kernel(x, ffn1_w_gate, ffn1_w_up, ffn1_w_down, ln1_g, ln1_b, w_in, rel_bias, sb_out_g, ca_out_g, w_out, ln2_g, ln2_b, ffn2_w_gate, ffn2_w_up, ffn2_w_down, ln3_g, ln3_b, loss_target, m_ffn1_w_gate, m_ffn1_w_up, m_ffn1_w_down, m_ln1_g, m_ln1_b, m_w_in, m_rel_bias, m_sb_out_g, m_ca_out_g, m_w_out, m_ln2_g, m_ln2_b, m_ffn2_w_gate, m_ffn2_w_up, m_ffn2_w_down, m_ln3_g, m_ln3_b, v_ffn1_w_gate, v_ffn1_w_up, v_ffn1_w_down, v_ln1_g, v_ln1_b, v_w_in, v_rel_bias, v_sb_out_g, v_ca_out_g, v_w_out, v_ln2_g, v_ln2_b, v_ffn2_w_gate, v_ffn2_w_up, v_ffn2_w_down, v_ln3_g, v_ln3_b):
    w = dict(ffn1_w_gate=ffn1_w_gate, ffn1_w_up=ffn1_w_up, ffn1_w_down=ffn1_w_down, ln1_g=ln1_g, ln1_b=ln1_b, w_in=w_in, rel_bias=rel_bias, sb_out_g=sb_out_g, ca_out_g=ca_out_g, w_out=w_out, ln2_g=ln2_g, ln2_b=ln2_b, ffn2_w_gate=ffn2_w_gate, ffn2_w_up=ffn2_w_up, ffn2_w_down=ffn2_w_down, ln3_g=ln3_g, ln3_b=ln3_b)
    mom = dict(ffn1_w_gate=m_ffn1_w_gate, ffn1_w_up=m_ffn1_w_up, ffn1_w_down=m_ffn1_w_down, ln1_g=m_ln1_g, ln1_b=m_ln1_b, w_in=m_w_in, rel_bias=m_rel_bias, sb_out_g=m_sb_out_g, ca_out_g=m_ca_out_g, w_out=m_w_out, ln2_g=m_ln2_g, ln2_b=m_ln2_b, ffn2_w_gate=m_ffn2_w_gate, ffn2_w_up=m_ffn2_w_up, ffn2_w_down=m_ffn2_w_down, ln3_g=m_ln3_g, ln3_b=m_ln3_b)
    var = dict(ffn1_w_gate=v_ffn1_w_gate, ffn1_w_up=v_ffn1_w_up, ffn1_w_down=v_ffn1_w_down, ln1_g=v_ln1_g, ln1_b=v_ln1_b, w_in=v_w_in, rel_bias=v_rel_bias, sb_out_g=v_sb_out_g, ca_out_g=v_ca_out_g, w_out=v_w_out, ln2_g=v_ln2_g, ln2_b=v_ln2_b, ffn2_w_gate=v_ffn2_w_gate, ffn2_w_up=v_ffn2_w_up, ffn2_w_down=v_ffn2_w_down, ln3_g=v_ln3_g, ln3_b=v_ln3_b)

    bl, seq, d = x.shape
    t = bl * seq
    depth = w_in.shape[0]
    dg = d // 2
    heads = dg // HEAD_DIM
    alpha = (2 * depth) ** 0.25
    n_rel = rel_bias.shape[-1]

    chip = (2 * lax.axis_index("x") + lax.axis_index("y")).astype(jnp.int32).reshape(1)
    bufs = list(_gather_weights([_cast_into_slot(w[n], chip) for n in BIG], 0))
    with_sb = [BIG.index(n) for n in ("w_in", "ffn2_w_gate", "ffn2_w_up", "ffn2_w_down")]
    with_ca = [k for k in range(len(BIG)) if k not in with_sb]

    qd = dg // 2
    shards = 4
    ffn_terms = [(0, s, None, s) for s in range(shards)]
    out_terms = [(i, None, (k * qd, (k + 1) * qd), 2 * i + k) for i in range(2) for k in range(2)]
    bias_all, bias_vjp = jax.vjp(jax.vmap(_bias_expand), rel_bias)

    xs = x.reshape(t, d)
    saved = []
    for l in range(depth):
        lw = {n: _LayerOf((b, l)) for n, b in zip(BIG, bufs)}
        row = lambda a: a[l][None, :]
        g1, u1, h1 = _ffn_up(xs, lw["ffn1_w_gate"], lw["ffn1_w_up"])
        x1, xh1, rs1 = _proj_ln("ffn_down_ln", [h1], lw["ffn1_w_down"], ffn_terms, xs, row(ln1_g), row(ln1_b),
                                FFN_RESIDUAL, alpha)
        qkv = _qkv_proj(x1, lw["w_in"])
        if l + 1 < depth:
            o_sb, m_sb, c_sb, new = _sb_fwd(qkv, bl, seq, dg, row(sb_out_g), [bufs[k] for k in with_sb], l + 1)
            for k, b in zip(with_sb, new):
                bufs[k] = b
            o_ca, m_ca, new = _ca_fwd(qkv, bias_all[l], bl, seq, dg, row(ca_out_g), [bufs[k] for k in with_ca], l + 1)
            for k, b in zip(with_ca, new):
                bufs[k] = b
            lw = {n: _LayerOf((b, l)) for n, b in zip(BIG, bufs)}
        else:
            o_sb, m_sb, c_sb = _sb_fwd(qkv, bl, seq, dg, row(sb_out_g))
            o_ca, m_ca = _ca_fwd(qkv, bias_all[l], bl, seq, dg, row(ca_out_g))
        x2, xh2, rs2 = _proj_ln("attn_out_ln", [m_sb, m_ca], lw["w_out"], out_terms, x1, row(ln2_g), row(ln2_b),
                                1.0, alpha)
        g2, u2, h2 = _ffn_up(x2, lw["ffn2_w_gate"], lw["ffn2_w_up"])
        x3, xh3, rs3 = _proj_ln("ffn_down_ln", [h2], lw["ffn2_w_down"], ffn_terms, x2, row(ln3_g), row(ln3_b),
                                FFN_RESIDUAL, alpha)
        saved.append(dict(xin=xs, g1=g1, u1=u1, h1=h1, x1=x1, xh1=xh1, rs1=rs1, qkv=qkv, o_sb=o_sb, m_sb=m_sb,
                          c_sb=c_sb, o_ca=o_ca, m_ca=m_ca, x2=x2, xh2=xh2, rs2=rs2, g2=g2, u2=u2, h2=h2, xh3=xh3,
                          rs3=rs3))
        xs = x3

    last = saved[-1]
    dr3, drs3, pg, pb, sq = _loss_grad(xs, loss_target.reshape(t, d), last["xh3"], last["rs3"],
                                       ln3_g[depth - 1][None, :], FFN_RESIDUAL)
    loss = lax.psum(0.5 * jnp.sum(sq) / d, ("x", "y", "c"))

    big_grads = {n: [None] * depth for n in BIG}
    small_grads = {n: [None] * depth for n in SMALL}
    s8 = lambda a: jnp.sum(a, axis=0)
    wd_ = 3 * d // shards
    in_pieces = [_shard_cols(6, dg, s, wd_) for s in range(shards)]
    dbias = [None] * depth
    chip_sums, arrived = [None] * depth, [None] * depth
    for l in reversed(range(depth)):
        sv = saved[l]
        lw = {n: _LayerOf((b, l)) for n, b in zip(BIG, bufs)}
        row = lambda a: a[l][None, :]
        small_grads["ln3_g"][l], small_grads["ln3_b"][l] = s8(pg), s8(pb)
        dgt, dut = _swiglu_bwd(drs3, lw["ffn2_w_down"], sv["g2"], sv["u2"])
        dr2, drs2, pg, pb = _ffn_dx(dgt, dut, lw["ffn2_w_gate"], lw["ffn2_w_up"], dr3, alpha,
                                    (sv["xh2"], sv["rs2"], row(ln2_g), 1.0))
        small_grads["ln2_g"][l], small_grads["ln2_b"][l] = s8(pg), s8(pb)
        big_grads["ffn2_w_down"][l] = _wgrad("wgrad_down", [sv["h2"]], [drs3], shards, _lshape(lw["ffn2_w_down"])[1:],
                                             lambda a, s: a[0][s], lambda b, s: b[0][...])
        big_grads["ffn2_w_gate"][l] = _wgrad("wgrad_up", [sv["x2"]], [dgt], shards, _lshape(lw["ffn2_w_gate"])[1:],
                                             lambda a, s: a[0][...], lambda b, s: b[0][s])
        big_grads["ffn2_w_up"][l] = _wgrad("wgrad_up", [sv["x2"]], [dut], shards, _lshape(lw["ffn2_w_up"])[1:],
                                           lambda a, s: a[0][...], lambda b, s: b[0][s])
        do_sb, do_ca, pgs, pgc = _attn_dmix(drs2, lw["w_out"], sv["o_sb"], sv["o_ca"], row(sb_out_g), row(ca_out_g))
        small_grads["sb_out_g"][l], small_grads["ca_out_g"][l] = s8(pgs), s8(pgc)
        big_grads["w_out"][l] = _wgrad(
            "wgrad_out", [sv["m_sb"], sv["m_ca"]], [drs2], shards, (qd, d),
            lambda a, s: a[s // 2][:, (s % 2) * qd:(s % 2 + 1) * qd], lambda b, s: b[0][...])
        dqa, dka, dva = _sb_bwd(sv["qkv"], sv["c_sb"], do_sb, bl, seq, dg)
        if l + 1 < depth:
            dqb, dkb, dvb, dbias[l], arrived[l + 1] = _ca_bwd(sv["qkv"], bias_all[l], sv["o_ca"], do_ca, bl, seq, dg,
                                                             chip_sums[l + 1])
        else:
            dqb, dkb, dvb, dbias[l] = _ca_bwd(sv["qkv"], bias_all[l], sv["o_ca"], do_ca, bl, seq, dg)
        dqkv = [dqa, dka, dva, dqb, dkb, dvb]
        dr1, drs1, pg, pb = _attn_dx(dqkv, lw["w_in"], dr2, alpha, (sv["xh1"], sv["rs1"], row(ln1_g), FFN_RESIDUAL))
        small_grads["ln1_g"][l], small_grads["ln1_b"][l] = s8(pg), s8(pb)
        big_grads["w_in"][l] = _wgrad(
            "wgrad_in", [sv["x1"]], dqkv, shards, (d, wd_), lambda a, s: a[0][...],
            lambda b, s: jnp.concatenate([b[gi][:, lo:hi] for gi, lo, hi in in_pieces[s]], axis=1))
        dgt, dut = _swiglu_bwd(drs1, lw["ffn1_w_down"], sv["g1"], sv["u1"])
        if l > 0:
            prev = saved[l - 1]
            dr3, drs3, pg, pb = _ffn_dx(dgt, dut, lw["ffn1_w_gate"], lw["ffn1_w_up"], dr1, alpha,
                                        (prev["xh3"], prev["rs3"], ln3_g[l - 1][None, :], FFN_RESIDUAL))
        else:
            grad_x = _ffn_dx(dgt, dut, lw["ffn1_w_gate"], lw["ffn1_w_up"], dr1, alpha, None).reshape(bl, seq, d)
        big_grads["ffn1_w_down"][l] = _wgrad("wgrad_down", [sv["h1"]], [drs1], shards, _lshape(lw["ffn1_w_down"])[1:],
                                             lambda a, s: a[0][s], lambda b, s: b[0][...])
        big_grads["ffn1_w_gate"][l] = _wgrad("wgrad_up", [sv["xin"]], [dgt], shards, _lshape(lw["ffn1_w_gate"])[1:],
                                             lambda a, s: a[0][...], lambda b, s: b[0][s])
        big_grads["ffn1_w_up"][l] = _wgrad("wgrad_up", [sv["xin"]], [dut], shards, _lshape(lw["ffn1_w_up"])[1:],
                                           lambda a, s: a[0][...], lambda b, s: b[0][s])
        partial = [big_grads[n][l][None] for n in BIG]
        chip_sums[l] = _add_pair(partial, _swap_halves(partial))
    arrived[0] = _scatter_chips(chip_sums[0])
    rel_grads = bias_vjp(jnp.stack(dbias))[0]
    for l in range(depth):
        small_grads["rel_bias"][l] = rel_grads[l]

    halves = []
    for lo_k in range(0, len(BIG), 4):
        part = None
        for l in range(depth):
            part = _add_chips(chip_sums[l][lo_k:lo_k + 4], arrived[l][lo_k:lo_k + 4], part, l, depth)
        halves += list(part)
    joined = dict(zip(BIG, _join_halves(halves)))
    packed = _allreduce_small(_pack_small({n: jnp.stack(small_grads[n]) for n in SMALL}, depth, d))

    grads, delta, new_m, new_v = {}, {}, {}, {}
    for n in BIG:
        grads[n], delta[n], new_m[n], new_v[n] = _adamw(w[n], joined[n], mom[n], var[n])
    pw, pm, pv = (_pack_small({n: src[n] for n in SMALL}, depth, d) for src in (w, mom, var))
    sg, sd, sm, sv_ = _adamw(pw, packed, pm, pv)
    for dst, src in ((grads, sg), (delta, sd), (new_m, sm), (new_v, sv_)):
        dst.update(_unpack_small(src, depth, d, heads, n_rel))

    return (loss, grad_x, *[grads[n] for n in ORDER], *[delta[n] for n in ORDER],
            *[new_m[n] for n in ORDER], *[new_v[n] for n in ORDER])
```

```python
import functools
import math

import jax
import jax.numpy as jnp
from jax import lax
from jax.experimental import pallas as pl
from jax.experimental.pallas import tpu as pltpu

F32 = jnp.float32
BF = jnp.bfloat16

CHUNK = 64
N_PREV_CHUNKS = 8
BAND = CHUNK * (N_PREV_CHUNKS + 1)
MAX_REL = 128
HEAD_DIM = 64
FFN_RESIDUAL = 0.5
LN_EPS = 1e-5
RMS_EPS = 1e-6
ADAM_LR = 0.001
ADAM_B1 = 0.9
ADAM_B2 = 0.999
ADAM_EPS = 1e-08
ADAM_WD = 0.01
ADAM_STEP = 10

LANES = 128
ATT_TILE = 4 * CHUNK
SB_TQ = 512
SB_TK = 256
SB_DEAD = -105.0
SB_UNSEEN = -1e30
NEG_BIAS = -1e30
VMEM_LIMIT = 56 * 1024 * 1024
ROW_TILE = 512
MESH_IDS = pl.DeviceIdType.MESH

_NT = (((1,), (1,)), ((), ()))
_TN = (((0,), (0,)), ((), ()))


def _dot(a, b):
    return jnp.dot(a, b, preferred_element_type=F32)


def _dot_nt(a, b):
    return lax.dot_general(a, b, _NT, preferred_element_type=F32)


def _dot_tn(a, b):
    return lax.dot_general(a, b, _TN, preferred_element_type=F32)


def _tile(n, pref):
    t = min(n, pref)
    while n % t:
        t //= 2
    return t


def _cparams(sem):
    return pltpu.CompilerParams(dimension_semantics=sem, vmem_limit_bytes=VMEM_LIMIT)


def _sds(shape, dtype):
    return jax.ShapeDtypeStruct(tuple(shape), dtype)


def _sigmoid(v):
    return pl.reciprocal(1.0 + jnp.exp(-v), approx=True)


def _rows8(v):
    tm, c = v.shape
    return v.reshape(tm // 8, 8, c).sum(axis=0)


class _LayerOf(tuple):
    pass


def _lshape(item):
    return item[0].shape[1:] if isinstance(item, _LayerOf) else item.shape


def _accumulate(m, ref, part):
    @pl.when(m == 0)
    def _():
        ref[...] = part

    @pl.when(m > 0)
    def _():
        ref[...] += part


def _ln_bwd_tail(m, dy, xh_ref, rs_ref, g_ref, scale, dr_ref, drs_ref, dg_ref, db_ref):
    xhv = xh_ref[...]
    dxh = dy * g_ref[...]
    m1 = jnp.mean(dxh, axis=-1, keepdims=True)
    m2 = jnp.mean(dxh * xhv, axis=-1, keepdims=True)
    dr = rs_ref[...] * (dxh - m1 - xhv * m2)
    dr_ref[...] = dr
    drs_ref[...] = (scale * dr).astype(BF)
    _accumulate(m, dg_ref, _rows8(dy * xhv))
    _accumulate(m, db_ref, _rows8(dy))


def _rowcall(name, body, n_rows, tm, row_ins, res_ins, row_outs, acc_outs=()):
    def rspec(shape):
        if len(shape) == 2:
            return pl.BlockSpec((tm, shape[1]), lambda m: (m, 0))
        return pl.BlockSpec((shape[0], tm, shape[2]), lambda m: (0, m, 0))

    def cspec(shape, single=False):
        zeros = (0,) * len(shape)
        mode = pl.Buffered(1) if single else None
        return pl.BlockSpec(tuple(shape), lambda m: zeros, pipeline_mode=mode)

    def split(items, mk):
        arrs, specs = [], []
        for it in items:
            if isinstance(it, _LayerOf):
                arr, l = it
                tail = (0,) * (arr.ndim - 1)
                arrs.append(arr)
                specs.append(pl.BlockSpec((None,) + arr.shape[1:], functools.partial(lambda m, l, tail: (l,) + tail, l=l, tail=tail),
                                          pipeline_mode=pl.Buffered(1)))
            elif isinstance(it, tuple):
                arrs.append(it[0])
                specs.append(it[1])
            else:
                arrs.append(it)
                specs.append(mk(it.shape))
        return arrs, specs

    ra, rs = split(row_ins, rspec)
    ca, cs = split(res_ins, functools.partial(cspec, single=True))
    out_specs = [rspec(s.shape) for s in row_outs] + [cspec(s.shape) for s in acc_outs]
    sem = ("arbitrary",) if acc_outs else ("parallel",)
    return pl.pallas_call(
        body, name=name, grid=(n_rows // tm,), in_specs=rs + cs, out_specs=out_specs,
        out_shape=list(row_outs) + list(acc_outs), compiler_params=_cparams(sem),
    )(*ra, *ca)


def _ffn_up(x, wg, wu):
    t, _ = x.shape
    s_n, _, fs = _lshape(wg)
    tm = _tile(t, ROW_TILE)

    def body(x_ref, wg_ref, wu_ref, g_ref, u_ref, h_ref):
        xb = x_ref[...].astype(BF)
        for s in range(s_n):
            g = _dot(xb, wg_ref[s])
            u = _dot(xb, wu_ref[s])
            g_ref[s] = g.astype(BF)
            u_ref[s] = u.astype(BF)
            h_ref[s] = (g * _sigmoid(g) * u).astype(BF)

    out = _sds((s_n, t, fs), BF)
    return _rowcall("ffn_up", body, t, tm, [x], [wg, wu], [out, out, out])


def _proj_ln(name, a_list, w, terms, xres, gain, bias, scale, alpha):
    t, d = xres.shape
    tm = _tile(t, ROW_TILE)
    na = len(a_list)

    def body(*refs):
        a_refs = refs[:na]
        x_ref, w_ref, g_ref, bb_ref = refs[na:na + 4]
        y_ref, xh_ref, rs_ref = refs[na + 4:]
        acc = None
        for i, s, cols, sw in terms:
            a = a_refs[i][s] if cols is None else a_refs[i][:, cols[0]:cols[1]]
            p = _dot(a.astype(BF), w_ref[sw])
            acc = p if acc is None else acc + p
        if scale != 1.0:
            acc = scale * acc
        r = alpha * x_ref[...] + acc
        mu = jnp.mean(r, axis=-1, keepdims=True)
        xc = r - mu
        var = jnp.mean(xc * xc, axis=-1, keepdims=True)
        rstd = lax.rsqrt(var + LN_EPS)
        xh = xc * rstd
        y_ref[...] = xh * g_ref[...] + bb_ref[...]
        xh_ref[...] = xh
        rs_ref[...] = rstd

    outs = [_sds((t, d), F32), _sds((t, d), F32), _sds((t, 1), F32)]
    return _rowcall(name, body, t, tm, list(a_list) + [xres], [w, gain, bias], outs)


def _shard_cols(groups, dg, s, width):
    pieces, lo = [], s * width
    while lo < (s + 1) * width:
        gi = lo // dg
        hi = min((gi + 1) * dg, (s + 1) * width)
        pieces.append((gi, lo - gi * dg, hi - gi * dg))
        lo = hi
    assert all(p[0] < groups for p in pieces)
    return pieces


def _qkv_proj(x, win):
    t, _ = x.shape
    s_n, _, wd_ = _lshape(win)
    tm = _tile(t, ROW_TILE)

    def body(x_ref, w_ref, o_ref):
        xb = x_ref[...].astype(BF)
        for s in range(s_n):
            o_ref[:, s * wd_:(s + 1) * wd_] = _dot(xb, w_ref[s]).astype(BF)

    return _rowcall("qkv_proj", body, t, tm, [x], [win], [_sds((t, s_n * wd_), BF)])[0]


def _dx_call(name, rows, res, matmul, resid, alpha, ln):
    t, d = resid.shape
    tm = _tile(t, ROW_TILE)
    nr, nc = len(rows), len(res)

    def body(*refs):
        r_refs, resid_ref = refs[:nr], refs[nr]
        if ln is None:
            c_refs, outs = refs[nr + 1:nr + 1 + nc], refs[nr + 1 + nc:]
            outs[0][...] = matmul(r_refs, c_refs) + alpha * resid_ref[...]
        else:
            xh_ref, rs_ref = refs[nr + 1:nr + 3]
            c_refs, g_ref = refs[nr + 3:nr + 3 + nc], refs[nr + 3 + nc]
            outs = refs[nr + 4 + nc:]
            dx = matmul(r_refs, c_refs) + alpha * resid_ref[...]
            _ln_bwd_tail(pl.program_id(0), dx, xh_ref, rs_ref, g_ref, ln[3], *outs)

    if ln is None:
        return _rowcall(name, body, t, tm, list(rows) + [resid], list(res), [_sds((t, d), F32)])[0]
    outs = [_sds((t, d), F32), _sds((t, d), BF)]
    accs = [_sds((8, d), F32), _sds((8, d), F32)]
    return _rowcall(name, body, t, tm, list(rows) + [resid, ln[0], ln[1]], list(res) + [ln[2]], outs, accs)


def _ffn_dx(dg_, du_, wg, wu, resid, alpha, ln):
    s_n = dg_.shape[0]

    def matmul(r, c):
        acc = None
        for i in range(2):
            for s in range(s_n):
                p = _dot_nt(r[i][s], c[i][s])
                acc = p if acc is None else acc + p
        return acc

    return _dx_call("ffn_dx", [dg_, du_], [wg, wu], matmul, resid, alpha, ln)


def _attn_dx(dqkv, win, resid, alpha, ln):
    s_n, _, wd_ = _lshape(win)
    dg = dqkv[0].shape[1]
    pieces = [_shard_cols(len(dqkv), dg, s, wd_) for s in range(s_n)]

    def matmul(r, c):
        acc = None
        for s in range(s_n):
            a = jnp.concatenate([r[gi][:, lo:hi] for gi, lo, hi in pieces[s]], axis=1)
            p = _dot_nt(a, c[0][s])
            acc = p if acc is None else acc + p
        return acc

    return _dx_call("attn_dx", list(dqkv), [win], matmul, resid, alpha, ln)


def _head_rms_bwd(dm, ov, gain, lo):
    ss = ov * ov
    s0 = jnp.sum(jnp.where(lo, ss, 0.0), axis=1, keepdims=True)
    s1 = jnp.sum(jnp.where(lo, 0.0, ss), axis=1, keepdims=True)
    r = jnp.where(lo, lax.rsqrt(s0 / HEAD_DIM + RMS_EPS), lax.rsqrt(s1 / HEAD_DIM + RMS_EPS))
    n = ov * r
    dn = dm * gain
    tt = dn * n
    t0 = jnp.sum(jnp.where(lo, tt, 0.0), axis=1, keepdims=True) / HEAD_DIM
    t1 = jnp.sum(jnp.where(lo, 0.0, tt), axis=1, keepdims=True) / HEAD_DIM
    return r * (dn - n * jnp.where(lo, t0, t1)), dm * n


def _attn_dmix(drs, wout, o_sb, o_ca, g_sb, g_ca):
    t, dg = o_sb.shape
    s_n, qd, _ = _lshape(wout)
    tm = _tile(t, ROW_TILE)
    per = dg // qd

    def body(d_ref, osb_ref, oca_ref, w_ref, gs_ref, gc_ref, dsb_ref, dca_ref, pgs_ref, pgc_ref):
        m = pl.program_id(0)
        dv = d_ref[...]
        lo = lax.broadcasted_iota(jnp.int32, (1, LANES), 1) < HEAD_DIM
        for grp, (o_ref, g_ref, do_ref, pg_ref) in enumerate(((osb_ref, gs_ref, dsb_ref, pgs_ref),
                                                               (oca_ref, gc_ref, dca_ref, pgc_ref))):
            dm = jnp.concatenate([_dot_nt(dv, w_ref[grp * per + k]) for k in range(per)], axis=1)
            parts = []
            for gi in range(dg // LANES):
                sl = slice(gi * LANES, (gi + 1) * LANES)
                do, pg = _head_rms_bwd(dm[:, sl], o_ref[:, sl], g_ref[:, sl], lo)
                do_ref[:, sl] = do.astype(BF)
                parts.append(_rows8(pg))
            _accumulate(m, pg_ref, jnp.concatenate(parts, axis=1))

    outs = [_sds((t, dg), BF), _sds((t, dg), BF)]
    accs = [_sds((8, dg), F32), _sds((8, dg), F32)]
    return _rowcall("attn_dmix", body, t, tm, [drs, o_sb, o_ca], [wout, g_sb, g_ca], outs, accs)


def _swiglu_bwd(drs, wd, g, u):
    s_n, t, fs = g.shape
    tm = _tile(t, ROW_TILE)

    def body(d_ref, g_ref, u_ref, wd_ref, dg_ref, du_ref):
        dv = d_ref[...]
        rc = _tile(tm, 128)
        for s in range(s_n):
            dh_all = _dot_nt(dv, wd_ref[s])
            for r0 in range(0, tm, rc):
                dh = dh_all[r0:r0 + rc]
                gv = g_ref[s, r0:r0 + rc, :].astype(F32)
                uv = u_ref[s, r0:r0 + rc, :].astype(F32)
                sg = _sigmoid(gv)
                gs = gv * sg
                dg_ref[s, r0:r0 + rc, :] = (dh * uv * (sg + gs * (1.0 - sg))).astype(BF)
                du_ref[s, r0:r0 + rc, :] = (dh * gs).astype(BF)

    out = _sds((s_n, t, fs), BF)
    return _rowcall("swiglu_bwd", body, t, tm, [drs, g, u], [wd], [out, out])


def _loss_grad(y, target, xh, rstd, gain, scale):
    t, d = y.shape
    tm = _tile(t, ROW_TILE)

    def body(y_ref, t_ref, xh_ref, rs_ref, g_ref, dr_ref, drs_ref, dg_ref, db_ref, sq_ref):
        m = pl.program_id(0)
        diff = y_ref[...] - t_ref[...]
        _ln_bwd_tail(m, diff / d, xh_ref, rs_ref, g_ref, scale, dr_ref, drs_ref, dg_ref, db_ref)
        _accumulate(m, sq_ref, _rows8(diff * diff))

    outs = [_sds((t, d), F32), _sds((t, d), BF)]
    accs = [_sds((8, d), F32)] * 3
    return _rowcall("loss_grad", body, t, tm, [y, target, xh, rstd], [gain], outs, accs)


def _adamw(w, g, m, v):
    shape = w.shape
    c = shape[-1]
    rows = math.prod(shape[:-1])
    tm = _tile(rows, 512)

    def body(w_ref, g_ref, m_ref, v_ref, go_ref, d_ref, nm_ref, nv_ref):
        gv = g_ref[...]
        go_ref[...] = gv
        nm = ADAM_B1 * m_ref[...] + (1.0 - ADAM_B1) * gv
        nv = ADAM_B2 * v_ref[...] + (1.0 - ADAM_B2) * (gv * gv)
        m_hat = nm / (1.0 - ADAM_B1 ** ADAM_STEP)
        v_hat = nv / (1.0 - ADAM_B2 ** ADAM_STEP)
        d_ref[...] = -ADAM_LR * (m_hat / (jnp.sqrt(v_hat) + ADAM_EPS) + ADAM_WD * w_ref[...])
        nm_ref[...] = nm
        nv_ref[...] = nv

    flat = [a.reshape(rows, c) for a in (w, g, m, v)]
    out = _sds((rows, c), F32)
    res = _rowcall("adamw", body, rows, tm, flat, [], [out, out, out, out])
    return tuple(r.reshape(shape) for r in res)


def _wgrad(name, a_list, b_list, out_sub, sub_shape, a_get, b_get):
    t = a_list[0].shape[-2]
    tk = _tile(t, 1024)
    na, nb = len(a_list), len(b_list)
    nk = t // tk

    def tspec(shape):
        if len(shape) == 2:
            return pl.BlockSpec((tk, shape[1]), lambda k: (k, 0))
        return pl.BlockSpec((shape[0], tk, shape[2]), lambda k: (0, k, 0))

    def body(*refs):
        a_refs = refs[:na]
        b_refs = refs[na:na + nb]
        o_ref = refs[na + nb]
        acc_ref = refs[na + nb + 1]
        k = pl.program_id(0)

        @pl.when(k == 0)
        def _():
            acc_ref[...] = jnp.zeros_like(acc_ref)

        for s in range(out_sub):
            acc_ref[s] += _dot_tn(a_get(a_refs, s).astype(BF), b_get(b_refs, s).astype(BF))

        @pl.when(k == nk - 1)
        def _():
            o_ref[...] = acc_ref[...].astype(BF)

    full = (out_sub,) + tuple(sub_shape)
    return pl.pallas_call(
        body, name=name, grid=(nk,), in_specs=[tspec(a.shape) for a in a_list] + [tspec(b.shape) for b in b_list],
        out_specs=pl.BlockSpec(full, lambda k: (0, 0, 0)), out_shape=_sds(full, BF),
        scratch_shapes=[pltpu.VMEM(full, F32)], compiler_params=_cparams(("arbitrary",)),
    )(*a_list, *b_list)


def _head_masks(x):
    lo = lax.broadcasted_iota(jnp.int32, (1, LANES), 1) < HEAD_DIM
    zero = jnp.zeros_like(x)
    return lo, (jnp.where(lo, x, zero), jnp.where(lo, zero, x))


def _logistic_parts(z):
    nz = -z
    t = jnp.exp(jnp.minimum(z, nz))
    one_t = 1.0 + t
    return jnp.minimum(nz, 0.0) - jnp.log(one_t), t, one_t


def _split2(x):
    hi = x.astype(BF)
    lo = (x - hi.astype(F32)).astype(BF)
    return jnp.concatenate([hi, lo], axis=1)


def _tri2(cond):
    tri = jnp.where(cond, 1.0, 0.0).astype(BF)
    return jnp.concatenate([tri, tri], axis=0)


def _sb_specs(bl, seq, p_n, col0, tq):
    nq = seq // tq
    q_spec = pl.BlockSpec((tq, LANES), lambda b, p, i: (b * nq + i, col0 + p))
    k_spec = pl.BlockSpec((seq, LANES), lambda b, p, i: (b, col0 + p_n + p))
    v_spec = pl.BlockSpec((seq, LANES), lambda b, p, i: (b, col0 + 2 * p_n + p))
    tile_spec = pl.BlockSpec((tq, LANES), lambda b, p, i: (b * nq + i, p))
    seq_spec = pl.BlockSpec((seq, LANES), lambda b, p, i: (b, p))
    g_spec = pl.BlockSpec((1, LANES), lambda b, p, i: (0, p))
    return nq, q_spec, k_spec, v_spec, tile_spec, seq_spec, g_spec


def _rms_out(acc, lo, gain):
    ss = acc * acc
    s0 = jnp.sum(jnp.where(lo, ss, 0.0), axis=1, keepdims=True)
    s1 = jnp.sum(jnp.where(lo, 0.0, ss), axis=1, keepdims=True)
    r = jnp.where(lo, lax.rsqrt(s0 / HEAD_DIM + RMS_EPS), lax.rsqrt(s1 / HEAD_DIM + RMS_EPS))
    return acc * r * gain


def _sb_fwd(qkv, bl, seq, dg, gain, bufs=(), layer=None):
    t = bl * seq
    p_n = dg // LANES
    tq, tk = min(SB_TQ, seq), SB_TK
    ratio = tq // tk
    scale = 1.0 / math.sqrt(HEAD_DIM)
    nq, q_spec, k_spec, v_spec, tile_spec, _, g_spec = _sb_specs(bl, seq, p_n, 0, tq)
    assert seq // tk <= LANES
    nt = len(bufs)

    def body(q_ref, k_ref, v_ref, g_ref, *rest):
        o_ref, m_ref, c_ref = rest[nt:nt + 3]
        i = pl.program_id(2)
        if nt:
            start, finish = _gather_plan(rest[nt + 3:2 * nt + 3], [b.shape[2] for b in bufs], rest[2 * nt + 3],
                                         rest[2 * nt + 4], layer)
            grid_step = (pl.program_id(0) * p_n + pl.program_id(1)) * nq + i
            pl.when(grid_step == 0)(start)
        row = lax.broadcasted_iota(jnp.int32, (tq, tk), 0)
        col = lax.broadcasted_iota(jnp.int32, (tq, tk), 1)
        krow = lax.broadcasted_iota(jnp.int32, (tk, tk), 0)
        kcol = lax.broadcasted_iota(jnp.int32, (tk, tk), 1)
        tri2 = _tri2(krow > kcol)
        lo, qh = _head_masks(q_ref[...] * jnp.asarray(scale, BF))
        lane = lax.broadcasted_iota(jnp.int32, (1, LANES), 1)

        def step(j, carry, diag):
            acc, c0, c1, s0, s1 = carry
            cars, seen = (c0, c1), (s0, s1)
            mask = None if diag is None else row > col + diag
            off = pl.multiple_of(j * tk, tk)
            kj = k_ref[pl.ds(off, tk), :]
            vj = v_ref[pl.ds(off, tk), :]
            zs = [_dot_nt(qh[h], kj) for h in range(2)]
            lks = [_logistic_parts(z)[0] for z in zs]
            if mask is not None:
                lks = [jnp.where(mask, lk, 0.0) for lk in lks]
            sins = [_dot(_split2(lk), tri2) for lk in lks]
            ws = []
            for h in range(2):
                w = jnp.exp((lks[h] + zs[h]) + (sins[h] + cars[h]))
                ws.append((w if mask is None else jnp.where(mask, w, 0.0)).astype(BF))
            pvs = [_dot(w, vj) for w in ws]
            new_seen = [jnp.where(lane == j, cars[h], seen[h]) for h in range(2)]
            new_cars = [cars[h] + jnp.sum(lks[h], axis=1, keepdims=True) for h in range(2)]
            return acc + jnp.where(lo, pvs[0], pvs[1]), new_cars[0], new_cars[1], new_seen[0], new_seen[1]

        def corner_step(j):
            top = tq - tk
            mask = krow > kcol
            kj = k_ref[pl.ds(pl.multiple_of(j * tk, tk), tk), :]
            vj = v_ref[pl.ds(pl.multiple_of(j * tk, tk), tk), :]
            zs = [_dot_nt(qh[h][top:], kj) for h in range(2)]
            lks = [jnp.where(mask, _logistic_parts(z)[0], 0.0) for z in zs]
            sins = [_dot(_split2(lk), tri2) for lk in lks]
            ws = [jnp.where(mask, jnp.exp((lks[h] + zs[h]) + sins[h]), 0.0).astype(BF) for h in range(2)]
            pvs = [_dot(w, vj) for w in ws]
            pad = lambda low: jnp.concatenate([jnp.zeros((top,) + low.shape[1:], F32), low], axis=0)
            cars = [pad(jnp.sum(lk, axis=1, keepdims=True)) for lk in lks]
            seen = jnp.where(lane == j, 0.0, jnp.full((tq, LANES), SB_UNSEEN, F32))
            return pad(jnp.where(lo, pvs[0], pvs[1])), cars[0], cars[1], seen, seen

        if ratio == 2:
            carry = step(ratio * i, corner_step(ratio * i + 1), 0)
        else:
            zc = jnp.zeros((tq, 1), F32)
            unseen = jnp.full((tq, LANES), SB_UNSEEN, F32)
            carry = (jnp.zeros((tq, LANES), F32), zc, zc, unseen, unseen)
            for m in reversed(range(ratio)):
                carry = step(ratio * i + m, carry, m * tk)

        def alive(state):
            j, c = state
            return jnp.logical_and(j >= 0, jnp.max(jnp.maximum(c[1], c[2])) >= SB_DEAD)

        _, carry = lax.while_loop(alive, lambda st: (st[0] - 1, step(st[0], st[1], None)), (ratio * i - 1, carry))
        acc = carry[0]
        o_ref[...] = acc
        m_ref[...] = _rms_out(acc, lo, g_ref[...]).astype(BF)
        c_ref[:, :LANES] = carry[3]
        c_ref[:, LANES:] = carry[4]
        if nt:
            pl.when(grid_step == bl * p_n * nq - 1)(finish)

    car_spec = pl.BlockSpec((tq, 2 * LANES), lambda b, p, i: (b * nq + i, p))
    outs = [_sds((t, dg), F32), _sds((t, dg), BF), _sds((t, 2 * dg), F32)]
    if not nt:
        return pl.pallas_call(
            body, name="sb_fwd", grid=(bl, p_n, nq), in_specs=[q_spec, k_spec, v_spec, g_spec],
            out_specs=[tile_spec, tile_spec, car_spec], out_shape=outs,
            compiler_params=_cparams(("parallel", "parallel", "parallel")),
        )(qkv, qkv, qkv, gain)
    res = pl.pallas_call(
        body, name="sb_fwd_gather", grid=(bl, p_n, nq), in_specs=[q_spec, k_spec, v_spec, g_spec] + [_ANY] * nt,
        out_specs=[tile_spec, tile_spec, car_spec] + [_ANY] * nt,
        out_shape=outs + [_sds(b.shape, b.dtype) for b in bufs],
        input_output_aliases={4 + k: 3 + k for k in range(nt)},
        scratch_shapes=[pltpu.SemaphoreType.DMA((6 * nt,)), pltpu.SemaphoreType.DMA((6 * nt,))],
        compiler_params=_cparams(("arbitrary", "arbitrary", "arbitrary")),
    )(qkv, qkv, qkv, gain, *bufs)
    return res[0], res[1], res[2], list(res[3:])


def _sb_bwd(qkv, cars, do, bl, seq, dg, grads=()):
    t = bl * seq
    p_n = dg // LANES
    tq, tk = min(SB_TQ, seq), SB_TK
    ratio = tq // tk
    scale = 1.0 / math.sqrt(HEAD_DIM)
    nq, q_spec, k_spec, v_spec, tile_spec, seq_spec, _ = _sb_specs(bl, seq, p_n, 0, tq)
    car_spec = pl.BlockSpec((tq, 2 * LANES), lambda b, p, i: (b * nq + i, p))
    nt = len(grads)

    def body(q_ref, k_ref, v_ref, c_ref, do_ref, *rest):
        dq_ref, dk_ref, dv_ref = rest[nt:nt + 3]
        dk_acc, dv_acc = rest[2 * nt + 3:2 * nt + 5]
        i = pl.program_id(2)
        if nt:
            start, finish = _swap_plan(rest[:nt], rest[nt + 3:2 * nt + 3], [g.shape[2] for g in grads],
                                       rest[2 * nt + 5], rest[2 * nt + 6])
            grid_step = (pl.program_id(0) * p_n + pl.program_id(1)) * nq + i
            pl.when(grid_step == 0)(start)

        @pl.when(i == 0)
        def _():
            dk_acc[...] = jnp.zeros_like(dk_acc)
            dv_acc[...] = jnp.zeros_like(dv_acc)

        row = lax.broadcasted_iota(jnp.int32, (tq, tk), 0)
        col = lax.broadcasted_iota(jnp.int32, (tq, tk), 1)
        krow = lax.broadcasted_iota(jnp.int32, (tk, tk), 0)
        kcol = lax.broadcasted_iota(jnp.int32, (tk, tk), 1)
        tri2 = _tri2(krow > kcol)
        tri_pre2 = _tri2(krow < kcol)
        lane = lax.broadcasted_iota(jnp.int32, (1, LANES), 1)
        sc = jnp.asarray(scale, BF)
        qs = q_ref[...] * sc
        dob = do_ref[...]
        lo, qh = _head_masks(qs)
        _, doh = _head_masks(dob)
        seen = (c_ref[:, :LANES], c_ref[:, LANES:])

        def step(j, carry, diag):
            dq, e0, e1 = carry
            ecars = (e0, e1)
            mask = None if diag is None else row > col + diag
            off = pl.multiple_of(j * tk, tk)
            kj = k_ref[pl.ds(off, tk), :]
            vj = v_ref[pl.ds(off, tk), :]
            zs = [_dot_nt(qh[h], kj) for h in range(2)]
            dws = [_dot_nt(doh[h], vj) for h in range(2)]
            parts = [_logistic_parts(z) for z in zs]
            lks = [p[0] if mask is None else jnp.where(mask, p[0], 0.0) for p in parts]
            sins = [_dot(_split2(lk), tri2) for lk in lks]
            ws, es = [], []
            for h in range(2):
                car = jnp.sum(jnp.where(lane == j, seen[h], 0.0), axis=1, keepdims=True)
                w = jnp.exp((lks[h] + zs[h]) + (sins[h] + car))
                w = w if mask is None else jnp.where(mask, w, 0.0)
                ws.append(w.astype(BF))
                es.append(dws[h] * w)
            pres = [_dot(_split2(e), tri_pre2) for e in es]
            dzs = []
            for h in range(2):
                t_, e = parts[h][1], es[h]
                before = pres[h] + ecars[h]
                dz = pl.reciprocal(parts[h][2], approx=True) * jnp.where(zs[h] > 0.0, e * t_ - before, e - t_ * before)
                dzs.append((dz if mask is None else jnp.where(mask, dz, 0.0)).astype(BF))
            kjs = kj * sc
            dqs = [_dot(dz, kjs) for dz in dzs]
            dks = [_dot_tn(dz, qs) for dz in dzs]
            dvs = [_dot_tn(w, dob) for w in ws]
            dk_acc[pl.ds(off, tk), :] += jnp.where(lo, dks[0], dks[1])
            dv_acc[pl.ds(off, tk), :] += jnp.where(lo, dvs[0], dvs[1])
            new_e = [ecars[h] + jnp.sum(es[h], axis=1, keepdims=True) for h in range(2)]
            return dq + jnp.where(lo, dqs[0], dqs[1]), new_e[0], new_e[1]

        reach = jnp.max(jnp.maximum(seen[0], seen[1]), axis=0, keepdims=True)
        skipped = jnp.logical_and(reach < SB_DEAD, lane < ratio * i)
        first = jnp.minimum(jnp.sum(jnp.where(skipped, 1, 0)).astype(jnp.int32), ratio * i)
        zc = jnp.zeros((tq, 1), F32)
        def corner_step(j, carry):
            dq, e0, e1 = carry
            top = tq - tk
            mask = krow > kcol
            off = pl.multiple_of(j * tk, tk)
            kj = k_ref[pl.ds(off, tk), :]
            vj = v_ref[pl.ds(off, tk), :]
            zs = [_dot_nt(qh[h][top:], kj) for h in range(2)]
            dws = [_dot_nt(doh[h][top:], vj) for h in range(2)]
            parts = [_logistic_parts(z) for z in zs]
            lks = [jnp.where(mask, p[0], 0.0) for p in parts]
            sins = [_dot(_split2(lk), tri2) for lk in lks]
            ws, es = [], []
            for h in range(2):
                car = jnp.sum(jnp.where(lane == j, seen[h][top:], 0.0), axis=1, keepdims=True)
                w = jnp.where(mask, jnp.exp((lks[h] + zs[h]) + (sins[h] + car)), 0.0)
                ws.append(w.astype(BF))
                es.append(dws[h] * w)
            pres = [_dot(_split2(e), tri_pre2) for e in es]
            dzs = []
            for h, ecar in enumerate((e0, e1)):
                t_, e = parts[h][1], es[h]
                before = pres[h] + ecar[top:]
                dz = pl.reciprocal(parts[h][2], approx=True) * jnp.where(zs[h] > 0.0, e * t_ - before, e - t_ * before)
                dzs.append(jnp.where(mask, dz, 0.0).astype(BF))
            kjs = kj * sc
            dqs = [_dot(dz, kjs) for dz in dzs]
            dks = [_dot_tn(dz, qs[top:]) for dz in dzs]
            dvs = [_dot_tn(w, dob[top:]) for w in ws]
            dk_acc[pl.ds(off, tk), :] += jnp.where(lo, dks[0], dks[1])
            dv_acc[pl.ds(off, tk), :] += jnp.where(lo, dvs[0], dvs[1])
            return dq + jnp.concatenate([jnp.zeros((top, LANES), F32), jnp.where(lo, dqs[0], dqs[1])], axis=0)

        carry = lax.fori_loop(first, ratio * i, lambda j, c: step(j, c, None), (jnp.zeros((tq, LANES), F32), zc, zc))
        if ratio == 2:
            dq = corner_step(ratio * i + 1, step(ratio * i, carry, 0))
        else:
            for m in range(ratio):
                carry = step(ratio * i + m, carry, m * tk)
            dq = carry[0]
        dq_ref[...] = dq.astype(BF)

        @pl.when(i == nq - 1)
        def _():
            dk_ref[...] = dk_acc[...].astype(BF)
            dv_ref[...] = dv_acc[...].astype(BF)

        if nt:
            pl.when(grid_step == bl * p_n * nq - 1)(finish)

    out = _sds((t, dg), BF)
    acc = [pltpu.VMEM((seq, LANES), F32), pltpu.VMEM((seq, LANES), F32)]
    if not nt:
        return pl.pallas_call(
            body, name="sb_bwd", grid=(bl, p_n, nq),
            in_specs=[q_spec, k_spec, v_spec, car_spec, tile_spec],
            out_specs=[tile_spec, seq_spec, seq_spec], out_shape=[out, out, out],
            scratch_shapes=acc, compiler_params=_cparams(("parallel", "parallel", "arbitrary")),
        )(qkv, qkv, qkv, cars, do)
    res = pl.pallas_call(
        body, name="sb_bwd_swap", grid=(bl, p_n, nq),
        in_specs=[q_spec, k_spec, v_spec, car_spec, tile_spec] + [_ANY] * nt,
        out_specs=[tile_spec, seq_spec, seq_spec] + [_ANY] * nt,
        out_shape=[out, out, out] + _swap_shapes(grads),
        scratch_shapes=acc + [pltpu.SemaphoreType.DMA((nt,)), pltpu.SemaphoreType.DMA((nt,))],
        compiler_params=_cparams(("arbitrary", "arbitrary", "arbitrary")),
    )(qkv, qkv, qkv, cars, do, *grads)
    return res[0], res[1], res[2], list(res[3:])


def _bias_expand(rb):
    h = rb.shape[0]
    pad = CHUNK * N_PREV_CHUNKS
    n_f = BAND + CHUNK - 1
    f = jnp.concatenate([rb[:, MAX_REL - (CHUNK - 1):2 * MAX_REL],
                         jnp.broadcast_to(rb[:, 2 * MAX_REL:], (h, pad + CHUNK - MAX_REL))], axis=1)
    fr = f[:, ::-1]
    skew = jnp.pad(jnp.broadcast_to(fr[:, None, :], (h, CHUNK, n_f)), ((0, 0), (0, 0), (0, 1)))
    skew = skew.reshape(h, CHUNK * (n_f + 1))[:, :CHUNK * n_f].reshape(h, CHUNK, n_f)
    table = skew[:, :, CHUNK - 1:CHUNK - 1 + BAND]
    n_c = ATT_TILE // CHUNK
    width = 3 * ATT_TILE
    rows = [jnp.pad(table, ((0, 0), (0, 0), (c * CHUNK, width - BAND - c * CHUNK)), constant_values=NEG_BIAS)
            for c in range(n_c)]
    return jnp.concatenate(rows, axis=1)


def _ca_scores(qh_h, k_ref, bias_ref, h, i, scale):
    tq = ATT_TILE
    ss, offs = [], []
    for kb in range(3):
        jb = i - 2 + kb
        off = pl.multiple_of(jnp.maximum(jb, 0) * tq, tq)
        s = _dot_nt(qh_h, k_ref[pl.ds(off, tq), :]) * scale + bias_ref[h, :, kb * tq:(kb + 1) * tq]
        ss.append(jnp.where(jb >= 0, s, NEG_BIAS))
        offs.append(off)
    m = jnp.max(jnp.maximum(jnp.maximum(ss[0], ss[1]), ss[2]), axis=1, keepdims=True)
    ps = [jnp.exp(s - m) for s in ss]
    den = jnp.sum(ps[0] + ps[1] + ps[2], axis=1, keepdims=True)
    inv = 1.0 / den
    return [p * inv for p in ps], offs


def _ca_fwd(qkv, bias4, bl, seq, dg, gain, bufs=(), layer=None):
    t = bl * seq
    p_n = dg // LANES
    tq = ATT_TILE
    scale = 1.0 / math.sqrt(HEAD_DIM)
    nq, q_spec, k_spec, v_spec, tile_spec, _, g_spec = _sb_specs(bl, seq, p_n, 3 * p_n, tq)
    b_spec = pl.BlockSpec((2, tq, 3 * tq), lambda b, p, i: (p, 0, 0))
    nt = len(bufs)

    def body(q_ref, k_ref, v_ref, bias_ref, g_ref, *rest):
        o_ref, m_ref = rest[nt:nt + 2]
        i = pl.program_id(2)
        if nt:
            start, finish = _gather_plan(rest[nt + 2:2 * nt + 2], [b.shape[2] for b in bufs], rest[2 * nt + 2],
                                         rest[2 * nt + 3], layer)
            step = (pl.program_id(0) * p_n + pl.program_id(1)) * nq + i
            pl.when(step == 0)(start)
        lo, qh = _head_masks(q_ref[...])
        outs = []
        for h in range(2):
            ps, offs = _ca_scores(qh[h], k_ref, bias_ref, h, i, scale)
            acc = None
            for kb in range(3):
                pv = _dot(ps[kb].astype(BF), v_ref[pl.ds(offs[kb], tq), :])
                acc = pv if acc is None else acc + pv
            outs.append(acc)
        acc = jnp.where(lo, outs[0], outs[1])
        o_ref[...] = acc
        m_ref[...] = _rms_out(acc, lo, g_ref[...]).astype(BF)
        if nt:
            pl.when(step == bl * p_n * nq - 1)(finish)

    if not nt:
        return pl.pallas_call(
            body, name="ca_fwd", grid=(bl, p_n, nq), in_specs=[q_spec, k_spec, v_spec, b_spec, g_spec],
            out_specs=[tile_spec, tile_spec], out_shape=[_sds((t, dg), F32), _sds((t, dg), BF)],
            compiler_params=_cparams(("parallel", "parallel", "parallel")),
        )(qkv, qkv, qkv, bias4, gain)
    res = pl.pallas_call(
        body, name="ca_fwd_gather", grid=(bl, p_n, nq),
        in_specs=[q_spec, k_spec, v_spec, b_spec, g_spec] + [_ANY] * nt,
        out_specs=[tile_spec, tile_spec] + [_ANY] * nt,
        out_shape=[_sds((t, dg), F32), _sds((t, dg), BF)] + [_sds(b.shape, b.dtype) for b in bufs],
        input_output_aliases={5 + k: 2 + k for k in range(nt)},
        scratch_shapes=[pltpu.SemaphoreType.DMA((6 * nt,)), pltpu.SemaphoreType.DMA((6 * nt,))],
        compiler_params=_cparams(("arbitrary", "arbitrary", "arbitrary")),
    )(qkv, qkv, qkv, bias4, gain, *bufs)
    return res[0], res[1], list(res[2:])


def _ca_bwd(qkv, bias4, o, do, bl, seq, dg, psums=()):
    t = bl * seq
    p_n = dg // LANES
    tq = ATT_TILE
    scale = 1.0 / math.sqrt(HEAD_DIM)
    nq = seq // tq
    c0 = 3 * p_n
    q_spec = pl.BlockSpec((tq, LANES), lambda p, b, i: (b * nq + i, c0 + p))
    k_spec = pl.BlockSpec((seq, LANES), lambda p, b, i: (b, c0 + p_n + p))
    v_spec = pl.BlockSpec((seq, LANES), lambda p, b, i: (b, c0 + 2 * p_n + p))
    tile_spec = pl.BlockSpec((tq, LANES), lambda p, b, i: (b * nq + i, p))
    seq_spec = pl.BlockSpec((seq, LANES), lambda p, b, i: (b, p))
    b_spec = pl.BlockSpec((2, tq, 3 * tq), lambda p, b, i: (p, 0, 0))

    nt = len(psums)

    def body(q_ref, k_ref, v_ref, bias_ref, o_ref, do_ref, *rest):
        dq_ref, dk_ref, dv_ref, db_ref = rest[nt:nt + 4]
        dk_acc, dv_acc = rest[2 * nt + 4:2 * nt + 6]
        b = pl.program_id(1)
        i = pl.program_id(2)
        if nt:
            start, finish = _scatter_plan(rest[:nt], rest[nt + 4:2 * nt + 4], rest[2 * nt + 6], rest[2 * nt + 7])
            step = (pl.program_id(0) * bl + b) * nq + i
            pl.when(step == 0)(start)

        @pl.when(i == 0)
        def _():
            dk_acc[...] = jnp.zeros_like(dk_acc)
            dv_acc[...] = jnp.zeros_like(dv_acc)

        @pl.when((i == 0) & (b == 0))
        def _():
            db_ref[...] = jnp.zeros_like(db_ref)

        q = q_ref[...]
        dob = do_ref[...]
        lo, qh = _head_masks(q)
        _, doh = _head_masks(dob)
        prod = dob.astype(F32) * o_ref[...]
        delta = (jnp.sum(jnp.where(lo, prod, 0.0), axis=1, keepdims=True),
                 jnp.sum(jnp.where(lo, 0.0, prod), axis=1, keepdims=True))
        dqs = []
        for h in range(2):
            ps, offs = _ca_scores(qh[h], k_ref, bias_ref, h, i, scale)
            dq = None
            dks, dvs = [], []
            for kb in range(3):
                dp = _dot_nt(doh[h], v_ref[pl.ds(offs[kb], tq), :])
                ds = ps[kb] * (dp - delta[h])
                db_ref[h, :, kb * tq:(kb + 1) * tq] += ds
                dsb = (ds * scale).astype(BF)
                part = _dot(dsb, k_ref[pl.ds(offs[kb], tq), :])
                dq = part if dq is None else dq + part
                dks.append(_dot_tn(dsb, q))
                dvs.append(_dot_tn(ps[kb].astype(BF), dob))
            dqs.append(dq)
            hm = lo if h == 0 else jnp.logical_not(lo)
            for kb in range(3):
                dk_acc[pl.ds(offs[kb], tq), :] += jnp.where(hm, dks[kb], 0.0)
                dv_acc[pl.ds(offs[kb], tq), :] += jnp.where(hm, dvs[kb], 0.0)
        dq_ref[...] = jnp.where(lo, dqs[0], dqs[1]).astype(BF)

        @pl.when(i == nq - 1)
        def _():
            dk_ref[...] = dk_acc[...].astype(BF)
            dv_ref[...] = dv_acc[...].astype(BF)

        if nt:
            pl.when(step == p_n * bl * nq - 1)(finish)

    out = _sds((t, dg), BF)
    acc = [pltpu.VMEM((seq, LANES), F32), pltpu.VMEM((seq, LANES), F32)]
    if not nt:
        return pl.pallas_call(
            body, name="ca_bwd", grid=(p_n, bl, nq),
            in_specs=[q_spec, k_spec, v_spec, b_spec, tile_spec, tile_spec],
            out_specs=[tile_spec, seq_spec, seq_spec, b_spec], out_shape=[out, out, out, _sds(bias4.shape, F32)],
            scratch_shapes=acc, compiler_params=_cparams(("parallel", "arbitrary", "arbitrary")),
        )(qkv, qkv, qkv, bias4, o, do)
    res = pl.pallas_call(
        body, name="ca_bwd_scatter", grid=(p_n, bl, nq),
        in_specs=[q_spec, k_spec, v_spec, b_spec, tile_spec, tile_spec] + [_ANY] * nt,
        out_specs=[tile_spec, seq_spec, seq_spec, b_spec] + [_ANY] * nt,
        out_shape=[out, out, out, _sds(bias4.shape, F32)] + _scatter_shapes(psums),
        scratch_shapes=acc + [pltpu.SemaphoreType.DMA((3 * nt,)), pltpu.SemaphoreType.DMA((3 * nt,))],
        compiler_params=_cparams(("arbitrary", "arbitrary", "arbitrary")),
    )(qkv, qkv, qkv, bias4, o, do, *psums)
    return res[0], res[1], res[2], res[3], list(res[4:])


_ANY = pl.BlockSpec(memory_space=pl.ANY)


def _mesh_pos():
    x, y, c = lax.axis_index("x"), lax.axis_index("y"), lax.axis_index("c")
    chips = [(1 - x, y), (x, 1 - y), (1 - x, 1 - y)]
    return x, y, c, chips


def _half(ref_rows, c):
    return pl.ds(c * (ref_rows // 2), ref_rows // 2)


def _cast_into_slot(w, slot):
    l_n, r, c_n = w.shape
    tr = _tile(r, 512)

    def body(s_ref, w_ref, o_ref):
        o_ref[...] = w_ref[...].astype(BF)

    return pl.pallas_call(
        body, name="cast_into_slot", out_shape=_sds((l_n, 4, r, c_n), BF),
        grid_spec=pltpu.PrefetchScalarGridSpec(
            num_scalar_prefetch=1, grid=(l_n, r // tr),
            in_specs=[pl.BlockSpec((None, tr, c_n), lambda l, m, s: (l, m, 0))],
            out_specs=pl.BlockSpec((None, None, tr, c_n), lambda l, m, s: (l, s[0], m, 0))),
        compiler_params=_cparams(("parallel", "parallel")),
    )(slot, w)


def _gather_plan(outs, n_rows, send_sems, recv_sems, layer):
    nt = len(outs)

    def blk(t, s, cc):
        return outs[t].at[pl.ds(layer, 1), s, _half(n_rows[t], cc)]

    def copy(k, ref, to):
        return pltpu.make_async_remote_copy(src_ref=ref, dst_ref=ref, send_sem=send_sems.at[k],
                                            recv_sem=recv_sems.at[k], device_id=to, device_id_type=MESH_IDS)

    def over_ici():
        x, y, c, chips = _mesh_pos()
        return [copy(j * nt + t, blk(t, 2 * x + y, c), (px, py, c)) for j, (px, py) in enumerate(chips) for t in range(nt)]

    def start():
        for cp in over_ici():
            cp.start()

    def finish():
        x, y, c, chips = _mesh_pos()
        passed = []
        for j, (px, py) in enumerate(chips):
            for t in range(nt):
                got = blk(t, 2 * px + py, c)
                copy(j * nt + t, got, (px, py, c)).wait_recv()
                relay = copy(3 * nt + j * nt + t, got, (x, y, 1 - c))
                relay.start()
                passed.append(relay)
        for j, (px, py) in enumerate(chips):
            for t in range(nt):
                copy(3 * nt + j * nt + t, blk(t, 2 * px + py, 1 - c), (x, y, 1 - c)).wait_recv()
        for cp in over_ici() + passed:
            cp.wait_send()

    return start, finish


def _gather_weights(bufs, layer):
    nt = len(bufs)

    def body(*refs):
        start, finish = _gather_plan(refs[nt:2 * nt], [b.shape[2] for b in bufs], refs[2 * nt], refs[2 * nt + 1], layer)
        start()
        finish()

    return pl.pallas_call(
        body, name="gather_weights", in_specs=[_ANY] * nt, out_specs=[_ANY] * nt,
        out_shape=[_sds(b.shape, b.dtype) for b in bufs], input_output_aliases={t: t for t in range(nt)},
        scratch_shapes=[pltpu.SemaphoreType.DMA((6 * nt,)), pltpu.SemaphoreType.DMA((6 * nt,))],
    )(*bufs)


def _swap_halves(grads):
    nt = len(grads)

    def body(*refs):
        start, finish = _swap_plan(refs[:nt], refs[nt:2 * nt], [g.shape[2] for g in grads], refs[2 * nt], refs[2 * nt + 1])
        start()
        finish()

    return pl.pallas_call(
        body, name="swap_halves", in_specs=[_ANY] * nt, out_specs=[_ANY] * nt, out_shape=_swap_shapes(grads),
        scratch_shapes=[pltpu.SemaphoreType.DMA((nt,)), pltpu.SemaphoreType.DMA((nt,))],
    )(*grads)


def _swap_shapes(grads):
    return [_sds(g.shape[:2] + (g.shape[2] // 2, g.shape[3]), g.dtype) for g in grads]


def _swap_plan(ins, theirs, n_rows, send_sems, recv_sems):
    def sends():
        x, y, c, _ = _mesh_pos()
        return [pltpu.make_async_remote_copy(
            src_ref=ins[t].at[:, :, _half(n_rows[t], 1 - c)], dst_ref=theirs[t], send_sem=send_sems.at[t],
            recv_sem=recv_sems.at[t], device_id=(x, y, 1 - c), device_id_type=MESH_IDS) for t in range(len(ins))]

    def start():
        for cp in sends():
            cp.start()

    def finish():
        for cp in sends():
            cp.wait_recv()
        for cp in sends():
            cp.wait_send()

    return start, finish


def _add_pair(mine, theirs):
    nt = len(mine)
    r2s = [th.shape[2] for th in theirs]

    def body(*refs):
        c = lax.axis_index("c")
        for k in range(nt):
            rows = pl.ds(pl.multiple_of(c * r2s[k], 16), r2s[k])
            refs[2 * nt + k][...] = (refs[k][rows, :].astype(F32) + refs[nt + k][...].astype(F32)).astype(BF)

    def spec(shape):
        return pl.BlockSpec((None, None) + tuple(shape[2:]), lambda s: (0, s, 0, 0))

    return pl.pallas_call(
        body, name="add_pair", out_shape=[_sds(th.shape, BF) for th in theirs], grid=(mine[0].shape[1],),
        in_specs=[spec(a.shape) for a in mine] + [spec(b.shape) for b in theirs],
        out_specs=[spec(b.shape) for b in theirs], compiler_params=_cparams(("parallel",)),
    )(*mine, *theirs)


def _scatter_chips(psums):
    nt = len(psums)

    def body(*refs):
        start, finish = _scatter_plan(refs[:nt], refs[nt:2 * nt], refs[2 * nt], refs[2 * nt + 1])
        start()
        finish()

    return pl.pallas_call(
        body, name="scatter_chips", in_specs=[_ANY] * nt, out_specs=[_ANY] * nt, out_shape=_scatter_shapes(psums),
        scratch_shapes=[pltpu.SemaphoreType.DMA((3 * nt,)), pltpu.SemaphoreType.DMA((3 * nt,))],
    )(*psums)


def _scatter_shapes(psums):
    return [_sds((3, p.shape[0]) + p.shape[2:], p.dtype) for p in psums]


def _scatter_plan(ins, got, send_sems, recv_sems):
    nt = len(ins)

    def sends():
        _, _, c, chips = _mesh_pos()
        return [pltpu.make_async_remote_copy(
            src_ref=ins[t].at[:, 2 * px + py], dst_ref=got[t].at[j], send_sem=send_sems.at[j * nt + t],
            recv_sem=recv_sems.at[j * nt + t], device_id=(px, py, c), device_id_type=MESH_IDS)
            for j, (px, py) in enumerate(chips) for t in range(nt)]

    def start():
        for cp in sends():
            cp.start()

    def finish():
        for cp in sends():
            cp.wait_recv()
        for cp in sends():
            cp.wait_send()

    return start, finish


def _add_chips(psums, gots, fulls, layer, depth):
    nt = len(psums)
    r2s = [p.shape[2] for p in psums]
    n_alias = nt if fulls is not None else 0

    def body(*refs):
        c = lax.axis_index("c")
        chip = 2 * lax.axis_index("x") + lax.axis_index("y")
        outs = refs[n_alias + 2 * nt:]
        for k in range(nt):
            acc = refs[n_alias + k][chip].astype(F32)
            for j in range(3):
                acc = acc + refs[n_alias + nt + k][j].astype(F32)
            outs[k][pl.ds(pl.multiple_of(c * r2s[k], 8), r2s[k]), :] = acc
            outs[k][pl.ds(pl.multiple_of((1 - c) * r2s[k], 8), r2s[k]), :] = jnp.zeros_like(acc)

    once = pl.Buffered(1)
    in_specs = [_ANY] * n_alias
    in_specs += [pl.BlockSpec((None,) + p.shape[1:], lambda i: (0, 0, 0, 0), pipeline_mode=once) for p in psums]
    in_specs += [pl.BlockSpec((3, None) + g.shape[2:], lambda i: (0, 0, 0, 0), pipeline_mode=once) for g in gots]
    out_shape = [_sds((depth, 2 * p.shape[2], p.shape[3]), F32) for p in psums]
    out_specs = [pl.BlockSpec((None,) + s.shape[1:], lambda i: (layer, 0, 0), pipeline_mode=once) for s in out_shape]
    return pl.pallas_call(
        body, name="add_chips", out_shape=out_shape, grid=(1,), in_specs=in_specs, out_specs=out_specs,
        input_output_aliases={k: k for k in range(n_alias)}, compiler_params=_cparams(("arbitrary",)),
    )(*(fulls or []), *psums, *gots)


def _join_halves(fulls):
    nt = len(fulls)

    def body(*refs):
        outs = refs[nt:2 * nt]
        send_sems, recv_sems = refs[2 * nt:]
        x, y, c, _ = _mesh_pos()

        def copy(t, cc):
            blk = outs[t].at[:, _half(fulls[t].shape[1], cc)]
            return pltpu.make_async_remote_copy(
                src_ref=blk, dst_ref=blk, send_sem=send_sems.at[t], recv_sem=recv_sems.at[t],
                device_id=(x, y, 1 - c), device_id_type=MESH_IDS)

        sends = [copy(t, c) for t in range(nt)]
        for cp in sends:
            cp.start()
        for t in range(nt):
            copy(t, 1 - c).wait_recv()
        for cp in sends:
            cp.wait_send()

    return pl.pallas_call(
        body, name="join_halves", in_specs=[_ANY] * nt, out_specs=[_ANY] * nt,
        out_shape=[_sds(f.shape, f.dtype) for f in fulls], input_output_aliases={t: t for t in range(nt)},
        scratch_shapes=[pltpu.SemaphoreType.DMA((nt,)), pltpu.SemaphoreType.DMA((nt,))],
    )(*fulls)


def _allreduce_small(part):
    r, d = part.shape
    flips = [(fx, fy, fc) for fx in (0, 1) for fy in (0, 1) for fc in (0, 1) if fx + fy + fc]

    def body(p_ref, o_ref, buf, send_sems, recv_sems):
        x, y, c, _ = _mesh_pos()
        me = 4 * x + 2 * y + c
        buf[me] = p_ref[...]
        sends = []
        for k, (fx, fy, fc) in enumerate(flips):
            sends.append(pltpu.make_async_remote_copy(
                src_ref=p_ref, dst_ref=buf.at[me], send_sem=send_sems.at[k], recv_sem=recv_sems.at[k],
                device_id=(x ^ fx, y ^ fy, c ^ fc), device_id_type=MESH_IDS))
        for cp in sends:
            cp.start()
        for k, (fx, fy, fc) in enumerate(flips):
            peer = 4 * (x ^ fx) + 2 * (y ^ fy) + (c ^ fc)
            pltpu.make_async_remote_copy(
                src_ref=p_ref, dst_ref=buf.at[peer], send_sem=send_sems.at[k], recv_sem=recv_sems.at[k],
                device_id=(x ^ fx, y ^ fy, c ^ fc), device_id_type=MESH_IDS).wait_recv()
        acc = buf[0]
        for dev in range(1, 8):
            acc = acc + buf[dev]
        o_ref[...] = acc
        for cp in sends:
            cp.wait_send()

    vmem = pl.BlockSpec(memory_space=pltpu.VMEM)
    return pl.pallas_call(
        body, name="allreduce_small", in_specs=[vmem], out_specs=vmem, out_shape=_sds((r, d), F32),
        scratch_shapes=[pltpu.VMEM((8, r, d), F32), pltpu.SemaphoreType.DMA((7,)), pltpu.SemaphoreType.DMA((7,))],
    )(part)


BIG = ("ffn1_w_gate", "ffn1_w_up", "ffn1_w_down", "w_in", "w_out", "ffn2_w_gate", "ffn2_w_up", "ffn2_w_down")
SMALL = ("ln1_g", "ln1_b", "rel_bias", "sb_out_g", "ca_out_g", "ln2_g", "ln2_b", "ln3_g", "ln3_b")
ORDER = ("ffn1_w_gate", "ffn1_w_up", "ffn1_w_down", "ln1_g", "ln1_b", "w_in", "rel_bias", "sb_out_g", "ca_out_g",
         "w_out", "ln2_g", "ln2_b", "ffn2_w_gate", "ffn2_w_up", "ffn2_w_down", "ln3_g", "ln3_b")


def _pack_small(vals, depth, d):
    rows = []
    for l in range(depth):
        for n in ("ln1_g", "ln1_b", "ln2_g", "ln2_b", "ln3_g", "ln3_b"):
            rows.append(vals[n][l][None, :])
        rows.append(jnp.concatenate([vals["sb_out_g"][l], vals["ca_out_g"][l]])[None, :])
        rb = vals["rel_bias"][l]
        rows.append(jnp.pad(rb, ((0, 0), (0, d - rb.shape[1]))))
    packed = jnp.concatenate(rows, axis=0)
    pad = (-packed.shape[0]) % 8
    return jnp.pad(packed, ((0, pad), (0, 0)))


def _unpack_small(packed, depth, d, heads, n_rel):
    per = 7 + heads
    out = {n: [] for n in SMALL}
    for l in range(depth):
        base = l * per
        for i, n in enumerate(("ln1_g", "ln1_b", "ln2_g", "ln2_b", "ln3_g", "ln3_b")):
            out[n].append(packed[base + i])
        out["sb_out_g"].append(packed[base + 6, :d // 2])
        out["ca_out_g"].append(packed[base + 6, d // 2:])
        out["rel_bias"].append(packed[base + 7:base + 7 + heads, :n_rel])
    return {n: jnp.stack(v) for n, v in out.items()}


def kernel(x, ffn1_w_gate, ffn1_w_up, ffn1_w_down, ln1_g, ln1_b, w_in, rel_bias, sb_out_g, ca_out_g, w_out, ln2_g, ln2_b, ffn2_w_gate, ffn2_w_up, ffn2_w_down, ln3_g, ln3_b, loss_target, m_ffn1_w_gate, m_ffn1_w_up, m_ffn1_w_down, m_ln1_g, m_ln1_b, m_w_in, m_rel_bias, m_sb_out_g, m_ca_out_g, m_w_out, m_ln2_g, m_ln2_b, m_ffn2_w_gate, m_ffn2_w_up, m_ffn2_w_down, m_ln3_g, m_ln3_b, v_ffn1_w_gate, v_ffn1_w_up, v_ffn1_w_down, v_ln1_g, v_ln1_b, v_w_in, v_rel_bias, v_sb_out_g, v_ca_out_g, v_w_out, v_ln2_g, v_ln2_b, v_ffn2_w_gate, v_ffn2_w_up, v_ffn2_w_down, v_ln3_g, v_ln3_b):
    w = dict(ffn1_w_gate=ffn1_w_gate, ffn1_w_up=ffn1_w_up, ffn1_w_down=ffn1_w_down, ln1_g=ln1_g, ln1_b=ln1_b, w_in=w_in, rel_bias=rel_bias, sb_out_g=sb_out_g, ca_out_g=ca_out_g, w_out=w_out, ln2_g=ln2_g, ln2_b=ln2_b, ffn2_w_gate=ffn2_w_gate, ffn2_w_up=ffn2_w_up, ffn2_w_down=ffn2_w_down, ln3_g=ln3_g, ln3_b=ln3_b)
    mom = dict(ffn1_w_gate=m_ffn1_w_gate, ffn1_w_up=m_ffn1_w_up, ffn1_w_down=m_ffn1_w_down, ln1_g=m_ln1_g, ln1_b=m_ln1_b, w_in=m_w_in, rel_bias=m_rel_bias, sb_out_g=m_sb_out_g, ca_out_g=m_ca_out_g, w_out=m_w_out, ln2_g=m_ln2_g, ln2_b=m_ln2_b, ffn2_w_gate=m_ffn2_w_gate, ffn2_w_up=m_ffn2_w_up, ffn2_w_down=m_ffn2_w_down, ln3_g=m_ln3_g, ln3_b=m_ln3_b)
    var = dict(ffn1_w_gate=v_ffn1_w_gate, ffn1_w_up=v_ffn1_w_up, ffn1_w_down=v_ffn1_w_down, ln1_g=v_ln1_g, ln1_b=v_ln1_b, w_in=v_w_in, rel_bias=v_rel_bias, sb_out_g=v_sb_out_g, ca_out_g=v_ca_out_g, w_out=v_w_out, ln2_g=v_ln2_g, ln2_b=v_ln2_b, ffn2_w_gate=v_ffn2_w_gate, ffn2_w_up=v_ffn2_w_up, ffn2_w_down=v_ffn2_w_down, ln3_g=v_ln3_g, ln3_b=v_ln3_b)

    bl, seq, d = x.shape
    t = bl * seq
    depth = w_in.shape[0]
    dg = d // 2
    heads = dg // HEAD_DIM
    alpha = (2 * depth) ** 0.25
    n_rel = rel_bias.shape[-1]

    chip = (2 * lax.axis_index("x") + lax.axis_index("y")).astype(jnp.int32).reshape(1)
    bufs = list(_gather_weights([_cast_into_slot(w[n], chip) for n in BIG], 0))
    with_sb = [BIG.index(n) for n in ("w_in", "ffn2_w_gate", "ffn2_w_up", "ffn2_w_down")]
    with_ca = [k for k in range(len(BIG)) if k not in with_sb]

    qd = dg // 2
    shards = 4
    ffn_terms = [(0, s, None, s) for s in range(shards)]
    out_terms = [(i, None, (k * qd, (k + 1) * qd), 2 * i + k) for i in range(2) for k in range(2)]
    bias_all, bias_vjp = jax.vjp(jax.vmap(_bias_expand), rel_bias)

    xs = x.reshape(t, d)
    saved = []
    for l in range(depth):
        lw = {n: _LayerOf((b, l)) for n, b in zip(BIG, bufs)}
        row = lambda a: a[l][None, :]
        g1, u1, h1 = _ffn_up(xs, lw["ffn1_w_gate"], lw["ffn1_w_up"])
        x1, xh1, rs1 = _proj_ln("ffn_down_ln", [h1], lw["ffn1_w_down"], ffn_terms, xs, row(ln1_g), row(ln1_b),
                                FFN_RESIDUAL, alpha)
        qkv = _qkv_proj(x1, lw["w_in"])
        if l + 1 < depth:
            o_sb, m_sb, c_sb, new = _sb_fwd(qkv, bl, seq, dg, row(sb_out_g), [bufs[k] for k in with_sb], l + 1)
            for k, b in zip(with_sb, new):
                bufs[k] = b
            o_ca, m_ca, new = _ca_fwd(qkv, bias_all[l], bl, seq, dg, row(ca_out_g), [bufs[k] for k in with_ca], l + 1)
            for k, b in zip(with_ca, new):
                bufs[k] = b
            lw = {n: _LayerOf((b, l)) for n, b in zip(BIG, bufs)}
        else:
            o_sb, m_sb, c_sb = _sb_fwd(qkv, bl, seq, dg, row(sb_out_g))
            o_ca, m_ca = _ca_fwd(qkv, bias_all[l], bl, seq, dg, row(ca_out_g))
        x2, xh2, rs2 = _proj_ln("attn_out_ln", [m_sb, m_ca], lw["w_out"], out_terms, x1, row(ln2_g), row(ln2_b),
                                1.0, alpha)
        g2, u2, h2 = _ffn_up(x2, lw["ffn2_w_gate"], lw["ffn2_w_up"])
        x3, xh3, rs3 = _proj_ln("ffn_down_ln", [h2], lw["ffn2_w_down"], ffn_terms, x2, row(ln3_g), row(ln3_b),
                                FFN_RESIDUAL, alpha)
        saved.append(dict(xin=xs, g1=g1, u1=u1, h1=h1, x1=x1, xh1=xh1, rs1=rs1, qkv=qkv, o_sb=o_sb, m_sb=m_sb,
                          c_sb=c_sb, o_ca=o_ca, m_ca=m_ca, x2=x2, xh2=xh2, rs2=rs2, g2=g2, u2=u2, h2=h2, xh3=xh3,
                          rs3=rs3))
        xs = x3

    last = saved[-1]
    dr3, drs3, pg, pb, sq = _loss_grad(xs, loss_target.reshape(t, d), last["xh3"], last["rs3"],
                                       ln3_g[depth - 1][None, :], FFN_RESIDUAL)
    loss = lax.psum(0.5 * jnp.sum(sq) / d, ("x", "y", "c"))

    big_grads = {n: [None] * depth for n in BIG}
    small_grads = {n: [None] * depth for n in SMALL}
    s8 = lambda a: jnp.sum(a, axis=0)
    wd_ = 3 * d // shards
    in_pieces = [_shard_cols(6, dg, s, wd_) for s in range(shards)]
    dbias = [None] * depth
    chip_sums, arrived = [None] * depth, [None] * depth
    for l in reversed(range(depth)):
        sv = saved[l]
        lw = {n: _LayerOf((b, l)) for n, b in zip(BIG, bufs)}
        row = lambda a: a[l][None, :]
        small_grads["ln3_g"][l], small_grads["ln3_b"][l] = s8(pg), s8(pb)
        dgt, dut = _swiglu_bwd(drs3, lw["ffn2_w_down"], sv["g2"], sv["u2"])
        dr2, drs2, pg, pb = _ffn_dx(dgt, dut, lw["ffn2_w_gate"], lw["ffn2_w_up"], dr3, alpha,
                                    (sv["xh2"], sv["rs2"], row(ln2_g), 1.0))
        small_grads["ln2_g"][l], small_grads["ln2_b"][l] = s8(pg), s8(pb)
        big_grads["ffn2_w_down"][l] = _wgrad("wgrad_down", [sv["h2"]], [drs3], shards, _lshape(lw["ffn2_w_down"])[1:],
                                             lambda a, s: a[0][s], lambda b, s: b[0][...])
        big_grads["ffn2_w_gate"][l] = _wgrad("wgrad_up", [sv["x2"]], [dgt], shards, _lshape(lw["ffn2_w_gate"])[1:],
                                             lambda a, s: a[0][...], lambda b, s: b[0][s])
        big_grads["ffn2_w_up"][l] = _wgrad("wgrad_up", [sv["x2"]], [dut], shards, _lshape(lw["ffn2_w_up"])[1:],
                                           lambda a, s: a[0][...], lambda b, s: b[0][s])
        do_sb, do_ca, pgs, pgc = _attn_dmix(drs2, lw["w_out"], sv["o_sb"], sv["o_ca"], row(sb_out_g), row(ca_out_g))
        small_grads["sb_out_g"][l], small_grads["ca_out_g"][l] = s8(pgs), s8(pgc)
        big_grads["w_out"][l] = _wgrad(
            "wgrad_out", [sv["m_sb"], sv["m_ca"]], [drs2], shards, (qd, d),
            lambda a, s: a[s // 2][:, (s % 2) * qd:(s % 2 + 1) * qd], lambda b, s: b[0][...])
        if l + 1 < depth:
            dqa, dka, dva, theirs = _sb_bwd(sv["qkv"], sv["c_sb"], do_sb, bl, seq, dg, partial)
            chip_sum = _add_pair(partial, theirs)
            dqb, dkb, dvb, dbias[l], arrived[l + 1] = _ca_bwd(sv["qkv"], bias_all[l], sv["o_ca"], do_ca, bl, seq, dg,
                                                             chip_sum)
            chip_sums[l + 1] = chip_sum
        else:
            dqa, dka, dva = _sb_bwd(sv["qkv"], sv["c_sb"], do_sb, bl, seq, dg)
            dqb, dkb, dvb, dbias[l] = _ca_bwd(sv["qkv"], bias_all[l], sv["o_ca"], do_ca, bl, seq, dg)
        dqkv = [dqa, dka, dva, dqb, dkb, dvb]
        dr1, drs1, pg, pb = _attn_dx(dqkv, lw["w_in"], dr2, alpha, (sv["xh1"], sv["rs1"], row(ln1_g), FFN_RESIDUAL))
        small_grads["ln1_g"][l], small_grads["ln1_b"][l] = s8(pg), s8(pb)
        big_grads["w_in"][l] = _wgrad(
            "wgrad_in", [sv["x1"]], dqkv, shards, (d, wd_), lambda a, s: a[0][...],
            lambda b, s: jnp.concatenate([b[gi][:, lo:hi] for gi, lo, hi in in_pieces[s]], axis=1))
        dgt, dut = _swiglu_bwd(drs1, lw["ffn1_w_down"], sv["g1"], sv["u1"])
        if l > 0:
            prev = saved[l - 1]
            dr3, drs3, pg, pb = _ffn_dx(dgt, dut, lw["ffn1_w_gate"], lw["ffn1_w_up"], dr1, alpha,
                                        (prev["xh3"], prev["rs3"], ln3_g[l - 1][None, :], FFN_RESIDUAL))
        else:
            grad_x = _ffn_dx(dgt, dut, lw["ffn1_w_gate"], lw["ffn1_w_up"], dr1, alpha, None).reshape(bl, seq, d)
        big_grads["ffn1_w_down"][l] = _wgrad("wgrad_down", [sv["h1"]], [drs1], shards, _lshape(lw["ffn1_w_down"])[1:],
                                             lambda a, s: a[0][s], lambda b, s: b[0][...])
        big_grads["ffn1_w_gate"][l] = _wgrad("wgrad_up", [sv["xin"]], [dgt], shards, _lshape(lw["ffn1_w_gate"])[1:],
                                             lambda a, s: a[0][...], lambda b, s: b[0][s])
        big_grads["ffn1_w_up"][l] = _wgrad("wgrad_up", [sv["xin"]], [dut], shards, _lshape(lw["ffn1_w_up"])[1:],
                                           lambda a, s: a[0][...], lambda b, s: b[0][s])
        partial = [big_grads[n][l][None] for n in BIG]
    chip_sums[0] = _add_pair(partial, _swap_halves(partial))
    arrived[0] = _scatter_chips(chip_sums[0])
    rel_grads = bias_vjp(jnp.stack(dbias))[0]
    for l in range(depth):
        small_grads["rel_bias"][l] = rel_grads[l]

    halves = []
    for lo_k in range(0, len(BIG), 4):
        part = None
        for l in range(depth):
            part = _add_chips(chip_sums[l][lo_k:lo_k + 4], arrived[l][lo_k:lo_k + 4], part, l, depth)
        halves += list(part)
    joined = dict(zip(BIG, _join_halves(halves)))
    packed = _allreduce_small(_pack_small({n: jnp.stack(small_grads[n]) for n in SMALL}, depth, d))

    grads, delta, new_m, new_v = {}, {}, {}, {}
    for n in BIG:
        grads[n], delta[n], new_m[n], new_v[n] = _adamw(w[n], joined[n], mom[n], var[n])
    pw, pm, pv = (_pack_small({n: src[n] for n in SMALL}, depth, d) for src in (w, mom, var))
    sg, sd, sm, sv_ = _adamw(pw, packed, pm, pv)
    for dst, src in ((grads, sg), (delta, sd), (new_m, sm), (new_v, sv_)):
        dst.update(_unpack_small(src, depth, d, heads, n_rel))

    return (loss, grad_x, *[grads[n] for n in ORDER], *[delta[n] for n in ORDER],
            *[new_m[n] for n in ORDER], *[new_v[n] for n in ORDER])
```

```python
import functools
import math

import jax
import jax.numpy as jnp
from jax import lax
from jax.experimental import pallas as pl
from jax.experimental.pallas import tpu as pltpu

F32 = jnp.float32
BF = jnp.bfloat16

CHUNK = 64
N_PREV_CHUNKS = 8
BAND = CHUNK * (N_PREV_CHUNKS + 1)
MAX_REL = 128
HEAD_DIM = 64
FFN_RESIDUAL = 0.5
LN_EPS = 1e-5
RMS_EPS = 1e-6
ADAM_LR = 0.001
ADAM_B1 = 0.9
ADAM_B2 = 0.999
ADAM_EPS = 1e-08
ADAM_WD = 0.01
ADAM_STEP = 10

LANES = 128
ATT_TILE = 4 * CHUNK
SB_TQ = 512
SB_TK = 256
SB_DEAD = -105.0
SB_UNSEEN = -1e30
NEG_BIAS = -1e30
VMEM_LIMIT = 56 * 1024 * 1024
ROW_TILE = 512
MESH_IDS = pl.DeviceIdType.MESH

_NT = (((1,), (1,)), ((), ()))
_TN = (((0,), (0,)), ((), ()))


def _dot(a, b):
    return jnp.dot(a, b, preferred_element_type=F32)


def _dot_nt(a, b):
    return lax.dot_general(a, b, _NT, preferred_element_type=F32)


def _dot_tn(a, b):
    return lax.dot_general(a, b, _TN, preferred_element_type=F32)


def _tile(n, pref):
    t = min(n, pref)
    while n % t:
        t //= 2
    return t


def _cparams(sem):
    return pltpu.CompilerParams(dimension_semantics=sem, vmem_limit_bytes=VMEM_LIMIT)


def _sds(shape, dtype):
    return jax.ShapeDtypeStruct(tuple(shape), dtype)


def _sigmoid(v):
    return pl.reciprocal(1.0 + jnp.exp(-v), approx=True)


def _rows8(v):
    tm, c = v.shape
    return v.reshape(tm // 8, 8, c).sum(axis=0)


class _LayerOf(tuple):
    pass


def _lshape(item):
    return item[0].shape[1:] if isinstance(item, _LayerOf) else item.shape


def _accumulate(m, ref, part):
    @pl.when(m == 0)
    def _():
        ref[...] = part

    @pl.when(m > 0)
    def _():
        ref[...] += part


def _ln_bwd_tail(m, dy, xh_ref, rs_ref, g_ref, scale, dr_ref, drs_ref, dg_ref, db_ref):
    xhv = xh_ref[...]
    dxh = dy * g_ref[...]
    m1 = jnp.mean(dxh, axis=-1, keepdims=True)
    m2 = jnp.mean(dxh * xhv, axis=-1, keepdims=True)
    dr = rs_ref[...] * (dxh - m1 - xhv * m2)
    dr_ref[...] = dr
    drs_ref[...] = (scale * dr).astype(BF)
    _accumulate(m, dg_ref, _rows8(dy * xhv))
    _accumulate(m, db_ref, _rows8(dy))


def _rowcall(name, body, n_rows, tm, row_ins, res_ins, row_outs, acc_outs=()):
    def rspec(shape):
        if len(shape) == 2:
            return pl.BlockSpec((tm, shape[1]), lambda m: (m, 0))
        return pl.BlockSpec((shape[0], tm, shape[2]), lambda m: (0, m, 0))

    def cspec(shape, single=False):
        zeros = (0,) * len(shape)
        mode = pl.Buffered(1) if single else None
        return pl.BlockSpec(tuple(shape), lambda m: zeros, pipeline_mode=mode)

    def split(items, mk):
        arrs, specs = [], []
        for it in items:
            if isinstance(it, _LayerOf):
                arr, l = it
                tail = (0,) * (arr.ndim - 1)
                arrs.append(arr)
                specs.append(pl.BlockSpec((None,) + arr.shape[1:], functools.partial(lambda m, l, tail: (l,) + tail, l=l, tail=tail),
                                          pipeline_mode=pl.Buffered(1)))
            elif isinstance(it, tuple):
                arrs.append(it[0])
                specs.append(it[1])
            else:
                arrs.append(it)
                specs.append(mk(it.shape))
        return arrs, specs

    ra, rs = split(row_ins, rspec)
    ca, cs = split(res_ins, functools.partial(cspec, single=True))
    out_specs = [rspec(s.shape) for s in row_outs] + [cspec(s.shape) for s in acc_outs]
    sem = ("arbitrary",) if acc_outs else ("parallel",)
    return pl.pallas_call(
        body, name=name, grid=(n_rows // tm,), in_specs=rs + cs, out_specs=out_specs,
        out_shape=list(row_outs) + list(acc_outs), compiler_params=_cparams(sem),
    )(*ra, *ca)


def _ffn_up(x, wg, wu):
    t, _ = x.shape
    s_n, _, fs = _lshape(wg)
    tm = _tile(t, ROW_TILE)

    def body(x_ref, wg_ref, wu_ref, g_ref, u_ref, h_ref):
        xb = x_ref[...].astype(BF)
        for s in range(s_n):
            g = _dot(xb, wg_ref[s])
            u = _dot(xb, wu_ref[s])
            g_ref[s] = g.astype(BF)
            u_ref[s] = u.astype(BF)
            h_ref[s] = (g * _sigmoid(g) * u).astype(BF)

    out = _sds((s_n, t, fs), BF)
    return _rowcall("ffn_up", body, t, tm, [x], [wg, wu], [out, out, out])


def _proj_ln(name, a_list, w, terms, xres, gain, bias, scale, alpha):
    t, d = xres.shape
    tm = _tile(t, ROW_TILE)
    na = len(a_list)

    def body(*refs):
        a_refs = refs[:na]
        x_ref, w_ref, g_ref, bb_ref = refs[na:na + 4]
        y_ref, xh_ref, rs_ref = refs[na + 4:]
        acc = None
        for i, s, cols, sw in terms:
            a = a_refs[i][s] if cols is None else a_refs[i][:, cols[0]:cols[1]]
            p = _dot(a.astype(BF), w_ref[sw])
            acc = p if acc is None else acc + p
        if scale != 1.0:
            acc = scale * acc
        r = alpha * x_ref[...] + acc
        mu = jnp.mean(r, axis=-1, keepdims=True)
        xc = r - mu
        var = jnp.mean(xc * xc, axis=-1, keepdims=True)
        rstd = lax.rsqrt(var + LN_EPS)
        xh = xc * rstd
        y_ref[...] = xh * g_ref[...] + bb_ref[...]
        xh_ref[...] = xh
        rs_ref[...] = rstd

    outs = [_sds((t, d), F32), _sds((t, d), F32), _sds((t, 1), F32)]
    return _rowcall(name, body, t, tm, list(a_list) + [xres], [w, gain, bias], outs)


def _shard_cols(groups, dg, s, width):
    pieces, lo = [], s * width
    while lo < (s + 1) * width:
        gi = lo // dg
        hi = min((gi + 1) * dg, (s + 1) * width)
        pieces.append((gi, lo - gi * dg, hi - gi * dg))
        lo = hi
    assert all(p[0] < groups for p in pieces)
    return pieces


def _qkv_proj(x, win):
    t, _ = x.shape
    s_n, _, wd_ = _lshape(win)
    tm = _tile(t, ROW_TILE)

    def body(x_ref, w_ref, o_ref):
        xb = x_ref[...].astype(BF)
        for s in range(s_n):
            o_ref[:, s * wd_:(s + 1) * wd_] = _dot(xb, w_ref[s]).astype(BF)

    return _rowcall("qkv_proj", body, t, tm, [x], [win], [_sds((t, s_n * wd_), BF)])[0]


def _dx_call(name, rows, res, matmul, resid, alpha, ln):
    t, d = resid.shape
    tm = _tile(t, ROW_TILE)
    nr, nc = len(rows), len(res)

    def body(*refs):
        r_refs, resid_ref = refs[:nr], refs[nr]
        if ln is None:
            c_refs, outs = refs[nr + 1:nr + 1 + nc], refs[nr + 1 + nc:]
            outs[0][...] = matmul(r_refs, c_refs) + alpha * resid_ref[...]
        else:
            xh_ref, rs_ref = refs[nr + 1:nr + 3]
            c_refs, g_ref = refs[nr + 3:nr + 3 + nc], refs[nr + 3 + nc]
            outs = refs[nr + 4 + nc:]
            dx = matmul(r_refs, c_refs) + alpha * resid_ref[...]
            _ln_bwd_tail(pl.program_id(0), dx, xh_ref, rs_ref, g_ref, ln[3], *outs)

    if ln is None:
        return _rowcall(name, body, t, tm, list(rows) + [resid], list(res), [_sds((t, d), F32)])[0]
    outs = [_sds((t, d), F32), _sds((t, d), BF)]
    accs = [_sds((8, d), F32), _sds((8, d), F32)]
    return _rowcall(name, body, t, tm, list(rows) + [resid, ln[0], ln[1]], list(res) + [ln[2]], outs, accs)


def _ffn_dx(dg_, du_, wg, wu, resid, alpha, ln):
    s_n = dg_.shape[0]

    def matmul(r, c):
        acc = None
        for i in range(2):
            for s in range(s_n):
                p = _dot_nt(r[i][s], c[i][s])
                acc = p if acc is None else acc + p
        return acc

    return _dx_call("ffn_dx", [dg_, du_], [wg, wu], matmul, resid, alpha, ln)


def _attn_dx(dqkv, win, resid, alpha, ln):
    s_n, _, wd_ = _lshape(win)
    dg = dqkv[0].shape[1]
    pieces = [_shard_cols(len(dqkv), dg, s, wd_) for s in range(s_n)]

    def matmul(r, c):
        acc = None
        for s in range(s_n):
            a = jnp.concatenate([r[gi][:, lo:hi] for gi, lo, hi in pieces[s]], axis=1)
            p = _dot_nt(a, c[0][s])
            acc = p if acc is None else acc + p
        return acc

    return _dx_call("attn_dx", list(dqkv), [win], matmul, resid, alpha, ln)


def _head_rms_bwd(dm, ov, gain, lo):
    ss = ov * ov
    s0 = jnp.sum(jnp.where(lo, ss, 0.0), axis=1, keepdims=True)
    s1 = jnp.sum(jnp.where(lo, 0.0, ss), axis=1, keepdims=True)
    r = jnp.where(lo, lax.rsqrt(s0 / HEAD_DIM + RMS_EPS), lax.rsqrt(s1 / HEAD_DIM + RMS_EPS))
    n = ov * r
    dn = dm * gain
    tt = dn * n
    t0 = jnp.sum(jnp.where(lo, tt, 0.0), axis=1, keepdims=True) / HEAD_DIM
    t1 = jnp.sum(jnp.where(lo, 0.0, tt), axis=1, keepdims=True) / HEAD_DIM
    return r * (dn - n * jnp.where(lo, t0, t1)), dm * n


def _attn_dmix(drs, wout, o_sb, o_ca, g_sb, g_ca):
    t, dg = o_sb.shape
    s_n, qd, _ = _lshape(wout)
    tm = _tile(t, ROW_TILE)
    per = dg // qd

    def body(d_ref, osb_ref, oca_ref, w_ref, gs_ref, gc_ref, dsb_ref, dca_ref, pgs_ref, pgc_ref):
        m = pl.program_id(0)
        dv = d_ref[...]
        lo = lax.broadcasted_iota(jnp.int32, (1, LANES), 1) < HEAD_DIM
        for grp, (o_ref, g_ref, do_ref, pg_ref) in enumerate(((osb_ref, gs_ref, dsb_ref, pgs_ref),
                                                               (oca_ref, gc_ref, dca_ref, pgc_ref))):
            dm = jnp.concatenate([_dot_nt(dv, w_ref[grp * per + k]) for k in range(per)], axis=1)
            parts = []
            for gi in range(dg // LANES):
                sl = slice(gi * LANES, (gi + 1) * LANES)
                do, pg = _head_rms_bwd(dm[:, sl], o_ref[:, sl], g_ref[:, sl], lo)
                do_ref[:, sl] = do.astype(BF)
                parts.append(_rows8(pg))
            _accumulate(m, pg_ref, jnp.concatenate(parts, axis=1))

    outs = [_sds((t, dg), BF), _sds((t, dg), BF)]
    accs = [_sds((8, dg), F32), _sds((8, dg), F32)]
    return _rowcall("attn_dmix", body, t, tm, [drs, o_sb, o_ca], [wout, g_sb, g_ca], outs, accs)


def _swiglu_bwd(drs, wd, g, u):
    s_n, t, fs = g.shape
    tm = _tile(t, ROW_TILE)

    def body(d_ref, g_ref, u_ref, wd_ref, dg_ref, du_ref):
        dv = d_ref[...]
        rc = _tile(tm, 128)
        for s in range(s_n):
            dh_all = _dot_nt(dv, wd_ref[s])
            for r0 in range(0, tm, rc):
                dh = dh_all[r0:r0 + rc]
                gv = g_ref[s, r0:r0 + rc, :].astype(F32)
                uv = u_ref[s, r0:r0 + rc, :].astype(F32)
                sg = _sigmoid(gv)
                gs = gv * sg
                dg_ref[s, r0:r0 + rc, :] = (dh * uv * (sg + gs * (1.0 - sg))).astype(BF)
                du_ref[s, r0:r0 + rc, :] = (dh * gs).astype(BF)

    out = _sds((s_n, t, fs), BF)
    return _rowcall("swiglu_bwd", body, t, tm, [drs, g, u], [wd], [out, out])


def _loss_grad(y, target, xh, rstd, gain, scale):
    t, d = y.shape
    tm = _tile(t, ROW_TILE)

    def body(y_ref, t_ref, xh_ref, rs_ref, g_ref, dr_ref, drs_ref, dg_ref, db_ref, sq_ref):
        m = pl.program_id(0)
        diff = y_ref[...] - t_ref[...]
        _ln_bwd_tail(m, diff / d, xh_ref, rs_ref, g_ref, scale, dr_ref, drs_ref, dg_ref, db_ref)
        _accumulate(m, sq_ref, _rows8(diff * diff))

    outs = [_sds((t, d), F32), _sds((t, d), BF)]
    accs = [_sds((8, d), F32)] * 3
    return _rowcall("loss_grad", body, t, tm, [y, target, xh, rstd], [gain], outs, accs)


def _adamw(w, g, m, v):
    shape = w.shape
    c = shape[-1]
    rows = math.prod(shape[:-1])
    tm = _tile(rows, 512)

    def body(w_ref, g_ref, m_ref, v_ref, go_ref, d_ref, nm_ref, nv_ref):
        gv = g_ref[...]
        go_ref[...] = gv
        nm = ADAM_B1 * m_ref[...] + (1.0 - ADAM_B1) * gv
        nv = ADAM_B2 * v_ref[...] + (1.0 - ADAM_B2) * (gv * gv)
        m_hat = nm / (1.0 - ADAM_B1 ** ADAM_STEP)
        v_hat = nv / (1.0 - ADAM_B2 ** ADAM_STEP)
        d_ref[...] = -ADAM_LR * (m_hat / (jnp.sqrt(v_hat) + ADAM_EPS) + ADAM_WD * w_ref[...])
        nm_ref[...] = nm
        nv_ref[...] = nv

    flat = [a.reshape(rows, c) for a in (w, g, m, v)]
    out = _sds((rows, c), F32)
    res = _rowcall("adamw", body, rows, tm, flat, [], [out, out, out, out])
    return tuple(r.reshape(shape) for r in res)


def _wgrad(name, a_list, b_list, out_sub, sub_shape, a_get, b_get):
    t = a_list[0].shape[-2]
    tk = _tile(t, 1024)
    na, nb = len(a_list), len(b_list)
    nk = t // tk

    def tspec(shape):
        if len(shape) == 2:
            return pl.BlockSpec((tk, shape[1]), lambda k: (k, 0))
        return pl.BlockSpec((shape[0], tk, shape[2]), lambda k: (0, k, 0))

    def body(*refs):
        a_refs = refs[:na]
        b_refs = refs[na:na + nb]
        o_ref = refs[na + nb]
        acc_ref = refs[na + nb + 1]
        k = pl.program_id(0)

        @pl.when(k == 0)
        def _():
            acc_ref[...] = jnp.zeros_like(acc_ref)

        for s in range(out_sub):
            acc_ref[s] += _dot_tn(a_get(a_refs, s).astype(BF), b_get(b_refs, s).astype(BF))

        @pl.when(k == nk - 1)
        def _():
            o_ref[...] = acc_ref[...].astype(BF)

    full = (out_sub,) + tuple(sub_shape)
    return pl.pallas_call(
        body, name=name, grid=(nk,), in_specs=[tspec(a.shape) for a in a_list] + [tspec(b.shape) for b in b_list],
        out_specs=pl.BlockSpec(full, lambda k: (0, 0, 0)), out_shape=_sds(full, BF),
        scratch_shapes=[pltpu.VMEM(full, F32)], compiler_params=_cparams(("arbitrary",)),
    )(*a_list, *b_list)


def _head_masks(x):
    lo = lax.broadcasted_iota(jnp.int32, (1, LANES), 1) < HEAD_DIM
    zero = jnp.zeros_like(x)
    return lo, (jnp.where(lo, x, zero), jnp.where(lo, zero, x))


def _logistic_parts(z):
    nz = -z
    t = jnp.exp(jnp.minimum(z, nz))
    one_t = 1.0 + t
    return jnp.minimum(nz, 0.0) - jnp.log(one_t), t, one_t


def _split2(x):
    hi = x.astype(BF)
    lo = (x - hi.astype(F32)).astype(BF)
    return jnp.concatenate([hi, lo], axis=1)


def _tri2(cond):
    tri = jnp.where(cond, 1.0, 0.0).astype(BF)
    return jnp.concatenate([tri, tri], axis=0)


def _sb_specs(bl, seq, p_n, col0, tq):
    nq = seq // tq
    q_spec = pl.BlockSpec((tq, LANES), lambda b, p, i: (b * nq + i, col0 + p))
    k_spec = pl.BlockSpec((seq, LANES), lambda b, p, i: (b, col0 + p_n + p))
    v_spec = pl.BlockSpec((seq, LANES), lambda b, p, i: (b, col0 + 2 * p_n + p))
    tile_spec = pl.BlockSpec((tq, LANES), lambda b, p, i: (b * nq + i, p))
    seq_spec = pl.BlockSpec((seq, LANES), lambda b, p, i: (b, p))
    g_spec = pl.BlockSpec((1, LANES), lambda b, p, i: (0, p))
    return nq, q_spec, k_spec, v_spec, tile_spec, seq_spec, g_spec


def _rms_out(acc, lo, gain):
    ss = acc * acc
    s0 = jnp.sum(jnp.where(lo, ss, 0.0), axis=1, keepdims=True)
    s1 = jnp.sum(jnp.where(lo, 0.0, ss), axis=1, keepdims=True)
    r = jnp.where(lo, lax.rsqrt(s0 / HEAD_DIM + RMS_EPS), lax.rsqrt(s1 / HEAD_DIM + RMS_EPS))
    return acc * r * gain


def _sb_fwd(qkv, bl, seq, dg, gain, bufs=(), layer=None):
    t = bl * seq
    p_n = dg // LANES
    tq, tk = min(SB_TQ, seq), SB_TK
    ratio = tq // tk
    scale = 1.0 / math.sqrt(HEAD_DIM)
    nq, q_spec, k_spec, v_spec, tile_spec, _, g_spec = _sb_specs(bl, seq, p_n, 0, tq)
    assert seq // tk <= LANES
    nt = len(bufs)

    def body(q_ref, k_ref, v_ref, g_ref, *rest):
        o_ref, m_ref, c_ref = rest[nt:nt + 3]
        i = pl.program_id(2)
        if nt:
            start, finish = _gather_plan(rest[nt + 3:2 * nt + 3], [b.shape[2] for b in bufs], rest[2 * nt + 3],
                                         rest[2 * nt + 4], layer)
            grid_step = (pl.program_id(0) * p_n + pl.program_id(1)) * nq + i
            pl.when(grid_step == 0)(start)
        row = lax.broadcasted_iota(jnp.int32, (tq, tk), 0)
        col = lax.broadcasted_iota(jnp.int32, (tq, tk), 1)
        krow = lax.broadcasted_iota(jnp.int32, (tk, tk), 0)
        kcol = lax.broadcasted_iota(jnp.int32, (tk, tk), 1)
        tri2 = _tri2(krow > kcol)
        lo, qh = _head_masks(q_ref[...] * jnp.asarray(scale, BF))
        lane = lax.broadcasted_iota(jnp.int32, (1, LANES), 1)

        def step(j, carry, diag):
            acc, c0, c1, s0, s1 = carry
            cars, seen = (c0, c1), (s0, s1)
            mask = None if diag is None else row > col + diag
            off = pl.multiple_of(j * tk, tk)
            kj = k_ref[pl.ds(off, tk), :]
            vj = v_ref[pl.ds(off, tk), :]
            zs = [_dot_nt(qh[h], kj) for h in range(2)]
            lks = [_logistic_parts(z)[0] for z in zs]
            if mask is not None:
                lks = [jnp.where(mask, lk, 0.0) for lk in lks]
            sins = [_dot(_split2(lk), tri2) for lk in lks]
            ws = []
            for h in range(2):
                w = jnp.exp((lks[h] + zs[h]) + (sins[h] + cars[h]))
                ws.append((w if mask is None else jnp.where(mask, w, 0.0)).astype(BF))
            pvs = [_dot(w, vj) for w in ws]
            new_seen = [jnp.where(lane == j, cars[h], seen[h]) for h in range(2)]
            new_cars = [cars[h] + jnp.sum(lks[h], axis=1, keepdims=True) for h in range(2)]
            return acc + jnp.where(lo, pvs[0], pvs[1]), new_cars[0], new_cars[1], new_seen[0], new_seen[1]

        def corner_step(j):
            top = tq - tk
            mask = krow > kcol
            kj = k_ref[pl.ds(pl.multiple_of(j * tk, tk), tk), :]
            vj = v_ref[pl.ds(pl.multiple_of(j * tk, tk), tk), :]
            zs = [_dot_nt(qh[h][top:], kj) for h in range(2)]
            lks = [jnp.where(mask, _logistic_parts(z)[0], 0.0) for z in zs]
            sins = [_dot(_split2(lk), tri2) for lk in lks]
            ws = [jnp.where(mask, jnp.exp((lks[h] + zs[h]) + sins[h]), 0.0).astype(BF) for h in range(2)]
            pvs = [_dot(w, vj) for w in ws]
            pad = lambda low: jnp.concatenate([jnp.zeros((top,) + low.shape[1:], F32), low], axis=0)
            cars = [pad(jnp.sum(lk, axis=1, keepdims=True)) for lk in lks]
            seen = jnp.where(lane == j, 0.0, jnp.full((tq, LANES), SB_UNSEEN, F32))
            return pad(jnp.where(lo, pvs[0], pvs[1])), cars[0], cars[1], seen, seen

        if ratio == 2:
            carry = step(ratio * i, corner_step(ratio * i + 1), 0)
        else:
            zc = jnp.zeros((tq, 1), F32)
            unseen = jnp.full((tq, LANES), SB_UNSEEN, F32)
            carry = (jnp.zeros((tq, LANES), F32), zc, zc, unseen, unseen)
            for m in reversed(range(ratio)):
                carry = step(ratio * i + m, carry, m * tk)

        def alive(state):
            j, c = state
            return jnp.logical_and(j >= 0, jnp.max(jnp.maximum(c[1], c[2])) >= SB_DEAD)

        _, carry = lax.while_loop(alive, lambda st: (st[0] - 1, step(st[0], st[1], None)), (ratio * i - 1, carry))
        acc = carry[0]
        o_ref[...] = acc
        m_ref[...] = _rms_out(acc, lo, g_ref[...]).astype(BF)
        c_ref[:, :LANES] = carry[3]
        c_ref[:, LANES:] = carry[4]
        if nt:
            pl.when(grid_step == bl * p_n * nq - 1)(finish)

    car_spec = pl.BlockSpec((tq, 2 * LANES), lambda b, p, i: (b * nq + i, p))
    outs = [_sds((t, dg), F32), _sds((t, dg), BF), _sds((t, 2 * dg), F32)]
    if not nt:
        return pl.pallas_call(
            body, name="sb_fwd", grid=(bl, p_n, nq), in_specs=[q_spec, k_spec, v_spec, g_spec],
            out_specs=[tile_spec, tile_spec, car_spec], out_shape=outs,
            compiler_params=_cparams(("parallel", "parallel", "parallel")),
        )(qkv, qkv, qkv, gain)
    res = pl.pallas_call(
        body, name="sb_fwd_gather", grid=(bl, p_n, nq), in_specs=[q_spec, k_spec, v_spec, g_spec] + [_ANY] * nt,
        out_specs=[tile_spec, tile_spec, car_spec] + [_ANY] * nt,
        out_shape=outs + [_sds(b.shape, b.dtype) for b in bufs],
        input_output_aliases={4 + k: 3 + k for k in range(nt)},
        scratch_shapes=[pltpu.SemaphoreType.DMA((6 * nt,)), pltpu.SemaphoreType.DMA((6 * nt,))],
        compiler_params=_cparams(("arbitrary", "arbitrary", "arbitrary")),
    )(qkv, qkv, qkv, gain, *bufs)
    return res[0], res[1], res[2], list(res[3:])


def _sb_bwd(qkv, cars, do, bl, seq, dg, grads=()):
    t = bl * seq
    p_n = dg // LANES
    tq, tk = min(SB_TQ, seq), SB_TK
    ratio = tq // tk
    scale = 1.0 / math.sqrt(HEAD_DIM)
    nq, q_spec, k_spec, v_spec, tile_spec, seq_spec, _ = _sb_specs(bl, seq, p_n, 0, tq)
    car_spec = pl.BlockSpec((tq, 2 * LANES), lambda b, p, i: (b * nq + i, p))
    nt = len(grads)

    def body(q_ref, k_ref, v_ref, c_ref, do_ref, *rest):
        dq_ref, dk_ref, dv_ref = rest[nt:nt + 3]
        dk_acc, dv_acc = rest[2 * nt + 3:2 * nt + 5]
        i = pl.program_id(2)
        if nt:
            start, finish = _swap_plan(rest[:nt], rest[nt + 3:2 * nt + 3], [g.shape[2] for g in grads],
                                       rest[2 * nt + 5], rest[2 * nt + 6])
            grid_step = (pl.program_id(0) * p_n + pl.program_id(1)) * nq + i
            pl.when(grid_step == 0)(start)

        @pl.when(i == 0)
        def _():
            dk_acc[...] = jnp.zeros_like(dk_acc)
            dv_acc[...] = jnp.zeros_like(dv_acc)

        row = lax.broadcasted_iota(jnp.int32, (tq, tk), 0)
        col = lax.broadcasted_iota(jnp.int32, (tq, tk), 1)
        krow = lax.broadcasted_iota(jnp.int32, (tk, tk), 0)
        kcol = lax.broadcasted_iota(jnp.int32, (tk, tk), 1)
        tri2 = _tri2(krow > kcol)
        tri_pre = jnp.where(krow < kcol, 1.0, 0.0).astype(BF)
        lane = lax.broadcasted_iota(jnp.int32, (1, LANES), 1)
        sc = jnp.asarray(scale, BF)
        qs = q_ref[...] * sc
        dob = do_ref[...]
        lo, qh = _head_masks(qs)
        _, doh = _head_masks(dob)
        seen = (c_ref[:, :LANES], c_ref[:, LANES:])

        def step(j, carry, diag):
            dq, e0, e1 = carry
            ecars = (e0, e1)
            mask = None if diag is None else row > col + diag
            off = pl.multiple_of(j * tk, tk)
            kj = k_ref[pl.ds(off, tk), :]
            vj = v_ref[pl.ds(off, tk), :]
            zs = [_dot_nt(qh[h], kj) for h in range(2)]
            dws = [_dot_nt(doh[h], vj) for h in range(2)]
            parts = [_logistic_parts(z) for z in zs]
            lks = [p[0] if mask is None else jnp.where(mask, p[0], 0.0) for p in parts]
            sins = [_dot(_split2(lk), tri2) for lk in lks]
            ws, es = [], []
            for h in range(2):
                car = jnp.sum(jnp.where(lane == j, seen[h], 0.0), axis=1, keepdims=True)
                w = jnp.exp((lks[h] + zs[h]) + (sins[h] + car))
                w = w if mask is None else jnp.where(mask, w, 0.0)
                ws.append(w.astype(BF))
                es.append(dws[h] * w)
            pres = [_dot(e.astype(BF), tri_pre) for e in es]
            dzs = []
            for h in range(2):
                t_, e = parts[h][1], es[h]
                before = pres[h] + ecars[h]
                dz = pl.reciprocal(parts[h][2], approx=True) * jnp.where(zs[h] > 0.0, e * t_ - before, e - t_ * before)
                dzs.append((dz if mask is None else jnp.where(mask, dz, 0.0)).astype(BF))
            kjs = kj * sc
            dqs = [_dot(dz, kjs) for dz in dzs]
            dks = [_dot_tn(dz, qs) for dz in dzs]
            dvs = [_dot_tn(w, dob) for w in ws]
            dk_acc[pl.ds(off, tk), :] += jnp.where(lo, dks[0], dks[1])
            dv_acc[pl.ds(off, tk), :] += jnp.where(lo, dvs[0], dvs[1])
            new_e = [ecars[h] + jnp.sum(es[h], axis=1, keepdims=True) for h in range(2)]
            return dq + jnp.where(lo, dqs[0], dqs[1]), new_e[0], new_e[1]

        reach = jnp.max(jnp.maximum(seen[0], seen[1]), axis=0, keepdims=True)
        skipped = jnp.logical_and(reach < SB_DEAD, lane < ratio * i)
        first = jnp.minimum(jnp.sum(jnp.where(skipped, 1, 0)).astype(jnp.int32), ratio * i)
        zc = jnp.zeros((tq, 1), F32)
        def corner_step(j, carry):
            dq, e0, e1 = carry
            top = tq - tk
            mask = krow > kcol
            off = pl.multiple_of(j * tk, tk)
            kj = k_ref[pl.ds(off, tk), :]
            vj = v_ref[pl.ds(off, tk), :]
            zs = [_dot_nt(qh[h][top:], kj) for h in range(2)]
            dws = [_dot_nt(doh[h][top:], vj) for h in range(2)]
            parts = [_logistic_parts(z) for z in zs]
            lks = [jnp.where(mask, p[0], 0.0) for p in parts]
            sins = [_dot(_split2(lk), tri2) for lk in lks]
            ws, es = [], []
            for h in range(2):
                car = jnp.sum(jnp.where(lane == j, seen[h][top:], 0.0), axis=1, keepdims=True)
                w = jnp.where(mask, jnp.exp((lks[h] + zs[h]) + (sins[h] + car)), 0.0)
                ws.append(w.astype(BF))
                es.append(dws[h] * w)
            pres = [_dot(e.astype(BF), tri_pre) for e in es]
            dzs = []
            for h, ecar in enumerate((e0, e1)):
                t_, e = parts[h][1], es[h]
                before = pres[h] + ecar[top:]
                dz = pl.reciprocal(parts[h][2], approx=True) * jnp.where(zs[h] > 0.0, e * t_ - before, e - t_ * before)
                dzs.append(jnp.where(mask, dz, 0.0).astype(BF))
            kjs = kj * sc
            dqs = [_dot(dz, kjs) for dz in dzs]
            dks = [_dot_tn(dz, qs[top:]) for dz in dzs]
            dvs = [_dot_tn(w, dob[top:]) for w in ws]
            dk_acc[pl.ds(off, tk), :] += jnp.where(lo, dks[0], dks[1])
            dv_acc[pl.ds(off, tk), :] += jnp.where(lo, dvs[0], dvs[1])
            return dq + jnp.concatenate([jnp.zeros((top, LANES), F32), jnp.where(lo, dqs[0], dqs[1])], axis=0)

        carry = lax.fori_loop(first, ratio * i, lambda j, c: step(j, c, None), (jnp.zeros((tq, LANES), F32), zc, zc))
        if ratio == 2:
            dq = corner_step(ratio * i + 1, step(ratio * i, carry, 0))
        else:
            for m in range(ratio):
                carry = step(ratio * i + m, carry, m * tk)
            dq = carry[0]
        dq_ref[...] = dq.astype(BF)

        @pl.when(i == nq - 1)
        def _():
            dk_ref[...] = dk_acc[...].astype(BF)
            dv_ref[...] = dv_acc[...].astype(BF)

        if nt:
            pl.when(grid_step == bl * p_n * nq - 1)(finish)

    out = _sds((t, dg), BF)
    acc = [pltpu.VMEM((seq, LANES), F32), pltpu.VMEM((seq, LANES), F32)]
    if not nt:
        return pl.pallas_call(
            body, name="sb_bwd", grid=(bl, p_n, nq),
            in_specs=[q_spec, k_spec, v_spec, car_spec, tile_spec],
            out_specs=[tile_spec, seq_spec, seq_spec], out_shape=[out, out, out],
            scratch_shapes=acc, compiler_params=_cparams(("parallel", "parallel", "arbitrary")),
        )(qkv, qkv, qkv, cars, do)
    res = pl.pallas_call(
        body, name="sb_bwd_swap", grid=(bl, p_n, nq),
        in_specs=[q_spec, k_spec, v_spec, car_spec, tile_spec] + [_ANY] * nt,
        out_specs=[tile_spec, seq_spec, seq_spec] + [_ANY] * nt,
        out_shape=[out, out, out] + _swap_shapes(grads),
        scratch_shapes=acc + [pltpu.SemaphoreType.DMA((nt,)), pltpu.SemaphoreType.DMA((nt,))],
        compiler_params=_cparams(("arbitrary", "arbitrary", "arbitrary")),
    )(qkv, qkv, qkv, cars, do, *grads)
    return res[0], res[1], res[2], list(res[3:])


def _bias_expand(rb):
    h = rb.shape[0]
    pad = CHUNK * N_PREV_CHUNKS
    n_f = BAND + CHUNK - 1
    f = jnp.concatenate([rb[:, MAX_REL - (CHUNK - 1):2 * MAX_REL],
                         jnp.broadcast_to(rb[:, 2 * MAX_REL:], (h, pad + CHUNK - MAX_REL))], axis=1)
    fr = f[:, ::-1]
    skew = jnp.pad(jnp.broadcast_to(fr[:, None, :], (h, CHUNK, n_f)), ((0, 0), (0, 0), (0, 1)))
    skew = skew.reshape(h, CHUNK * (n_f + 1))[:, :CHUNK * n_f].reshape(h, CHUNK, n_f)
    table = skew[:, :, CHUNK - 1:CHUNK - 1 + BAND]
    n_c = ATT_TILE // CHUNK
    width = 3 * ATT_TILE
    rows = [jnp.pad(table, ((0, 0), (0, 0), (c * CHUNK, width - BAND - c * CHUNK)), constant_values=NEG_BIAS)
            for c in range(n_c)]
    return jnp.concatenate(rows, axis=1)


def _ca_scores(qh_h, k_ref, bias_ref, h, i, scale):
    tq = ATT_TILE
    ss, offs = [], []
    for kb in range(3):
        jb = i - 2 + kb
        off = pl.multiple_of(jnp.maximum(jb, 0) * tq, tq)
        s = _dot_nt(qh_h, k_ref[pl.ds(off, tq), :]) * scale + bias_ref[h, :, kb * tq:(kb + 1) * tq]
        ss.append(jnp.where(jb >= 0, s, NEG_BIAS))
        offs.append(off)
    m = jnp.max(jnp.maximum(jnp.maximum(ss[0], ss[1]), ss[2]), axis=1, keepdims=True)
    ps = [jnp.exp(s - m) for s in ss]
    den = jnp.sum(ps[0] + ps[1] + ps[2], axis=1, keepdims=True)
    inv = 1.0 / den
    return [p * inv for p in ps], offs


def _ca_fwd(qkv, bias4, bl, seq, dg, gain, bufs=(), layer=None):
    t = bl * seq
    p_n = dg // LANES
    tq = ATT_TILE
    scale = 1.0 / math.sqrt(HEAD_DIM)
    nq, q_spec, k_spec, v_spec, tile_spec, _, g_spec = _sb_specs(bl, seq, p_n, 3 * p_n, tq)
    b_spec = pl.BlockSpec((2, tq, 3 * tq), lambda b, p, i: (p, 0, 0))
    nt = len(bufs)

    def body(q_ref, k_ref, v_ref, bias_ref, g_ref, *rest):
        o_ref, m_ref = rest[nt:nt + 2]
        i = pl.program_id(2)
        if nt:
            start, finish = _gather_plan(rest[nt + 2:2 * nt + 2], [b.shape[2] for b in bufs], rest[2 * nt + 2],
                                         rest[2 * nt + 3], layer)
            step = (pl.program_id(0) * p_n + pl.program_id(1)) * nq + i
            pl.when(step == 0)(start)
        lo, qh = _head_masks(q_ref[...])
        outs = []
        for h in range(2):
            ps, offs = _ca_scores(qh[h], k_ref, bias_ref, h, i, scale)
            acc = None
            for kb in range(3):
                pv = _dot(ps[kb].astype(BF), v_ref[pl.ds(offs[kb], tq), :])
                acc = pv if acc is None else acc + pv
            outs.append(acc)
        acc = jnp.where(lo, outs[0], outs[1])
        o_ref[...] = acc
        m_ref[...] = _rms_out(acc, lo, g_ref[...]).astype(BF)
        if nt:
            pl.when(step == bl * p_n * nq - 1)(finish)

    if not nt:
        return pl.pallas_call(
            body, name="ca_fwd", grid=(bl, p_n, nq), in_specs=[q_spec, k_spec, v_spec, b_spec, g_spec],
            out_specs=[tile_spec, tile_spec], out_shape=[_sds((t, dg), F32), _sds((t, dg), BF)],
            compiler_params=_cparams(("parallel", "parallel", "parallel")),
        )(qkv, qkv, qkv, bias4, gain)
    res = pl.pallas_call(
        body, name="ca_fwd_gather", grid=(bl, p_n, nq),
        in_specs=[q_spec, k_spec, v_spec, b_spec, g_spec] + [_ANY] * nt,
        out_specs=[tile_spec, tile_spec] + [_ANY] * nt,
        out_shape=[_sds((t, dg), F32), _sds((t, dg), BF)] + [_sds(b.shape, b.dtype) for b in bufs],
        input_output_aliases={5 + k: 2 + k for k in range(nt)},
        scratch_shapes=[pltpu.SemaphoreType.DMA((6 * nt,)), pltpu.SemaphoreType.DMA((6 * nt,))],
        compiler_params=_cparams(("arbitrary", "arbitrary", "arbitrary")),
    )(qkv, qkv, qkv, bias4, gain, *bufs)
    return res[0], res[1], list(res[2:])


def _ca_bwd(qkv, bias4, o, do, bl, seq, dg, psums=()):
    t = bl * seq
    p_n = dg // LANES
    tq = ATT_TILE
    scale = 1.0 / math.sqrt(HEAD_DIM)
    nq = seq // tq
    c0 = 3 * p_n
    q_spec = pl.BlockSpec((tq, LANES), lambda p, b, i: (b * nq + i, c0 + p))
    k_spec = pl.BlockSpec((seq, LANES), lambda p, b, i: (b, c0 + p_n + p))
    v_spec = pl.BlockSpec((seq, LANES), lambda p, b, i: (b, c0 + 2 * p_n + p))
    tile_spec = pl.BlockSpec((tq, LANES), lambda p, b, i: (b * nq + i, p))
    seq_spec = pl.BlockSpec((seq, LANES), lambda p, b, i: (b, p))
    b_spec = pl.BlockSpec((2, tq, 3 * tq), lambda p, b, i: (p, 0, 0))

    nt = len(psums)

    def body(q_ref, k_ref, v_ref, bias_ref, o_ref, do_ref, *rest):
        dq_ref, dk_ref, dv_ref, db_ref = rest[nt:nt + 4]
        dk_acc, dv_acc = rest[2 * nt + 4:2 * nt + 6]
        b = pl.program_id(1)
        i = pl.program_id(2)
        if nt:
            start, finish = _scatter_plan(rest[:nt], rest[nt + 4:2 * nt + 4], rest[2 * nt + 6], rest[2 * nt + 7])
            step = (pl.program_id(0) * bl + b) * nq + i
            pl.when(step == 0)(start)

        @pl.when(i == 0)
        def _():
            dk_acc[...] = jnp.zeros_like(dk_acc)
            dv_acc[...] = jnp.zeros_like(dv_acc)

        @pl.when((i == 0) & (b == 0))
        def _():
            db_ref[...] = jnp.zeros_like(db_ref)

        q = q_ref[...]
        dob = do_ref[...]
        lo, qh = _head_masks(q)
        _, doh = _head_masks(dob)
        prod = dob.astype(F32) * o_ref[...]
        delta = (jnp.sum(jnp.where(lo, prod, 0.0), axis=1, keepdims=True),
                 jnp.sum(jnp.where(lo, 0.0, prod), axis=1, keepdims=True))
        dqs = []
        for h in range(2):
            ps, offs = _ca_scores(qh[h], k_ref, bias_ref, h, i, scale)
            dq = None
            dks, dvs = [], []
            for kb in range(3):
                dp = _dot_nt(doh[h], v_ref[pl.ds(offs[kb], tq), :])
                ds = ps[kb] * (dp - delta[h])
                db_ref[h, :, kb * tq:(kb + 1) * tq] += ds
                dsb = (ds * scale).astype(BF)
                part = _dot(dsb, k_ref[pl.ds(offs[kb], tq), :])
                dq = part if dq is None else dq + part
                dks.append(_dot_tn(dsb, q))
                dvs.append(_dot_tn(ps[kb].astype(BF), dob))
            dqs.append(dq)
            hm = lo if h == 0 else jnp.logical_not(lo)
            for kb in range(3):
                dk_acc[pl.ds(offs[kb], tq), :] += jnp.where(hm, dks[kb], 0.0)
                dv_acc[pl.ds(offs[kb], tq), :] += jnp.where(hm, dvs[kb], 0.0)
        dq_ref[...] = jnp.where(lo, dqs[0], dqs[1]).astype(BF)

        @pl.when(i == nq - 1)
        def _():
            dk_ref[...] = dk_acc[...].astype(BF)
            dv_ref[...] = dv_acc[...].astype(BF)

        if nt:
            pl.when(step == p_n * bl * nq - 1)(finish)

    out = _sds((t, dg), BF)
    acc = [pltpu.VMEM((seq, LANES), F32), pltpu.VMEM((seq, LANES), F32)]
    if not nt:
        return pl.pallas_call(
            body, name="ca_bwd", grid=(p_n, bl, nq),
            in_specs=[q_spec, k_spec, v_spec, b_spec, tile_spec, tile_spec],
            out_specs=[tile_spec, seq_spec, seq_spec, b_spec], out_shape=[out, out, out, _sds(bias4.shape, F32)],
            scratch_shapes=acc, compiler_params=_cparams(("parallel", "arbitrary", "arbitrary")),
        )(qkv, qkv, qkv, bias4, o, do)
    res = pl.pallas_call(
        body, name="ca_bwd_scatter", grid=(p_n, bl, nq),
        in_specs=[q_spec, k_spec, v_spec, b_spec, tile_spec, tile_spec] + [_ANY] * nt,
        out_specs=[tile_spec, seq_spec, seq_spec, b_spec] + [_ANY] * nt,
        out_shape=[out, out, out, _sds(bias4.shape, F32)] + _scatter_shapes(psums),
        scratch_shapes=acc + [pltpu.SemaphoreType.DMA((3 * nt,)), pltpu.SemaphoreType.DMA((3 * nt,))],
        compiler_params=_cparams(("arbitrary", "arbitrary", "arbitrary")),
    )(qkv, qkv, qkv, bias4, o, do, *psums)
    return res[0], res[1], res[2], res[3], list(res[4:])


_ANY = pl.BlockSpec(memory_space=pl.ANY)


def _mesh_pos():
    x, y, c = lax.axis_index("x"), lax.axis_index("y"), lax.axis_index("c")
    chips = [(1 - x, y), (x, 1 - y), (1 - x, 1 - y)]
    return x, y, c, chips


def _half(ref_rows, c):
    return pl.ds(c * (ref_rows // 2), ref_rows // 2)


def _cast_into_slot(w, slot):
    l_n, r, c_n = w.shape
    tr = _tile(r, 512)

    def body(s_ref, w_ref, o_ref):
        o_ref[...] = w_ref[...].astype(BF)

    return pl.pallas_call(
        body, name="cast_into_slot", out_shape=_sds((l_n, 4, r, c_n), BF),
        grid_spec=pltpu.PrefetchScalarGridSpec(
            num_scalar_prefetch=1, grid=(l_n, r // tr),
            in_specs=[pl.BlockSpec((None, tr, c_n), lambda l, m, s: (l, m, 0))],
            out_specs=pl.BlockSpec((None, None, tr, c_n), lambda l, m, s: (l, s[0], m, 0))),
        compiler_params=_cparams(("parallel", "parallel")),
    )(slot, w)


def _gather_plan(outs, n_rows, send_sems, recv_sems, layer):
    nt = len(outs)

    def blk(t, s, cc):
        return outs[t].at[pl.ds(layer, 1), s, _half(n_rows[t], cc)]

    def copy(k, ref, to):
        return pltpu.make_async_remote_copy(src_ref=ref, dst_ref=ref, send_sem=send_sems.at[k],
                                            recv_sem=recv_sems.at[k], device_id=to, device_id_type=MESH_IDS)

    def over_ici():
        x, y, c, chips = _mesh_pos()
        return [copy(j * nt + t, blk(t, 2 * x + y, c), (px, py, c)) for j, (px, py) in enumerate(chips) for t in range(nt)]

    def start():
        for cp in over_ici():
            cp.start()

    def finish():
        x, y, c, chips = _mesh_pos()
        passed = []
        for j, (px, py) in enumerate(chips):
            for t in range(nt):
                got = blk(t, 2 * px + py, c)
                copy(j * nt + t, got, (px, py, c)).wait_recv()
                relay = copy(3 * nt + j * nt + t, got, (x, y, 1 - c))
                relay.start()
                passed.append(relay)
        for j, (px, py) in enumerate(chips):
            for t in range(nt):
                copy(3 * nt + j * nt + t, blk(t, 2 * px + py, 1 - c), (x, y, 1 - c)).wait_recv()
        for cp in over_ici() + passed:
            cp.wait_send()

    return start, finish


def _gather_weights(bufs, layer):
    nt = len(bufs)

    def body(*refs):
        start, finish = _gather_plan(refs[nt:2 * nt], [b.shape[2] for b in bufs], refs[2 * nt], refs[2 * nt + 1], layer)
        start()
        finish()

    return pl.pallas_call(
        body, name="gather_weights", in_specs=[_ANY] * nt, out_specs=[_ANY] * nt,
        out_shape=[_sds(b.shape, b.dtype) for b in bufs], input_output_aliases={t: t for t in range(nt)},
        scratch_shapes=[pltpu.SemaphoreType.DMA((6 * nt,)), pltpu.SemaphoreType.DMA((6 * nt,))],
    )(*bufs)


def _swap_halves(grads):
    nt = len(grads)

    def body(*refs):
        start, finish = _swap_plan(refs[:nt], refs[nt:2 * nt], [g.shape[2] for g in grads], refs[2 * nt], refs[2 * nt + 1])
        start()
        finish()

    return pl.pallas_call(
        body, name="swap_halves", in_specs=[_ANY] * nt, out_specs=[_ANY] * nt, out_shape=_swap_shapes(grads),
        scratch_shapes=[pltpu.SemaphoreType.DMA((nt,)), pltpu.SemaphoreType.DMA((nt,))],
    )(*grads)


def _swap_shapes(grads):
    return [_sds(g.shape[:2] + (g.shape[2] // 2, g.shape[3]), g.dtype) for g in grads]


def _swap_plan(ins, theirs, n_rows, send_sems, recv_sems):
    def sends():
        x, y, c, _ = _mesh_pos()
        return [pltpu.make_async_remote_copy(
            src_ref=ins[t].at[:, :, _half(n_rows[t], 1 - c)], dst_ref=theirs[t], send_sem=send_sems.at[t],
            recv_sem=recv_sems.at[t], device_id=(x, y, 1 - c), device_id_type=MESH_IDS) for t in range(len(ins))]

    def start():
        for cp in sends():
            cp.start()

    def finish():
        for cp in sends():
            cp.wait_recv()
        for cp in sends():
            cp.wait_send()

    return start, finish


def _add_pair(mine, theirs):
    nt = len(mine)
    r2s = [th.shape[2] for th in theirs]

    def body(*refs):
        c = lax.axis_index("c")
        for k in range(nt):
            rows = pl.ds(pl.multiple_of(c * r2s[k], 16), r2s[k])
            refs[2 * nt + k][...] = (refs[k][rows, :].astype(F32) + refs[nt + k][...].astype(F32)).astype(BF)

    def spec(shape):
        return pl.BlockSpec((None, None) + tuple(shape[2:]), lambda s: (0, s, 0, 0))

    return pl.pallas_call(
        body, name="add_pair", out_shape=[_sds(th.shape, BF) for th in theirs], grid=(mine[0].shape[1],),
        in_specs=[spec(a.shape) for a in mine] + [spec(b.shape) for b in theirs],
        out_specs=[spec(b.shape) for b in theirs], compiler_params=_cparams(("parallel",)),
    )(*mine, *theirs)


def _scatter_chips(psums):
    nt = len(psums)

    def body(*refs):
        start, finish = _scatter_plan(refs[:nt], refs[nt:2 * nt], refs[2 * nt], refs[2 * nt + 1])
        start()
        finish()

    return pl.pallas_call(
        body, name="scatter_chips", in_specs=[_ANY] * nt, out_specs=[_ANY] * nt, out_shape=_scatter_shapes(psums),
        scratch_shapes=[pltpu.SemaphoreType.DMA((3 * nt,)), pltpu.SemaphoreType.DMA((3 * nt,))],
    )(*psums)


def _scatter_shapes(psums):
    return [_sds((3, p.shape[0]) + p.shape[2:], p.dtype) for p in psums]


def _scatter_plan(ins, got, send_sems, recv_sems):
    nt = len(ins)

    def sends():
        _, _, c, chips = _mesh_pos()
        return [pltpu.make_async_remote_copy(
            src_ref=ins[t].at[:, 2 * px + py], dst_ref=got[t].at[j], send_sem=send_sems.at[j * nt + t],
            recv_sem=recv_sems.at[j * nt + t], device_id=(px, py, c), device_id_type=MESH_IDS)
            for j, (px, py) in enumerate(chips) for t in range(nt)]

    def start():
        for cp in sends():
            cp.start()

    def finish():
        for cp in sends():
            cp.wait_recv()
        for cp in sends():
            cp.wait_send()

    return start, finish


def _add_chips(psums, gots, fulls, layer, depth):
    nt = len(psums)
    r2s = [p.shape[2] for p in psums]
    n_alias = nt if fulls is not None else 0

    def body(*refs):
        c = lax.axis_index("c")
        chip = 2 * lax.axis_index("x") + lax.axis_index("y")
        outs = refs[n_alias + 2 * nt:]
        for k in range(nt):
            acc = refs[n_alias + k][chip].astype(F32)
            for j in range(3):
                acc = acc + refs[n_alias + nt + k][j].astype(F32)
            outs[k][pl.ds(pl.multiple_of(c * r2s[k], 8), r2s[k]), :] = acc
            outs[k][pl.ds(pl.multiple_of((1 - c) * r2s[k], 8), r2s[k]), :] = jnp.zeros_like(acc)

    once = pl.Buffered(1)
    in_specs = [_ANY] * n_alias
    in_specs += [pl.BlockSpec((None,) + p.shape[1:], lambda i: (0, 0, 0, 0), pipeline_mode=once) for p in psums]
    in_specs += [pl.BlockSpec((3, None) + g.shape[2:], lambda i: (0, 0, 0, 0), pipeline_mode=once) for g in gots]
    out_shape = [_sds((depth, 2 * p.shape[2], p.shape[3]), F32) for p in psums]
    out_specs = [pl.BlockSpec((None,) + s.shape[1:], lambda i: (layer, 0, 0), pipeline_mode=once) for s in out_shape]
    return pl.pallas_call(
        body, name="add_chips", out_shape=out_shape, grid=(1,), in_specs=in_specs, out_specs=out_specs,
        input_output_aliases={k: k for k in range(n_alias)}, compiler_params=_cparams(("arbitrary",)),
    )(*(fulls or []), *psums, *gots)


def _join_halves(fulls):
    nt = len(fulls)

    def body(*refs):
        outs = refs[nt:2 * nt]
        send_sems, recv_sems = refs[2 * nt:]
        x, y, c, _ = _mesh_pos()

        def copy(t, cc):
            blk = outs[t].at[:, _half(fulls[t].shape[1], cc)]
            return pltpu.make_async_remote_copy(
                src_ref=blk, dst_ref=blk, send_sem=send_sems.at[t], recv_sem=recv_sems.at[t],
                device_id=(x, y, 1 - c), device_id_type=MESH_IDS)

        sends = [copy(t, c) for t in range(nt)]
        for cp in sends:
            cp.start()
        for t in range(nt):
            copy(t, 1 - c).wait_recv()
        for cp in sends:
            cp.wait_send()

    return pl.pallas_call(
        body, name="join_halves", in_specs=[_ANY] * nt, out_specs=[_ANY] * nt,
        out_shape=[_sds(f.shape, f.dtype) for f in fulls], input_output_aliases={t: t for t in range(nt)},
        scratch_shapes=[pltpu.SemaphoreType.DMA((nt,)), pltpu.SemaphoreType.DMA((nt,))],
    )(*fulls)


def _allreduce_small(part):
    r, d = part.shape
    flips = [(fx, fy, fc) for fx in (0, 1) for fy in (0, 1) for fc in (0, 1) if fx + fy + fc]

    def body(p_ref, o_ref, buf, send_sems, recv_sems):
        x, y, c, _ = _mesh_pos()
        me = 4 * x + 2 * y + c
        buf[me] = p_ref[...]
        sends = []
        for k, (fx, fy, fc) in enumerate(flips):
            sends.append(pltpu.make_async_remote_copy(
                src_ref=p_ref, dst_ref=buf.at[me], send_sem=send_sems.at[k], recv_sem=recv_sems.at[k],
                device_id=(x ^ fx, y ^ fy, c ^ fc), device_id_type=MESH_IDS))
        for cp in sends:
            cp.start()
        for k, (fx, fy, fc) in enumerate(flips):
            peer = 4 * (x ^ fx) + 2 * (y ^ fy) + (c ^ fc)
            pltpu.make_async_remote_copy(
                src_ref=p_ref, dst_ref=buf.at[peer], send_sem=send_sems.at[k], recv_sem=recv_sems.at[k],
                device_id=(x ^ fx, y ^ fy, c ^ fc), device_id_type=MESH_IDS).wait_recv()
        acc = buf[0]
        for dev in range(1, 8):
            acc = acc + buf[dev]
        o_ref[...] = acc
        for cp in sends:
            cp.wait_send()

    vmem = pl.BlockSpec(memory_space=pltpu.VMEM)
    return pl.pallas_call(
        body, name="allreduce_small", in_specs=[vmem], out_specs=vmem, out_shape=_sds((r, d), F32),
        scratch_shapes=[pltpu.VMEM((8, r, d), F32), pltpu.SemaphoreType.DMA((7,)), pltpu.SemaphoreType.DMA((7,))],
    )(part)


BIG = ("ffn1_w_gate", "ffn1_w_up", "ffn1_w_down", "w_in", "w_out", "ffn2_w_gate", "ffn2_w_up", "ffn2_w_down")
SMALL = ("ln1_g", "ln1_b", "rel_bias", "sb_out_g", "ca_out_g", "ln2_g", "ln2_b", "ln3_g", "ln3_b")
ORDER = ("ffn1_w_gate", "ffn1_w_up", "ffn1_w_down", "ln1_g", "ln1_b", "w_in", "rel_bias", "sb_out_g", "ca_out_g",
         "w_out", "ln2_g", "ln2_b", "ffn2_w_gate", "ffn2_w_up", "ffn2_w_down", "ln3_g", "ln3_b")


def _pack_small(vals, depth, d):
    rows = []
    for l in range(depth):
        for n in ("ln1_g", "ln1_b", "ln2_g", "ln2_b", "ln3_g", "ln3_b"):
            rows.append(vals[n][l][None, :])
        rows.append(jnp.concatenate([vals["sb_out_g"][l], vals["ca_out_g"][l]])[None, :])
        rb = vals["rel_bias"][l]
        rows.append(jnp.pad(rb, ((0, 0), (0, d - rb.shape[1]))))
    packed = jnp.concatenate(rows, axis=0)
    pad = (-packed.shape[0]) % 8
    return jnp.pad(packed, ((0, pad), (0, 0)))


def _unpack_small(packed, depth, d, heads, n_rel):
    per = 7 + heads
    out = {n: [] for n in SMALL}
    for l in range(depth):
        base = l * per
        for i, n in enumerate(("ln1_g", "ln1_b", "ln2_g", "ln2_b", "ln3_g", "ln3_b")):
            out[n].append(packed[base + i])
        out["sb_out_g"].append(packed[base + 6, :d // 2])
        out["ca_out_g"].append(packed[base + 6, d // 2:])
        out["rel_bias"].append(packed[base + 7:base + 7 + heads, :n_rel])
    return {n: jnp.stack(v) for n, v in out.items()}


def kernel(x, ffn1_w_gate, ffn1_w_up, ffn1_w_down, ln1_g, ln1_b, w_in, rel_bias, sb_out_g, ca_out_g, w_out, ln2_g, ln2_b, ffn2_w_gate, ffn2_w_up, ffn2_w_down, ln3_g, ln3_b, loss_target, m_ffn1_w_gate, m_ffn1_w_up, m_ffn1_w_down, m_ln1_g, m_ln1_b, m_w_in, m_rel_bias, m_sb_out_g, m_ca_out_g, m_w_out, m_ln2_g, m_ln2_b, m_ffn2_w_gate, m_ffn2_w_up, m_ffn2_w_down, m_ln3_g, m_ln3_b, v_ffn1_w_gate, v_ffn1_w_up, v_ffn1_w_down, v_ln1_g, v_ln1_b, v_w_in, v_rel_bias, v_sb_out_g, v_ca_out_g, v_w_out, v_ln2_g, v_ln2_b, v_ffn2_w_gate, v_ffn2_w_up, v_ffn2_w_down, v_ln3_g, v_ln3_b):
    w = dict(ffn1_w_gate=ffn1_w_gate, ffn1_w_up=ffn1_w_up, ffn1_w_down=ffn1_w_down, ln1_g=ln1_g, ln1_b=ln1_b, w_in=w_in, rel_bias=rel_bias, sb_out_g=sb_out_g, ca_out_g=ca_out_g, w_out=w_out, ln2_g=ln2_g, ln2_b=ln2_b, ffn2_w_gate=ffn2_w_gate, ffn2_w_up=ffn2_w_up, ffn2_w_down=ffn2_w_down, ln3_g=ln3_g, ln3_b=ln3_b)
    mom = dict(ffn1_w_gate=m_ffn1_w_gate, ffn1_w_up=m_ffn1_w_up, ffn1_w_down=m_ffn1_w_down, ln1_g=m_ln1_g, ln1_b=m_ln1_b, w_in=m_w_in, rel_bias=m_rel_bias, sb_out_g=m_sb_out_g, ca_out_g=m_ca_out_g, w_out=m_w_out, ln2_g=m_ln2_g, ln2_b=m_ln2_b, ffn2_w_gate=m_ffn2_w_gate, ffn2_w_up=m_ffn2_w_up, ffn2_w_down=m_ffn2_w_down, ln3_g=m_ln3_g, ln3_b=m_ln3_b)
    var = dict(ffn1_w_gate=v_ffn1_w_gate, ffn1_w_up=v_ffn1_w_up, ffn1_w_down=v_ffn1_w_down, ln1_g=v_ln1_g, ln1_b=v_ln1_b, w_in=v_w_in, rel_bias=v_rel_bias, sb_out_g=v_sb_out_g, ca_out_g=v_ca_out_g, w_out=v_w_out, ln2_g=v_ln2_g, ln2_b=v_ln2_b, ffn2_w_gate=v_ffn2_w_gate, ffn2_w_up=v_ffn2_w_up, ffn2_w_down=v_ffn2_w_down, ln3_g=v_ln3_g, ln3_b=v_ln3_b)

    bl, seq, d = x.shape
    t = bl * seq
    depth = w_in.shape[0]
    dg = d // 2
    heads = dg // HEAD_DIM
    alpha = (2 * depth) ** 0.25
    n_rel = rel_bias.shape[-1]

    chip = (2 * lax.axis_index("x") + lax.axis_index("y")).astype(jnp.int32).reshape(1)
    bufs = list(_gather_weights([_cast_into_slot(w[n], chip) for n in BIG], 0))
    with_sb = [BIG.index(n) for n in ("w_in", "ffn2_w_gate", "ffn2_w_up", "ffn2_w_down")]
    with_ca = [k for k in range(len(BIG)) if k not in with_sb]

    qd = dg // 2
    shards = 4
    ffn_terms = [(0, s, None, s) for s in range(shards)]
    out_terms = [(i, None, (k * qd, (k + 1) * qd), 2 * i + k) for i in range(2) for k in range(2)]
    bias_all, bias_vjp = jax.vjp(jax.vmap(_bias_expand), rel_bias)

    xs = x.reshape(t, d)
    saved = []
    for l in range(depth):
        lw = {n: _LayerOf((b, l)) for n, b in zip(BIG, bufs)}
        row = lambda a: a[l][None, :]
        g1, u1, h1 = _ffn_up(xs, lw["ffn1_w_gate"], lw["ffn1_w_up"])
        x1, xh1, rs1 = _proj_ln("ffn_down_ln", [h1], lw["ffn1_w_down"], ffn_terms, xs, row(ln1_g), row(ln1_b),
                                FFN_RESIDUAL, alpha)
        qkv = _qkv_proj(x1, lw["w_in"])
        if l + 1 < depth:
            o_sb, m_sb, c_sb, new = _sb_fwd(qkv, bl, seq, dg, row(sb_out_g), [bufs[k] for k in with_sb], l + 1)
            for k, b in zip(with_sb, new):
                bufs[k] = b
            o_ca, m_ca, new = _ca_fwd(qkv, bias_all[l], bl, seq, dg, row(ca_out_g), [bufs[k] for k in with_ca], l + 1)
            for k, b in zip(with_ca, new):
                bufs[k] = b
            lw = {n: _LayerOf((b, l)) for n, b in zip(BIG, bufs)}
        else:
            o_sb, m_sb, c_sb = _sb_fwd(qkv, bl, seq, dg, row(sb_out_g))
            o_ca, m_ca = _ca_fwd(qkv, bias_all[l], bl, seq, dg, row(ca_out_g))
        x2, xh2, rs2 = _proj_ln("attn_out_ln", [m_sb, m_ca], lw["w_out"], out_terms, x1, row(ln2_g), row(ln2_b),
                                1.0, alpha)
        g2, u2, h2 = _ffn_up(x2, lw["ffn2_w_gate"], lw["ffn2_w_up"])
        x3, xh3, rs3 = _proj_ln("ffn_down_ln", [h2], lw["ffn2_w_down"], ffn_terms, x2, row(ln3_g), row(ln3_b),
                                FFN_RESIDUAL, alpha)
        saved.append(dict(xin=xs, g1=g1, u1=u1, h1=h1, x1=x1, xh1=xh1, rs1=rs1, qkv=qkv, o_sb=o_sb, m_sb=m_sb,
                          c_sb=c_sb, o_ca=o_ca, m_ca=m_ca, x2=x2, xh2=xh2, rs2=rs2, g2=g2, u2=u2, h2=h2, xh3=xh3,
                          rs3=rs3))
        xs = x3

    last = saved[-1]
    dr3, drs3, pg, pb, sq = _loss_grad(xs, loss_target.reshape(t, d), last["xh3"], last["rs3"],
                                       ln3_g[depth - 1][None, :], FFN_RESIDUAL)
    loss = lax.psum(0.5 * jnp.sum(sq) / d, ("x", "y", "c"))

    big_grads = {n: [None] * depth for n in BIG}
    small_grads = {n: [None] * depth for n in SMALL}
    s8 = lambda a: jnp.sum(a, axis=0)
    wd_ = 3 * d // shards
    in_pieces = [_shard_cols(6, dg, s, wd_) for s in range(shards)]
    dbias = [None] * depth
    chip_sums, arrived = [None] * depth, [None] * depth
    for l in reversed(range(depth)):
        sv = saved[l]
        lw = {n: _LayerOf((b, l)) for n, b in zip(BIG, bufs)}
        row = lambda a: a[l][None, :]
        small_grads["ln3_g"][l], small_grads["ln3_b"][l] = s8(pg), s8(pb)
        dgt, dut = _swiglu_bwd(drs3, lw["ffn2_w_down"], sv["g2"], sv["u2"])
        dr2, drs2, pg, pb = _ffn_dx(dgt, dut, lw["ffn2_w_gate"], lw["ffn2_w_up"], dr3, alpha,
                                    (sv["xh2"], sv["rs2"], row(ln2_g), 1.0))
        small_grads["ln2_g"][l], small_grads["ln2_b"][l] = s8(pg), s8(pb)
        big_grads["ffn2_w_down"][l] = _wgrad("wgrad_down", [sv["h2"]], [drs3], shards, _lshape(lw["ffn2_w_down"])[1:],
                                             lambda a, s: a[0][s], lambda b, s: b[0][...])
        big_grads["ffn2_w_gate"][l] = _wgrad("wgrad_up", [sv["x2"]], [dgt], shards, _lshape(lw["ffn2_w_gate"])[1:],
                                             lambda a, s: a[0][...], lambda b, s: b[0][s])
        big_grads["ffn2_w_up"][l] = _wgrad("wgrad_up", [sv["x2"]], [dut], shards, _lshape(lw["ffn2_w_up"])[1:],
                                           lambda a, s: a[0][...], lambda b, s: b[0][s])
        do_sb, do_ca, pgs, pgc = _attn_dmix(drs2, lw["w_out"], sv["o_sb"], sv["o_ca"], row(sb_out_g), row(ca_out_g))
        small_grads["sb_out_g"][l], small_grads["ca_out_g"][l] = s8(pgs), s8(pgc)
        big_grads["w_out"][l] = _wgrad(
            "wgrad_out", [sv["m_sb"], sv["m_ca"]], [drs2], shards, (qd, d),
            lambda a, s: a[s // 2][:, (s % 2) * qd:(s % 2 + 1) * qd], lambda b, s: b[0][...])
        if l + 1 < depth:
            dqa, dka, dva, theirs = _sb_bwd(sv["qkv"], sv["c_sb"], do_sb, bl, seq, dg, partial)
            chip_sum = _add_pair(partial, theirs)
            dqb, dkb, dvb, dbias[l], arrived[l + 1] = _ca_bwd(sv["qkv"], bias_all[l], sv["o_ca"], do_ca, bl, seq, dg,
                                                             chip_sum)
            chip_sums[l + 1] = chip_sum
        else:
            dqa, dka, dva = _sb_bwd(sv["qkv"], sv["c_sb"], do_sb, bl, seq, dg)
            dqb, dkb, dvb, dbias[l] = _ca_bwd(sv["qkv"], bias_all[l], sv["o_ca"], do_ca, bl, seq, dg)
        dqkv = [dqa, dka, dva, dqb, dkb, dvb]
        dr1, drs1, pg, pb = _attn_dx(dqkv, lw["w_in"], dr2, alpha, (sv["xh1"], sv["rs1"], row(ln1_g), FFN_RESIDUAL))
        small_grads["ln1_g"][l], small_grads["ln1_b"][l] = s8(pg), s8(pb)
        big_grads["w_in"][l] = _wgrad(
            "wgrad_in", [sv["x1"]], dqkv, shards, (d, wd_), lambda a, s: a[0][...],
            lambda b, s: jnp.concatenate([b[gi][:, lo:hi] for gi, lo, hi in in_pieces[s]], axis=1))
        dgt, dut = _swiglu_bwd(drs1, lw["ffn1_w_down"], sv["g1"], sv["u1"])
        if l > 0:
            prev = saved[l - 1]
            dr3, drs3, pg, pb = _ffn_dx(dgt, dut, lw["ffn1_w_gate"], lw["ffn1_w_up"], dr1, alpha,
                                        (prev["xh3"], prev["rs3"], ln3_g[l - 1][None, :], FFN_RESIDUAL))
        else:
            grad_x = _ffn_dx(dgt, dut, lw["ffn1_w_gate"], lw["ffn1_w_up"], dr1, alpha, None).reshape(bl, seq, d)
        big_grads["ffn1_w_down"][l] = _wgrad("wgrad_down", [sv["h1"]], [drs1], shards, _lshape(lw["ffn1_w_down"])[1:],
                                             lambda a, s: a[0][s], lambda b, s: b[0][...])
        big_grads["ffn1_w_gate"][l] = _wgrad("wgrad_up", [sv["xin"]], [dgt], shards, _lshape(lw["ffn1_w_gate"])[1:],
                                             lambda a, s: a[0][...], lambda b, s: b[0][s])
        big_grads["ffn1_w_up"][l] = _wgrad("wgrad_up", [sv["xin"]], [dut], shards, _lshape(lw["ffn1_w_up"])[1:],
                                           lambda a, s: a[0][...], lambda b, s: b[0][s])
        partial = [big_grads[n][l][None] for n in BIG]
    chip_sums[0] = _add_pair(partial, _swap_halves(partial))
    arrived[0] = _scatter_chips(chip_sums[0])
    rel_grads = bias_vjp(jnp.stack(dbias))[0]
    for l in range(depth):
        small_grads["rel_bias"][l] = rel_grads[l]

    halves = []
    for lo_k in range(0, len(BIG), 4):
        part = None
        for l in range(depth):
            part = _add_chips(chip_sums[l][lo_k:lo_k + 4], arrived[l][lo_k:lo_k + 4], part, l, depth)
        halves += list(part)
    joined = dict(zip(BIG, _join_halves(halves)))
    packed = _allreduce_small(_pack_small({n: jnp.stack(small_grads[n]) for n in SMALL}, depth, d))

    grads, delta, new_m, new_v = {}, {}, {}, {}
    for n in BIG:
        grads[n], delta[n], new_m[n], new_v[n] = _adamw(w[n], joined[n], mom[n], var[n])
    pw, pm, pv = (_pack_small({n: src[n] for n in SMALL}, depth, d) for src in (w, mom, var))
    sg, sd, sm, sv_ = _adamw(pw, packed, pm, pv)
    for dst, src in ((grads, sg), (delta, sd), (new_m, sm), (new_v, sv_)):
        dst.update(_unpack_small(src, depth, d, heads, n_rel))

    return (loss, grad_x, *[grads[n] for n in ORDER], *[delta[n] for n in ORDER],
            *[new_m[n] for n in ORDER], *[new_v[n] for n in ORDER])
```

```python
import functools
import math

import jax
import jax.numpy as jnp
from jax import lax
from jax.experimental import pallas as pl
from jax.experimental.pallas import tpu as pltpu

F32 = jnp.float32
BF = jnp.bfloat16

CHUNK = 64
N_PREV_CHUNKS = 8
BAND = CHUNK * (N_PREV_CHUNKS + 1)
MAX_REL = 128
HEAD_DIM = 64
FFN_RESIDUAL = 0.5
LN_EPS = 1e-5
RMS_EPS = 1e-6
ADAM_LR = 0.001
ADAM_B1 = 0.9
ADAM_B2 = 0.999
ADAM_EPS = 1e-08
ADAM_WD = 0.01
ADAM_STEP = 10

LANES = 128
ATT_TILE = 4 * CHUNK
SB_TQ = 512
SB_TK = 256
SB_DEAD = -105.0
SB_UNSEEN = -1e30
NEG_BIAS = -1e30
VMEM_LIMIT = 56 * 1024 * 1024
ROW_TILE = 512
MESH_IDS = pl.DeviceIdType.MESH

_NT = (((1,), (1,)), ((), ()))
_TN = (((0,), (0,)), ((), ()))


def _dot(a, b):
    return jnp.dot(a, b, preferred_element_type=F32)


def _dot_nt(a, b):
    return lax.dot_general(a, b, _NT, preferred_element_type=F32)


def _dot_tn(a, b):
    return lax.dot_general(a, b, _TN, preferred_element_type=F32)


def _tile(n, pref):
    t = min(n, pref)
    while n % t:
        t //= 2
    return t


def _cparams(sem):
    return pltpu.CompilerParams(dimension_semantics=sem, vmem_limit_bytes=VMEM_LIMIT)


def _sds(shape, dtype):
    return jax.ShapeDtypeStruct(tuple(shape), dtype)


def _sigmoid(v):
    return pl.reciprocal(1.0 + jnp.exp(-v), approx=True)


def _rows8(v):
    tm, c = v.shape
    return v.reshape(tm // 8, 8, c).sum(axis=0)


class _LayerOf(tuple):
    pass


def _lshape(item):
    return item[0].shape[1:] if isinstance(item, _LayerOf) else item.shape


def _accumulate(m, ref, part):
    @pl.when(m == 0)
    def _():
        ref[...] = part

    @pl.when(m > 0)
    def _():
        ref[...] += part


def _ln_bwd_tail(m, dy, xh_ref, rs_ref, g_ref, scale, dr_ref, drs_ref, dg_ref, db_ref):
    xhv = xh_ref[...]
    dxh = dy * g_ref[...]
    m1 = jnp.mean(dxh, axis=-1, keepdims=True)
    m2 = jnp.mean(dxh * xhv, axis=-1, keepdims=True)
    dr = rs_ref[...] * (dxh - m1 - xhv * m2)
    dr_ref[...] = dr
    drs_ref[...] = (scale * dr).astype(BF)
    _accumulate(m, dg_ref, _rows8(dy * xhv))
    _accumulate(m, db_ref, _rows8(dy))


def _rowcall(name, body, n_rows, tm, row_ins, res_ins, row_outs, acc_outs=()):
    def rspec(shape):
        if len(shape) == 2:
            return pl.BlockSpec((tm, shape[1]), lambda m: (m, 0))
        return pl.BlockSpec((shape[0], tm, shape[2]), lambda m: (0, m, 0))

    def cspec(shape, single=False):
        zeros = (0,) * len(shape)
        mode = pl.Buffered(1) if single else None
        return pl.BlockSpec(tuple(shape), lambda m: zeros, pipeline_mode=mode)

    def split(items, mk):
        arrs, specs = [], []
        for it in items:
            if isinstance(it, _LayerOf):
                arr, l = it
                tail = (0,) * (arr.ndim - 1)
                arrs.append(arr)
                specs.append(pl.BlockSpec((None,) + arr.shape[1:], functools.partial(lambda m, l, tail: (l,) + tail, l=l, tail=tail),
                                          pipeline_mode=pl.Buffered(1)))
            elif isinstance(it, tuple):
                arrs.append(it[0])
                specs.append(it[1])
            else:
                arrs.append(it)
                specs.append(mk(it.shape))
        return arrs, specs

    ra, rs = split(row_ins, rspec)
    ca, cs = split(res_ins, functools.partial(cspec, single=True))
    out_specs = [rspec(s.shape) for s in row_outs] + [cspec(s.shape) for s in acc_outs]
    sem = ("arbitrary",) if acc_outs else ("parallel",)
    return pl.pallas_call(
        body, name=name, grid=(n_rows // tm,), in_specs=rs + cs, out_specs=out_specs,
        out_shape=list(row_outs) + list(acc_outs), compiler_params=_cparams(sem),
    )(*ra, *ca)


def _ffn_up(x, wg, wu):
    t, _ = x.shape
    s_n, _, fs = _lshape(wg)
    tm = _tile(t, ROW_TILE)

    def body(x_ref, wg_ref, wu_ref, g_ref, u_ref, h_ref):
        xb = x_ref[...].astype(BF)
        for s in range(s_n):
            g = _dot(xb, wg_ref[s])
            u = _dot(xb, wu_ref[s])
            g_ref[s] = g.astype(BF)
            u_ref[s] = u.astype(BF)
            h_ref[s] = (g * _sigmoid(g) * u).astype(BF)

    out = _sds((s_n, t, fs), BF)
    return _rowcall("ffn_up", body, t, tm, [x], [wg, wu], [out, out, out])


def _proj_ln(name, a_list, w, terms, xres, gain, bias, scale, alpha):
    t, d = xres.shape
    tm = _tile(t, ROW_TILE)
    na = len(a_list)

    def body(*refs):
        a_refs = refs[:na]
        x_ref, w_ref, g_ref, bb_ref = refs[na:na + 4]
        y_ref, xh_ref, rs_ref = refs[na + 4:]
        acc = None
        for i, s, cols, sw in terms:
            a = a_refs[i][s] if cols is None else a_refs[i][:, cols[0]:cols[1]]
            p = _dot(a.astype(BF), w_ref[sw])
            acc = p if acc is None else acc + p
        if scale != 1.0:
            acc = scale * acc
        r = alpha * x_ref[...] + acc
        mu = jnp.mean(r, axis=-1, keepdims=True)
        xc = r - mu
        var = jnp.mean(xc * xc, axis=-1, keepdims=True)
        rstd = lax.rsqrt(var + LN_EPS)
        xh = xc * rstd
        y_ref[...] = xh * g_ref[...] + bb_ref[...]
        xh_ref[...] = xh
        rs_ref[...] = rstd

    outs = [_sds((t, d), F32), _sds((t, d), F32), _sds((t, 1), F32)]
    return _rowcall(name, body, t, tm, list(a_list) + [xres], [w, gain, bias], outs)


def _shard_cols(groups, dg, s, width):
    pieces, lo = [], s * width
    while lo < (s + 1) * width:
        gi = lo // dg
        hi = min((gi + 1) * dg, (s + 1) * width)
        pieces.append((gi, lo - gi * dg, hi - gi * dg))
        lo = hi
    assert all(p[0] < groups for p in pieces)
    return pieces


def _qkv_proj(x, win):
    t, _ = x.shape
    s_n, _, wd_ = _lshape(win)
    tm = _tile(t, ROW_TILE)

    def body(x_ref, w_ref, o_ref):
        xb = x_ref[...].astype(BF)
        for s in range(s_n):
            o_ref[:, s * wd_:(s + 1) * wd_] = _dot(xb, w_ref[s]).astype(BF)

    return _rowcall("qkv_proj", body, t, tm, [x], [win], [_sds((t, s_n * wd_), BF)])[0]


def _dx_call(name, rows, res, matmul, resid, alpha, ln):
    t, d = resid.shape
    tm = _tile(t, ROW_TILE)
    nr, nc = len(rows), len(res)

    def body(*refs):
        r_refs, resid_ref = refs[:nr], refs[nr]
        if ln is None:
            c_refs, outs = refs[nr + 1:nr + 1 + nc], refs[nr + 1 + nc:]
            outs[0][...] = matmul(r_refs, c_refs) + alpha * resid_ref[...]
        else:
            xh_ref, rs_ref = refs[nr + 1:nr + 3]
            c_refs, g_ref = refs[nr + 3:nr + 3 + nc], refs[nr + 3 + nc]
            outs = refs[nr + 4 + nc:]
            dx = matmul(r_refs, c_refs) + alpha * resid_ref[...]
            _ln_bwd_tail(pl.program_id(0), dx, xh_ref, rs_ref, g_ref, ln[3], *outs)

    if ln is None:
        return _rowcall(name, body, t, tm, list(rows) + [resid], list(res), [_sds((t, d), F32)])[0]
    outs = [_sds((t, d), F32), _sds((t, d), BF)]
    accs = [_sds((8, d), F32), _sds((8, d), F32)]
    return _rowcall(name, body, t, tm, list(rows) + [resid, ln[0], ln[1]], list(res) + [ln[2]], outs, accs)


def _ffn_dx(dg_, du_, wg, wu, resid, alpha, ln):
    s_n = dg_.shape[0]

    def matmul(r, c):
        acc = None
        for i in range(2):
            for s in range(s_n):
                p = _dot_nt(r[i][s], c[i][s])
                acc = p if acc is None else acc + p
        return acc

    return _dx_call("ffn_dx", [dg_, du_], [wg, wu], matmul, resid, alpha, ln)


def _attn_dx(dqkv, win, resid, alpha, ln):
    s_n, _, wd_ = _lshape(win)
    dg = dqkv[0].shape[1]
    pieces = [_shard_cols(len(dqkv), dg, s, wd_) for s in range(s_n)]

    def matmul(r, c):
        acc = None
        for s in range(s_n):
            a = jnp.concatenate([r[gi][:, lo:hi] for gi, lo, hi in pieces[s]], axis=1)
            p = _dot_nt(a, c[0][s])
            acc = p if acc is None else acc + p
        return acc

    return _dx_call("attn_dx", list(dqkv), [win], matmul, resid, alpha, ln)


def _head_rms_bwd(dm, ov, gain, lo):
    ss = ov * ov
    s0 = jnp.sum(jnp.where(lo, ss, 0.0), axis=1, keepdims=True)
    s1 = jnp.sum(jnp.where(lo, 0.0, ss), axis=1, keepdims=True)
    r = jnp.where(lo, lax.rsqrt(s0 / HEAD_DIM + RMS_EPS), lax.rsqrt(s1 / HEAD_DIM + RMS_EPS))
    n = ov * r
    dn = dm * gain
    tt = dn * n
    t0 = jnp.sum(jnp.where(lo, tt, 0.0), axis=1, keepdims=True) / HEAD_DIM
    t1 = jnp.sum(jnp.where(lo, 0.0, tt), axis=1, keepdims=True) / HEAD_DIM
    return r * (dn - n * jnp.where(lo, t0, t1)), dm * n


def _attn_dmix(drs, wout, o_sb, o_ca, g_sb, g_ca):
    t, dg = o_sb.shape
    s_n, qd, _ = _lshape(wout)
    tm = _tile(t, ROW_TILE)
    per = dg // qd

    def body(d_ref, osb_ref, oca_ref, w_ref, gs_ref, gc_ref, dsb_ref, dca_ref, pgs_ref, pgc_ref):
        m = pl.program_id(0)
        dv = d_ref[...]
        lo = lax.broadcasted_iota(jnp.int32, (1, LANES), 1) < HEAD_DIM
        for grp, (o_ref, g_ref, do_ref, pg_ref) in enumerate(((osb_ref, gs_ref, dsb_ref, pgs_ref),
                                                               (oca_ref, gc_ref, dca_ref, pgc_ref))):
            dm = jnp.concatenate([_dot_nt(dv, w_ref[grp * per + k]) for k in range(per)], axis=1)
            parts = []
            for gi in range(dg // LANES):
                sl = slice(gi * LANES, (gi + 1) * LANES)
                do, pg = _head_rms_bwd(dm[:, sl], o_ref[:, sl], g_ref[:, sl], lo)
                do_ref[:, sl] = do.astype(BF)
                parts.append(_rows8(pg))
            _accumulate(m, pg_ref, jnp.concatenate(parts, axis=1))

    outs = [_sds((t, dg), BF), _sds((t, dg), BF)]
    accs = [_sds((8, dg), F32), _sds((8, dg), F32)]
    return _rowcall("attn_dmix", body, t, tm, [drs, o_sb, o_ca], [wout, g_sb, g_ca], outs, accs)


def _swiglu_bwd(drs, wd, g, u):
    s_n, t, fs = g.shape
    tm = _tile(t, ROW_TILE)

    def body(d_ref, g_ref, u_ref, wd_ref, dg_ref, du_ref):
        dv = d_ref[...]
        rc = _tile(tm, 128)
        for s in range(s_n):
            dh_all = _dot_nt(dv, wd_ref[s])
            for r0 in range(0, tm, rc):
                dh = dh_all[r0:r0 + rc]
                gv = g_ref[s, r0:r0 + rc, :].astype(F32)
                uv = u_ref[s, r0:r0 + rc, :].astype(F32)
                sg = _sigmoid(gv)
                gs = gv * sg
                dg_ref[s, r0:r0 + rc, :] = (dh * uv * (sg + gs * (1.0 - sg))).astype(BF)
                du_ref[s, r0:r0 + rc, :] = (dh * gs).astype(BF)

    out = _sds((s_n, t, fs), BF)
    return _rowcall("swiglu_bwd", body, t, tm, [drs, g, u], [wd], [out, out])


def _loss_grad(y, target, xh, rstd, gain, scale):
    t, d = y.shape
    tm = _tile(t, ROW_TILE)

    def body(y_ref, t_ref, xh_ref, rs_ref, g_ref, dr_ref, drs_ref, dg_ref, db_ref, sq_ref):
        m = pl.program_id(0)
        diff = y_ref[...] - t_ref[...]
        _ln_bwd_tail(m, diff / d, xh_ref, rs_ref, g_ref, scale, dr_ref, drs_ref, dg_ref, db_ref)
        _accumulate(m, sq_ref, _rows8(diff * diff))

    outs = [_sds((t, d), F32), _sds((t, d), BF)]
    accs = [_sds((8, d), F32)] * 3
    return _rowcall("loss_grad", body, t, tm, [y, target, xh, rstd], [gain], outs, accs)


def _adamw(w, g, m, v):
    shape = w.shape
    c = shape[-1]
    rows = math.prod(shape[:-1])
    tm = _tile(rows, 512)

    def body(w_ref, g_ref, m_ref, v_ref, go_ref, d_ref, nm_ref, nv_ref):
        gv = g_ref[...]
        go_ref[...] = gv
        nm = ADAM_B1 * m_ref[...] + (1.0 - ADAM_B1) * gv
        nv = ADAM_B2 * v_ref[...] + (1.0 - ADAM_B2) * (gv * gv)
        m_hat = nm / (1.0 - ADAM_B1 ** ADAM_STEP)
        v_hat = nv / (1.0 - ADAM_B2 ** ADAM_STEP)
        d_ref[...] = -ADAM_LR * (m_hat / (jnp.sqrt(v_hat) + ADAM_EPS) + ADAM_WD * w_ref[...])
        nm_ref[...] = nm
        nv_ref[...] = nv

    flat = [a.reshape(rows, c) for a in (w, g, m, v)]
    out = _sds((rows, c), F32)
    res = _rowcall("adamw", body, rows, tm, flat, [], [out, out, out, out])
    return tuple(r.reshape(shape) for r in res)


def _wgrad(name, a_list, b_list, out_sub, sub_shape, a_get, b_get):
    t = a_list[0].shape[-2]
    tk = _tile(t, 1024)
    na, nb = len(a_list), len(b_list)
    nk = t // tk

    def tspec(shape):
        if len(shape) == 2:
            return pl.BlockSpec((tk, shape[1]), lambda k: (k, 0))
        return pl.BlockSpec((shape[0], tk, shape[2]), lambda k: (0, k, 0))

    def body(*refs):
        a_refs = refs[:na]
        b_refs = refs[na:na + nb]
        o_ref = refs[na + nb]
        acc_ref = refs[na + nb + 1]
        k = pl.program_id(0)

        @pl.when(k == 0)
        def _():
            acc_ref[...] = jnp.zeros_like(acc_ref)

        for s in range(out_sub):
            acc_ref[s] += _dot_tn(a_get(a_refs, s).astype(BF), b_get(b_refs, s).astype(BF))

        @pl.when(k == nk - 1)
        def _():
            o_ref[...] = acc_ref[...].astype(BF)

    full = (out_sub,) + tuple(sub_shape)
    return pl.pallas_call(
        body, name=name, grid=(nk,), in_specs=[tspec(a.shape) for a in a_list] + [tspec(b.shape) for b in b_list],
        out_specs=pl.BlockSpec(full, lambda k: (0, 0, 0)), out_shape=_sds(full, BF),
        scratch_shapes=[pltpu.VMEM(full, F32)], compiler_params=_cparams(("arbitrary",)),
    )(*a_list, *b_list)


def _head_masks(x):
    lo = lax.broadcasted_iota(jnp.int32, (1, LANES), 1) < HEAD_DIM
    zero = jnp.zeros_like(x)
    return lo, (jnp.where(lo, x, zero), jnp.where(lo, zero, x))


def _logistic_parts(z):
    nz = -z
    t = jnp.exp(jnp.minimum(z, nz))
    one_t = 1.0 + t
    return jnp.minimum(nz, 0.0) - jnp.log(one_t), t, one_t


def _split2(x):
    hi = x.astype(BF)
    lo = (x - hi.astype(F32)).astype(BF)
    return jnp.concatenate([hi, lo], axis=1)


def _tri2(cond):
    tri = jnp.where(cond, 1.0, 0.0).astype(BF)
    return jnp.concatenate([tri, tri], axis=0)


def _sb_specs(bl, seq, p_n, col0, tq):
    nq = seq // tq
    q_spec = pl.BlockSpec((tq, LANES), lambda b, p, i: (b * nq + i, col0 + p))
    k_spec = pl.BlockSpec((seq, LANES), lambda b, p, i: (b, col0 + p_n + p))
    v_spec = pl.BlockSpec((seq, LANES), lambda b, p, i: (b, col0 + 2 * p_n + p))
    tile_spec = pl.BlockSpec((tq, LANES), lambda b, p, i: (b * nq + i, p))
    seq_spec = pl.BlockSpec((seq, LANES), lambda b, p, i: (b, p))
    g_spec = pl.BlockSpec((1, LANES), lambda b, p, i: (0, p))
    return nq, q_spec, k_spec, v_spec, tile_spec, seq_spec, g_spec


def _rms_out(acc, lo, gain):
    ss = acc * acc
    s0 = jnp.sum(jnp.where(lo, ss, 0.0), axis=1, keepdims=True)
    s1 = jnp.sum(jnp.where(lo, 0.0, ss), axis=1, keepdims=True)
    r = jnp.where(lo, lax.rsqrt(s0 / HEAD_DIM + RMS_EPS), lax.rsqrt(s1 / HEAD_DIM + RMS_EPS))
    return acc * r * gain


def _sb_fwd(qkv, bl, seq, dg, gain, bufs=(), layer=None):
    t = bl * seq
    p_n = dg // LANES
    tq, tk = min(SB_TQ, seq), SB_TK
    ratio = tq // tk
    scale = 1.0 / math.sqrt(HEAD_DIM)
    nq, q_spec, k_spec, v_spec, tile_spec, _, g_spec = _sb_specs(bl, seq, p_n, 0, tq)
    assert seq // tk <= LANES
    nt = len(bufs)

    def body(q_ref, k_ref, v_ref, g_ref, *rest):
        o_ref, m_ref, c_ref = rest[nt:nt + 3]
        i = pl.program_id(2)
        if nt:
            start, finish = _gather_plan(rest[nt + 3:2 * nt + 3], [b.shape[2] for b in bufs], rest[2 * nt + 3],
                                         rest[2 * nt + 4], layer)
            grid_step = (pl.program_id(0) * p_n + pl.program_id(1)) * nq + i
            pl.when(grid_step == 0)(start)
        row = lax.broadcasted_iota(jnp.int32, (tq, tk), 0)
        col = lax.broadcasted_iota(jnp.int32, (tq, tk), 1)
        krow = lax.broadcasted_iota(jnp.int32, (tk, tk), 0)
        kcol = lax.broadcasted_iota(jnp.int32, (tk, tk), 1)
        tri2 = _tri2(krow > kcol)
        lo, qh = _head_masks(q_ref[...] * jnp.asarray(scale, BF))
        lane = lax.broadcasted_iota(jnp.int32, (1, LANES), 1)

        def step(j, carry, diag):
            acc, c0, c1, s0, s1 = carry
            cars, seen = (c0, c1), (s0, s1)
            mask = None if diag is None else row > col + diag
            off = pl.multiple_of(j * tk, tk)
            kj = k_ref[pl.ds(off, tk), :]
            vj = v_ref[pl.ds(off, tk), :]
            zs = [_dot_nt(qh[h], kj) for h in range(2)]
            lks = [_logistic_parts(z)[0] for z in zs]
            if mask is not None:
                lks = [jnp.where(mask, lk, 0.0) for lk in lks]
            sins = [_dot(_split2(lk), tri2) for lk in lks]
            ws = []
            for h in range(2):
                w = jnp.exp((lks[h] + zs[h]) + (sins[h] + cars[h]))
                ws.append((w if mask is None else jnp.where(mask, w, 0.0)).astype(BF))
            pvs = [_dot(w, vj) for w in ws]
            new_seen = [jnp.where(lane == j, cars[h], seen[h]) for h in range(2)]
            new_cars = [cars[h] + jnp.sum(lks[h], axis=1, keepdims=True) for h in range(2)]
            return acc + jnp.where(lo, pvs[0], pvs[1]), new_cars[0], new_cars[1], new_seen[0], new_seen[1]

        def corner_step(j):
            top = tq - tk
            mask = krow > kcol
            kj = k_ref[pl.ds(pl.multiple_of(j * tk, tk), tk), :]
            vj = v_ref[pl.ds(pl.multiple_of(j * tk, tk), tk), :]
            zs = [_dot_nt(qh[h][top:], kj) for h in range(2)]
            lks = [jnp.where(mask, _logistic_parts(z)[0], 0.0) for z in zs]
            sins = [_dot(_split2(lk), tri2) for lk in lks]
            ws = [jnp.where(mask, jnp.exp((lks[h] + zs[h]) + sins[h]), 0.0).astype(BF) for h in range(2)]
            pvs = [_dot(w, vj) for w in ws]
            pad = lambda low: jnp.concatenate([jnp.zeros((top,) + low.shape[1:], F32), low], axis=0)
            cars = [pad(jnp.sum(lk, axis=1, keepdims=True)) for lk in lks]
            seen = jnp.where(lane == j, 0.0, jnp.full((tq, LANES), SB_UNSEEN, F32))
            return pad(jnp.where(lo, pvs[0], pvs[1])), cars[0], cars[1], seen, seen

        if ratio == 2:
            carry = step(ratio * i, corner_step(ratio * i + 1), 0)
        else:
            zc = jnp.zeros((tq, 1), F32)
            unseen = jnp.full((tq, LANES), SB_UNSEEN, F32)
            carry = (jnp.zeros((tq, LANES), F32), zc, zc, unseen, unseen)
            for m in reversed(range(ratio)):
                carry = step(ratio * i + m, carry, m * tk)

        def alive(state):
            j, c = state
            return jnp.logical_and(j >= 0, jnp.max(jnp.maximum(c[1], c[2])) >= SB_DEAD)

        _, carry = lax.while_loop(alive, lambda st: (st[0] - 1, step(st[0], st[1], None)), (ratio * i - 1, carry))
        acc = carry[0]
        o_ref[...] = acc
        m_ref[...] = _rms_out(acc, lo, g_ref[...]).astype(BF)
        c_ref[:, :LANES] = carry[3]
        c_ref[:, LANES:] = carry[4]
        if nt:
            pl.when(grid_step == bl * p_n * nq - 1)(finish)

    car_spec = pl.BlockSpec((tq, 2 * LANES), lambda b, p, i: (b * nq + i, p))
    outs = [_sds((t, dg), F32), _sds((t, dg), BF), _sds((t, 2 * dg), F32)]
    if not nt:
        return pl.pallas_call(
            body, name="sb_fwd", grid=(bl, p_n, nq), in_specs=[q_spec, k_spec, v_spec, g_spec],
            out_specs=[tile_spec, tile_spec, car_spec], out_shape=outs,
            compiler_params=_cparams(("parallel", "parallel", "parallel")),
        )(qkv, qkv, qkv, gain)
    res = pl.pallas_call(
        body, name="sb_fwd_gather", grid=(bl, p_n, nq), in_specs=[q_spec, k_spec, v_spec, g_spec] + [_ANY] * nt,
        out_specs=[tile_spec, tile_spec, car_spec] + [_ANY] * nt,
        out_shape=outs + [_sds(b.shape, b.dtype) for b in bufs],
        input_output_aliases={4 + k: 3 + k for k in range(nt)},
        scratch_shapes=[pltpu.SemaphoreType.DMA((6 * nt,)), pltpu.SemaphoreType.DMA((6 * nt,))],
        compiler_params=_cparams(("arbitrary", "arbitrary", "arbitrary")),
    )(qkv, qkv, qkv, gain, *bufs)
    return res[0], res[1], res[2], list(res[3:])


def _sb_bwd(qkv, cars, do, bl, seq, dg, grads=()):
    t = bl * seq
    p_n = dg // LANES
    tq, tk = min(SB_TQ, seq), SB_TK
    ratio = tq // tk
    scale = 1.0 / math.sqrt(HEAD_DIM)
    nq, q_spec, k_spec, v_spec, tile_spec, seq_spec, _ = _sb_specs(bl, seq, p_n, 0, tq)
    car_spec = pl.BlockSpec((tq, 2 * LANES), lambda b, p, i: (b * nq + i, p))
    nt = len(grads)

    def body(q_ref, k_ref, v_ref, c_ref, do_ref, *rest):
        dq_ref, dk_ref, dv_ref = rest[nt:nt + 3]
        dk_acc, dv_acc = rest[2 * nt + 3:2 * nt + 5]
        i = pl.program_id(2)
        if nt:
            start, finish = _swap_plan(rest[:nt], rest[nt + 3:2 * nt + 3], [g.shape[2] for g in grads],
                                       rest[2 * nt + 5], rest[2 * nt + 6])
            grid_step = (pl.program_id(0) * p_n + pl.program_id(1)) * nq + i
            pl.when(grid_step == 0)(start)

        @pl.when(i == 0)
        def _():
            dk_acc[...] = jnp.zeros_like(dk_acc)
            dv_acc[...] = jnp.zeros_like(dv_acc)

        row = lax.broadcasted_iota(jnp.int32, (tq, tk), 0)
        col = lax.broadcasted_iota(jnp.int32, (tq, tk), 1)
        krow = lax.broadcasted_iota(jnp.int32, (tk, tk), 0)
        kcol = lax.broadcasted_iota(jnp.int32, (tk, tk), 1)
        tri2 = _tri2(krow > kcol)
        tri_pre = jnp.where(krow < kcol, 1.0, 0.0).astype(BF)
        lane = lax.broadcasted_iota(jnp.int32, (1, LANES), 1)
        sc = jnp.asarray(scale, BF)
        qs = q_ref[...] * sc
        dob = do_ref[...]
        lo, qh = _head_masks(qs)
        _, doh = _head_masks(dob)
        seen = (c_ref[:, :LANES], c_ref[:, LANES:])

        def step(j, carry, diag):
            dq, e0, e1 = carry
            ecars = (e0, e1)
            mask = None if diag is None else row > col + diag
            off = pl.multiple_of(j * tk, tk)
            kj = k_ref[pl.ds(off, tk), :]
            vj = v_ref[pl.ds(off, tk), :]
            zs = [_dot_nt(qh[h], kj) for h in range(2)]
            dws = [_dot_nt(doh[h], vj) for h in range(2)]
            parts = [_logistic_parts(z) for z in zs]
            lks = [p[0] if mask is None else jnp.where(mask, p[0], 0.0) for p in parts]
            sins = [_dot(_split2(lk), tri2) for lk in lks]
            ws, es = [], []
            for h in range(2):
                car = jnp.sum(jnp.where(lane == j, seen[h], 0.0), axis=1, keepdims=True)
                w = jnp.exp((lks[h] + zs[h]) + (sins[h] + car))
                w = w if mask is None else jnp.where(mask, w, 0.0)
                ws.append(w.astype(BF))
                es.append(dws[h] * w)
            pres = [_dot(e.astype(BF), tri_pre) for e in es]
            dzs = []
            for h in range(2):
                t_, e = parts[h][1], es[h]
                before = pres[h] + ecars[h]
                dz = pl.reciprocal(parts[h][2], approx=True) * jnp.where(zs[h] > 0.0, e * t_ - before, e - t_ * before)
                dzs.append((dz if mask is None else jnp.where(mask, dz, 0.0)).astype(BF))
            kjs = kj * sc
            dqs = [_dot(dz, kjs) for dz in dzs]
            dks = [_dot_tn(dz, qs) for dz in dzs]
            dvs = [_dot_tn(w, dob) for w in ws]
            dk_acc[pl.ds(off, tk), :] += jnp.where(lo, dks[0], dks[1])
            dv_acc[pl.ds(off, tk), :] += jnp.where(lo, dvs[0], dvs[1])
            new_e = [ecars[h] + jnp.sum(es[h], axis=1, keepdims=True) for h in range(2)]
            return dq + jnp.where(lo, dqs[0], dqs[1]), new_e[0], new_e[1]

        reach = jnp.max(jnp.maximum(seen[0], seen[1]), axis=0, keepdims=True)
        skipped = jnp.logical_and(reach < SB_DEAD, lane < ratio * i)
        first = jnp.minimum(jnp.sum(jnp.where(skipped, 1, 0)).astype(jnp.int32), ratio * i)
        zc = jnp.zeros((tq, 1), F32)
        def corner_step(j, carry):
            dq, e0, e1 = carry
            top = tq - tk
            mask = krow > kcol
            off = pl.multiple_of(j * tk, tk)
            kj = k_ref[pl.ds(off, tk), :]
            vj = v_ref[pl.ds(off, tk), :]
            zs = [_dot_nt(qh[h][top:], kj) for h in range(2)]
            dws = [_dot_nt(doh[h][top:], vj) for h in range(2)]
            parts = [_logistic_parts(z) for z in zs]
            lks = [jnp.where(mask, p[0], 0.0) for p in parts]
            sins = [_dot(_split2(lk), tri2) for lk in lks]
            ws, es = [], []
            for h in range(2):
                car = jnp.sum(jnp.where(lane == j, seen[h][top:], 0.0), axis=1, keepdims=True)
                w = jnp.where(mask, jnp.exp((lks[h] + zs[h]) + (sins[h] + car)), 0.0)
                ws.append(w.astype(BF))
                es.append(dws[h] * w)
            pres = [_dot(e.astype(BF), tri_pre) for e in es]
            dzs = []
            for h, ecar in enumerate((e0, e1)):
                t_, e = parts[h][1], es[h]
                before = pres[h] + ecar[top:]
                dz = pl.reciprocal(parts[h][2], approx=True) * jnp.where(zs[h] > 0.0, e * t_ - before, e - t_ * before)
                dzs.append(jnp.where(mask, dz, 0.0).astype(BF))
            kjs = kj * sc
            dqs = [_dot(dz, kjs) for dz in dzs]
            dks = [_dot_tn(dz, qs[top:]) for dz in dzs]
            dvs = [_dot_tn(w, dob[top:]) for w in ws]
            dk_acc[pl.ds(off, tk), :] += jnp.where(lo, dks[0], dks[1])
            dv_acc[pl.ds(off, tk), :] += jnp.where(lo, dvs[0], dvs[1])
            return dq + jnp.concatenate([jnp.zeros((top, LANES), F32), jnp.where(lo, dqs[0], dqs[1])], axis=0)

        carry = lax.fori_loop(first, ratio * i, lambda j, c: step(j, c, None), (jnp.zeros((tq, LANES), F32), zc, zc))
        if ratio == 2:
            dq = corner_step(ratio * i + 1, step(ratio * i, carry, 0))
        else:
            for m in range(ratio):
                carry = step(ratio * i + m, carry, m * tk)
            dq = carry[0]
        dq_ref[...] = dq.astype(BF)

        @pl.when(i == nq - 1)
        def _():
            dk_ref[...] = dk_acc[...].astype(BF)
            dv_ref[...] = dv_acc[...].astype(BF)

        if nt:
            pl.when(grid_step == bl * p_n * nq - 1)(finish)

    out = _sds((t, dg), BF)
    acc = [pltpu.VMEM((seq, LANES), F32), pltpu.VMEM((seq, LANES), F32)]
    if not nt:
        return pl.pallas_call(
            body, name="sb_bwd", grid=(bl, p_n, nq),
            in_specs=[q_spec, k_spec, v_spec, car_spec, tile_spec],
            out_specs=[tile_spec, seq_spec, seq_spec], out_shape=[out, out, out],
            scratch_shapes=acc, compiler_params=_cparams(("parallel", "parallel", "arbitrary")),
        )(qkv, qkv, qkv, cars, do)
    res = pl.pallas_call(
        body, name="sb_bwd_swap", grid=(bl, p_n, nq),
        in_specs=[q_spec, k_spec, v_spec, car_spec, tile_spec] + [_ANY] * nt,
        out_specs=[tile_spec, seq_spec, seq_spec] + [_ANY] * nt,
        out_shape=[out, out, out] + _swap_shapes(grads),
        scratch_shapes=acc + [pltpu.SemaphoreType.DMA((nt,)), pltpu.SemaphoreType.DMA((nt,))],
        compiler_params=_cparams(("arbitrary", "arbitrary", "arbitrary")),
    )(qkv, qkv, qkv, cars, do, *grads)
    return res[0], res[1], res[2], list(res[3:])


def _bias_expand(rb):
    h = rb.shape[0]
    pad = CHUNK * N_PREV_CHUNKS
    n_f = BAND + CHUNK - 1
    f = jnp.concatenate([rb[:, MAX_REL - (CHUNK - 1):2 * MAX_REL],
                         jnp.broadcast_to(rb[:, 2 * MAX_REL:], (h, pad + CHUNK - MAX_REL))], axis=1)
    fr = f[:, ::-1]
    skew = jnp.pad(jnp.broadcast_to(fr[:, None, :], (h, CHUNK, n_f)), ((0, 0), (0, 0), (0, 1)))
    skew = skew.reshape(h, CHUNK * (n_f + 1))[:, :CHUNK * n_f].reshape(h, CHUNK, n_f)
    table = skew[:, :, CHUNK - 1:CHUNK - 1 + BAND]
    n_c = ATT_TILE // CHUNK
    width = 3 * ATT_TILE
    rows = [jnp.pad(table, ((0, 0), (0, 0), (c * CHUNK, width - BAND - c * CHUNK)), constant_values=NEG_BIAS)
            for c in range(n_c)]
    return jnp.concatenate(rows, axis=1)


def _ca_scores(qh_h, k_ref, bias_ref, h, i, scale):
    tq = ATT_TILE
    ss, offs = [], []
    for kb in range(3):
        jb = i - 2 + kb
        off = pl.multiple_of(jnp.maximum(jb, 0) * tq, tq)
        s = _dot_nt(qh_h, k_ref[pl.ds(off, tq), :]) * scale + bias_ref[h, :, kb * tq:(kb + 1) * tq]
        ss.append(jnp.where(jb >= 0, s, NEG_BIAS))
        offs.append(off)
    m = jnp.max(jnp.maximum(jnp.maximum(ss[0], ss[1]), ss[2]), axis=1, keepdims=True)
    ps = [jnp.exp(s - m) for s in ss]
    den = jnp.sum(ps[0] + ps[1] + ps[2], axis=1, keepdims=True)
    inv = 1.0 / den
    return [p * inv for p in ps], offs


def _ca_fwd(qkv, bias4, bl, seq, dg, gain, bufs=(), layer=None):
    t = bl * seq
    p_n = dg // LANES
    tq = ATT_TILE
    scale = 1.0 / math.sqrt(HEAD_DIM)
    nq, q_spec, k_spec, v_spec, tile_spec, _, g_spec = _sb_specs(bl, seq, p_n, 3 * p_n, tq)
    b_spec = pl.BlockSpec((2, tq, 3 * tq), lambda b, p, i: (p, 0, 0))
    nt = len(bufs)

    def body(q_ref, k_ref, v_ref, bias_ref, g_ref, *rest):
        o_ref, m_ref = rest[nt:nt + 2]
        i = pl.program_id(2)
        if nt:
            start, finish = _gather_plan(rest[nt + 2:2 * nt + 2], [b.shape[2] for b in bufs], rest[2 * nt + 2],
                                         rest[2 * nt + 3], layer)
            step = (pl.program_id(0) * p_n + pl.program_id(1)) * nq + i
            pl.when(step == 0)(start)
        lo, qh = _head_masks(q_ref[...])
        outs = []
        for h in range(2):
            ps, offs = _ca_scores(qh[h], k_ref, bias_ref, h, i, scale)
            acc = None
            for kb in range(3):
                pv = _dot(ps[kb].astype(BF), v_ref[pl.ds(offs[kb], tq), :])
                acc = pv if acc is None else acc + pv
            outs.append(acc)
        acc = jnp.where(lo, outs[0], outs[1])
        o_ref[...] = acc
        m_ref[...] = _rms_out(acc, lo, g_ref[...]).astype(BF)
        if nt:
            pl.when(step == bl * p_n * nq - 1)(finish)

    if not nt:
        return pl.pallas_call(
            body, name="ca_fwd", grid=(bl, p_n, nq), in_specs=[q_spec, k_spec, v_spec, b_spec, g_spec],
            out_specs=[tile_spec, tile_spec], out_shape=[_sds((t, dg), F32), _sds((t, dg), BF)],
            compiler_params=_cparams(("parallel", "parallel", "parallel")),
        )(qkv, qkv, qkv, bias4, gain)
    res = pl.pallas_call(
        body, name="ca_fwd_gather", grid=(bl, p_n, nq),
        in_specs=[q_spec, k_spec, v_spec, b_spec, g_spec] + [_ANY] * nt,
        out_specs=[tile_spec, tile_spec] + [_ANY] * nt,
        out_shape=[_sds((t, dg), F32), _sds((t, dg), BF)] + [_sds(b.shape, b.dtype) for b in bufs],
        input_output_aliases={5 + k: 2 + k for k in range(nt)},
        scratch_shapes=[pltpu.SemaphoreType.DMA((6 * nt,)), pltpu.SemaphoreType.DMA((6 * nt,))],
        compiler_params=_cparams(("arbitrary", "arbitrary", "arbitrary")),
    )(qkv, qkv, qkv, bias4, gain, *bufs)
    return res[0], res[1], list(res[2:])


def _ca_bwd(qkv, bias4, o, do, bl, seq, dg, psums=()):
    t = bl * seq
    p_n = dg // LANES
    tq = ATT_TILE
    scale = 1.0 / math.sqrt(HEAD_DIM)
    nq = seq // tq
    c0 = 3 * p_n
    q_spec = pl.BlockSpec((tq, LANES), lambda p, b, i: (b * nq + i, c0 + p))
    k_spec = pl.BlockSpec((seq, LANES), lambda p, b, i: (b, c0 + p_n + p))
    v_spec = pl.BlockSpec((seq, LANES), lambda p, b, i: (b, c0 + 2 * p_n + p))
    tile_spec = pl.BlockSpec((tq, LANES), lambda p, b, i: (b * nq + i, p))
    seq_spec = pl.BlockSpec((seq, LANES), lambda p, b, i: (b, p))
    b_spec = pl.BlockSpec((2, tq, 3 * tq), lambda p, b, i: (p, 0, 0))

    nt = len(psums)

    def body(q_ref, k_ref, v_ref, bias_ref, o_ref, do_ref, *rest):
        dq_ref, dk_ref, dv_ref, db_ref = rest[nt:nt + 4]
        dk_acc, dv_acc = rest[2 * nt + 4:2 * nt + 6]
        b = pl.program_id(1)
        i = pl.program_id(2)
        if nt:
            start, finish = _scatter_plan(rest[:nt], rest[nt + 4:2 * nt + 4], rest[2 * nt + 6], rest[2 * nt + 7])
            step = (pl.program_id(0) * bl + b) * nq + i
            pl.when(step == 0)(start)

        @pl.when(i == 0)
        def _():
            dk_acc[...] = jnp.zeros_like(dk_acc)
            dv_acc[...] = jnp.zeros_like(dv_acc)

        @pl.when((i == 0) & (b == 0))
        def _():
            db_ref[...] = jnp.zeros_like(db_ref)

        q = q_ref[...]
        dob = do_ref[...]
        lo, qh = _head_masks(q)
        _, doh = _head_masks(dob)
        prod = dob.astype(F32) * o_ref[...]
        delta = (jnp.sum(jnp.where(lo, prod, 0.0), axis=1, keepdims=True),
                 jnp.sum(jnp.where(lo, 0.0, prod), axis=1, keepdims=True))
        dqs, dks, dvs = [], [], []
        for h in range(2):
            ps, offs = _ca_scores(qh[h], k_ref, bias_ref, h, i, scale)
            dq = None
            dks.append([])
            dvs.append([])
            for kb in range(3):
                dp = _dot_nt(doh[h], v_ref[pl.ds(offs[kb], tq), :])
                ds = ps[kb] * (dp - delta[h])
                db_ref[h, :, kb * tq:(kb + 1) * tq] += ds
                dsb = (ds * scale).astype(BF)
                part = _dot(dsb, k_ref[pl.ds(offs[kb], tq), :])
                dq = part if dq is None else dq + part
                dks[h].append(_dot_tn(dsb, q))
                dvs[h].append(_dot_tn(ps[kb].astype(BF), dob))
            dqs.append(dq)
        for kb in range(3):
            dk_acc[pl.ds(offs[kb], tq), :] += jnp.where(lo, dks[0][kb], dks[1][kb])
            dv_acc[pl.ds(offs[kb], tq), :] += jnp.where(lo, dvs[0][kb], dvs[1][kb])
        dq_ref[...] = jnp.where(lo, dqs[0], dqs[1]).astype(BF)

        @pl.when(i == nq - 1)
        def _():
            dk_ref[...] = dk_acc[...].astype(BF)
            dv_ref[...] = dv_acc[...].astype(BF)

        if nt:
            pl.when(step == p_n * bl * nq - 1)(finish)

    out = _sds((t, dg), BF)
    acc = [pltpu.VMEM((seq, LANES), F32), pltpu.VMEM((seq, LANES), F32)]
    if not nt:
        return pl.pallas_call(
            body, name="ca_bwd", grid=(p_n, bl, nq),
            in_specs=[q_spec, k_spec, v_spec, b_spec, tile_spec, tile_spec],
            out_specs=[tile_spec, seq_spec, seq_spec, b_spec], out_shape=[out, out, out, _sds(bias4.shape, F32)],
            scratch_shapes=acc, compiler_params=_cparams(("parallel", "arbitrary", "arbitrary")),
        )(qkv, qkv, qkv, bias4, o, do)
    res = pl.pallas_call(
        body, name="ca_bwd_scatter", grid=(p_n, bl, nq),
        in_specs=[q_spec, k_spec, v_spec, b_spec, tile_spec, tile_spec] + [_ANY] * nt,
        out_specs=[tile_spec, seq_spec, seq_spec, b_spec] + [_ANY] * nt,
        out_shape=[out, out, out, _sds(bias4.shape, F32)] + _scatter_shapes(psums),
        scratch_shapes=acc + [pltpu.SemaphoreType.DMA((3 * nt,)), pltpu.SemaphoreType.DMA((3 * nt,))],
        compiler_params=_cparams(("arbitrary", "arbitrary", "arbitrary")),
    )(qkv, qkv, qkv, bias4, o, do, *psums)
    return res[0], res[1], res[2], res[3], list(res[4:])


_ANY = pl.BlockSpec(memory_space=pl.ANY)


def _mesh_pos():
    x, y, c = lax.axis_index("x"), lax.axis_index("y"), lax.axis_index("c")
    chips = [(1 - x, y), (x, 1 - y), (1 - x, 1 - y)]
    return x, y, c, chips


def _half(ref_rows, c):
    return pl.ds(c * (ref_rows // 2), ref_rows // 2)


def _cast_into_slot(w, slot):
    l_n, r, c_n = w.shape
    tr = _tile(r, 512)

    def body(s_ref, w_ref, o_ref):
        o_ref[...] = w_ref[...].astype(BF)

    return pl.pallas_call(
        body, name="cast_into_slot", out_shape=_sds((l_n, 4, r, c_n), BF),
        grid_spec=pltpu.PrefetchScalarGridSpec(
            num_scalar_prefetch=1, grid=(l_n, r // tr),
            in_specs=[pl.BlockSpec((None, tr, c_n), lambda l, m, s: (l, m, 0))],
            out_specs=pl.BlockSpec((None, None, tr, c_n), lambda l, m, s: (l, s[0], m, 0))),
        compiler_params=_cparams(("parallel", "parallel")),
    )(slot, w)


def _gather_plan(outs, n_rows, send_sems, recv_sems, layer):
    nt = len(outs)

    def blk(t, s, cc):
        return outs[t].at[pl.ds(layer, 1), s, _half(n_rows[t], cc)]

    def copy(k, ref, to):
        return pltpu.make_async_remote_copy(src_ref=ref, dst_ref=ref, send_sem=send_sems.at[k],
                                            recv_sem=recv_sems.at[k], device_id=to, device_id_type=MESH_IDS)

    def over_ici():
        x, y, c, chips = _mesh_pos()
        return [copy(j * nt + t, blk(t, 2 * x + y, c), (px, py, c)) for j, (px, py) in enumerate(chips) for t in range(nt)]

    def start():
        for cp in over_ici():
            cp.start()

    def finish():
        x, y, c, chips = _mesh_pos()
        passed = []
        for j, (px, py) in enumerate(chips):
            for t in range(nt):
                got = blk(t, 2 * px + py, c)
                copy(j * nt + t, got, (px, py, c)).wait_recv()
                relay = copy(3 * nt + j * nt + t, got, (x, y, 1 - c))
                relay.start()
                passed.append(relay)
        for j, (px, py) in enumerate(chips):
            for t in range(nt):
                copy(3 * nt + j * nt + t, blk(t, 2 * px + py, 1 - c), (x, y, 1 - c)).wait_recv()
        for cp in over_ici() + passed:
            cp.wait_send()

    return start, finish


def _gather_weights(bufs, layer):
    nt = len(bufs)

    def body(*refs):
        start, finish = _gather_plan(refs[nt:2 * nt], [b.shape[2] for b in bufs], refs[2 * nt], refs[2 * nt + 1], layer)
        start()
        finish()

    return pl.pallas_call(
        body, name="gather_weights", in_specs=[_ANY] * nt, out_specs=[_ANY] * nt,
        out_shape=[_sds(b.shape, b.dtype) for b in bufs], input_output_aliases={t: t for t in range(nt)},
        scratch_shapes=[pltpu.SemaphoreType.DMA((6 * nt,)), pltpu.SemaphoreType.DMA((6 * nt,))],
    )(*bufs)


def _swap_halves(grads):
    nt = len(grads)

    def body(*refs):
        start, finish = _swap_plan(refs[:nt], refs[nt:2 * nt], [g.shape[2] for g in grads], refs[2 * nt], refs[2 * nt + 1])
        start()
        finish()

    return pl.pallas_call(
        body, name="swap_halves", in_specs=[_ANY] * nt, out_specs=[_ANY] * nt, out_shape=_swap_shapes(grads),
        scratch_shapes=[pltpu.SemaphoreType.DMA((nt,)), pltpu.SemaphoreType.DMA((nt,))],
    )(*grads)


def _swap_shapes(grads):
    return [_sds(g.shape[:2] + (g.shape[2] // 2, g.shape[3]), g.dtype) for g in grads]


def _swap_plan(ins, theirs, n_rows, send_sems, recv_sems):
    def sends():
        x, y, c, _ = _mesh_pos()
        return [pltpu.make_async_remote_copy(
            src_ref=ins[t].at[:, :, _half(n_rows[t], 1 - c)], dst_ref=theirs[t], send_sem=send_sems.at[t],
            recv_sem=recv_sems.at[t], device_id=(x, y, 1 - c), device_id_type=MESH_IDS) for t in range(len(ins))]

    def start():
        for cp in sends():
            cp.start()

    def finish():
        for cp in sends():
            cp.wait_recv()
        for cp in sends():
            cp.wait_send()

    return start, finish


def _add_pair(mine, theirs):
    nt = len(mine)
    r2s = [th.shape[2] for th in theirs]

    def body(*refs):
        c = lax.axis_index("c")
        for k in range(nt):
            rows = pl.ds(pl.multiple_of(c * r2s[k], 16), r2s[k])
            refs[2 * nt + k][...] = (refs[k][rows, :].astype(F32) + refs[nt + k][...].astype(F32)).astype(BF)

    def spec(shape):
        return pl.BlockSpec((None, None) + tuple(shape[2:]), lambda s: (0, s, 0, 0))

    return pl.pallas_call(
        body, name="add_pair", out_shape=[_sds(th.shape, BF) for th in theirs], grid=(mine[0].shape[1],),
        in_specs=[spec(a.shape) for a in mine] + [spec(b.shape) for b in theirs],
        out_specs=[spec(b.shape) for b in theirs], compiler_params=_cparams(("parallel",)),
    )(*mine, *theirs)


def _scatter_chips(psums):
    nt = len(psums)

    def body(*refs):
        start, finish = _scatter_plan(refs[:nt], refs[nt:2 * nt], refs[2 * nt], refs[2 * nt + 1])
        start()
        finish()

    return pl.pallas_call(
        body, name="scatter_chips", in_specs=[_ANY] * nt, out_specs=[_ANY] * nt, out_shape=_scatter_shapes(psums),
        scratch_shapes=[pltpu.SemaphoreType.DMA((3 * nt,)), pltpu.SemaphoreType.DMA((3 * nt,))],
    )(*psums)


def _scatter_shapes(psums):
    return [_sds((3, p.shape[0]) + p.shape[2:], p.dtype) for p in psums]


def _scatter_plan(ins, got, send_sems, recv_sems):
    nt = len(ins)

    def sends():
        _, _, c, chips = _mesh_pos()
        return [pltpu.make_async_remote_copy(
            src_ref=ins[t].at[:, 2 * px + py], dst_ref=got[t].at[j], send_sem=send_sems.at[j * nt + t],
            recv_sem=recv_sems.at[j * nt + t], device_id=(px, py, c), device_id_type=MESH_IDS)
            for j, (px, py) in enumerate(chips) for t in range(nt)]

    def start():
        for cp in sends():
            cp.start()

    def finish():
        for cp in sends():
            cp.wait_recv()
        for cp in sends():
            cp.wait_send()

    return start, finish


def _add_chips(psums, gots, fulls, layer, depth):
    nt = len(psums)
    r2s = [p.shape[2] for p in psums]
    n_alias = nt if fulls is not None else 0

    def body(*refs):
        c = lax.axis_index("c")
        chip = 2 * lax.axis_index("x") + lax.axis_index("y")
        outs = refs[n_alias + 2 * nt:]
        for k in range(nt):
            acc = refs[n_alias + k][chip].astype(F32)
            for j in range(3):
                acc = acc + refs[n_alias + nt + k][j].astype(F32)
            outs[k][pl.ds(pl.multiple_of(c * r2s[k], 8), r2s[k]), :] = acc
            outs[k][pl.ds(pl.multiple_of((1 - c) * r2s[k], 8), r2s[k]), :] = jnp.zeros_like(acc)

    once = pl.Buffered(1)
    in_specs = [_ANY] * n_alias
    in_specs += [pl.BlockSpec((None,) + p.shape[1:], lambda i: (0, 0, 0, 0), pipeline_mode=once) for p in psums]
    in_specs += [pl.BlockSpec((3, None) + g.shape[2:], lambda i: (0, 0, 0, 0), pipeline_mode=once) for g in gots]
    out_shape = [_sds((depth, 2 * p.shape[2], p.shape[3]), F32) for p in psums]
    out_specs = [pl.BlockSpec((None,) + s.shape[1:], lambda i: (layer, 0, 0), pipeline_mode=once) for s in out_shape]
    return pl.pallas_call(
        body, name="add_chips", out_shape=out_shape, grid=(1,), in_specs=in_specs, out_specs=out_specs,
        input_output_aliases={k: k for k in range(n_alias)}, compiler_params=_cparams(("arbitrary",)),
    )(*(fulls or []), *psums, *gots)


def _join_halves(fulls):
    nt = len(fulls)

    def body(*refs):
        outs = refs[nt:2 * nt]
        send_sems, recv_sems = refs[2 * nt:]
        x, y, c, _ = _mesh_pos()

        def copy(t, cc):
            blk = outs[t].at[:, _half(fulls[t].shape[1], cc)]
            return pltpu.make_async_remote_copy(
                src_ref=blk, dst_ref=blk, send_sem=send_sems.at[t], recv_sem=recv_sems.at[t],
                device_id=(x, y, 1 - c), device_id_type=MESH_IDS)

        sends = [copy(t, c) for t in range(nt)]
        for cp in sends:
            cp.start()
        for t in range(nt):
            copy(t, 1 - c).wait_recv()
        for cp in sends:
            cp.wait_send()

    return pl.pallas_call(
        body, name="join_halves", in_specs=[_ANY] * nt, out_specs=[_ANY] * nt,
        out_shape=[_sds(f.shape, f.dtype) for f in fulls], input_output_aliases={t: t for t in range(nt)},
        scratch_shapes=[pltpu.SemaphoreType.DMA((nt,)), pltpu.SemaphoreType.DMA((nt,))],
    )(*fulls)


def _allreduce_small(part):
    r, d = part.shape
    flips = [(fx, fy, fc) for fx in (0, 1) for fy in (0, 1) for fc in (0, 1) if fx + fy + fc]

    def body(p_ref, o_ref, buf, send_sems, recv_sems):
        x, y, c, _ = _mesh_pos()
        me = 4 * x + 2 * y + c
        buf[me] = p_ref[...]
        sends = []
        for k, (fx, fy, fc) in enumerate(flips):
            sends.append(pltpu.make_async_remote_copy(
                src_ref=p_ref, dst_ref=buf.at[me], send_sem=send_sems.at[k], recv_sem=recv_sems.at[k],
                device_id=(x ^ fx, y ^ fy, c ^ fc), device_id_type=MESH_IDS))
        for cp in sends:
            cp.start()
        for k, (fx, fy, fc) in enumerate(flips):
            peer = 4 * (x ^ fx) + 2 * (y ^ fy) + (c ^ fc)
            pltpu.make_async_remote_copy(
                src_ref=p_ref, dst_ref=buf.at[peer], send_sem=send_sems.at[k], recv_sem=recv_sems.at[k],
                device_id=(x ^ fx, y ^ fy, c ^ fc), device_id_type=MESH_IDS).wait_recv()
        acc = buf[0]
        for dev in range(1, 8):
            acc = acc + buf[dev]
        o_ref[...] = acc
        for cp in sends:
            cp.wait_send()

    vmem = pl.BlockSpec(memory_space=pltpu.VMEM)
    return pl.pallas_call(
        body, name="allreduce_small", in_specs=[vmem], out_specs=vmem, out_shape=_sds((r, d), F32),
        scratch_shapes=[pltpu.VMEM((8, r, d), F32), pltpu.SemaphoreType.DMA((7,)), pltpu.SemaphoreType.DMA((7,))],
    )(part)


BIG = ("ffn1_w_gate", "ffn1_w_up", "ffn1_w_down", "w_in", "w_out", "ffn2_w_gate", "ffn2_w_up", "ffn2_w_down")
SMALL = ("ln1_g", "ln1_b", "rel_bias", "sb_out_g", "ca_out_g", "ln2_g", "ln2_b", "ln3_g", "ln3_b")
ORDER = ("ffn1_w_gate", "ffn1_w_up", "ffn1_w_down", "ln1_g", "ln1_b", "w_in", "rel_bias", "sb_out_g", "ca_out_g",
         "w_out", "ln2_g", "ln2_b", "ffn2_w_gate", "ffn2_w_up", "ffn2_w_down", "ln3_g", "ln3_b")


def _pack_small(vals, depth, d):
    rows = []
    for l in range(depth):
        for n in ("ln1_g", "ln1_b", "ln2_g", "ln2_b", "ln3_g", "ln3_b"):
            rows.append(vals[n][l][None, :])
        rows.append(jnp.concatenate([vals["sb_out_g"][l], vals["ca_out_g"][l]])[None, :])
        rb = vals["rel_bias"][l]
        rows.append(jnp.pad(rb, ((0, 0), (0, d - rb.shape[1]))))
    packed = jnp.concatenate(rows, axis=0)
    pad = (-packed.shape[0]) % 8
    return jnp.pad(packed, ((0, pad), (0, 0)))


def _unpack_small(packed, depth, d, heads, n_rel):
    per = 7 + heads
    out = {n: [] for n in SMALL}
    for l in range(depth):
        base = l * per
        for i, n in enumerate(("ln1_g", "ln1_b", "ln2_g", "ln2_b", "ln3_g", "ln3_b")):
            out[n].append(packed[base + i])
        out["sb_out_g"].append(packed[base + 6, :d // 2])
        out["ca_out_g"].append(packed[base + 6, d // 2:])
        out["rel_bias"].append(packed[base + 7:base + 7 + heads, :n_rel])
    return {n: jnp.stack(v) for n, v in out.items()}


def kernel(x, ffn1_w_gate, ffn1_w_up, ffn1_w_down, ln1_g, ln1_b, w_in, rel_bias, sb_out_g, ca_out_g, w_out, ln2_g, ln2_b, ffn2_w_gate, ffn2_w_up, ffn2_w_down, ln3_g, ln3_b, loss_target, m_ffn1_w_gate, m_ffn1_w_up, m_ffn1_w_down, m_ln1_g, m_ln1_b, m_w_in, m_rel_bias, m_sb_out_g, m_ca_out_g, m_w_out, m_ln2_g, m_ln2_b, m_ffn2_w_gate, m_ffn2_w_up, m_ffn2_w_down, m_ln3_g, m_ln3_b, v_ffn1_w_gate, v_ffn1_w_up, v_ffn1_w_down, v_ln1_g, v_ln1_b, v_w_in, v_rel_bias, v_sb_out_g, v_ca_out_g, v_w_out, v_ln2_g, v_ln2_b, v_ffn2_w_gate, v_ffn2_w_up, v_ffn2_w_down, v_ln3_g, v_ln3_b):
    w = dict(ffn1_w_gate=ffn1_w_gate, ffn1_w_up=ffn1_w_up, ffn1_w_down=ffn1_w_down, ln1_g=ln1_g, ln1_b=ln1_b, w_in=w_in, rel_bias=rel_bias, sb_out_g=sb_out_g, ca_out_g=ca_out_g, w_out=w_out, ln2_g=ln2_g, ln2_b=ln2_b, ffn2_w_gate=ffn2_w_gate, ffn2_w_up=ffn2_w_up, ffn2_w_down=ffn2_w_down, ln3_g=ln3_g, ln3_b=ln3_b)
    mom = dict(ffn1_w_gate=m_ffn1_w_gate, ffn1_w_up=m_ffn1_w_up, ffn1_w_down=m_ffn1_w_down, ln1_g=m_ln1_g, ln1_b=m_ln1_b, w_in=m_w_in, rel_bias=m_rel_bias, sb_out_g=m_sb_out_g, ca_out_g=m_ca_out_g, w_out=m_w_out, ln2_g=m_ln2_g, ln2_b=m_ln2_b, ffn2_w_gate=m_ffn2_w_gate, ffn2_w_up=m_ffn2_w_up, ffn2_w_down=m_ffn2_w_down, ln3_g=m_ln3_g, ln3_b=m_ln3_b)
    var = dict(ffn1_w_gate=v_ffn1_w_gate, ffn1_w_up=v_ffn1_w_up, ffn1_w_down=v_ffn1_w_down, ln1_g=v_ln1_g, ln1_b=v_ln1_b, w_in=v_w_in, rel_bias=v_rel_bias, sb_out_g=v_sb_out_g, ca_out_g=v_ca_out_g, w_out=v_w_out, ln2_g=v_ln2_g, ln2_b=v_ln2_b, ffn2_w_gate=v_ffn2_w_gate, ffn2_w_up=v_ffn2_w_up, ffn2_w_down=v_ffn2_w_down, ln3_g=v_ln3_g, ln3_b=v_ln3_b)

    bl, seq, d = x.shape
    t = bl * seq
    depth = w_in.shape[0]
    dg = d // 2
    heads = dg // HEAD_DIM
    alpha = (2 * depth) ** 0.25
    n_rel = rel_bias.shape[-1]

    chip = (2 * lax.axis_index("x") + lax.axis_index("y")).astype(jnp.int32).reshape(1)
    bufs = list(_gather_weights([_cast_into_slot(w[n], chip) for n in BIG], 0))
    with_sb = [BIG.index(n) for n in ("w_in", "ffn2_w_gate", "ffn2_w_up", "ffn2_w_down")]
    with_ca = [k for k in range(len(BIG)) if k not in with_sb]

    qd = dg // 2
    shards = 4
    ffn_terms = [(0, s, None, s) for s in range(shards)]
    out_terms = [(i, None, (k * qd, (k + 1) * qd), 2 * i + k) for i in range(2) for k in range(2)]
    bias_all, bias_vjp = jax.vjp(jax.vmap(_bias_expand), rel_bias)

    xs = x.reshape(t, d)
    saved = []
    for l in range(depth):
        lw = {n: _LayerOf((b, l)) for n, b in zip(BIG, bufs)}
        row = lambda a: a[l][None, :]
        g1, u1, h1 = _ffn_up(xs, lw["ffn1_w_gate"], lw["ffn1_w_up"])
        x1, xh1, rs1 = _proj_ln("ffn_down_ln", [h1], lw["ffn1_w_down"], ffn_terms, xs, row(ln1_g), row(ln1_b),
                                FFN_RESIDUAL, alpha)
        qkv = _qkv_proj(x1, lw["w_in"])
        if l + 1 < depth:
            o_sb, m_sb, c_sb, new = _sb_fwd(qkv, bl, seq, dg, row(sb_out_g), [bufs[k] for k in with_sb], l + 1)
            for k, b in zip(with_sb, new):
                bufs[k] = b
            o_ca, m_ca, new = _ca_fwd(qkv, bias_all[l], bl, seq, dg, row(ca_out_g), [bufs[k] for k in with_ca], l + 1)
            for k, b in zip(with_ca, new):
                bufs[k] = b
            lw = {n: _LayerOf((b, l)) for n, b in zip(BIG, bufs)}
        else:
            o_sb, m_sb, c_sb = _sb_fwd(qkv, bl, seq, dg, row(sb_out_g))
            o_ca, m_ca = _ca_fwd(qkv, bias_all[l], bl, seq, dg, row(ca_out_g))
        x2, xh2, rs2 = _proj_ln("attn_out_ln", [m_sb, m_ca], lw["w_out"], out_terms, x1, row(ln2_g), row(ln2_b),
                                1.0, alpha)
        g2, u2, h2 = _ffn_up(x2, lw["ffn2_w_gate"], lw["ffn2_w_up"])
        x3, xh3, rs3 = _proj_ln("ffn_down_ln", [h2], lw["ffn2_w_down"], ffn_terms, x2, row(ln3_g), row(ln3_b),
                                FFN_RESIDUAL, alpha)
        saved.append(dict(xin=xs, g1=g1, u1=u1, h1=h1, x1=x1, xh1=xh1, rs1=rs1, qkv=qkv, o_sb=o_sb, m_sb=m_sb,
                          c_sb=c_sb, o_ca=o_ca, m_ca=m_ca, x2=x2, xh2=xh2, rs2=rs2, g2=g2, u2=u2, h2=h2, xh3=xh3,
                          rs3=rs3))
        xs = x3

    last = saved[-1]
    dr3, drs3, pg, pb, sq = _loss_grad(xs, loss_target.reshape(t, d), last["xh3"], last["rs3"],
                                       ln3_g[depth - 1][None, :], FFN_RESIDUAL)
    loss = lax.psum(0.5 * jnp.sum(sq) / d, ("x", "y", "c"))

    big_grads = {n: [None] * depth for n in BIG}
    small_grads = {n: [None] * depth for n in SMALL}
    s8 = lambda a: jnp.sum(a, axis=0)
    wd_ = 3 * d // shards
    in_pieces = [_shard_cols(6, dg, s, wd_) for s in range(shards)]
    dbias = [None] * depth
    chip_sums, arrived = [None] * depth, [None] * depth
    for l in reversed(range(depth)):
        sv = saved[l]
        lw = {n: _LayerOf((b, l)) for n, b in zip(BIG, bufs)}
        row = lambda a: a[l][None, :]
        small_grads["ln3_g"][l], small_grads["ln3_b"][l] = s8(pg), s8(pb)
        dgt, dut = _swiglu_bwd(drs3, lw["ffn2_w_down"], sv["g2"], sv["u2"])
        dr2, drs2, pg, pb = _ffn_dx(dgt, dut, lw["ffn2_w_gate"], lw["ffn2_w_up"], dr3, alpha,
                                    (sv["xh2"], sv["rs2"], row(ln2_g), 1.0))
        small_grads["ln2_g"][l], small_grads["ln2_b"][l] = s8(pg), s8(pb)
        big_grads["ffn2_w_down"][l] = _wgrad("wgrad_down", [sv["h2"]], [drs3], shards, _lshape(lw["ffn2_w_down"])[1:],
                                             lambda a, s: a[0][s], lambda b, s: b[0][...])
        big_grads["ffn2_w_gate"][l] = _wgrad("wgrad_up", [sv["x2"]], [dgt], shards, _lshape(lw["ffn2_w_gate"])[1:],
                                             lambda a, s: a[0][...], lambda b, s: b[0][s])
        big_grads["ffn2_w_up"][l] = _wgrad("wgrad_up", [sv["x2"]], [dut], shards, _lshape(lw["ffn2_w_up"])[1:],
                                           lambda a, s: a[0][...], lambda b, s: b[0][s])
        do_sb, do_ca, pgs, pgc = _attn_dmix(drs2, lw["w_out"], sv["o_sb"], sv["o_ca"], row(sb_out_g), row(ca_out_g))
        small_grads["sb_out_g"][l], small_grads["ca_out_g"][l] = s8(pgs), s8(pgc)
        big_grads["w_out"][l] = _wgrad(
            "wgrad_out", [sv["m_sb"], sv["m_ca"]], [drs2], shards, (qd, d),
            lambda a, s: a[s // 2][:, (s % 2) * qd:(s % 2 + 1) * qd], lambda b, s: b[0][...])
        if l + 1 < depth:
            dqa, dka, dva, theirs = _sb_bwd(sv["qkv"], sv["c_sb"], do_sb, bl, seq, dg, partial)
            chip_sum = _add_pair(partial, theirs)
            dqb, dkb, dvb, dbias[l], arrived[l + 1] = _ca_bwd(sv["qkv"], bias_all[l], sv["o_ca"], do_ca, bl, seq, dg,
                                                             chip_sum)
            chip_sums[l + 1] = chip_sum
        else:
            dqa, dka, dva = _sb_bwd(sv["qkv"], sv["c_sb"], do_sb, bl, seq, dg)
            dqb, dkb, dvb, dbias[l] = _ca_bwd(sv["qkv"], bias_all[l], sv["o_ca"], do_ca, bl, seq, dg)
        dqkv = [dqa, dka, dva, dqb, dkb, dvb]
        dr1, drs1, pg, pb = _attn_dx(dqkv, lw["w_in"], dr2, alpha, (sv["xh1"], sv["rs1"], row(ln1_g), FFN_RESIDUAL))
        small_grads["ln1_g"][l], small_grads["ln1_b"][l] = s8(pg), s8(pb)
        big_grads["w_in"][l] = _wgrad(
            "wgrad_in", [sv["x1"]], dqkv, shards, (d, wd_), lambda a, s: a[0][...],
            lambda b, s: jnp.concatenate([b[gi][:, lo:hi] for gi, lo, hi in in_pieces[s]], axis=1))
        dgt, dut = _swiglu_bwd(drs1, lw["ffn1_w_down"], sv["g1"], sv["u1"])
        if l > 0:
            prev = saved[l - 1]
            dr3, drs3, pg, pb = _ffn_dx(dgt, dut, lw["ffn1_w_gate"], lw["ffn1_w_up"], dr1, alpha,
                                        (prev["xh3"], prev["rs3"], ln3_g[l - 1][None, :], FFN_RESIDUAL))
        else:
            grad_x = _ffn_dx(dgt, dut, lw["ffn1_w_gate"], lw["ffn1_w_up"], dr1, alpha, None).reshape(bl, seq, d)
        big_grads["ffn1_w_down"][l] = _wgrad("wgrad_down", [sv["h1"]], [drs1], shards, _lshape(lw["ffn1_w_down"])[1:],
                                             lambda a, s: a[0][s], lambda b, s: b[0][...])
        big_grads["ffn1_w_gate"][l] = _wgrad("wgrad_up", [sv["xin"]], [dgt], shards, _lshape(lw["ffn1_w_gate"])[1:],
                                             lambda a, s: a[0][...], lambda b, s: b[0][s])
        big_grads["ffn1_w_up"][l] = _wgrad("wgrad_up", [sv["xin"]], [dut], shards, _lshape(lw["ffn1_w_up"])[1:],
                                           lambda a, s: a[0][...], lambda b, s: b[0][s])
        partial = [big_grads[n][l][None] for n in BIG]
    chip_sums[0] = _add_pair(partial, _swap_halves(partial))
    arrived[0] = _scatter_chips(chip_sums[0])
    rel_grads = bias_vjp(jnp.stack(dbias))[0]
    for l in range(depth):
        small_grads["rel_bias"][l] = rel_grads[l]

    halves = []
    for lo_k in range(0, len(BIG), 4):
        part = None
        for l in range(depth):
            part = _add_chips(chip_sums[l][lo_k:lo_k + 4], arrived[l][lo_k:lo_k + 4], part, l, depth)
        halves += list(part)
    joined = dict(zip(BIG, _join_halves(halves)))
    packed = _allreduce_small(_pack_small({n: jnp.stack(small_grads[n]) for n in SMALL}, depth, d))

    grads, delta, new_m, new_v = {}, {}, {}, {}
    for n in BIG:
        grads[n], delta[n], new_m[n], new_v[n] = _adamw(w[n], joined[n], mom[n], var[n])
    pw, pm, pv = (_pack_small({n: src[n] for n in SMALL}, depth, d) for src in (w, mom, var))
    sg, sd, sm, sv_ = _adamw(pw, packed, pm, pv)
    for dst, src in ((grads, sg), (delta, sd), (new_m, sm), (new_v, sv_)):
        dst.update(_unpack_small(src, depth, d, heads, n_rel))

    return (loss, grad_x, *[grads[n] for n in ORDER], *[delta[n] for n in ORDER],
            *[new_m[n] for n in ORDER], *[new_v[n] for n in ORDER])
```

```python
import functools
import math

import jax
import jax.numpy as jnp
from jax import lax
from jax.experimental import pallas as pl
from jax.experimental.pallas import tpu as pltpu

F32 = jnp.float32
BF = jnp.bfloat16

CHUNK = 64
N_PREV_CHUNKS = 8
BAND = CHUNK * (N_PREV_CHUNKS + 1)
MAX_REL = 128
HEAD_DIM = 64
FFN_RESIDUAL = 0.5
LN_EPS = 1e-5
RMS_EPS = 1e-6
ADAM_LR = 0.001
ADAM_B1 = 0.9
ADAM_B2 = 0.999
ADAM_EPS = 1e-08
ADAM_WD = 0.01
ADAM_STEP = 10

LANES = 128
ATT_TILE = 4 * CHUNK
SB_TQ = 512
SB_TK = 256
SB_DEAD = -105.0
SB_UNSEEN = -1e30
NEG_BIAS = -1e30
VMEM_LIMIT = 56 * 1024 * 1024
ROW_TILE = 512
MESH_IDS = pl.DeviceIdType.MESH

_NT = (((1,), (1,)), ((), ()))
_TN = (((0,), (0,)), ((), ()))


def _dot(a, b):
    return jnp.dot(a, b, preferred_element_type=F32)


def _dot_nt(a, b):
    return lax.dot_general(a, b, _NT, preferred_element_type=F32)


def _dot_tn(a, b):
    return lax.dot_general(a, b, _TN, preferred_element_type=F32)


def _tile(n, pref):
    t = min(n, pref)
    while n % t:
        t //= 2
    return t


def _cparams(sem):
    return pltpu.CompilerParams(dimension_semantics=sem, vmem_limit_bytes=VMEM_LIMIT)


def _sds(shape, dtype):
    return jax.ShapeDtypeStruct(tuple(shape), dtype)


def _sigmoid(v):
    return pl.reciprocal(1.0 + jnp.exp(-v), approx=True)


def _rows8(v):
    tm, c = v.shape
    return v.reshape(tm // 8, 8, c).sum(axis=0)


class _LayerOf(tuple):
    pass


def _lshape(item):
    return item[0].shape[1:] if isinstance(item, _LayerOf) else item.shape


def _accumulate(m, ref, part):
    @pl.when(m == 0)
    def _():
        ref[...] = part

    @pl.when(m > 0)
    def _():
        ref[...] += part


def _ln_bwd_tail(m, dy, xh_ref, rs_ref, g_ref, scale, dr_ref, drs_ref, dg_ref, db_ref):
    xhv = xh_ref[...]
    dxh = dy * g_ref[...]
    m1 = jnp.mean(dxh, axis=-1, keepdims=True)
    m2 = jnp.mean(dxh * xhv, axis=-1, keepdims=True)
    dr = rs_ref[...] * (dxh - m1 - xhv * m2)
    dr_ref[...] = dr
    drs_ref[...] = (scale * dr).astype(BF)
    _accumulate(m, dg_ref, _rows8(dy * xhv))
    _accumulate(m, db_ref, _rows8(dy))


def _rowcall(name, body, n_rows, tm, row_ins, res_ins, row_outs, acc_outs=()):
    def rspec(shape):
        if len(shape) == 2:
            return pl.BlockSpec((tm, shape[1]), lambda m: (m, 0))
        return pl.BlockSpec((shape[0], tm, shape[2]), lambda m: (0, m, 0))

    def cspec(shape, single=False):
        zeros = (0,) * len(shape)
        mode = pl.Buffered(1) if single else None
        return pl.BlockSpec(tuple(shape), lambda m: zeros, pipeline_mode=mode)

    def split(items, mk):
        arrs, specs = [], []
        for it in items:
            if isinstance(it, _LayerOf):
                arr, l = it
                tail = (0,) * (arr.ndim - 1)
                arrs.append(arr)
                specs.append(pl.BlockSpec((None,) + arr.shape[1:], functools.partial(lambda m, l, tail: (l,) + tail, l=l, tail=tail),
                                          pipeline_mode=pl.Buffered(1)))
            elif isinstance(it, tuple):
                arrs.append(it[0])
                specs.append(it[1])
            else:
                arrs.append(it)
                specs.append(mk(it.shape))
        return arrs, specs

    ra, rs = split(row_ins, rspec)
    ca, cs = split(res_ins, functools.partial(cspec, single=True))
    out_specs = [rspec(s.shape) for s in row_outs] + [cspec(s.shape) for s in acc_outs]
    sem = ("arbitrary",) if acc_outs else ("parallel",)
    return pl.pallas_call(
        body, name=name, grid=(n_rows // tm,), in_specs=rs + cs, out_specs=out_specs,
        out_shape=list(row_outs) + list(acc_outs), compiler_params=_cparams(sem),
    )(*ra, *ca)


def _ffn_up(x, wg, wu):
    t, _ = x.shape
    s_n, _, fs = _lshape(wg)
    tm = _tile(t, ROW_TILE)

    def body(x_ref, wg_ref, wu_ref, g_ref, u_ref, h_ref):
        xb = x_ref[...].astype(BF)
        for s in range(s_n):
            g = _dot(xb, wg_ref[s])
            u = _dot(xb, wu_ref[s])
            g_ref[s] = g.astype(BF)
            u_ref[s] = u.astype(BF)
            h_ref[s] = (g * _sigmoid(g) * u).astype(BF)

    out = _sds((s_n, t, fs), BF)
    return _rowcall("ffn_up", body, t, tm, [x], [wg, wu], [out, out, out])


def _proj_ln(name, a_list, w, terms, xres, gain, bias, scale, alpha):
    t, d = xres.shape
    tm = _tile(t, ROW_TILE)
    na = len(a_list)

    def body(*refs):
        a_refs = refs[:na]
        x_ref, w_ref, g_ref, bb_ref = refs[na:na + 4]
        y_ref, xh_ref, rs_ref = refs[na + 4:]
        acc = None
        for i, s, cols, sw in terms:
            a = a_refs[i][s] if cols is None else a_refs[i][:, cols[0]:cols[1]]
            p = _dot(a.astype(BF), w_ref[sw])
            acc = p if acc is None else acc + p
        if scale != 1.0:
            acc = scale * acc
        r = alpha * x_ref[...] + acc
        mu = jnp.mean(r, axis=-1, keepdims=True)
        xc = r - mu
        var = jnp.mean(xc * xc, axis=-1, keepdims=True)
        rstd = lax.rsqrt(var + LN_EPS)
        xh = xc * rstd
        y_ref[...] = xh * g_ref[...] + bb_ref[...]
        xh_ref[...] = xh
        rs_ref[...] = rstd

    outs = [_sds((t, d), F32), _sds((t, d), F32), _sds((t, 1), F32)]
    return _rowcall(name, body, t, tm, list(a_list) + [xres], [w, gain, bias], outs)


def _shard_cols(groups, dg, s, width):
    pieces, lo = [], s * width
    while lo < (s + 1) * width:
        gi = lo // dg
        hi = min((gi + 1) * dg, (s + 1) * width)
        pieces.append((gi, lo - gi * dg, hi - gi * dg))
        lo = hi
    assert all(p[0] < groups for p in pieces)
    return pieces


def _qkv_proj(x, win):
    t, _ = x.shape
    s_n, _, wd_ = _lshape(win)
    tm = _tile(t, ROW_TILE)

    def body(x_ref, w_ref, o_ref):
        xb = x_ref[...].astype(BF)
        for s in range(s_n):
            o_ref[:, s * wd_:(s + 1) * wd_] = _dot(xb, w_ref[s]).astype(BF)

    return _rowcall("qkv_proj", body, t, tm, [x], [win], [_sds((t, s_n * wd_), BF)])[0]


def _dx_call(name, rows, res, matmul, resid, alpha, ln):
    t, d = resid.shape
    tm = _tile(t, ROW_TILE)
    nr, nc = len(rows), len(res)

    def body(*refs):
        r_refs, resid_ref = refs[:nr], refs[nr]
        if ln is None:
            c_refs, outs = refs[nr + 1:nr + 1 + nc], refs[nr + 1 + nc:]
            outs[0][...] = matmul(r_refs, c_refs) + alpha * resid_ref[...]
        else:
            xh_ref, rs_ref = refs[nr + 1:nr + 3]
            c_refs, g_ref = refs[nr + 3:nr + 3 + nc], refs[nr + 3 + nc]
            outs = refs[nr + 4 + nc:]
            dx = matmul(r_refs, c_refs) + alpha * resid_ref[...]
            _ln_bwd_tail(pl.program_id(0), dx, xh_ref, rs_ref, g_ref, ln[3], *outs)

    if ln is None:
        return _rowcall(name, body, t, tm, list(rows) + [resid], list(res), [_sds((t, d), F32)])[0]
    outs = [_sds((t, d), F32), _sds((t, d), BF)]
    accs = [_sds((8, d), F32), _sds((8, d), F32)]
    return _rowcall(name, body, t, tm, list(rows) + [resid, ln[0], ln[1]], list(res) + [ln[2]], outs, accs)


def _ffn_dx(dg_, du_, wg, wu, resid, alpha, ln):
    s_n = dg_.shape[0]

    def matmul(r, c):
        acc = None
        for i in range(2):
            for s in range(s_n):
                p = _dot_nt(r[i][s], c[i][s])
                acc = p if acc is None else acc + p
        return acc

    return _dx_call("ffn_dx", [dg_, du_], [wg, wu], matmul, resid, alpha, ln)


def _attn_dx(dqkv, win, resid, alpha, ln):
    s_n, _, wd_ = _lshape(win)
    dg = dqkv[0].shape[1]
    pieces = [_shard_cols(len(dqkv), dg, s, wd_) for s in range(s_n)]

    def matmul(r, c):
        acc = None
        for s in range(s_n):
            a = jnp.concatenate([r[gi][:, lo:hi] for gi, lo, hi in pieces[s]], axis=1)
            p = _dot_nt(a, c[0][s])
            acc = p if acc is None else acc + p
        return acc

    return _dx_call("attn_dx", list(dqkv), [win], matmul, resid, alpha, ln)


def _head_rms_bwd(dm, ov, gain, lo):
    ss = ov * ov
    s0 = jnp.sum(jnp.where(lo, ss, 0.0), axis=1, keepdims=True)
    s1 = jnp.sum(jnp.where(lo, 0.0, ss), axis=1, keepdims=True)
    r = jnp.where(lo, lax.rsqrt(s0 / HEAD_DIM + RMS_EPS), lax.rsqrt(s1 / HEAD_DIM + RMS_EPS))
    n = ov * r
    dn = dm * gain
    tt = dn * n
    t0 = jnp.sum(jnp.where(lo, tt, 0.0), axis=1, keepdims=True) / HEAD_DIM
    t1 = jnp.sum(jnp.where(lo, 0.0, tt), axis=1, keepdims=True) / HEAD_DIM
    return r * (dn - n * jnp.where(lo, t0, t1)), dm * n


def _attn_dmix(drs, wout, o_sb, o_ca, g_sb, g_ca):
    t, dg = o_sb.shape
    s_n, qd, _ = _lshape(wout)
    tm = _tile(t, ROW_TILE)
    per = dg // qd

    def body(d_ref, osb_ref, oca_ref, w_ref, gs_ref, gc_ref, dsb_ref, dca_ref, pgs_ref, pgc_ref):
        m = pl.program_id(0)
        dv = d_ref[...]
        lo = lax.broadcasted_iota(jnp.int32, (1, LANES), 1) < HEAD_DIM
        for grp, (o_ref, g_ref, do_ref, pg_ref) in enumerate(((osb_ref, gs_ref, dsb_ref, pgs_ref),
                                                               (oca_ref, gc_ref, dca_ref, pgc_ref))):
            dm = jnp.concatenate([_dot_nt(dv, w_ref[grp * per + k]) for k in range(per)], axis=1)
            parts = []
            for gi in range(dg // LANES):
                sl = slice(gi * LANES, (gi + 1) * LANES)
                do, pg = _head_rms_bwd(dm[:, sl], o_ref[:, sl], g_ref[:, sl], lo)
                do_ref[:, sl] = do.astype(BF)
                parts.append(_rows8(pg))
            _accumulate(m, pg_ref, jnp.concatenate(parts, axis=1))

    outs = [_sds((t, dg), BF), _sds((t, dg), BF)]
    accs = [_sds((8, dg), F32), _sds((8, dg), F32)]
    return _rowcall("attn_dmix", body, t, tm, [drs, o_sb, o_ca], [wout, g_sb, g_ca], outs, accs)


def _swiglu_bwd(drs, wd, g, u):
    s_n, t, fs = g.shape
    tm = _tile(t, ROW_TILE)
    w_all, layer = wd
    d = w_all.shape[-1]

    def body(d_ref, g_ref, u_ref, wd_ref, dg_ref, du_ref):
        dh = _dot_nt(d_ref[...], wd_ref[pl.program_id(1)])
        gv = g_ref[...].astype(F32)
        uv = u_ref[...].astype(F32)
        sg = _sigmoid(gv)
        gs = gv * sg
        dg_ref[...] = (dh * uv * (sg + gs * (1.0 - sg))).astype(BF)
        du_ref[...] = (dh * gs).astype(BF)

    blk = pl.BlockSpec((None, tm, fs), lambda m, s: (s, m, 0))
    out = _sds((s_n, t, fs), BF)
    return pl.pallas_call(
        body, name="swiglu_bwd", grid=(t // tm, s_n),
        in_specs=[pl.BlockSpec((tm, d), lambda m, s: (m, 0)), blk, blk,
                  pl.BlockSpec((None,) + w_all.shape[1:], lambda m, s: (layer, 0, 0, 0), pipeline_mode=pl.Buffered(1))],
        out_specs=[blk, blk], out_shape=[out, out], compiler_params=_cparams(("parallel", "arbitrary")),
    )(drs, g, u, w_all)


def _loss_grad(y, target, xh, rstd, gain, scale):
    t, d = y.shape
    tm = _tile(t, ROW_TILE)

    def body(y_ref, t_ref, xh_ref, rs_ref, g_ref, dr_ref, drs_ref, dg_ref, db_ref, sq_ref):
        m = pl.program_id(0)
        diff = y_ref[...] - t_ref[...]
        _ln_bwd_tail(m, diff / d, xh_ref, rs_ref, g_ref, scale, dr_ref, drs_ref, dg_ref, db_ref)
        _accumulate(m, sq_ref, _rows8(diff * diff))

    outs = [_sds((t, d), F32), _sds((t, d), BF)]
    accs = [_sds((8, d), F32)] * 3
    return _rowcall("loss_grad", body, t, tm, [y, target, xh, rstd], [gain], outs, accs)


def _adamw(w, g, m, v):
    shape = w.shape
    c = shape[-1]
    rows = math.prod(shape[:-1])
    tm = _tile(rows, 512)

    def body(w_ref, g_ref, m_ref, v_ref, go_ref, d_ref, nm_ref, nv_ref):
        gv = g_ref[...]
        go_ref[...] = gv
        nm = ADAM_B1 * m_ref[...] + (1.0 - ADAM_B1) * gv
        nv = ADAM_B2 * v_ref[...] + (1.0 - ADAM_B2) * (gv * gv)
        m_hat = nm / (1.0 - ADAM_B1 ** ADAM_STEP)
        v_hat = nv / (1.0 - ADAM_B2 ** ADAM_STEP)
        d_ref[...] = -ADAM_LR * (m_hat / (jnp.sqrt(v_hat) + ADAM_EPS) + ADAM_WD * w_ref[...])
        nm_ref[...] = nm
        nv_ref[...] = nv

    flat = [a.reshape(rows, c) for a in (w, g, m, v)]
    out = _sds((rows, c), F32)
    res = _rowcall("adamw", body, rows, tm, flat, [], [out, out, out, out])
    return tuple(r.reshape(shape) for r in res)


def _wgrad(name, a_list, b_list, out_sub, sub_shape, a_get, b_get):
    t = a_list[0].shape[-2]
    tk = _tile(t, 1024)
    na, nb = len(a_list), len(b_list)
    nk = t // tk

    def tspec(shape):
        if len(shape) == 2:
            return pl.BlockSpec((tk, shape[1]), lambda k: (k, 0))
        return pl.BlockSpec((shape[0], tk, shape[2]), lambda k: (0, k, 0))

    def body(*refs):
        a_refs = refs[:na]
        b_refs = refs[na:na + nb]
        o_ref = refs[na + nb]
        acc_ref = refs[na + nb + 1]
        k = pl.program_id(0)

        @pl.when(k == 0)
        def _():
            acc_ref[...] = jnp.zeros_like(acc_ref)

        for s in range(out_sub):
            acc_ref[s] += _dot_tn(a_get(a_refs, s).astype(BF), b_get(b_refs, s).astype(BF))

        @pl.when(k == nk - 1)
        def _():
            o_ref[...] = acc_ref[...].astype(BF)

    full = (out_sub,) + tuple(sub_shape)
    return pl.pallas_call(
        body, name=name, grid=(nk,), in_specs=[tspec(a.shape) for a in a_list] + [tspec(b.shape) for b in b_list],
        out_specs=pl.BlockSpec(full, lambda k: (0, 0, 0)), out_shape=_sds(full, BF),
        scratch_shapes=[pltpu.VMEM(full, F32)], compiler_params=_cparams(("arbitrary",)),
    )(*a_list, *b_list)


def _head_masks(x):
    lo = lax.broadcasted_iota(jnp.int32, (1, LANES), 1) < HEAD_DIM
    zero = jnp.zeros_like(x)
    return lo, (jnp.where(lo, x, zero), jnp.where(lo, zero, x))


def _logistic_parts(z):
    nz = -z
    t = jnp.exp(jnp.minimum(z, nz))
    one_t = 1.0 + t
    return jnp.minimum(nz, 0.0) - jnp.log(one_t), t, one_t


def _split2(x):
    hi = x.astype(BF)
    lo = (x - hi.astype(F32)).astype(BF)
    return jnp.concatenate([hi, lo], axis=1)


def _tri2(cond):
    tri = jnp.where(cond, 1.0, 0.0).astype(BF)
    return jnp.concatenate([tri, tri], axis=0)


def _sb_specs(bl, seq, p_n, col0, tq):
    nq = seq // tq
    q_spec = pl.BlockSpec((tq, LANES), lambda b, p, i: (b * nq + i, col0 + p))
    k_spec = pl.BlockSpec((seq, LANES), lambda b, p, i: (b, col0 + p_n + p))
    v_spec = pl.BlockSpec((seq, LANES), lambda b, p, i: (b, col0 + 2 * p_n + p))
    tile_spec = pl.BlockSpec((tq, LANES), lambda b, p, i: (b * nq + i, p))
    seq_spec = pl.BlockSpec((seq, LANES), lambda b, p, i: (b, p))
    g_spec = pl.BlockSpec((1, LANES), lambda b, p, i: (0, p))
    return nq, q_spec, k_spec, v_spec, tile_spec, seq_spec, g_spec


def _rms_out(acc, lo, gain):
    ss = acc * acc
    s0 = jnp.sum(jnp.where(lo, ss, 0.0), axis=1, keepdims=True)
    s1 = jnp.sum(jnp.where(lo, 0.0, ss), axis=1, keepdims=True)
    r = jnp.where(lo, lax.rsqrt(s0 / HEAD_DIM + RMS_EPS), lax.rsqrt(s1 / HEAD_DIM + RMS_EPS))
    return acc * r * gain


def _sb_fwd(qkv, bl, seq, dg, gain, bufs=(), layer=None):
    t = bl * seq
    p_n = dg // LANES
    tq, tk = min(SB_TQ, seq), SB_TK
    ratio = tq // tk
    scale = 1.0 / math.sqrt(HEAD_DIM)
    nq, q_spec, k_spec, v_spec, tile_spec, _, g_spec = _sb_specs(bl, seq, p_n, 0, tq)
    assert seq // tk <= LANES
    nt = len(bufs)

    def body(q_ref, k_ref, v_ref, g_ref, *rest):
        o_ref, m_ref, c_ref = rest[nt:nt + 3]
        i = pl.program_id(2)
        if nt:
            start, finish = _gather_plan(rest[nt + 3:2 * nt + 3], [b.shape[2] for b in bufs], rest[2 * nt + 3],
                                         rest[2 * nt + 4], layer)
            grid_step = (pl.program_id(0) * p_n + pl.program_id(1)) * nq + i
            pl.when(grid_step == 0)(start)
        row = lax.broadcasted_iota(jnp.int32, (tq, tk), 0)
        col = lax.broadcasted_iota(jnp.int32, (tq, tk), 1)
        krow = lax.broadcasted_iota(jnp.int32, (tk, tk), 0)
        kcol = lax.broadcasted_iota(jnp.int32, (tk, tk), 1)
        tri2 = _tri2(krow > kcol)
        lo, qh = _head_masks(q_ref[...] * jnp.asarray(scale, BF))
        lane = lax.broadcasted_iota(jnp.int32, (1, LANES), 1)

        def step(j, carry, diag):
            acc, c0, c1, s0, s1 = carry
            cars, seen = (c0, c1), (s0, s1)
            mask = None if diag is None else row > col + diag
            off = pl.multiple_of(j * tk, tk)
            kj = k_ref[pl.ds(off, tk), :]
            vj = v_ref[pl.ds(off, tk), :]
            zs = [_dot_nt(qh[h], kj) for h in range(2)]
            lks = [_logistic_parts(z)[0] for z in zs]
            if mask is not None:
                lks = [jnp.where(mask, lk, 0.0) for lk in lks]
            sins = [_dot(_split2(lk), tri2) for lk in lks]
            ws = []
            for h in range(2):
                w = jnp.exp((lks[h] + zs[h]) + (sins[h] + cars[h]))
                ws.append((w if mask is None else jnp.where(mask, w, 0.0)).astype(BF))
            pvs = [_dot(w, vj) for w in ws]
            new_seen = [jnp.where(lane == j, cars[h], seen[h]) for h in range(2)]
            new_cars = [cars[h] + jnp.sum(lks[h], axis=1, keepdims=True) for h in range(2)]
            return acc + jnp.where(lo, pvs[0], pvs[1]), new_cars[0], new_cars[1], new_seen[0], new_seen[1]

        def corner_step(j):
            top = tq - tk
            mask = krow > kcol
            kj = k_ref[pl.ds(pl.multiple_of(j * tk, tk), tk), :]
            vj = v_ref[pl.ds(pl.multiple_of(j * tk, tk), tk), :]
            zs = [_dot_nt(qh[h][top:], kj) for h in range(2)]
            lks = [jnp.where(mask, _logistic_parts(z)[0], 0.0) for z in zs]
            sins = [_dot(_split2(lk), tri2) for lk in lks]
            ws = [jnp.where(mask, jnp.exp((lks[h] + zs[h]) + sins[h]), 0.0).astype(BF) for h in range(2)]
            pvs = [_dot(w, vj) for w in ws]
            pad = lambda low: jnp.concatenate([jnp.zeros((top,) + low.shape[1:], F32), low], axis=0)
            cars = [pad(jnp.sum(lk, axis=1, keepdims=True)) for lk in lks]
            seen = jnp.where(lane == j, 0.0, jnp.full((tq, LANES), SB_UNSEEN, F32))
            return pad(jnp.where(lo, pvs[0], pvs[1])), cars[0], cars[1], seen, seen

        if ratio == 2:
            carry = step(ratio * i, corner_step(ratio * i + 1), 0)
        else:
            zc = jnp.zeros((tq, 1), F32)
            unseen = jnp.full((tq, LANES), SB_UNSEEN, F32)
            carry = (jnp.zeros((tq, LANES), F32), zc, zc, unseen, unseen)
            for m in reversed(range(ratio)):
                carry = step(ratio * i + m, carry, m * tk)

        def alive(state):
            j, c = state
            return jnp.logical_and(j >= 0, jnp.max(jnp.maximum(c[1], c[2])) >= SB_DEAD)

        _, carry = lax.while_loop(alive, lambda st: (st[0] - 1, step(st[0], st[1], None)), (ratio * i - 1, carry))
        acc = carry[0]
        o_ref[...] = acc
        m_ref[...] = _rms_out(acc, lo, g_ref[...]).astype(BF)
        c_ref[:, :LANES] = carry[3]
        c_ref[:, LANES:] = carry[4]
        if nt:
            pl.when(grid_step == bl * p_n * nq - 1)(finish)

    car_spec = pl.BlockSpec((tq, 2 * LANES), lambda b, p, i: (b * nq + i, p))
    outs = [_sds((t, dg), F32), _sds((t, dg), BF), _sds((t, 2 * dg), F32)]
    if not nt:
        return pl.pallas_call(
            body, name="sb_fwd", grid=(bl, p_n, nq), in_specs=[q_spec, k_spec, v_spec, g_spec],
            out_specs=[tile_spec, tile_spec, car_spec], out_shape=outs,
            compiler_params=_cparams(("parallel", "parallel", "parallel")),
        )(qkv, qkv, qkv, gain)
    res = pl.pallas_call(
        body, name="sb_fwd_gather", grid=(bl, p_n, nq), in_specs=[q_spec, k_spec, v_spec, g_spec] + [_ANY] * nt,
        out_specs=[tile_spec, tile_spec, car_spec] + [_ANY] * nt,
        out_shape=outs + [_sds(b.shape, b.dtype) for b in bufs],
        input_output_aliases={4 + k: 3 + k for k in range(nt)},
        scratch_shapes=[pltpu.SemaphoreType.DMA((6 * nt,)), pltpu.SemaphoreType.DMA((6 * nt,))],
        compiler_params=_cparams(("arbitrary", "arbitrary", "arbitrary")),
    )(qkv, qkv, qkv, gain, *bufs)
    return res[0], res[1], res[2], list(res[3:])


def _sb_bwd(qkv, cars, do, bl, seq, dg, grads=()):
    t = bl * seq
    p_n = dg // LANES
    tq, tk = min(SB_TQ, seq), SB_TK
    ratio = tq // tk
    scale = 1.0 / math.sqrt(HEAD_DIM)
    nq, q_spec, k_spec, v_spec, tile_spec, seq_spec, _ = _sb_specs(bl, seq, p_n, 0, tq)
    car_spec = pl.BlockSpec((tq, 2 * LANES), lambda b, p, i: (b * nq + i, p))
    nt = len(grads)

    def body(q_ref, k_ref, v_ref, c_ref, do_ref, *rest):
        dq_ref, dk_ref, dv_ref = rest[nt:nt + 3]
        dk_acc, dv_acc = rest[2 * nt + 3:2 * nt + 5]
        i = pl.program_id(2)
        if nt:
            start, finish = _swap_plan(rest[:nt], rest[nt + 3:2 * nt + 3], [g.shape[2] for g in grads],
                                       rest[2 * nt + 5], rest[2 * nt + 6])
            grid_step = (pl.program_id(0) * p_n + pl.program_id(1)) * nq + i
            pl.when(grid_step == 0)(start)

        @pl.when(i == 0)
        def _():
            dk_acc[...] = jnp.zeros_like(dk_acc)
            dv_acc[...] = jnp.zeros_like(dv_acc)

        row = lax.broadcasted_iota(jnp.int32, (tq, tk), 0)
        col = lax.broadcasted_iota(jnp.int32, (tq, tk), 1)
        krow = lax.broadcasted_iota(jnp.int32, (tk, tk), 0)
        kcol = lax.broadcasted_iota(jnp.int32, (tk, tk), 1)
        tri2 = _tri2(krow > kcol)
        tri_pre = jnp.where(krow < kcol, 1.0, 0.0).astype(BF)
        lane = lax.broadcasted_iota(jnp.int32, (1, LANES), 1)
        sc = jnp.asarray(scale, BF)
        qs = q_ref[...] * sc
        dob = do_ref[...]
        lo, qh = _head_masks(qs)
        _, doh = _head_masks(dob)
        seen = (c_ref[:, :LANES], c_ref[:, LANES:])

        def step(j, carry, diag):
            dq, e0, e1 = carry
            ecars = (e0, e1)
            mask = None if diag is None else row > col + diag
            off = pl.multiple_of(j * tk, tk)
            kj = k_ref[pl.ds(off, tk), :]
            vj = v_ref[pl.ds(off, tk), :]
            zs = [_dot_nt(qh[h], kj) for h in range(2)]
            dws = [_dot_nt(doh[h], vj) for h in range(2)]
            parts = [_logistic_parts(z) for z in zs]
            lks = [p[0] if mask is None else jnp.where(mask, p[0], 0.0) for p in parts]
            sins = [_dot(_split2(lk), tri2) for lk in lks]
            ws, es = [], []
            for h in range(2):
                car = jnp.sum(jnp.where(lane == j, seen[h], 0.0), axis=1, keepdims=True)
                w = jnp.exp((lks[h] + zs[h]) + (sins[h] + car))
                w = w if mask is None else jnp.where(mask, w, 0.0)
                ws.append(w.astype(BF))
                es.append(dws[h] * w)
            pres = [_dot(e.astype(BF), tri_pre) for e in es]
            dzs = []
            for h in range(2):
                t_, e = parts[h][1], es[h]
                before = pres[h] + ecars[h]
                dz = pl.reciprocal(parts[h][2], approx=True) * jnp.where(zs[h] > 0.0, e * t_ - before, e - t_ * before)
                dzs.append((dz if mask is None else jnp.where(mask, dz, 0.0)).astype(BF))
            kjs = kj * sc
            dqs = [_dot(dz, kjs) for dz in dzs]
            dks = [_dot_tn(dz, qs) for dz in dzs]
            dvs = [_dot_tn(w, dob) for w in ws]
            dk_acc[pl.ds(off, tk), :] += jnp.where(lo, dks[0], dks[1])
            dv_acc[pl.ds(off, tk), :] += jnp.where(lo, dvs[0], dvs[1])
            new_e = [ecars[h] + jnp.sum(es[h], axis=1, keepdims=True) for h in range(2)]
            return dq + jnp.where(lo, dqs[0], dqs[1]), new_e[0], new_e[1]

        reach = jnp.max(jnp.maximum(seen[0], seen[1]), axis=0, keepdims=True)
        skipped = jnp.logical_and(reach < SB_DEAD, lane < ratio * i)
        first = jnp.minimum(jnp.sum(jnp.where(skipped, 1, 0)).astype(jnp.int32), ratio * i)
        zc = jnp.zeros((tq, 1), F32)
        def corner_step(j, carry):
            dq, e0, e1 = carry
            top = tq - tk
            mask = krow > kcol
            off = pl.multiple_of(j * tk, tk)
            kj = k_ref[pl.ds(off, tk), :]
            vj = v_ref[pl.ds(off, tk), :]
            zs = [_dot_nt(qh[h][top:], kj) for h in range(2)]
            dws = [_dot_nt(doh[h][top:], vj) for h in range(2)]
            parts = [_logistic_parts(z) for z in zs]
            lks = [jnp.where(mask, p[0], 0.0) for p in parts]
            sins = [_dot(_split2(lk), tri2) for lk in lks]
            ws, es = [], []
            for h in range(2):
                car = jnp.sum(jnp.where(lane == j, seen[h][top:], 0.0), axis=1, keepdims=True)
                w = jnp.where(mask, jnp.exp((lks[h] + zs[h]) + (sins[h] + car)), 0.0)
                ws.append(w.astype(BF))
                es.append(dws[h] * w)
            pres = [_dot(e.astype(BF), tri_pre) for e in es]
            dzs = []
            for h, ecar in enumerate((e0, e1)):
                t_, e = parts[h][1], es[h]
                before = pres[h] + ecar[top:]
                dz = pl.reciprocal(parts[h][2], approx=True) * jnp.where(zs[h] > 0.0, e * t_ - before, e - t_ * before)
                dzs.append(jnp.where(mask, dz, 0.0).astype(BF))
            kjs = kj * sc
            dqs = [_dot(dz, kjs) for dz in dzs]
            dks = [_dot_tn(dz, qs[top:]) for dz in dzs]
            dvs = [_dot_tn(w, dob[top:]) for w in ws]
            dk_acc[pl.ds(off, tk), :] += jnp.where(lo, dks[0], dks[1])
            dv_acc[pl.ds(off, tk), :] += jnp.where(lo, dvs[0], dvs[1])
            return dq + jnp.concatenate([jnp.zeros((top, LANES), F32), jnp.where(lo, dqs[0], dqs[1])], axis=0)

        carry = lax.fori_loop(first, ratio * i, lambda j, c: step(j, c, None), (jnp.zeros((tq, LANES), F32), zc, zc))
        if ratio == 2:
            dq = corner_step(ratio * i + 1, step(ratio * i, carry, 0))
        else:
            for m in range(ratio):
                carry = step(ratio * i + m, carry, m * tk)
            dq = carry[0]
        dq_ref[...] = dq.astype(BF)

        @pl.when(i == nq - 1)
        def _():
            dk_ref[...] = dk_acc[...].astype(BF)
            dv_ref[...] = dv_acc[...].astype(BF)

        if nt:
            pl.when(grid_step == bl * p_n * nq - 1)(finish)

    out = _sds((t, dg), BF)
    acc = [pltpu.VMEM((seq, LANES), F32), pltpu.VMEM((seq, LANES), F32)]
    if not nt:
        return pl.pallas_call(
            body, name="sb_bwd", grid=(bl, p_n, nq),
            in_specs=[q_spec, k_spec, v_spec, car_spec, tile_spec],
            out_specs=[tile_spec, seq_spec, seq_spec], out_shape=[out, out, out],
            scratch_shapes=acc, compiler_params=_cparams(("parallel", "parallel", "arbitrary")),
        )(qkv, qkv, qkv, cars, do)
    res = pl.pallas_call(
        body, name="sb_bwd_swap", grid=(bl, p_n, nq),
        in_specs=[q_spec, k_spec, v_spec, car_spec, tile_spec] + [_ANY] * nt,
        out_specs=[tile_spec, seq_spec, seq_spec] + [_ANY] * nt,
        out_shape=[out, out, out] + _swap_shapes(grads),
        scratch_shapes=acc + [pltpu.SemaphoreType.DMA((nt,)), pltpu.SemaphoreType.DMA((nt,))],
        compiler_params=_cparams(("arbitrary", "arbitrary", "arbitrary")),
    )(qkv, qkv, qkv, cars, do, *grads)
    return res[0], res[1], res[2], list(res[3:])


def _bias_expand(rb):
    h = rb.shape[0]
    pad = CHUNK * N_PREV_CHUNKS
    n_f = BAND + CHUNK - 1
    f = jnp.concatenate([rb[:, MAX_REL - (CHUNK - 1):2 * MAX_REL],
                         jnp.broadcast_to(rb[:, 2 * MAX_REL:], (h, pad + CHUNK - MAX_REL))], axis=1)
    fr = f[:, ::-1]
    skew = jnp.pad(jnp.broadcast_to(fr[:, None, :], (h, CHUNK, n_f)), ((0, 0), (0, 0), (0, 1)))
    skew = skew.reshape(h, CHUNK * (n_f + 1))[:, :CHUNK * n_f].reshape(h, CHUNK, n_f)
    table = skew[:, :, CHUNK - 1:CHUNK - 1 + BAND]
    n_c = ATT_TILE // CHUNK
    width = 3 * ATT_TILE
    rows = [jnp.pad(table, ((0, 0), (0, 0), (c * CHUNK, width - BAND - c * CHUNK)), constant_values=NEG_BIAS)
            for c in range(n_c)]
    return jnp.concatenate(rows, axis=1)


def _ca_scores(qh_h, k_ref, bias_ref, h, i, scale):
    tq = ATT_TILE
    ss, offs = [], []
    for kb in range(3):
        jb = i - 2 + kb
        off = pl.multiple_of(jnp.maximum(jb, 0) * tq, tq)
        s = _dot_nt(qh_h, k_ref[pl.ds(off, tq), :]) * scale + bias_ref[h, :, kb * tq:(kb + 1) * tq]
        ss.append(jnp.where(jb >= 0, s, NEG_BIAS))
        offs.append(off)
    m = jnp.max(jnp.maximum(jnp.maximum(ss[0], ss[1]), ss[2]), axis=1, keepdims=True)
    ps = [jnp.exp(s - m) for s in ss]
    den = jnp.sum(ps[0] + ps[1] + ps[2], axis=1, keepdims=True)
    inv = 1.0 / den
    return [p * inv for p in ps], offs


def _ca_fwd(qkv, bias4, bl, seq, dg, gain, bufs=(), layer=None):
    t = bl * seq
    p_n = dg // LANES
    tq = ATT_TILE
    scale = 1.0 / math.sqrt(HEAD_DIM)
    nq, q_spec, k_spec, v_spec, tile_spec, _, g_spec = _sb_specs(bl, seq, p_n, 3 * p_n, tq)
    b_spec = pl.BlockSpec((2, tq, 3 * tq), lambda b, p, i: (p, 0, 0))
    nt = len(bufs)

    def body(q_ref, k_ref, v_ref, bias_ref, g_ref, *rest):
        o_ref, m_ref = rest[nt:nt + 2]
        i = pl.program_id(2)
        if nt:
            start, finish = _gather_plan(rest[nt + 2:2 * nt + 2], [b.shape[2] for b in bufs], rest[2 * nt + 2],
                                         rest[2 * nt + 3], layer)
            step = (pl.program_id(0) * p_n + pl.program_id(1)) * nq + i
            pl.when(step == 0)(start)
        lo, qh = _head_masks(q_ref[...])
        outs = []
        for h in range(2):
            ps, offs = _ca_scores(qh[h], k_ref, bias_ref, h, i, scale)
            acc = None
            for kb in range(3):
                pv = _dot(ps[kb].astype(BF), v_ref[pl.ds(offs[kb], tq), :])
                acc = pv if acc is None else acc + pv
            outs.append(acc)
        acc = jnp.where(lo, outs[0], outs[1])
        o_ref[...] = acc
        m_ref[...] = _rms_out(acc, lo, g_ref[...]).astype(BF)
        if nt:
            pl.when(step == bl * p_n * nq - 1)(finish)

    if not nt:
        return pl.pallas_call(
            body, name="ca_fwd", grid=(bl, p_n, nq), in_specs=[q_spec, k_spec, v_spec, b_spec, g_spec],
            out_specs=[tile_spec, tile_spec], out_shape=[_sds((t, dg), F32), _sds((t, dg), BF)],
            compiler_params=_cparams(("parallel", "parallel", "parallel")),
        )(qkv, qkv, qkv, bias4, gain)
    res = pl.pallas_call(
        body, name="ca_fwd_gather", grid=(bl, p_n, nq),
        in_specs=[q_spec, k_spec, v_spec, b_spec, g_spec] + [_ANY] * nt,
        out_specs=[tile_spec, tile_spec] + [_ANY] * nt,
        out_shape=[_sds((t, dg), F32), _sds((t, dg), BF)] + [_sds(b.shape, b.dtype) for b in bufs],
        input_output_aliases={5 + k: 2 + k for k in range(nt)},
        scratch_shapes=[pltpu.SemaphoreType.DMA((6 * nt,)), pltpu.SemaphoreType.DMA((6 * nt,))],
        compiler_params=_cparams(("arbitrary", "arbitrary", "arbitrary")),
    )(qkv, qkv, qkv, bias4, gain, *bufs)
    return res[0], res[1], list(res[2:])


def _ca_bwd(qkv, bias4, o, do, bl, seq, dg, psums=()):
    t = bl * seq
    p_n = dg // LANES
    tq = ATT_TILE
    scale = 1.0 / math.sqrt(HEAD_DIM)
    nq = seq // tq
    c0 = 3 * p_n
    q_spec = pl.BlockSpec((tq, LANES), lambda p, b, i: (b * nq + i, c0 + p))
    k_spec = pl.BlockSpec((seq, LANES), lambda p, b, i: (b, c0 + p_n + p))
    v_spec = pl.BlockSpec((seq, LANES), lambda p, b, i: (b, c0 + 2 * p_n + p))
    tile_spec = pl.BlockSpec((tq, LANES), lambda p, b, i: (b * nq + i, p))
    seq_spec = pl.BlockSpec((seq, LANES), lambda p, b, i: (b, p))
    b_spec = pl.BlockSpec((2, tq, 3 * tq), lambda p, b, i: (p, 0, 0))

    nt = len(psums)

    def body(q_ref, k_ref, v_ref, bias_ref, o_ref, do_ref, *rest):
        dq_ref, dk_ref, dv_ref, db_ref = rest[nt:nt + 4]
        dk_acc, dv_acc = rest[2 * nt + 4:2 * nt + 6]
        b = pl.program_id(1)
        i = pl.program_id(2)
        if nt:
            start, finish = _scatter_plan(rest[:nt], rest[nt + 4:2 * nt + 4], rest[2 * nt + 6], rest[2 * nt + 7])
            step = (pl.program_id(0) * bl + b) * nq + i
            pl.when(step == 0)(start)

        @pl.when(i == 0)
        def _():
            dk_acc[...] = jnp.zeros_like(dk_acc)
            dv_acc[...] = jnp.zeros_like(dv_acc)

        @pl.when((i == 0) & (b == 0))
        def _():
            db_ref[...] = jnp.zeros_like(db_ref)

        q = q_ref[...]
        dob = do_ref[...]
        lo, qh = _head_masks(q)
        _, doh = _head_masks(dob)
        prod = dob.astype(F32) * o_ref[...]
        delta = (jnp.sum(jnp.where(lo, prod, 0.0), axis=1, keepdims=True),
                 jnp.sum(jnp.where(lo, 0.0, prod), axis=1, keepdims=True))
        dqs = []
        for h in range(2):
            ps, offs = _ca_scores(qh[h], k_ref, bias_ref, h, i, scale)
            dq = None
            dks, dvs = [], []
            for kb in range(3):
                dp = _dot_nt(doh[h], v_ref[pl.ds(offs[kb], tq), :])
                ds = ps[kb] * (dp - delta[h])
                db_ref[h, :, kb * tq:(kb + 1) * tq] += ds
                dsb = (ds * scale).astype(BF)
                part = _dot(dsb, k_ref[pl.ds(offs[kb], tq), :])
                dq = part if dq is None else dq + part
                dks.append(_dot_tn(dsb, q))
                dvs.append(_dot_tn(ps[kb].astype(BF), dob))
            dqs.append(dq)
            hm = lo if h == 0 else jnp.logical_not(lo)
            for kb in range(3):
                dk_acc[pl.ds(offs[kb], tq), :] += jnp.where(hm, dks[kb], 0.0)
                dv_acc[pl.ds(offs[kb], tq), :] += jnp.where(hm, dvs[kb], 0.0)
        dq_ref[...] = jnp.where(lo, dqs[0], dqs[1]).astype(BF)

        @pl.when(i == nq - 1)
        def _():
            dk_ref[...] = dk_acc[...].astype(BF)
            dv_ref[...] = dv_acc[...].astype(BF)

        if nt:
            pl.when(step == p_n * bl * nq - 1)(finish)

    out = _sds((t, dg), BF)
    acc = [pltpu.VMEM((seq, LANES), F32), pltpu.VMEM((seq, LANES), F32)]
    if not nt:
        return pl.pallas_call(
            body, name="ca_bwd", grid=(p_n, bl, nq),
            in_specs=[q_spec, k_spec, v_spec, b_spec, tile_spec, tile_spec],
            out_specs=[tile_spec, seq_spec, seq_spec, b_spec], out_shape=[out, out, out, _sds(bias4.shape, F32)],
            scratch_shapes=acc, compiler_params=_cparams(("parallel", "arbitrary", "arbitrary")),
        )(qkv, qkv, qkv, bias4, o, do)
    res = pl.pallas_call(
        body, name="ca_bwd_scatter", grid=(p_n, bl, nq),
        in_specs=[q_spec, k_spec, v_spec, b_spec, tile_spec, tile_spec] + [_ANY] * nt,
        out_specs=[tile_spec, seq_spec, seq_spec, b_spec] + [_ANY] * nt,
        out_shape=[out, out, out, _sds(bias4.shape, F32)] + _scatter_shapes(psums),
        scratch_shapes=acc + [pltpu.SemaphoreType.DMA((3 * nt,)), pltpu.SemaphoreType.DMA((3 * nt,))],
        compiler_params=_cparams(("arbitrary", "arbitrary", "arbitrary")),
    )(qkv, qkv, qkv, bias4, o, do, *psums)
    return res[0], res[1], res[2], res[3], list(res[4:])


_ANY = pl.BlockSpec(memory_space=pl.ANY)


def _mesh_pos():
    x, y, c = lax.axis_index("x"), lax.axis_index("y"), lax.axis_index("c")
    chips = [(1 - x, y), (x, 1 - y), (1 - x, 1 - y)]
    return x, y, c, chips


def _half(ref_rows, c):
    return pl.ds(c * (ref_rows // 2), ref_rows // 2)


def _cast_into_slot(w, slot):
    l_n, r, c_n = w.shape
    tr = _tile(r, 512)

    def body(s_ref, w_ref, o_ref):
        o_ref[...] = w_ref[...].astype(BF)

    return pl.pallas_call(
        body, name="cast_into_slot", out_shape=_sds((l_n, 4, r, c_n), BF),
        grid_spec=pltpu.PrefetchScalarGridSpec(
            num_scalar_prefetch=1, grid=(l_n, r // tr),
            in_specs=[pl.BlockSpec((None, tr, c_n), lambda l, m, s: (l, m, 0))],
            out_specs=pl.BlockSpec((None, None, tr, c_n), lambda l, m, s: (l, s[0], m, 0))),
        compiler_params=_cparams(("parallel", "parallel")),
    )(slot, w)


def _gather_plan(outs, n_rows, send_sems, recv_sems, layer):
    nt = len(outs)

    def blk(t, s, cc):
        return outs[t].at[pl.ds(layer, 1), s, _half(n_rows[t], cc)]

    def copy(k, ref, to):
        return pltpu.make_async_remote_copy(src_ref=ref, dst_ref=ref, send_sem=send_sems.at[k],
                                            recv_sem=recv_sems.at[k], device_id=to, device_id_type=MESH_IDS)

    def over_ici():
        x, y, c, chips = _mesh_pos()
        return [copy(j * nt + t, blk(t, 2 * x + y, c), (px, py, c)) for j, (px, py) in enumerate(chips) for t in range(nt)]

    def start():
        for cp in over_ici():
            cp.start()

    def finish():
        x, y, c, chips = _mesh_pos()
        passed = []
        for j, (px, py) in enumerate(chips):
            for t in range(nt):
                got = blk(t, 2 * px + py, c)
                copy(j * nt + t, got, (px, py, c)).wait_recv()
                relay = copy(3 * nt + j * nt + t, got, (x, y, 1 - c))
                relay.start()
                passed.append(relay)
        for j, (px, py) in enumerate(chips):
            for t in range(nt):
                copy(3 * nt + j * nt + t, blk(t, 2 * px + py, 1 - c), (x, y, 1 - c)).wait_recv()
        for cp in over_ici() + passed:
            cp.wait_send()

    return start, finish


def _gather_weights(bufs, layer):
    nt = len(bufs)

    def body(*refs):
        start, finish = _gather_plan(refs[nt:2 * nt], [b.shape[2] for b in bufs], refs[2 * nt], refs[2 * nt + 1], layer)
        start()
        finish()

    return pl.pallas_call(
        body, name="gather_weights", in_specs=[_ANY] * nt, out_specs=[_ANY] * nt,
        out_shape=[_sds(b.shape, b.dtype) for b in bufs], input_output_aliases={t: t for t in range(nt)},
        scratch_shapes=[pltpu.SemaphoreType.DMA((6 * nt,)), pltpu.SemaphoreType.DMA((6 * nt,))],
    )(*bufs)


def _swap_halves(grads):
    nt = len(grads)

    def body(*refs):
        start, finish = _swap_plan(refs[:nt], refs[nt:2 * nt], [g.shape[2] for g in grads], refs[2 * nt], refs[2 * nt + 1])
        start()
        finish()

    return pl.pallas_call(
        body, name="swap_halves", in_specs=[_ANY] * nt, out_specs=[_ANY] * nt, out_shape=_swap_shapes(grads),
        scratch_shapes=[pltpu.SemaphoreType.DMA((nt,)), pltpu.SemaphoreType.DMA((nt,))],
    )(*grads)


def _swap_shapes(grads):
    return [_sds(g.shape[:2] + (g.shape[2] // 2, g.shape[3]), g.dtype) for g in grads]


def _swap_plan(ins, theirs, n_rows, send_sems, recv_sems):
    def sends():
        x, y, c, _ = _mesh_pos()
        return [pltpu.make_async_remote_copy(
            src_ref=ins[t].at[:, :, _half(n_rows[t], 1 - c)], dst_ref=theirs[t], send_sem=send_sems.at[t],
            recv_sem=recv_sems.at[t], device_id=(x, y, 1 - c), device_id_type=MESH_IDS) for t in range(len(ins))]

    def start():
        for cp in sends():
            cp.start()

    def finish():
        for cp in sends():
            cp.wait_recv()
        for cp in sends():
            cp.wait_send()

    return start, finish


def _add_pair(mine, theirs):
    nt = len(mine)
    r2s = [th.shape[2] for th in theirs]

    def body(*refs):
        c = lax.axis_index("c")
        for k in range(nt):
            rows = pl.ds(pl.multiple_of(c * r2s[k], 16), r2s[k])
            refs[2 * nt + k][...] = (refs[k][rows, :].astype(F32) + refs[nt + k][...].astype(F32)).astype(BF)

    def spec(shape):
        return pl.BlockSpec((None, None) + tuple(shape[2:]), lambda s: (0, s, 0, 0))

    return pl.pallas_call(
        body, name="add_pair", out_shape=[_sds(th.shape, BF) for th in theirs], grid=(mine[0].shape[1],),
        in_specs=[spec(a.shape) for a in mine] + [spec(b.shape) for b in theirs],
        out_specs=[spec(b.shape) for b in theirs], compiler_params=_cparams(("parallel",)),
    )(*mine, *theirs)


def _scatter_chips(psums):
    nt = len(psums)

    def body(*refs):
        start, finish = _scatter_plan(refs[:nt], refs[nt:2 * nt], refs[2 * nt], refs[2 * nt + 1])
        start()
        finish()

    return pl.pallas_call(
        body, name="scatter_chips", in_specs=[_ANY] * nt, out_specs=[_ANY] * nt, out_shape=_scatter_shapes(psums),
        scratch_shapes=[pltpu.SemaphoreType.DMA((3 * nt,)), pltpu.SemaphoreType.DMA((3 * nt,))],
    )(*psums)


def _scatter_shapes(psums):
    return [_sds((3, p.shape[0]) + p.shape[2:], p.dtype) for p in psums]


def _scatter_plan(ins, got, send_sems, recv_sems):
    nt = len(ins)

    def sends():
        _, _, c, chips = _mesh_pos()
        return [pltpu.make_async_remote_copy(
            src_ref=ins[t].at[:, 2 * px + py], dst_ref=got[t].at[j], send_sem=send_sems.at[j * nt + t],
            recv_sem=recv_sems.at[j * nt + t], device_id=(px, py, c), device_id_type=MESH_IDS)
            for j, (px, py) in enumerate(chips) for t in range(nt)]

    def start():
        for cp in sends():
            cp.start()

    def finish():
        for cp in sends():
            cp.wait_recv()
        for cp in sends():
            cp.wait_send()

    return start, finish


def _add_chips(psums, gots, fulls, layer, depth):
    nt = len(psums)
    r2s = [p.shape[2] for p in psums]
    n_alias = nt if fulls is not None else 0

    def body(*refs):
        c = lax.axis_index("c")
        chip = 2 * lax.axis_index("x") + lax.axis_index("y")
        outs = refs[n_alias + 2 * nt:]
        for k in range(nt):
            acc = refs[n_alias + k][chip].astype(F32)
            for j in range(3):
                acc = acc + refs[n_alias + nt + k][j].astype(F32)
            outs[k][pl.ds(pl.multiple_of(c * r2s[k], 8), r2s[k]), :] = acc
            outs[k][pl.ds(pl.multiple_of((1 - c) * r2s[k], 8), r2s[k]), :] = jnp.zeros_like(acc)

    once = pl.Buffered(1)
    in_specs = [_ANY] * n_alias
    in_specs += [pl.BlockSpec((None,) + p.shape[1:], lambda i: (0, 0, 0, 0), pipeline_mode=once) for p in psums]
    in_specs += [pl.BlockSpec((3, None) + g.shape[2:], lambda i: (0, 0, 0, 0), pipeline_mode=once) for g in gots]
    out_shape = [_sds((depth, 2 * p.shape[2], p.shape[3]), F32) for p in psums]
    out_specs = [pl.BlockSpec((None,) + s.shape[1:], lambda i: (layer, 0, 0), pipeline_mode=once) for s in out_shape]
    return pl.pallas_call(
        body, name="add_chips", out_shape=out_shape, grid=(1,), in_specs=in_specs, out_specs=out_specs,
        input_output_aliases={k: k for k in range(n_alias)}, compiler_params=_cparams(("arbitrary",)),
    )(*(fulls or []), *psums, *gots)


def _join_halves(fulls):
    nt = len(fulls)

    def body(*refs):
        outs = refs[nt:2 * nt]
        send_sems, recv_sems = refs[2 * nt:]
        x, y, c, _ = _mesh_pos()

        def copy(t, cc):
            blk = outs[t].at[:, _half(fulls[t].shape[1], cc)]
            return pltpu.make_async_remote_copy(
                src_ref=blk, dst_ref=blk, send_sem=send_sems.at[t], recv_sem=recv_sems.at[t],
                device_id=(x, y, 1 - c), device_id_type=MESH_IDS)

        sends = [copy(t, c) for t in range(nt)]
        for cp in sends:
            cp.start()
        for t in range(nt):
            copy(t, 1 - c).wait_recv()
        for cp in sends:
            cp.wait_send()

    return pl.pallas_call(
        body, name="join_halves", in_specs=[_ANY] * nt, out_specs=[_ANY] * nt,
        out_shape=[_sds(f.shape, f.dtype) for f in fulls], input_output_aliases={t: t for t in range(nt)},
        scratch_shapes=[pltpu.SemaphoreType.DMA((nt,)), pltpu.SemaphoreType.DMA((nt,))],
    )(*fulls)


def _allreduce_small(part):
    r, d = part.shape
    flips = [(fx, fy, fc) for fx in (0, 1) for fy in (0, 1) for fc in (0, 1) if fx + fy + fc]

    def body(p_ref, o_ref, buf, send_sems, recv_sems):
        x, y, c, _ = _mesh_pos()
        me = 4 * x + 2 * y + c
        buf[me] = p_ref[...]
        sends = []
        for k, (fx, fy, fc) in enumerate(flips):
            sends.append(pltpu.make_async_remote_copy(
                src_ref=p_ref, dst_ref=buf.at[me], send_sem=send_sems.at[k], recv_sem=recv_sems.at[k],
                device_id=(x ^ fx, y ^ fy, c ^ fc), device_id_type=MESH_IDS))
        for cp in sends:
            cp.start()
        for k, (fx, fy, fc) in enumerate(flips):
            peer = 4 * (x ^ fx) + 2 * (y ^ fy) + (c ^ fc)
            pltpu.make_async_remote_copy(
                src_ref=p_ref, dst_ref=buf.at[peer], send_sem=send_sems.at[k], recv_sem=recv_sems.at[k],
                device_id=(x ^ fx, y ^ fy, c ^ fc), device_id_type=MESH_IDS).wait_recv()
        acc = buf[0]
        for dev in range(1, 8):
            acc = acc + buf[dev]
        o_ref[...] = acc
        for cp in sends:
            cp.wait_send()

    vmem = pl.BlockSpec(memory_space=pltpu.VMEM)
    return pl.pallas_call(
        body, name="allreduce_small", in_specs=[vmem], out_specs=vmem, out_shape=_sds((r, d), F32),
        scratch_shapes=[pltpu.VMEM((8, r, d), F32), pltpu.SemaphoreType.DMA((7,)), pltpu.SemaphoreType.DMA((7,))],
    )(part)


BIG = ("ffn1_w_gate", "ffn1_w_up", "ffn1_w_down", "w_in", "w_out", "ffn2_w_gate", "ffn2_w_up", "ffn2_w_down")
SMALL = ("ln1_g", "ln1_b", "rel_bias", "sb_out_g", "ca_out_g", "ln2_g", "ln2_b", "ln3_g", "ln3_b")
ORDER = ("ffn1_w_gate", "ffn1_w_up", "ffn1_w_down", "ln1_g", "ln1_b", "w_in", "rel_bias", "sb_out_g", "ca_out_g",
         "w_out", "ln2_g", "ln2_b", "ffn2_w_gate", "ffn2_w_up", "ffn2_w_down", "ln3_g", "ln3_b")


def _pack_small(vals, depth, d):
    rows = []
    for l in range(depth):
        for n in ("ln1_g", "ln1_b", "ln2_g", "ln2_b", "ln3_g", "ln3_b"):
            rows.append(vals[n][l][None, :])
        rows.append(jnp.concatenate([vals["sb_out_g"][l], vals["ca_out_g"][l]])[None, :])
        rb = vals["rel_bias"][l]
        rows.append(jnp.pad(rb, ((0, 0), (0, d - rb.shape[1]))))
    packed = jnp.concatenate(rows, axis=0)
    pad = (-packed.shape[0]) % 8
    return jnp.pad(packed, ((0, pad), (0, 0)))


def _unpack_small(packed, depth, d, heads, n_rel):
    per = 7 + heads
    out = {n: [] for n in SMALL}
    for l in range(depth):
        base = l * per
        for i, n in enumerate(("ln1_g", "ln1_b", "ln2_g", "ln2_b", "ln3_g", "ln3_b")):
            out[n].append(packed[base + i])
        out["sb_out_g"].append(packed[base + 6, :d // 2])
        out["ca_out_g"].append(packed[base + 6, d // 2:])
        out["rel_bias"].append(packed[base + 7:base + 7 + heads, :n_rel])
    return {n: jnp.stack(v) for n, v in out.items()}


def kernel(x, ffn1_w_gate, ffn1_w_up, ffn1_w_down, ln1_g, ln1_b, w_in, rel_bias, sb_out_g, ca_out_g, w_out, ln2_g, ln2_b, ffn2_w_gate, ffn2_w_up, ffn2_w_down, ln3_g, ln3_b, loss_target, m_ffn1_w_gate, m_ffn1_w_up, m_ffn1_w_down, m_ln1_g, m_ln1_b, m_w_in, m_rel_bias, m_sb_out_g, m_ca_out_g, m_w_out, m_ln2_g, m_ln2_b, m_ffn2_w_gate, m_ffn2_w_up, m_ffn2_w_down, m_ln3_g, m_ln3_b, v_ffn1_w_gate, v_ffn1_w_up, v_ffn1_w_down, v_ln1_g, v_ln1_b, v_w_in, v_rel_bias, v_sb_out_g, v_ca_out_g, v_w_out, v_ln2_g, v_ln2_b, v_ffn2_w_gate, v_ffn2_w_up, v_ffn2_w_down, v_ln3_g, v_ln3_b):
    w = dict(ffn1_w_gate=ffn1_w_gate, ffn1_w_up=ffn1_w_up, ffn1_w_down=ffn1_w_down, ln1_g=ln1_g, ln1_b=ln1_b, w_in=w_in, rel_bias=rel_bias, sb_out_g=sb_out_g, ca_out_g=ca_out_g, w_out=w_out, ln2_g=ln2_g, ln2_b=ln2_b, ffn2_w_gate=ffn2_w_gate, ffn2_w_up=ffn2_w_up, ffn2_w_down=ffn2_w_down, ln3_g=ln3_g, ln3_b=ln3_b)
    mom = dict(ffn1_w_gate=m_ffn1_w_gate, ffn1_w_up=m_ffn1_w_up, ffn1_w_down=m_ffn1_w_down, ln1_g=m_ln1_g, ln1_b=m_ln1_b, w_in=m_w_in, rel_bias=m_rel_bias, sb_out_g=m_sb_out_g, ca_out_g=m_ca_out_g, w_out=m_w_out, ln2_g=m_ln2_g, ln2_b=m_ln2_b, ffn2_w_gate=m_ffn2_w_gate, ffn2_w_up=m_ffn2_w_up, ffn2_w_down=m_ffn2_w_down, ln3_g=m_ln3_g, ln3_b=m_ln3_b)
    var = dict(ffn1_w_gate=v_ffn1_w_gate, ffn1_w_up=v_ffn1_w_up, ffn1_w_down=v_ffn1_w_down, ln1_g=v_ln1_g, ln1_b=v_ln1_b, w_in=v_w_in, rel_bias=v_rel_bias, sb_out_g=v_sb_out_g, ca_out_g=v_ca_out_g, w_out=v_w_out, ln2_g=v_ln2_g, ln2_b=v_ln2_b, ffn2_w_gate=v_ffn2_w_gate, ffn2_w_up=v_ffn2_w_up, ffn2_w_down=v_ffn2_w_down, ln3_g=v_ln3_g, ln3_b=v_ln3_b)

    bl, seq, d = x.shape
    t = bl * seq
    depth = w_in.shape[0]
    dg = d // 2
    heads = dg // HEAD_DIM
    alpha = (2 * depth) ** 0.25
    n_rel = rel_bias.shape[-1]

    chip = (2 * lax.axis_index("x") + lax.axis_index("y")).astype(jnp.int32).reshape(1)
    bufs = list(_gather_weights([_cast_into_slot(w[n], chip) for n in BIG], 0))
    with_sb = [BIG.index(n) for n in ("w_in", "ffn2_w_gate", "ffn2_w_up", "ffn2_w_down")]
    with_ca = [k for k in range(len(BIG)) if k not in with_sb]

    qd = dg // 2
    shards = 4
    ffn_terms = [(0, s, None, s) for s in range(shards)]
    out_terms = [(i, None, (k * qd, (k + 1) * qd), 2 * i + k) for i in range(2) for k in range(2)]
    bias_all, bias_vjp = jax.vjp(jax.vmap(_bias_expand), rel_bias)

    xs = x.reshape(t, d)
    saved = []
    for l in range(depth):
        lw = {n: _LayerOf((b, l)) for n, b in zip(BIG, bufs)}
        row = lambda a: a[l][None, :]
        g1, u1, h1 = _ffn_up(xs, lw["ffn1_w_gate"], lw["ffn1_w_up"])
        x1, xh1, rs1 = _proj_ln("ffn_down_ln", [h1], lw["ffn1_w_down"], ffn_terms, xs, row(ln1_g), row(ln1_b),
                                FFN_RESIDUAL, alpha)
        qkv = _qkv_proj(x1, lw["w_in"])
        if l + 1 < depth:
            o_sb, m_sb, c_sb, new = _sb_fwd(qkv, bl, seq, dg, row(sb_out_g), [bufs[k] for k in with_sb], l + 1)
            for k, b in zip(with_sb, new):
                bufs[k] = b
            o_ca, m_ca, new = _ca_fwd(qkv, bias_all[l], bl, seq, dg, row(ca_out_g), [bufs[k] for k in with_ca], l + 1)
            for k, b in zip(with_ca, new):
                bufs[k] = b
            lw = {n: _LayerOf((b, l)) for n, b in zip(BIG, bufs)}
        else:
            o_sb, m_sb, c_sb = _sb_fwd(qkv, bl, seq, dg, row(sb_out_g))
            o_ca, m_ca = _ca_fwd(qkv, bias_all[l], bl, seq, dg, row(ca_out_g))
        x2, xh2, rs2 = _proj_ln("attn_out_ln", [m_sb, m_ca], lw["w_out"], out_terms, x1, row(ln2_g), row(ln2_b),
                                1.0, alpha)
        g2, u2, h2 = _ffn_up(x2, lw["ffn2_w_gate"], lw["ffn2_w_up"])
        x3, xh3, rs3 = _proj_ln("ffn_down_ln", [h2], lw["ffn2_w_down"], ffn_terms, x2, row(ln3_g), row(ln3_b),
                                FFN_RESIDUAL, alpha)
        saved.append(dict(xin=xs, g1=g1, u1=u1, h1=h1, x1=x1, xh1=xh1, rs1=rs1, qkv=qkv, o_sb=o_sb, m_sb=m_sb,
                          c_sb=c_sb, o_ca=o_ca, m_ca=m_ca, x2=x2, xh2=xh2, rs2=rs2, g2=g2, u2=u2, h2=h2, xh3=xh3,
                          rs3=rs3))
        xs = x3

    last = saved[-1]
    dr3, drs3, pg, pb, sq = _loss_grad(xs, loss_target.reshape(t, d), last["xh3"], last["rs3"],
                                       ln3_g[depth - 1][None, :], FFN_RESIDUAL)
    loss = lax.psum(0.5 * jnp.sum(sq) / d, ("x", "y", "c"))

    big_grads = {n: [None] * depth for n in BIG}
    small_grads = {n: [None] * depth for n in SMALL}
    s8 = lambda a: jnp.sum(a, axis=0)
    wd_ = 3 * d // shards
    in_pieces = [_shard_cols(6, dg, s, wd_) for s in range(shards)]
    dbias = [None] * depth
    chip_sums, arrived = [None] * depth, [None] * depth
    for l in reversed(range(depth)):
        sv = saved[l]
        lw = {n: _LayerOf((b, l)) for n, b in zip(BIG, bufs)}
        row = lambda a: a[l][None, :]
        small_grads["ln3_g"][l], small_grads["ln3_b"][l] = s8(pg), s8(pb)
        dgt, dut = _swiglu_bwd(drs3, lw["ffn2_w_down"], sv["g2"], sv["u2"])
        dr2, drs2, pg, pb = _ffn_dx(dgt, dut, lw["ffn2_w_gate"], lw["ffn2_w_up"], dr3, alpha,
                                    (sv["xh2"], sv["rs2"], row(ln2_g), 1.0))
        small_grads["ln2_g"][l], small_grads["ln2_b"][l] = s8(pg), s8(pb)
        big_grads["ffn2_w_down"][l] = _wgrad("wgrad_down", [sv["h2"]], [drs3], shards, _lshape(lw["ffn2_w_down"])[1:],
                                             lambda a, s: a[0][s], lambda b, s: b[0][...])
        big_grads["ffn2_w_gate"][l] = _wgrad("wgrad_up", [sv["x2"]], [dgt], shards, _lshape(lw["ffn2_w_gate"])[1:],
                                             lambda a, s: a[0][...], lambda b, s: b[0][s])
        big_grads["ffn2_w_up"][l] = _wgrad("wgrad_up", [sv["x2"]], [dut], shards, _lshape(lw["ffn2_w_up"])[1:],
                                           lambda a, s: a[0][...], lambda b, s: b[0][s])
        do_sb, do_ca, pgs, pgc = _attn_dmix(drs2, lw["w_out"], sv["o_sb"], sv["o_ca"], row(sb_out_g), row(ca_out_g))
        small_grads["sb_out_g"][l], small_grads["ca_out_g"][l] = s8(pgs), s8(pgc)
        big_grads["w_out"][l] = _wgrad(
            "wgrad_out", [sv["m_sb"], sv["m_ca"]], [drs2], shards, (qd, d),
            lambda a, s: a[s // 2][:, (s % 2) * qd:(s % 2 + 1) * qd], lambda b, s: b[0][...])
        if l + 1 < depth:
            dqa, dka, dva, theirs = _sb_bwd(sv["qkv"], sv["c_sb"], do_sb, bl, seq, dg, partial)
            chip_sum = _add_pair(partial, theirs)
            dqb, dkb, dvb, dbias[l], arrived[l + 1] = _ca_bwd(sv["qkv"], bias_all[l], sv["o_ca"], do_ca, bl, seq, dg,
                                                             chip_sum)
            chip_sums[l + 1] = chip_sum
        else:
            dqa, dka, dva = _sb_bwd(sv["qkv"], sv["c_sb"], do_sb, bl, seq, dg)
            dqb, dkb, dvb, dbias[l] = _ca_bwd(sv["qkv"], bias_all[l], sv["o_ca"], do_ca, bl, seq, dg)
        dqkv = [dqa, dka, dva, dqb, dkb, dvb]
        dr1, drs1, pg, pb = _attn_dx(dqkv, lw["w_in"], dr2, alpha, (sv["xh1"], sv["rs1"], row(ln1_g), FFN_RESIDUAL))
        small_grads["ln1_g"][l], small_grads["ln1_b"][l] = s8(pg), s8(pb)
        big_grads["w_in"][l] = _wgrad(
            "wgrad_in", [sv["x1"]], dqkv, shards, (d, wd_), lambda a, s: a[0][...],
            lambda b, s: jnp.concatenate([b[gi][:, lo:hi] for gi, lo, hi in in_pieces[s]], axis=1))
        dgt, dut = _swiglu_bwd(drs1, lw["ffn1_w_down"], sv["g1"], sv["u1"])
        if l > 0:
            prev = saved[l - 1]
            dr3, drs3, pg, pb = _ffn_dx(dgt, dut, lw["ffn1_w_gate"], lw["ffn1_w_up"], dr1, alpha,
                                        (prev["xh3"], prev["rs3"], ln3_g[l - 1][None, :], FFN_RESIDUAL))
        else:
            grad_x = _ffn_dx(dgt, dut, lw["ffn1_w_gate"], lw["ffn1_w_up"], dr1, alpha, None).reshape(bl, seq, d)
        big_grads["ffn1_w_down"][l] = _wgrad("wgrad_down", [sv["h1"]], [drs1], shards, _lshape(lw["ffn1_w_down"])[1:],
                                             lambda a, s: a[0][s], lambda b, s: b[0][...])
        big_grads["ffn1_w_gate"][l] = _wgrad("wgrad_up", [sv["xin"]], [dgt], shards, _lshape(lw["ffn1_w_gate"])[1:],
                                             lambda a, s: a[0][...], lambda b, s: b[0][s])
        big_grads["ffn1_w_up"][l] = _wgrad("wgrad_up", [sv["xin"]], [dut], shards, _lshape(lw["ffn1_w_up"])[1:],
                                           lambda a, s: a[0][...], lambda b, s: b[0][s])
        partial = [big_grads[n][l][None] for n in BIG]
    chip_sums[0] = _add_pair(partial, _swap_halves(partial))
    arrived[0] = _scatter_chips(chip_sums[0])
    rel_grads = bias_vjp(jnp.stack(dbias))[0]
    for l in range(depth):
        small_grads["rel_bias"][l] = rel_grads[l]

    halves = []
    for lo_k in range(0, len(BIG), 4):
        part = None
        for l in range(depth):
            part = _add_chips(chip_sums[l][lo_k:lo_k + 4], arrived[l][lo_k:lo_k + 4], part, l, depth)
        halves += list(part)
    joined = dict(zip(BIG, _join_halves(halves)))
    packed = _allreduce_small(_pack_small({n: jnp.stack(small_grads[n]) for n in SMALL}, depth, d))

    grads, delta, new_m, new_v = {}, {}, {}, {}
    for n in BIG:
        grads[n], delta[n], new_m[n], new_v[n] = _adamw(w[n], joined[n], mom[n], var[n])
    pw, pm, pv = (_pack_small({n: src[n] for n in SMALL}, depth, d) for src in (w, mom, var))
    sg, sd, sm, sv_ = _adamw(pw, packed, pm, pv)
    for dst, src in ((grads, sg), (delta, sd), (new_m, sm), (new_v, sv_)):
        dst.update(_unpack_small(src, depth, d, heads, n_rel))

    return (loss, grad_x, *[grads[n] for n in ORDER], *[delta[n] for n in ORDER],
            *[new_m[n] for n in ORDER], *[new_v[n] for n in ORDER])
```
